```python
import jax, jax.numpy as jnp
from jax import lax
import numpy as np

D_MODEL = 2048
BATCH = 2
SEQ = 4096
DEPTH = 2

SWA_HEADS = 8
SWA_KV_HEADS = 2
SWA_HEAD_DIM = 64
SWA_WINDOW = 128
M_HEADS = 4
M_QK_DIM = 64
M_V_DIM = 128
M_CHUNK = 64
M_CONV = 4
C_HEADS = 8
C_NOPE_DIM = 128
C_ROPE_DIM = 64
C_V_DIM = 128
C_Q_LORA = 512
C_KV_LORA = 256
C_Q_BLOCK = 128
ROPE_THETA = 10000.0
N_EXPERTS = 32
TOP_K = 4
D_EXPERT = D_MODEL
SWIGLU_LIMIT = 7.0
SWIGLU_ALPHA = 1.702
MOE_BLOCK = 256
DN_ALPHA = (2 * DEPTH) ** 0.25
DN_BETA = (8 * DEPTH) ** -0.25
LN_EPS = 1e-5
RMS_EPS = 1e-6

A_Q = SWA_HEADS * SWA_HEAD_DIM
A_KV = SWA_KV_HEADS * SWA_HEAD_DIM
B_QK = M_HEADS * M_QK_DIM
B_V = M_HEADS * M_V_DIM
C_OUT = C_HEADS * C_V_DIM
IN_WIDTHS = (A_Q, A_KV, A_KV, B_QK, B_QK, B_V, B_V, 2 * M_HEADS, C_Q_LORA, C_KV_LORA, C_ROPE_DIM)
N_IN = sum(IN_WIDTHS)
D_MIX = A_Q + B_V + C_OUT

kernel_name = "hymba_style_swa_mlstm_mla_moe_deepnorm"

F32 = jnp.float32


def layer_norm(x, g, b):
    xf = x.astype(F32)
    mu = xf.mean(-1, keepdims=True)
    var = jnp.mean(jnp.square(xf - mu), -1, keepdims=True)
    return ((xf - mu) * lax.rsqrt(var + LN_EPS) * g.astype(F32) + b.astype(F32)).astype(x.dtype)


def rms_norm(x, g):
    xf = x.astype(F32)
    y = xf * lax.rsqrt(jnp.mean(jnp.square(xf), -1, keepdims=True) + RMS_EPS)
    return (y * g.astype(F32)).astype(x.dtype)


def head_layer_norm(h, g):
    mu = h.mean(-1, keepdims=True)
    var = jnp.mean(jnp.square(h - mu), -1, keepdims=True)
    return (h - mu) * lax.rsqrt(var + LN_EPS) * g.astype(F32).reshape(h.shape[-2:])


def rope_tables(seq, dim):
    inv = 1.0 / (ROPE_THETA ** (jnp.arange(0, dim, 2, dtype=F32) / dim))
    ang = jnp.arange(seq, dtype=F32)[:, None] * inv[None, :]
    return jnp.cos(ang), jnp.sin(ang)


def apply_rope(x, cos, sin):
    x1, x2 = jnp.split(x.astype(F32), 2, axis=-1)
    return jnp.concatenate([x1 * cos - x2 * sin, x2 * cos + x1 * sin], -1).astype(x.dtype)


def split_columns(p):
    outs, start = [], 0
    for w in IN_WIDTHS:
        outs.append(p[..., start:start + w])
        start += w
    return outs


def causal_depthwise_conv(x, w, bias):
    k = w.shape[0]
    y = lax.conv_general_dilated(x, w[:, None, :].astype(x.dtype), window_strides=(1,),
                                 padding=((k - 1, 0),), dimension_numbers=('NWC', 'WIO', 'NWC'),
                                 feature_group_count=x.shape[-1])
    return y + bias.astype(x.dtype)


def sliding_window_attention(q, k, v, sinks):
    b, s, _, dh = q.shape
    w = SWA_WINDOW
    nb = s // w
    g = SWA_HEADS // SWA_KV_HEADS
    qb = q.reshape(b, nb, w, SWA_KV_HEADS, g, dh)

    def with_prev(t):
        tb = t.reshape(b, nb, w, SWA_KV_HEADS, dh)
        prev = jnp.pad(tb, ((0, 0), (1, 0), (0, 0), (0, 0), (0, 0)))[:, :-1]
        return jnp.concatenate([prev, tb], axis=2)

    kk, vv = with_prev(k), with_prev(v)
    scores = jnp.einsum('bnqhgd,bnkhd->bnhgqk', qb, kk, preferred_element_type=F32) * (dh ** -0.5)
    q_idx = jnp.arange(w)[:, None] + w
    k_idx = jnp.arange(2 * w)[None, :]
    rel = q_idx - k_idx
    band = (rel >= 0) & (rel < w)
    has_prev = (jnp.arange(nb)[:, None, None] > 0) | (k_idx[None] >= w)
    mask = band[None] & has_prev
    scores = jnp.where(mask[None, :, None, None], scores, -jnp.inf)
    sink = sinks.astype(F32).reshape(1, 1, SWA_KV_HEADS, g, 1, 1)
    sink = jnp.broadcast_to(sink, scores.shape[:-1] + (1,))
    probs = jax.nn.softmax(jnp.concatenate([scores, sink], -1), axis=-1)[..., :-1]
    out = jnp.einsum('bnhgqk,bnkhd->bnqhgd', probs.astype(v.dtype), vv)
    return out.reshape(b, s, SWA_HEADS * dh)


def mlstm(q, k, v, i_pre, f_pre):
    b, s, h, dqk = q.shape
    L = M_CHUNK
    nc = s // L

    def chunks(t):
        t = t.reshape((b, nc, L, h) + t.shape[3:])
        return jnp.moveaxis(t, (1, 3), (0, 2))

    qc = chunks(q.astype(F32)) * (dqk ** -0.5)
    kc = chunks(k.astype(F32))
    vc = chunks(v.astype(F32))
    ic = chunks(i_pre.astype(F32))
    lfc = chunks(jax.nn.log_sigmoid(f_pre.astype(F32)))
    causal = jnp.tril(jnp.ones((L, L), bool))

    def step(carry, xs):
        C, n, m = carry
        qj, kj, vj, ij, lf = xs
        bcum = jnp.cumsum(lf, -1)
        D = bcum[..., :, None] - bcum[..., None, :] + ij[..., None, :]
        D = jnp.where(causal, D, -jnp.inf)
        m_inter = bcum + m[..., None]
        m_j = jnp.maximum(m_inter, D.max(-1))
        w_intra = jnp.exp(D - m_j[..., None])
        w_inter = jnp.exp(m_inter - m_j)
        qk = jnp.einsum('bhjd,bhsd->bhjs', qj, kj) * w_intra
        num = jnp.einsum('bhjs,bhsv->bhjv', qk, vj) + w_inter[..., None] * jnp.einsum('bhvd,bhjd->bhjv', C, qj)
        nq = qk.sum(-1) + w_inter * jnp.einsum('bhd,bhjd->bhj', n, qj)
        den = jnp.maximum(jnp.abs(nq), jnp.exp(-m_j))
        hj = num / den[..., None]
        m_new = m_j[..., -1]
        w_s = jnp.exp(bcum[..., -1:] - bcum + ij - m_new[..., None])
        w_c = jnp.exp(bcum[..., -1] + m - m_new)
        C_new = w_c[..., None, None] * C + jnp.einsum('bhs,bhsv,bhsd->bhvd', w_s, vj, kj)
        n_new = w_c[..., None] * n + jnp.einsum('bhs,bhsd->bhd', w_s, kj)
        return (C_new, n_new, m_new), hj

    init = (jnp.zeros((b, h, v.shape[-1], dqk), F32), jnp.zeros((b, h, dqk), F32), jnp.zeros((b, h), F32))
    _, hs = lax.scan(step, init, (qc, kc, vc, ic, lfc))
    return jnp.moveaxis(hs, (0, 2), (1, 3)).reshape(b, s, h, v.shape[-1])


def mla(c_q, c_kv, k_rope, q_norm_g, w_uq, kv_norm_g, w_ukv, cos, sin):
    b, s, _ = c_q.shape
    q = (rms_norm(c_q, q_norm_g) @ w_uq).reshape(b, s, C_HEADS, C_NOPE_DIM + C_ROPE_DIM)
    q_nope = q[..., :C_NOPE_DIM]
    q_rope = apply_rope(q[..., C_NOPE_DIM:], cos[:, None, :], sin[:, None, :])
    kv = (rms_norm(c_kv, kv_norm_g) @ w_ukv).reshape(b, s, C_HEADS, C_NOPE_DIM + C_V_DIM)
    k_nope, v = kv[..., :C_NOPE_DIM], kv[..., C_NOPE_DIM:]
    k_r = apply_rope(k_rope, cos, sin)
    scale = (C_NOPE_DIM + C_ROPE_DIM) ** -0.5
    nq = s // C_Q_BLOCK
    qn_b = q_nope.reshape(b, nq, C_Q_BLOCK, C_HEADS, C_NOPE_DIM).swapaxes(0, 1)
    qr_b = q_rope.reshape(b, nq, C_Q_BLOCK, C_HEADS, C_ROPE_DIM).swapaxes(0, 1)
    k_pos = jnp.arange(s)

    def attend(args):
        qn, qr, blk = args
        sc = (jnp.einsum('bqhd,bkhd->bhqk', qn, k_nope, preferred_element_type=F32)
              + jnp.einsum('bqhd,bkd->bhqk', qr, k_r, preferred_element_type=F32)) * scale
        q_pos = blk * C_Q_BLOCK + jnp.arange(C_Q_BLOCK)
        sc = jnp.where(q_pos[:, None] >= k_pos[None, :], sc, -jnp.inf)
        p = jax.nn.softmax(sc, axis=-1).astype(v.dtype)
        return jnp.einsum('bhqk,bkhd->bqhd', p, v)

    out = lax.map(attend, (qn_b, qr_b, jnp.arange(nq)))
    return out.swapaxes(0, 1).reshape(b, s, C_OUT)


def hybrid_mixer(x, w_in, conv_w, conv_b, m_gate_b, m_norm_g, sinks,
                 q_norm_g, w_uq, kv_norm_g, w_ukv, w_out, cos, sin):
    b, s, _ = x.shape
    proj = x @ w_in
    a_q, a_k, a_v, m_q, m_k, m_v, m_o, m_if, c_q, c_kv, c_kr = split_columns(proj)
    y_a = sliding_window_attention(a_q.reshape(b, s, SWA_HEADS, SWA_HEAD_DIM),
                                   a_k.reshape(b, s, SWA_KV_HEADS, SWA_HEAD_DIM),
                                   a_v.reshape(b, s, SWA_KV_HEADS, SWA_HEAD_DIM), sinks)
    qk = jax.nn.silu(causal_depthwise_conv(jnp.concatenate([m_q, m_k], -1), conv_w, conv_b))
    m_q, m_k = jnp.split(qk, 2, axis=-1)
    gates = m_if.astype(F32) + m_gate_b.astype(F32)
    h = mlstm(m_q.reshape(b, s, M_HEADS, M_QK_DIM), m_k.reshape(b, s, M_HEADS, M_QK_DIM),
              m_v.reshape(b, s, M_HEADS, M_V_DIM), gates[..., :M_HEADS], gates[..., M_HEADS:])
    h = jax.nn.sigmoid(m_o.astype(F32)).reshape(b, s, M_HEADS, M_V_DIM) * h
    y_b = head_layer_norm(h, m_norm_g).astype(x.dtype).reshape(b, s, B_V)
    y_c = mla(c_q, c_kv, c_kr, q_norm_g, w_uq, kv_norm_g, w_ukv, cos, sin)
    return jnp.concatenate([y_a, y_b, y_c], axis=-1) @ w_out


def moe_ffn(x, w_router, b_router, w_gate_up, b_gate_up, w_down, b_down):
    b, s, d = x.shape
    t = b * s
    xt = x.reshape(t, d)
    logits = (xt @ w_router).astype(F32) + b_router.astype(F32)
    top_logit, top_e = lax.top_k(logits, TOP_K)
    gate = jax.nn.softmax(top_logit, axis=-1)
    n_assign = t * TOP_K
    flat_e = top_e.reshape(-1)
    order = jnp.argsort(flat_e, stable=True)
    sorted_e = flat_e[order]
    sorted_tok = order // TOP_K
    sorted_gate = gate.reshape(-1)[order]
    counts = jnp.bincount(flat_e, length=N_EXPERTS)
    padded = (counts + MOE_BLOCK - 1) // MOE_BLOCK * MOE_BLOCK
    pad_end = jnp.cumsum(padded)
    pad_start = pad_end - padded
    grp_start = jnp.cumsum(counts) - counts
    dest = pad_start[sorted_e] + jnp.arange(n_assign) - grp_start[sorted_e]
    n_blocks = -(-n_assign // MOE_BLOCK) + N_EXPERTS
    rows = n_blocks * MOE_BLOCK
    row_tok = jnp.zeros((rows,), jnp.int32).at[dest].set(sorted_tok.astype(jnp.int32))
    block_e = jnp.minimum(jnp.searchsorted(pad_end, jnp.arange(n_blocks) * MOE_BLOCK, side='right'),
                          N_EXPERTS - 1)
    xb = xt[row_tok].reshape(n_blocks, MOE_BLOCK, d)

    def expert_block(args):
        xe, e = args
        hg = xe @ w_gate_up[e] + b_gate_up[e]
        g_, u_ = jnp.split(hg, 2, axis=-1)
        g_ = jnp.minimum(g_, SWIGLU_LIMIT)
        u_ = jnp.clip(u_, -SWIGLU_LIMIT, SWIGLU_LIMIT)
        act = (u_ + 1.0) * g_ * jax.nn.sigmoid(SWIGLU_ALPHA * g_)
        return act @ w_down[e] + b_down[e]

    yb = lax.map(expert_block, (xb, block_e)).reshape(rows, d)
    y_assign = yb[dest].astype(F32) * sorted_gate[:, None]
    out = jnp.zeros((t, d), F32).at[sorted_tok].add(y_assign)
    return out.astype(x.dtype).reshape(b, s, d)


def setup_inputs(seed: int = 0) -> dict:
    key = jax.random.key(seed)
    ks = jax.random.split(key, 24)
    L = DEPTH

    def normal(k, shape, scale):
        return jax.random.normal(k, shape, F32) * scale

    x = normal(ks[0], (BATCH, SEQ, D_MODEL), 1.0)
    w_in = normal(ks[1], (L, D_MODEL, N_IN), D_MODEL ** -0.5)
    conv_w = normal(ks[2], (L, M_CONV, 2 * B_QK), M_CONV ** -0.5)
    conv_b = normal(ks[3], (L, 2 * B_QK), 0.02)
    i_bias = -3.0 + normal(ks[4], (L, M_HEADS), 0.1)
    f_bias = jnp.linspace(3.0, 6.0, M_HEADS, dtype=F32)[None, :] + normal(ks[5], (L, M_HEADS), 0.1)
    m_gate_b = jnp.concatenate([i_bias, f_bias], axis=-1)
    m_norm_g = 1.0 + normal(ks[6], (L, B_V), 0.02)
    sinks = normal(ks[7], (L, SWA_HEADS), 0.5)
    q_norm_g = 1.0 + normal(ks[8], (L, C_Q_LORA), 0.02)
    w_uq = normal(ks[9], (L, C_Q_LORA, C_HEADS * (C_NOPE_DIM + C_ROPE_DIM)), C_Q_LORA ** -0.5)
    kv_norm_g = 1.0 + normal(ks[10], (L, C_KV_LORA), 0.02)
    w_ukv = normal(ks[11], (L, C_KV_LORA, C_HEADS * (C_NOPE_DIM + C_V_DIM)), C_KV_LORA ** -0.5)
    w_out = normal(ks[12], (L, D_MIX, D_MODEL), D_MIX ** -0.5 * DN_BETA)
    ln1_g = 1.0 + normal(ks[13], (L, D_MODEL), 0.02)
    ln1_b = normal(ks[14], (L, D_MODEL), 0.02)
    w_router = normal(ks[15], (L, D_MODEL, N_EXPERTS), D_MODEL ** -0.5)
    b_router = normal(ks[16], (L, N_EXPERTS), 0.01)
    w_gate_up = normal(ks[17], (L, N_EXPERTS, D_MODEL, 2 * D_EXPERT), D_MODEL ** -0.5)
    b_gate_up = normal(ks[18], (L, N_EXPERTS, 2 * D_EXPERT), 0.02)
    w_down = normal(ks[19], (L, N_EXPERTS, D_EXPERT, D_MODEL), D_EXPERT ** -0.5 * DN_BETA)
    b_down = normal(ks[20], (L, N_EXPERTS, D_MODEL), 0.02)
    ln2_g = 1.0 + normal(ks[21], (L, D_MODEL), 0.02)
    ln2_b = normal(ks[22], (L, D_MODEL), 0.02)
    return {"x": x, "w_in": w_in, "conv_w": conv_w, "conv_b": conv_b, "m_gate_b": m_gate_b,
            "m_norm_g": m_norm_g, "sinks": sinks, "q_norm_g": q_norm_g, "w_uq": w_uq,
            "kv_norm_g": kv_norm_g, "w_ukv": w_ukv, "w_out": w_out, "ln1_g": ln1_g, "ln1_b": ln1_b,
            "w_router": w_router, "b_router": b_router, "w_gate_up": w_gate_up, "b_gate_up": b_gate_up,
            "w_down": w_down, "b_down": b_down, "ln2_g": ln2_g, "ln2_b": ln2_b}


def reference(x, w_in, conv_w, conv_b, m_gate_b, m_norm_g, sinks, q_norm_g, w_uq, kv_norm_g, w_ukv,
              w_out, ln1_g, ln1_b, w_router, b_router, w_gate_up, b_gate_up, w_down, b_down, ln2_g, ln2_b):
    cos, sin = rope_tables(x.shape[1], C_ROPE_DIM)
    for l in range(DEPTH):
        mix = hybrid_mixer(x, w_in[l], conv_w[l], conv_b[l], m_gate_b[l], m_norm_g[l], sinks[l],
                           q_norm_g[l], w_uq[l], kv_norm_g[l], w_ukv[l], w_out[l], cos, sin)
        x = layer_norm(DN_ALPHA * x + mix, ln1_g[l], ln1_b[l])
        ffn = moe_ffn(x, w_router[l], b_router[l], w_gate_up[l], b_gate_up[l], w_down[l], b_down[l])
        x = layer_norm(DN_ALPHA * x + ffn, ln2_g[l], ln2_b[l])
    return x
```

```python
import functools

import jax
import jax.numpy as jnp
from jax import lax
from jax.experimental import pallas as pl
from jax.experimental.pallas import tpu as pltpu

F32 = jnp.float32
BF16 = jnp.bfloat16
NEG_INF = float("-inf")

D_MODEL = 2048
DEPTH = 2
SWA_HEADS, SWA_KV_HEADS, SWA_HEAD_DIM, SWA_WINDOW = 8, 2, 64, 128
M_HEADS, M_QK_DIM, M_V_DIM, M_CONV = 4, 64, 128, 4
C_HEADS, C_NOPE_DIM, C_ROPE_DIM, C_V_DIM = 8, 128, 64, 128
C_Q_LORA, C_KV_LORA = 512, 256
ROPE_THETA = 10000.0
N_EXPERTS, TOP_K = 32, 4
SWIGLU_LIMIT, SWIGLU_ALPHA = 7.0, 1.702
DN_ALPHA = (2 * DEPTH) ** 0.25
LN_EPS, RMS_EPS = 1e-5, 1e-6

LANES = 128
SUBLANES = 8
VMEM_LIMIT = 56 * 1024 * 1024

P_AQ, P_MV, P_MO, P_CQ, P_MQK, P_CKV, P_AK, P_AV, P_CKR, P_MIF, P_CKRS = (
    0, 512, 1024, 1536, 2048, 2560, 2816, 2944, 3072, 3200, 3328)
N_PROJ = 3456

PROJ_TM, PROJ_TN = 512, 1152
M_CHUNK = 128
MLA_TM = 512
FLASH_BQ = 512
OUT_TM = 256
MOE_PAD = 256
MOE_TMAX = 2048
MOE_TF = 256
CMB_TM = 64


def _cparams(sem, vmem=VMEM_LIMIT):
    return pltpu.CompilerParams(dimension_semantics=sem, vmem_limit_bytes=vmem)


def _proj_kernel(x_ref, w_ref, o_ref):
    o_ref[...] = jnp.dot(x_ref[...].astype(BF16), w_ref[...], preferred_element_type=F32)


def _proj(x2d, w_bf):
    t, d = x2d.shape
    n = w_bf.shape[1]
    return pl.pallas_call(
        _proj_kernel,
        grid=(t // PROJ_TM, n // PROJ_TN),
        in_specs=[pl.BlockSpec((PROJ_TM, d), lambda i, j: (i, 0)),
                  pl.BlockSpec((d, PROJ_TN), lambda i, j: (0, j))],
        out_specs=pl.BlockSpec((PROJ_TM, PROJ_TN), lambda i, j: (i, j)),
        out_shape=jax.ShapeDtypeStruct((t, n), F32),
        compiler_params=_cparams(("parallel", "arbitrary")),
        name="proj",
    )(x2d, w_bf)


def _swa_kernel(sinks_ref, q_ref, kc_ref, kp_ref, vc_ref, vp_ref, o_ref):
    i = pl.program_id(1)
    w = SWA_WINDOW
    dh = SWA_HEAD_DIM
    grp = SWA_HEADS // SWA_KV_HEADS
    q = q_ref[0]
    qi = lax.broadcasted_iota(jnp.int32, (w, 2 * w), 0)
    kj = lax.broadcasted_iota(jnp.int32, (w, 2 * w), 1)
    rel = qi + w - kj
    has_prev = jnp.where(i > 0, 0, w)
    mask = (rel >= 0) & (rel < w) & (kj >= has_prev)
    outs = []
    for kvh in range(SWA_KV_HEADS):
        sl = slice(kvh * dh, (kvh + 1) * dh)
        k_cat = jnp.concatenate([kp_ref[0][:, sl], kc_ref[0][:, sl]], axis=0).astype(BF16)
        v_cat = jnp.concatenate([vp_ref[0][:, sl], vc_ref[0][:, sl]], axis=0).astype(BF16)
        for g in range(grp):
            h = kvh * grp + g
            qh = (q[:, h * dh:(h + 1) * dh] * (dh ** -0.5)).astype(BF16)
            s = lax.dot_general(qh, k_cat, (((1,), (1,)), ((), ())), preferred_element_type=F32)
            s = jnp.where(mask, s, NEG_INF)
            sink = sinks_ref[h]
            m = jnp.maximum(jnp.max(s, axis=-1, keepdims=True), sink)
            p = jnp.exp(s - m)
            den = jnp.sum(p, axis=-1, keepdims=True) + jnp.exp(sink - m)
            o = jnp.dot(p.astype(BF16), v_cat, preferred_element_type=F32)
            outs.append(o / den)
    o_ref[0] = jnp.concatenate(outs, axis=-1).astype(o_ref.dtype)


def _swa(proj3, sinks):
    b, s, _ = proj3.shape
    w = SWA_WINDOW
    nb = s // w
    kw = SWA_KV_HEADS * SWA_HEAD_DIM
    qw = SWA_HEADS * SWA_HEAD_DIM
    cur = lambda col: (lambda bi, i: (bi, i, col))
    prev = lambda col: (lambda bi, i: (bi, jnp.maximum(i - 1, 0), col))
    return pl.pallas_call(
        _swa_kernel,
        grid=(b, nb),
        in_specs=[pl.BlockSpec(memory_space=pltpu.SMEM),
                  pl.BlockSpec((1, w, qw), cur(P_AQ // qw)),
                  pl.BlockSpec((1, w, kw), cur(P_AK // kw)),
                  pl.BlockSpec((1, w, kw), prev(P_AK // kw)),
                  pl.BlockSpec((1, w, kw), cur(P_AV // kw)),
                  pl.BlockSpec((1, w, kw), prev(P_AV // kw))],
        out_specs=pl.BlockSpec((1, w, qw), lambda bi, i: (bi, i, 0)),
        out_shape=jax.ShapeDtypeStruct((b, s, qw), BF16),
        compiler_params=_cparams(("parallel", "arbitrary")),
        name="swa",
    )(sinks, proj3, proj3, proj3, proj3, proj3)


def _mlstm_kernel(qk_ref, v_ref, og_ref, g_ref, cw_ref, cb_ref, gb_ref, ng_ref, y_ref, xbuf, ct_ref, m_ref):
    c = pl.program_id(0)
    nb = qk_ref.shape[0]
    ln = M_CHUNK
    dqk, dv, nh = M_QK_DIM, M_V_DIM, M_HEADS
    halo = SUBLANES

    @pl.when(c == 0)
    def _init():
        xbuf[:, 0:halo, :] = jnp.zeros((nb, halo, xbuf.shape[2]), F32)
        ct_ref[...] = jnp.zeros(ct_ref.shape, F32)
        m_ref[...] = jnp.zeros(m_ref.shape, F32)

    row = lax.broadcasted_iota(jnp.int32, (ln, ln), 0)
    col = lax.broadcasted_iota(jnp.int32, (ln, ln), 1)
    tril = row >= col
    trilf = jnp.where(tril, 1.0, 0.0).astype(F32)
    ones_blk = jnp.where(lax.broadcasted_iota(jnp.int32, (ln, dv), 1) == 0, 1.0, 0.0).astype(F32)

    for b in range(nb):
        xbuf[b, halo:halo + ln, :] = qk_ref[b]
        conv = cb_ref[...]
        for j in range(M_CONV):
            off = halo - (M_CONV - 1) + j
            conv = conv + cw_ref[j:j + 1, :] * xbuf[b, off:off + ln, :]
        qk = conv * jax.nn.sigmoid(conv)
        xbuf[b, 0:halo, :] = qk_ref[b, ln - halo:ln, :]

        g = g_ref[b] + gb_ref[...]
        lf = jnp.minimum(g, 0.0) - jnp.log1p(jnp.exp(-jnp.abs(g)))
        cum = jnp.dot(trilf, lf, preferred_element_type=F32, precision=lax.Precision.HIGHEST)
        cum_t = cum.T
        g_t = g.T
        vv = v_ref[b]
        hs = []
        for h in range(nh):
            chain = b * nh + h
            q = (qk[:, h * dqk:(h + 1) * dqk] * (dqk ** -0.5)).astype(BF16)
            k = qk[:, nh * dqk + h * dqk: nh * dqk + (h + 1) * dqk].astype(BF16)
            v_ext = jnp.concatenate([vv[:, h * dv:(h + 1) * dv], ones_blk], axis=-1)
            bc_col = cum[:, nh + h:nh + h + 1]
            bc_row = cum_t[nh + h:nh + h + 1, :]
            i_col = g[:, h:h + 1]
            i_row = g_t[h:h + 1, :]
            m_prev = m_ref[chain:chain + 1, 0:1]
            dmat = jnp.where(tril, bc_col - bc_row + i_row, NEG_INF)
            m_inter = bc_col + m_prev
            m_j = jnp.maximum(m_inter, jnp.max(dmat, axis=-1, keepdims=True))
            w_intra = jnp.exp(dmat - m_j)
            w_inter = jnp.exp(m_inter - m_j)
            s = lax.dot_general(q, k, (((1,), (1,)), ((), ())), preferred_element_type=F32) * w_intra
            ct = ct_ref[chain]
            num_ext = (jnp.dot(s.astype(BF16), v_ext.astype(BF16), preferred_element_type=F32)
                       + w_inter * jnp.dot(q, ct.astype(BF16), preferred_element_type=F32))
            num = num_ext[:, :dv]
            nq = num_ext[:, dv:dv + 1]
            den = jnp.maximum(jnp.abs(nq), jnp.exp(-m_j))
            hs.append(num / den)
            m_new = m_j[ln - 1:ln, :]
            bc_last = bc_col[ln - 1:ln, :]
            w_s = jnp.exp(bc_last - bc_col + i_col - m_new)
            w_c = jnp.exp(bc_last + m_prev - m_new)
            upd = lax.dot_general(k, (w_s * v_ext).astype(BF16), (((0,), (0,)), ((), ())),
                                  preferred_element_type=F32)
            ct_ref[chain] = w_c * ct + upd
            m_ref[chain:chain + 1, :] = jnp.broadcast_to(m_new, (1, m_ref.shape[1]))
        og = og_ref[b]
        outs = []
        for h in range(nh):
            seg = jax.nn.sigmoid(og[:, h * dv:(h + 1) * dv]) * hs[h]
            mu = jnp.mean(seg, axis=-1, keepdims=True)
            cen = seg - mu
            var = jnp.mean(cen * cen, axis=-1, keepdims=True)
            outs.append(cen * lax.rsqrt(var + LN_EPS) * ng_ref[:, h * dv:(h + 1) * dv])
        y_ref[b] = jnp.concatenate(outs, axis=-1).astype(y_ref.dtype)


def _mlstm(proj3, conv_w, conv_b, gate_b, norm_g):
    b, s, _ = proj3.shape
    ln = M_CHUNK
    wq = 2 * M_HEADS * M_QK_DIM
    wv = M_HEADS * M_V_DIM
    blk = lambda width, off: pl.BlockSpec((b, ln, width), lambda c: (0, c, off // width))
    full = lambda a: pl.BlockSpec(a.shape, lambda c: (0,) * a.ndim)
    gate_b128 = jnp.zeros((1, LANES), F32).at[0, :2 * M_HEADS].set(gate_b)
    conv_b2 = conv_b.reshape(1, wq)
    norm_g2 = norm_g.reshape(1, wv)
    return pl.pallas_call(
        _mlstm_kernel,
        grid=(s // ln,),
        in_specs=[blk(wq, P_MQK), blk(wv, P_MV), blk(wv, P_MO), blk(LANES, P_MIF),
                  full(conv_w), full(conv_b2), full(gate_b128), full(norm_g2)],
        out_specs=pl.BlockSpec((b, ln, wv), lambda c: (0, c, 0)),
        out_shape=jax.ShapeDtypeStruct((b, s, wv), BF16),
        scratch_shapes=[pltpu.VMEM((b, SUBLANES + ln, wq), F32),
                        pltpu.VMEM((b * M_HEADS, M_QK_DIM, 2 * M_V_DIM), F32),
                        pltpu.VMEM((b * M_HEADS, LANES), F32)],
        compiler_params=_cparams(("arbitrary",)),
        name="mlstm",
    )(proj3, proj3, proj3, proj3, conv_w, conv_b2, gate_b128, norm_g2)


def _mla_prep_kernel(cq_ref, ckv_ref, kr_ref, krs_ref, qg_ref, kvg_ref, wqa_ref, wqb_ref, wkv_ref,
                     cos_ref, sin_ref, q_out, k_out, v_out):
    nd, hd = C_NOPE_DIM, C_HEADS
    scale = (C_NOPE_DIM + C_ROPE_DIM) ** -0.5
    cos = cos_ref[...]
    sin = sin_ref[...]

    cq = cq_ref[0]
    qn = (cq * lax.rsqrt(jnp.mean(cq * cq, axis=-1, keepdims=True) + RMS_EPS) * qg_ref[...]).astype(BF16)
    qa = jnp.dot(qn, wqa_ref[...], preferred_element_type=F32)
    qb = jnp.dot(qn, wqb_ref[...], preferred_element_type=F32)
    for h in range(hd):
        nope = qa[:, h * 2 * nd: h * 2 * nd + nd]
        rope = qa[:, h * 2 * nd + nd:(h + 1) * 2 * nd] * cos + qb[:, h * nd:(h + 1) * nd] * sin
        q_out[0, h] = (jnp.concatenate([nope, rope], axis=-1) * scale).astype(q_out.dtype)

    ckv = ckv_ref[0]
    kvn = (ckv * lax.rsqrt(jnp.mean(ckv * ckv, axis=-1, keepdims=True) + RMS_EPS) * kvg_ref[...]).astype(BF16)
    kv = jnp.dot(kvn, wkv_ref[...], preferred_element_type=F32)
    kr = kr_ref[0] * cos + krs_ref[0] * sin
    for h in range(hd):
        k_out[0, h] = jnp.concatenate([kv[:, h * nd:(h + 1) * nd], kr], axis=-1).astype(k_out.dtype)
        v_out[0, h] = kv[:, hd * nd + h * nd: hd * nd + (h + 1) * nd].astype(v_out.dtype)


def _mla_prep(proj3, q_norm_g, kv_norm_g, wqa, wqb, wkv, cos128, sin128):
    b, s, _ = proj3.shape
    tm = MLA_TM
    hd, nd = C_HEADS, C_NOPE_DIM
    blk = lambda width, off: pl.BlockSpec((1, tm, width), lambda bi, i: (bi, i, off // width))
    full = lambda a: pl.BlockSpec(a.shape, lambda bi, i: (0,) * a.ndim)
    tab = pl.BlockSpec((tm, LANES), lambda bi, i: (i, 0))
    qg = q_norm_g.reshape(1, -1)
    kvg = kv_norm_g.reshape(1, -1)
    head_out = lambda width: pl.BlockSpec((1, hd, tm, width), lambda bi, i: (bi, 0, i, 0))
    return pl.pallas_call(
        _mla_prep_kernel,
        grid=(b, s // tm),
        in_specs=[blk(C_Q_LORA, P_CQ), blk(C_KV_LORA, P_CKV), blk(LANES, P_CKR), blk(LANES, P_CKRS),
                  full(qg), full(kvg), full(wqa), full(wqb), full(wkv), tab, tab],
        out_specs=[head_out(2 * nd), head_out(2 * nd), head_out(C_V_DIM)],
        out_shape=[jax.ShapeDtypeStruct((b, hd, s, 2 * nd), BF16),
                   jax.ShapeDtypeStruct((b, hd, s, 2 * nd), BF16),
                   jax.ShapeDtypeStruct((b, hd, s, C_V_DIM), BF16)],
        compiler_params=_cparams(("parallel", "arbitrary")),
        name="mla_prep",
    )(proj3, proj3, proj3, proj3, qg, kvg, wqa, wqb, wkv, cos128, sin128)


def _flash_kernel(q_ref, k_ref, v_ref, o_ref):
    qi = pl.program_id(2)
    bq = FLASH_BQ
    q = q_ref[0, 0]
    dv = v_ref.shape[3]

    def step(j, carry, diag):
        m, l, acc = carry
        start = pl.multiple_of(j * bq, bq)
        ks = k_ref[0, 0, pl.ds(start, bq), :]
        vs = v_ref[0, 0, pl.ds(start, bq), :]
        s = lax.dot_general(q, ks, (((1,), (1,)), ((), ())), preferred_element_type=F32)
        if diag:
            row = lax.broadcasted_iota(jnp.int32, (bq, bq), 0)
            col = lax.broadcasted_iota(jnp.int32, (bq, bq), 1)
            s = jnp.where(row >= col, s, NEG_INF)
        m_new = jnp.maximum(m, jnp.max(s, axis=-1, keepdims=True))
        alpha = jnp.exp(m - m_new)
        p = jnp.exp(s - m_new)
        l = alpha * l + jnp.sum(p, axis=-1, keepdims=True)
        acc = alpha * acc + jnp.dot(p.astype(BF16), vs, preferred_element_type=F32)
        return m_new, l, acc

    init = (jnp.full((bq, 1), NEG_INF, F32), jnp.zeros((bq, 1), F32), jnp.zeros((bq, dv), F32))
    carry = lax.fori_loop(0, qi, lambda j, cr: step(j, cr, False), init)
    _, l, acc = step(qi, carry, True)
    o_ref[0] = (acc / l).astype(o_ref.dtype)


def _flash(qc, kc, vc):
    b, hd, s, dk = qc.shape
    dv = vc.shape[3]
    bq = FLASH_BQ
    return pl.pallas_call(
        _flash_kernel,
        grid=(b, hd, s // bq),
        in_specs=[pl.BlockSpec((1, 1, bq, dk), lambda bi, h, i: (bi, h, i, 0)),
                  pl.BlockSpec((1, 1, s, dk), lambda bi, h, i: (bi, h, 0, 0)),
                  pl.BlockSpec((1, 1, s, dv), lambda bi, h, i: (bi, h, 0, 0))],
        out_specs=pl.BlockSpec((1, bq, dv), lambda bi, h, i: (bi, i, h)),
        out_shape=jax.ShapeDtypeStruct((b, s, hd * dv), BF16),
        compiler_params=_cparams(("parallel", "parallel", "arbitrary")),
        name="flash",
    )(qc, kc, vc)


def _layer_norm(z, g, b):
    mu = jnp.mean(z, axis=-1, keepdims=True)
    cen = z - mu
    var = jnp.mean(cen * cen, axis=-1, keepdims=True)
    return cen * lax.rsqrt(var + LN_EPS) * g + b


def _outproj_kernel(ya_ref, yb_ref, yc_ref, x_ref, w_ref, g_ref, b_ref, wr_ref, br_ref,
                    x1_ref, e_ref, gate_ref):
    wa, wb = ya_ref.shape[1], yb_ref.shape[1]
    mix = jnp.dot(ya_ref[...], w_ref[0:wa, :], preferred_element_type=F32)
    mix = mix + jnp.dot(yb_ref[...], w_ref[wa:wa + wb, :], preferred_element_type=F32)
    mix = mix + jnp.dot(yc_ref[...], w_ref[wa + wb:, :], preferred_element_type=F32)
    x1 = _layer_norm(DN_ALPHA * x_ref[...] + mix, g_ref[...], b_ref[...])
    x1_ref[...] = x1

    logits = jnp.dot(x1, wr_ref[...], preferred_element_type=F32, precision=lax.Precision.HIGHEST) + br_ref[...]
    lane = lax.broadcasted_iota(jnp.int32, logits.shape, 1)
    logits = jnp.where(lane < N_EXPERTS, logits, NEG_INF)
    e_out = jnp.zeros(logits.shape, jnp.int32)
    p_out = jnp.zeros(logits.shape, F32)
    top = None
    den = None
    for r in range(TOP_K):
        mx = jnp.max(logits, axis=-1, keepdims=True)
        idx = jnp.min(jnp.where(logits == mx, lane, LANES), axis=-1, keepdims=True)
        if r == 0:
            top = mx
        p = jnp.exp(mx - top)
        den = p if r == 0 else den + p
        e_out = jnp.where(lane == r, idx, e_out)
        p_out = jnp.where(lane == r, p, p_out)
        logits = jnp.where(lane == idx, NEG_INF, logits)
    e_ref[...] = e_out
    gate_ref[...] = p_out / den


def _outproj(ya, yb, yc, x2d, w_bf, g, b, wr128, br128):
    t, d = x2d.shape
    tm = OUT_TM
    rows = lambda a: pl.BlockSpec((tm, a.shape[1]), lambda i: (i, 0))
    full = lambda a: pl.BlockSpec(a.shape, lambda i: (0,) * a.ndim)
    g2, b2 = g.reshape(1, d), b.reshape(1, d)
    return pl.pallas_call(
        _outproj_kernel,
        grid=(t // tm,),
        in_specs=[rows(ya), rows(yb), rows(yc), rows(x2d), full(w_bf), full(g2), full(b2), full(wr128), full(br128)],
        out_specs=[pl.BlockSpec((tm, d), lambda i: (i, 0)),
                   pl.BlockSpec((tm, LANES), lambda i: (i, 0)),
                   pl.BlockSpec((tm, LANES), lambda i: (i, 0))],
        out_shape=[jax.ShapeDtypeStruct((t, d), F32),
                   jax.ShapeDtypeStruct((t, LANES), jnp.int32),
                   jax.ShapeDtypeStruct((t, LANES), F32)],
        compiler_params=_cparams(("parallel",)),
        name="outproj",
    )(ya, yb, yc, x2d, w_bf, g2, b2, wr128, br128)


def _row_copy(src_hbm, row, dst, slot, sem):
    return pltpu.make_async_copy(src_hbm.at[pl.ds(row, 1)], dst.at[pl.ds(slot, 1)], sem)


def _gather_kernel(tok_ref, x_hbm, o_ref, buf, sem):
    base = pl.program_id(0) * MOE_PAD

    def start(r, _):
        _row_copy(x_hbm, tok_ref[base + r], buf, r, sem).start()
        return 0

    def wait(r, _):
        _row_copy(x_hbm, 0, buf, r, sem).wait()
        return 0

    lax.fori_loop(0, MOE_PAD, start, 0)
    lax.fori_loop(0, MOE_PAD, wait, 0)
    o_ref[...] = buf[...].astype(o_ref.dtype)


def _gather(row_tok, x1):
    rows = row_tok.shape[0]
    d = x1.shape[1]
    return pl.pallas_call(
        _gather_kernel,
        grid_spec=pltpu.PrefetchScalarGridSpec(
            num_scalar_prefetch=1,
            grid=(rows // MOE_PAD,),
            in_specs=[pl.BlockSpec(memory_space=pl.ANY)],
            out_specs=pl.BlockSpec((MOE_PAD, d), lambda i, tok: (i, 0)),
            scratch_shapes=[pltpu.VMEM((MOE_PAD, d), F32), pltpu.SemaphoreType.DMA(())]),
        out_shape=jax.ShapeDtypeStruct((rows, d), BF16),
        compiler_params=_cparams(("arbitrary",)),
        name="gather",
    )(row_tok, x1)


def _moe_kernel(sbe_ref, sbr_ref, sbn_ref, xs_hbm, wg_ref, wu_ref, bg_ref, bu_ref, wd_ref, bd_ref, y_hbm,
                xbuf, acc, wg_bf, wu_bf, wd_bf, sem_in, sem_out):
    sb = pl.program_id(0)
    j = pl.program_id(1)
    nj = pl.num_programs(1)
    n = sbn_ref[sb]
    row0 = sbr_ref[sb]
    nchunk = n // MOE_PAD

    def chunk_in(c):
        return pltpu.make_async_copy(xs_hbm.at[pl.ds(pl.multiple_of(row0 + c * MOE_PAD, MOE_PAD), MOE_PAD)],
                                     xbuf.at[pl.ds(pl.multiple_of(c * MOE_PAD, MOE_PAD), MOE_PAD)], sem_in)

    def chunk_out(c):
        return pltpu.make_async_copy(acc.at[pl.ds(pl.multiple_of(c * MOE_PAD, MOE_PAD), MOE_PAD)],
                                     y_hbm.at[pl.ds(pl.multiple_of(row0 + c * MOE_PAD, MOE_PAD), MOE_PAD)], sem_out)

    def for_chunks(fn):
        def body(c, _):
            fn(c)
            return 0
        lax.fori_loop(0, nchunk, body, 0)

    @pl.when(j == 0)
    def _load_rows():
        for_chunks(lambda c: chunk_in(c).start())
        for_chunks(lambda c: chunk_in(c).wait())

    @pl.when(n > 0)
    def _cast_weights():
        wg_bf[...] = wg_ref[...].astype(BF16)
        wu_bf[...] = wu_ref[...].astype(BF16)
        wd_bf[...] = wd_ref[...].astype(BF16)

    def ffn_chunk(c, first):
        rs = pl.ds(pl.multiple_of(c * MOE_PAD, MOE_PAD), MOE_PAD)
        xr = xbuf[rs, :]
        gt = jnp.dot(xr, wg_bf[...], preferred_element_type=F32) + bg_ref[...]
        up = jnp.dot(xr, wu_bf[...], preferred_element_type=F32) + bu_ref[...]
        gt = jnp.minimum(gt, SWIGLU_LIMIT)
        up = jnp.clip(up, -SWIGLU_LIMIT, SWIGLU_LIMIT)
        act = (up + 1.0) * gt * jax.nn.sigmoid(SWIGLU_ALPHA * gt)
        contrib = jnp.dot(act.astype(BF16), wd_bf[...], preferred_element_type=F32)
        if first:
            acc[rs, :] = contrib + bd_ref[...]
        else:
            acc[rs, :] = acc[rs, :] + contrib

    @pl.when(j == 0)
    def _first_tile():
        for_chunks(lambda c: ffn_chunk(c, True))

    @pl.when(j > 0)
    def _next_tiles():
        for_chunks(lambda c: ffn_chunk(c, False))

    @pl.when(j == nj - 1)
    def _store_rows():
        for_chunks(lambda c: chunk_out(c).start())
        for_chunks(lambda c: chunk_out(c).wait())


def _moe(layer, sb_e, sb_row0, sb_n, xs, w_gate_up, b_gate_up, w_down, b_down):
    rows, d = xs.shape
    dff = w_down.shape[2]
    tf = MOE_TF
    nj = dff // tf
    nsb = sb_e.shape[0]
    bgu = b_gate_up.reshape(DEPTH, N_EXPERTS, 1, 2 * dff)
    bdn = b_down.reshape(DEPTH, N_EXPERTS, 1, d)

    def jeff(sb, j, n_ref):
        return jnp.where(n_ref[sb] > 0, j, nj - 1)

    gate_map = lambda sb, j, e, r, n: (layer, e[sb], 0, jeff(sb, j, n))
    up_map = lambda sb, j, e, r, n: (layer, e[sb], 0, nj + jeff(sb, j, n))
    down_map = lambda sb, j, e, r, n: (layer, e[sb], jeff(sb, j, n), 0)
    bd_map = lambda sb, j, e, r, n: (layer, e[sb], 0, 0)
    return pl.pallas_call(
        _moe_kernel,
        grid_spec=pltpu.PrefetchScalarGridSpec(
            num_scalar_prefetch=3,
            grid=(nsb, nj),
            in_specs=[pl.BlockSpec(memory_space=pl.ANY),
                      pl.BlockSpec((None, None, d, tf), gate_map),
                      pl.BlockSpec((None, None, d, tf), up_map),
                      pl.BlockSpec((None, None, 1, tf), gate_map),
                      pl.BlockSpec((None, None, 1, tf), up_map),
                      pl.BlockSpec((None, None, tf, d), down_map),
                      pl.BlockSpec((None, None, 1, d), bd_map)],
            out_specs=pl.BlockSpec(memory_space=pl.ANY),
            scratch_shapes=[pltpu.VMEM((MOE_TMAX, d), BF16),
                            pltpu.VMEM((MOE_TMAX, d), F32),
                            pltpu.VMEM((d, tf), BF16),
                            pltpu.VMEM((d, tf), BF16),
                            pltpu.VMEM((tf, d), BF16),
                            pltpu.SemaphoreType.DMA(()),
                            pltpu.SemaphoreType.DMA(())]),
        out_shape=jax.ShapeDtypeStruct((rows, d), F32),
        compiler_params=_cparams(("arbitrary", "arbitrary")),
        name="moe",
    )(sb_e, sb_row0, sb_n, xs, w_gate_up, w_gate_up, bgu, bgu, w_down, bdn)


def _combine_kernel(pos_ref, y_hbm, gate_ref, x1_ref, g_ref, b_ref, o_ref, buf, sem):
    tm = CMB_TM
    base = pl.program_id(0) * tm * TOP_K

    def start(a, _):
        t = a // TOP_K
        k = a % TOP_K
        _row_copy(y_hbm, pos_ref[base + a], buf.at[k], t, sem).start()
        return 0

    def wait(a, _):
        _row_copy(y_hbm, 0, buf.at[0], 0, sem).wait()
        return 0

    lax.fori_loop(0, tm * TOP_K, start, 0)
    lax.fori_loop(0, tm * TOP_K, wait, 0)
    gate = gate_ref[...]
    ffn = gate[:, 0:1] * buf[0]
    for k in range(1, TOP_K):
        ffn = ffn + gate[:, k:k + 1] * buf[k]
    o_ref[...] = _layer_norm(DN_ALPHA * x1_ref[...] + ffn, g_ref[...], b_ref[...])


def _combine(pos_flat, y, gate, x1, g, b):
    t, d = x1.shape
    tm = CMB_TM
    g2, b2 = g.reshape(1, d), b.reshape(1, d)
    return pl.pallas_call(
        _combine_kernel,
        grid_spec=pltpu.PrefetchScalarGridSpec(
            num_scalar_prefetch=1,
            grid=(t // tm,),
            in_specs=[pl.BlockSpec(memory_space=pl.ANY),
                      pl.BlockSpec((tm, LANES), lambda i, p: (i, 0)),
                      pl.BlockSpec((tm, d), lambda i, p: (i, 0)),
                      pl.BlockSpec((1, d), lambda i, p: (0, 0)),
                      pl.BlockSpec((1, d), lambda i, p: (0, 0))],
            out_specs=pl.BlockSpec((tm, d), lambda i, p: (i, 0)),
            scratch_shapes=[pltpu.VMEM((TOP_K, tm, d), F32), pltpu.SemaphoreType.DMA(())]),
        out_shape=jax.ShapeDtypeStruct((t, d), F32),
        compiler_params=_cparams(("arbitrary",)),
        name="combine",
    )(pos_flat, y, gate, x1, g2, b2)


def _swap_halves(w):
    half = w.shape[-1] // 2
    return jnp.concatenate([w[..., half:], w[..., :half]], axis=-1)


def _layout_w_in(w):
    d = w.shape[0]
    widths = (512, 128, 128, 256, 256, 512, 512, 8, 512, 256, 64)
    offs = [0]
    for wd in widths:
        offs.append(offs[-1] + wd)
    a_q, a_k, a_v, m_q, m_k, m_v, m_o, m_if, c_q, c_kv, c_kr = [w[:, offs[i]:offs[i + 1]] for i in range(len(widths))]
    z = lambda n: jnp.zeros((d, n), w.dtype)
    out = jnp.concatenate([a_q, m_v, m_o, c_q, m_q, m_k, c_kv, a_k, a_v,
                           c_kr, z(LANES - 64), m_if, z(LANES - 8), _swap_halves(c_kr), z(LANES - 64)], axis=1)
    return out.astype(BF16)


def _layout_w_uq(w):
    r = w.shape[0]
    w3 = w.reshape(r, C_HEADS, C_NOPE_DIM + C_ROPE_DIM)
    nope, rope = w3[..., :C_NOPE_DIM], w3[..., C_NOPE_DIM:]
    z = jnp.zeros((r, C_HEADS, LANES - C_ROPE_DIM), w.dtype)
    wa = jnp.concatenate([nope, rope, z], axis=-1).reshape(r, -1)
    wb = jnp.concatenate([_swap_halves(rope), z], axis=-1).reshape(r, -1)
    return wa.astype(BF16), wb.astype(BF16)


def _layout_w_ukv(w):
    r = w.shape[0]
    w3 = w.reshape(r, C_HEADS, C_NOPE_DIM + C_V_DIM)
    return jnp.concatenate([w3[..., :C_NOPE_DIM].reshape(r, -1), w3[..., C_NOPE_DIM:].reshape(r, -1)],
                           axis=-1).astype(BF16)


def _rope_tables(seq):
    dim = C_ROPE_DIM
    inv = 1.0 / (ROPE_THETA ** (jnp.arange(0, dim, 2, dtype=F32) / dim))
    ang = jnp.arange(seq, dtype=F32)[:, None] * inv[None, :]
    cos, sin = jnp.cos(ang), jnp.sin(ang)
    z = jnp.zeros((seq, LANES - dim), F32)
    return jnp.concatenate([cos, cos, z], axis=-1), jnp.concatenate([-sin, sin, z], axis=-1)


def _route(top_e, rows, nsb):
    t = top_e.shape[0]
    n_assign = t * TOP_K
    flat_e = top_e.reshape(-1)
    order = jnp.argsort(flat_e, stable=True).astype(jnp.int32)
    sorted_e = flat_e[order]
    counts = jnp.bincount(flat_e, length=N_EXPERTS).astype(jnp.int32)
    padded = (counts + MOE_PAD - 1) // MOE_PAD * MOE_PAD
    pad_end = jnp.cumsum(padded)
    pad_start = pad_end - padded
    grp_start = jnp.cumsum(counts) - counts
    dest = pad_start[sorted_e] + jnp.arange(n_assign, dtype=jnp.int32) - grp_start[sorted_e]
    row_tok = jnp.zeros((rows,), jnp.int32).at[dest].set(order // TOP_K)
    pos = jnp.zeros((n_assign,), jnp.int32).at[order].set(dest)
    nsb_e = (padded + MOE_TMAX - 1) // MOE_TMAX
    sb_end = jnp.cumsum(nsb_e)
    sb_start = sb_end - nsb_e
    sb_idx = jnp.arange(nsb, dtype=jnp.int32)
    n_valid = sb_end[-1]
    sb_eff = jnp.minimum(sb_idx, n_valid - 1)
    sb_e = jnp.searchsorted(sb_end, sb_eff, side="right").astype(jnp.int32)
    part = sb_eff - sb_start[sb_e]
    sb_row0 = pad_start[sb_e] + part * MOE_TMAX
    sb_n = jnp.where(sb_idx < n_valid, jnp.minimum(padded[sb_e] - part * MOE_TMAX, MOE_TMAX), 0)
    return row_tok, pos, sb_e, sb_row0.astype(jnp.int32), sb_n.astype(jnp.int32)


def kernel(x, w_in, conv_w, conv_b, m_gate_b, m_norm_g, sinks, q_norm_g, w_uq, kv_norm_g, w_ukv, w_out,
           ln1_g, ln1_b, w_router, b_router, w_gate_up, b_gate_up, w_down, b_down, ln2_g, ln2_b):
    b, s, d = x.shape
    t = b * s
    n_assign = t * TOP_K
    rows = (n_assign // MOE_PAD + N_EXPERTS) * MOE_PAD
    nsb = N_EXPERTS + rows // MOE_TMAX
    cos128, sin128 = _rope_tables(s)
    xt = x.reshape(t, d)
    for l in range(DEPTH):
        w_in_bf = _layout_w_in(w_in[l])
        wqa, wqb = _layout_w_uq(w_uq[l])
        wkv = _layout_w_ukv(w_ukv[l])
        w_out_bf = w_out[l].astype(BF16)
        wr128 = jnp.zeros((d, LANES), F32).at[:, :N_EXPERTS].set(w_router[l])
        br128 = jnp.zeros((1, LANES), F32).at[0, :N_EXPERTS].set(b_router[l])

        proj3 = _proj(xt, w_in_bf).reshape(b, s, N_PROJ)
        y_a = _swa(proj3, sinks[l])
        y_b = _mlstm(proj3, conv_w[l], conv_b[l], m_gate_b[l], m_norm_g[l])
        qc, kc, vc = _mla_prep(proj3, q_norm_g[l], kv_norm_g[l], wqa, wqb, wkv, cos128, sin128)
        y_c = _flash(qc, kc, vc)
        x1, top_e, gate = _outproj(y_a.reshape(t, -1), y_b.reshape(t, -1), y_c.reshape(t, -1), xt, w_out_bf,
                                   ln1_g[l], ln1_b[l], wr128, br128)
        row_tok, pos, sb_e, sb_row0, sb_n = _route(top_e[:, :TOP_K], rows, nsb)
        xs = _gather(row_tok, x1)
        y = _moe(l, sb_e, sb_row0, sb_n, xs, w_gate_up, b_gate_up, w_down, b_down)
        xt = _combine(pos, y, gate, x1, ln2_g[l], ln2_b[l])
    return xt.reshape(b, s, d)
```

```python
import functools

import jax
import jax.numpy as jnp
from jax import lax
from jax.experimental import pallas as pl
from jax.experimental.pallas import tpu as pltpu

F32 = jnp.float32
BF16 = jnp.bfloat16
NEG_INF = float("-inf")

D_MODEL = 2048
DEPTH = 2
SWA_HEADS, SWA_KV_HEADS, SWA_HEAD_DIM, SWA_WINDOW = 8, 2, 64, 128
M_HEADS, M_QK_DIM, M_V_DIM, M_CONV = 4, 64, 128, 4
C_HEADS, C_NOPE_DIM, C_ROPE_DIM, C_V_DIM = 8, 128, 64, 128
C_Q_LORA, C_KV_LORA = 512, 256
ROPE_THETA = 10000.0
N_EXPERTS, TOP_K = 32, 4
SWIGLU_LIMIT, SWIGLU_ALPHA = 7.0, 1.702
DN_ALPHA = (2 * DEPTH) ** 0.25
LN_EPS, RMS_EPS = 1e-5, 1e-6

LANES = 128
SUBLANES = 8
VMEM_LIMIT = 56 * 1024 * 1024

P_AQ, P_MV, P_MO, P_CQ, P_MQK, P_CKV, P_AK, P_AV, P_CKR, P_MIF, P_CKRS = (
    0, 512, 1024, 1536, 2048, 2560, 2816, 2944, 3072, 3200, 3328)
N_PROJ = 3456

PROJ_TM, PROJ_TN = 512, 1152
M_CHUNK = 128
MLA_TM = 512
FLASH_BQ = 512
OUT_TM = 256
MOE_PAD = 256
MOE_TMAX = 2048
MOE_TF = 256
CMB_TM = 64
DSP_TM = 64

ROW_TILES = D_MODEL // LANES


def _cparams(sem, vmem=VMEM_LIMIT):
    return pltpu.CompilerParams(dimension_semantics=sem, vmem_limit_bytes=vmem)


def _store_rc(ref, val):
    n = val.shape[0]
    for s in range(ROW_TILES):
        ref[pl.ds(s, n, stride=ROW_TILES), :] = val[:, s * LANES:(s + 1) * LANES]


def _load_rc(ref, n):
    return jnp.concatenate([ref[pl.ds(s, n, stride=ROW_TILES), :] for s in range(ROW_TILES)], axis=-1)


def _proj_kernel(x_ref, w_ref, o_ref):
    o_ref[...] = jnp.dot(x_ref[...].astype(BF16), w_ref[...], preferred_element_type=F32)


def _proj(x2d, w_bf):
    t, d = x2d.shape
    n = w_bf.shape[1]
    return pl.pallas_call(
        _proj_kernel,
        grid=(t // PROJ_TM, n // PROJ_TN),
        in_specs=[pl.BlockSpec((PROJ_TM, d), lambda i, j: (i, 0)),
                  pl.BlockSpec((d, PROJ_TN), lambda i, j: (0, j))],
        out_specs=pl.BlockSpec((PROJ_TM, PROJ_TN), lambda i, j: (i, j)),
        out_shape=jax.ShapeDtypeStruct((t, n), F32),
        compiler_params=_cparams(("parallel", "arbitrary")),
        name="proj",
    )(x2d, w_bf)


def _swa_kernel(sinks_ref, q_ref, kc_ref, kp_ref, vc_ref, vp_ref, o_ref):
    i = pl.program_id(1)
    w = SWA_WINDOW
    dh = SWA_HEAD_DIM
    grp = SWA_HEADS // SWA_KV_HEADS
    q = q_ref[0]
    qi = lax.broadcasted_iota(jnp.int32, (w, 2 * w), 0)
    kj = lax.broadcasted_iota(jnp.int32, (w, 2 * w), 1)
    rel = qi + w - kj
    has_prev = jnp.where(i > 0, 0, w)
    mask = (rel >= 0) & (rel < w) & (kj >= has_prev)
    outs = []
    for kvh in range(SWA_KV_HEADS):
        sl = slice(kvh * dh, (kvh + 1) * dh)
        k_cat = jnp.concatenate([kp_ref[0][:, sl], kc_ref[0][:, sl]], axis=0).astype(BF16)
        v_cat = jnp.concatenate([vp_ref[0][:, sl], vc_ref[0][:, sl]], axis=0).astype(BF16)
        for g in range(grp):
            h = kvh * grp + g
            qh = (q[:, h * dh:(h + 1) * dh] * (dh ** -0.5)).astype(BF16)
            s = lax.dot_general(qh, k_cat, (((1,), (1,)), ((), ())), preferred_element_type=F32)
            s = jnp.where(mask, s, NEG_INF)
            sink = sinks_ref[h]
            m = jnp.maximum(jnp.max(s, axis=-1, keepdims=True), sink)
            p = jnp.exp(s - m)
            den = jnp.sum(p, axis=-1, keepdims=True) + jnp.exp(sink - m)
            o = jnp.dot(p.astype(BF16), v_cat, preferred_element_type=F32)
            outs.append(o / den)
    o_ref[0] = jnp.concatenate(outs, axis=-1).astype(o_ref.dtype)


def _swa(proj3, sinks):
    b, s, _ = proj3.shape
    w = SWA_WINDOW
    nb = s // w
    kw = SWA_KV_HEADS * SWA_HEAD_DIM
    qw = SWA_HEADS * SWA_HEAD_DIM
    cur = lambda col: (lambda bi, i: (bi, i, col))
    prev = lambda col: (lambda bi, i: (bi, jnp.maximum(i - 1, 0), col))
    return pl.pallas_call(
        _swa_kernel,
        grid=(b, nb),
        in_specs=[pl.BlockSpec(memory_space=pltpu.SMEM),
                  pl.BlockSpec((1, w, qw), cur(P_AQ // qw)),
                  pl.BlockSpec((1, w, kw), cur(P_AK // kw)),
                  pl.BlockSpec((1, w, kw), prev(P_AK // kw)),
                  pl.BlockSpec((1, w, kw), cur(P_AV // kw)),
                  pl.BlockSpec((1, w, kw), prev(P_AV // kw))],
        out_specs=pl.BlockSpec((1, w, qw), lambda bi, i: (bi, i, 0)),
        out_shape=jax.ShapeDtypeStruct((b, s, qw), BF16),
        compiler_params=_cparams(("parallel", "arbitrary")),
        name="swa",
    )(sinks, proj3, proj3, proj3, proj3, proj3)


def _mlstm_kernel(qk_ref, v_ref, og_ref, g_ref, cw_ref, cb_ref, gb_ref, ng_ref, y_ref, xbuf, ct_ref, m_ref):
    c = pl.program_id(0)
    nb = qk_ref.shape[0]
    ln = M_CHUNK
    dqk, dv, nh = M_QK_DIM, M_V_DIM, M_HEADS
    halo = SUBLANES

    @pl.when(c == 0)
    def _init():
        xbuf[:, 0:halo, :] = jnp.zeros((nb, halo, xbuf.shape[2]), F32)
        ct_ref[...] = jnp.zeros(ct_ref.shape, F32)
        m_ref[...] = jnp.zeros(m_ref.shape, F32)

    row = lax.broadcasted_iota(jnp.int32, (ln, ln), 0)
    col = lax.broadcasted_iota(jnp.int32, (ln, ln), 1)
    tril = row >= col
    trilf = jnp.where(tril, 1.0, 0.0).astype(F32)
    ones_blk = jnp.where(lax.broadcasted_iota(jnp.int32, (ln, dv), 1) == 0, 1.0, 0.0).astype(F32)

    for b in range(nb):
        xbuf[b, halo:halo + ln, :] = qk_ref[b]
        conv = cb_ref[...]
        for j in range(M_CONV):
            off = halo - (M_CONV - 1) + j
            conv = conv + cw_ref[j:j + 1, :] * xbuf[b, off:off + ln, :]
        qk = conv * jax.nn.sigmoid(conv)
        xbuf[b, 0:halo, :] = qk_ref[b, ln - halo:ln, :]

        g = g_ref[b] + gb_ref[...]
        lf = jnp.minimum(g, 0.0) - jnp.log1p(jnp.exp(-jnp.abs(g)))
        cum = jnp.dot(trilf, lf, preferred_element_type=F32, precision=lax.Precision.HIGHEST)
        cum_t = cum.T
        g_t = g.T
        vv = v_ref[b]
        hs = []
        for h in range(nh):
            chain = b * nh + h
            q = (qk[:, h * dqk:(h + 1) * dqk] * (dqk ** -0.5)).astype(BF16)
            k = qk[:, nh * dqk + h * dqk: nh * dqk + (h + 1) * dqk].astype(BF16)
            v_ext = jnp.concatenate([vv[:, h * dv:(h + 1) * dv], ones_blk], axis=-1)
            bc_col = cum[:, nh + h:nh + h + 1]
            bc_row = cum_t[nh + h:nh + h + 1, :]
            i_col = g[:, h:h + 1]
            i_row = g_t[h:h + 1, :]
            m_prev = m_ref[chain:chain + 1, 0:1]
            dmat = jnp.where(tril, bc_col - bc_row + i_row, NEG_INF)
            m_inter = bc_col + m_prev
            m_j = jnp.maximum(m_inter, jnp.max(dmat, axis=-1, keepdims=True))
            w_intra = jnp.exp(dmat - m_j)
            w_inter = jnp.exp(m_inter - m_j)
            s = lax.dot_general(q, k, (((1,), (1,)), ((), ())), preferred_element_type=F32) * w_intra
            ct = ct_ref[chain]
            num_ext = (jnp.dot(s.astype(BF16), v_ext.astype(BF16), preferred_element_type=F32)
                       + w_inter * jnp.dot(q, ct.astype(BF16), preferred_element_type=F32))
            num = num_ext[:, :dv]
            nq = num_ext[:, dv:dv + 1]
            den = jnp.maximum(jnp.abs(nq), jnp.exp(-m_j))
            hs.append(num / den)
            m_new = m_j[ln - 1:ln, :]
            bc_last = bc_col[ln - 1:ln, :]
            w_s = jnp.exp(bc_last - bc_col + i_col - m_new)
            w_c = jnp.exp(bc_last + m_prev - m_new)
            upd = lax.dot_general(k, (w_s * v_ext).astype(BF16), (((0,), (0,)), ((), ())),
                                  preferred_element_type=F32)
            ct_ref[chain] = w_c * ct + upd
            m_ref[chain:chain + 1, :] = jnp.broadcast_to(m_new, (1, m_ref.shape[1]))
        og = og_ref[b]
        outs = []
        for h in range(nh):
            seg = jax.nn.sigmoid(og[:, h * dv:(h + 1) * dv]) * hs[h]
            mu = jnp.mean(seg, axis=-1, keepdims=True)
            cen = seg - mu
            var = jnp.mean(cen * cen, axis=-1, keepdims=True)
            outs.append(cen * lax.rsqrt(var + LN_EPS) * ng_ref[:, h * dv:(h + 1) * dv])
        y_ref[b] = jnp.concatenate(outs, axis=-1).astype(y_ref.dtype)


def _mlstm(proj3, conv_w, conv_b, gate_b, norm_g):
    b, s, _ = proj3.shape
    ln = M_CHUNK
    wq = 2 * M_HEADS * M_QK_DIM
    wv = M_HEADS * M_V_DIM
    blk = lambda width, off: pl.BlockSpec((b, ln, width), lambda c: (0, c, off // width))
    full = lambda a: pl.BlockSpec(a.shape, lambda c: (0,) * a.ndim)
    gate_b128 = jnp.zeros((1, LANES), F32).at[0, :2 * M_HEADS].set(gate_b)
    conv_b2 = conv_b.reshape(1, wq)
    norm_g2 = norm_g.reshape(1, wv)
    return pl.pallas_call(
        _mlstm_kernel,
        grid=(s // ln,),
        in_specs=[blk(wq, P_MQK), blk(wv, P_MV), blk(wv, P_MO), blk(LANES, P_MIF),
                  full(conv_w), full(conv_b2), full(gate_b128), full(norm_g2)],
        out_specs=pl.BlockSpec((b, ln, wv), lambda c: (0, c, 0)),
        out_shape=jax.ShapeDtypeStruct((b, s, wv), BF16),
        scratch_shapes=[pltpu.VMEM((b, SUBLANES + ln, wq), F32),
                        pltpu.VMEM((b * M_HEADS, M_QK_DIM, 2 * M_V_DIM), F32),
                        pltpu.VMEM((b * M_HEADS, LANES), F32)],
        compiler_params=_cparams(("arbitrary",)),
        name="mlstm",
    )(proj3, proj3, proj3, proj3, conv_w, conv_b2, gate_b128, norm_g2)


def _mla_prep_kernel(cq_ref, ckv_ref, kr_ref, krs_ref, qg_ref, kvg_ref, wqa_ref, wqb_ref, wkv_ref,
                     cos_ref, sin_ref, q_out, k_out, v_out):
    nd, hd = C_NOPE_DIM, C_HEADS
    scale = (C_NOPE_DIM + C_ROPE_DIM) ** -0.5
    cos = cos_ref[...]
    sin = sin_ref[...]

    cq = cq_ref[0]
    qn = (cq * lax.rsqrt(jnp.mean(cq * cq, axis=-1, keepdims=True) + RMS_EPS) * qg_ref[...]).astype(BF16)
    qa = jnp.dot(qn, wqa_ref[...], preferred_element_type=F32)
    qb = jnp.dot(qn, wqb_ref[...], preferred_element_type=F32)
    for h in range(hd):
        nope = qa[:, h * 2 * nd: h * 2 * nd + nd]
        rope = qa[:, h * 2 * nd + nd:(h + 1) * 2 * nd] * cos + qb[:, h * nd:(h + 1) * nd] * sin
        q_out[0, h] = (jnp.concatenate([nope, rope], axis=-1) * scale).astype(q_out.dtype)

    ckv = ckv_ref[0]
    kvn = (ckv * lax.rsqrt(jnp.mean(ckv * ckv, axis=-1, keepdims=True) + RMS_EPS) * kvg_ref[...]).astype(BF16)
    kv = jnp.dot(kvn, wkv_ref[...], preferred_element_type=F32)
    kr = kr_ref[0] * cos + krs_ref[0] * sin
    for h in range(hd):
        k_out[0, h] = jnp.concatenate([kv[:, h * nd:(h + 1) * nd], kr], axis=-1).astype(k_out.dtype)
        v_out[0, h] = kv[:, hd * nd + h * nd: hd * nd + (h + 1) * nd].astype(v_out.dtype)


def _mla_prep(proj3, q_norm_g, kv_norm_g, wqa, wqb, wkv, cos128, sin128):
    b, s, _ = proj3.shape
    tm = MLA_TM
    hd, nd = C_HEADS, C_NOPE_DIM
    blk = lambda width, off: pl.BlockSpec((1, tm, width), lambda bi, i: (bi, i, off // width))
    full = lambda a: pl.BlockSpec(a.shape, lambda bi, i: (0,) * a.ndim)
    tab = pl.BlockSpec((tm, LANES), lambda bi, i: (i, 0))
    qg = q_norm_g.reshape(1, -1)
    kvg = kv_norm_g.reshape(1, -1)
    head_out = lambda width: pl.BlockSpec((1, hd, tm, width), lambda bi, i: (bi, 0, i, 0))
    return pl.pallas_call(
        _mla_prep_kernel,
        grid=(b, s // tm),
        in_specs=[blk(C_Q_LORA, P_CQ), blk(C_KV_LORA, P_CKV), blk(LANES, P_CKR), blk(LANES, P_CKRS),
                  full(qg), full(kvg), full(wqa), full(wqb), full(wkv), tab, tab],
        out_specs=[head_out(2 * nd), head_out(2 * nd), head_out(C_V_DIM)],
        out_shape=[jax.ShapeDtypeStruct((b, hd, s, 2 * nd), BF16),
                   jax.ShapeDtypeStruct((b, hd, s, 2 * nd), BF16),
                   jax.ShapeDtypeStruct((b, hd, s, C_V_DIM), BF16)],
        compiler_params=_cparams(("parallel", "arbitrary")),
        name="mla_prep",
    )(proj3, proj3, proj3, proj3, qg, kvg, wqa, wqb, wkv, cos128, sin128)


def _flash_kernel(q_ref, k_ref, v_ref, o_ref):
    qi = pl.program_id(2)
    bq = FLASH_BQ
    q = q_ref[0, 0]
    dv = v_ref.shape[3]

    def step(j, carry, diag):
        m, l, acc = carry
        start = pl.multiple_of(j * bq, bq)
        ks = k_ref[0, 0, pl.ds(start, bq), :]
        vs = v_ref[0, 0, pl.ds(start, bq), :]
        s = lax.dot_general(q, ks, (((1,), (1,)), ((), ())), preferred_element_type=F32)
        if diag:
            row = lax.broadcasted_iota(jnp.int32, (bq, bq), 0)
            col = lax.broadcasted_iota(jnp.int32, (bq, bq), 1)
            s = jnp.where(row >= col, s, NEG_INF)
        m_new = jnp.maximum(m, jnp.max(s, axis=-1, keepdims=True))
        alpha = jnp.exp(m - m_new)
        p = jnp.exp(s - m_new)
        l = alpha * l + jnp.sum(p, axis=-1, keepdims=True)
        acc = alpha * acc + jnp.dot(p.astype(BF16), vs, preferred_element_type=F32)
        return m_new, l, acc

    init = (jnp.full((bq, 1), NEG_INF, F32), jnp.zeros((bq, 1), F32), jnp.zeros((bq, dv), F32))
    carry = lax.fori_loop(0, qi, lambda j, cr: step(j, cr, False), init)
    _, l, acc = step(qi, carry, True)
    o_ref[0] = (acc / l).astype(o_ref.dtype)


def _flash(qc, kc, vc):
    b, hd, s, dk = qc.shape
    dv = vc.shape[3]
    bq = FLASH_BQ
    return pl.pallas_call(
        _flash_kernel,
        grid=(b, hd, s // bq),
        in_specs=[pl.BlockSpec((1, 1, bq, dk), lambda bi, h, i: (bi, h, i, 0)),
                  pl.BlockSpec((1, 1, s, dk), lambda bi, h, i: (bi, h, 0, 0)),
                  pl.BlockSpec((1, 1, s, dv), lambda bi, h, i: (bi, h, 0, 0))],
        out_specs=pl.BlockSpec((1, bq, dv), lambda bi, h, i: (bi, i, h)),
        out_shape=jax.ShapeDtypeStruct((b, s, hd * dv), BF16),
        compiler_params=_cparams(("parallel", "parallel", "arbitrary")),
        name="flash",
    )(qc, kc, vc)


def _layer_norm(z, g, b):
    mu = jnp.mean(z, axis=-1, keepdims=True)
    cen = z - mu
    var = jnp.mean(cen * cen, axis=-1, keepdims=True)
    return cen * lax.rsqrt(var + LN_EPS) * g + b


def _outproj_kernel(ya_ref, yb_ref, yc_ref, x_ref, w_ref, g_ref, b_ref, wr_ref, br_ref,
                    x1_ref, x1rc_ref, e_ref, gate_ref, rank_ref, cnt_ref, cnt_sc):
    i = pl.program_id(0)
    tm = x_ref.shape[0]
    wa, wb = ya_ref.shape[1], yb_ref.shape[1]
    mix = jnp.dot(ya_ref[...], w_ref[0:wa, :], preferred_element_type=F32)
    mix = mix + jnp.dot(yb_ref[...], w_ref[wa:wa + wb, :], preferred_element_type=F32)
    mix = mix + jnp.dot(yc_ref[...], w_ref[wa + wb:, :], preferred_element_type=F32)
    x1 = _layer_norm(DN_ALPHA * x_ref[...] + mix, g_ref[...], b_ref[...])
    x1_ref[...] = x1
    _store_rc(x1rc_ref, x1)

    logits = jnp.dot(x1, wr_ref[...], preferred_element_type=F32, precision=lax.Precision.HIGHEST) + br_ref[...]
    lane = lax.broadcasted_iota(jnp.int32, logits.shape, 1)
    logits = jnp.where(lane < N_EXPERTS, logits, NEG_INF)
    e_out = jnp.zeros(logits.shape, jnp.int32)
    p_out = jnp.zeros(logits.shape, F32)
    top = None
    den = None
    onehots = []
    for r in range(TOP_K):
        mx = jnp.max(logits, axis=-1, keepdims=True)
        idx = jnp.min(jnp.where(logits == mx, lane, LANES), axis=-1, keepdims=True)
        if r == 0:
            top = mx
        p = jnp.exp(mx - top)
        den = p if r == 0 else den + p
        sel = lane == idx
        onehots.append(jnp.where(sel, 1.0, 0.0).astype(F32))
        e_out = jnp.where(lane == r, idx, e_out)
        p_out = jnp.where(lane == r, p, p_out)
        logits = jnp.where(sel, NEG_INF, logits)
    e_ref[...] = e_out
    gate_ref[...] = p_out / den

    @pl.when(i == 0)
    def _init():
        cnt_sc[...] = jnp.zeros(cnt_sc.shape, F32)

    oh_sum = onehots[0] + onehots[1] + onehots[2] + onehots[3]
    row = lax.broadcasted_iota(jnp.int32, (tm, tm), 0)
    col = lax.broadcasted_iota(jnp.int32, (tm, tm), 1)
    before = jnp.where(row > col, 1.0, 0.0).astype(BF16)
    base = jnp.dot(before, oh_sum.astype(BF16), preferred_element_type=F32) + cnt_sc[0:1, :]
    rank_out = jnp.zeros(logits.shape, jnp.int32)
    for r in range(TOP_K):
        rk = jnp.sum(onehots[r] * base, axis=-1, keepdims=True)
        rank_out = jnp.where(lane == r, rk.astype(jnp.int32), rank_out)
    rank_ref[...] = rank_out
    total = cnt_sc[0:1, :] + jnp.sum(oh_sum, axis=0, keepdims=True)
    cnt_sc[0:1, :] = total
    cnt_ref[...] = jnp.broadcast_to(total, cnt_ref.shape).astype(jnp.int32)


def _outproj(ya, yb, yc, x2d, w_bf, g, b, wr128, br128):
    t, d = x2d.shape
    tm = OUT_TM
    rows = lambda a: pl.BlockSpec((tm, a.shape[1]), lambda i: (i, 0))
    full = lambda a: pl.BlockSpec(a.shape, lambda i: (0,) * a.ndim)
    g2, b2 = g.reshape(1, d), b.reshape(1, d)
    lane_blk = pl.BlockSpec((tm, LANES), lambda i: (i, 0))
    return pl.pallas_call(
        _outproj_kernel,
        grid=(t // tm,),
        in_specs=[rows(ya), rows(yb), rows(yc), rows(x2d), full(w_bf), full(g2), full(b2), full(wr128), full(br128)],
        out_specs=[pl.BlockSpec((tm, d), lambda i: (i, 0)),
                   pl.BlockSpec((tm * ROW_TILES, LANES), lambda i: (i, 0)),
                   lane_blk, lane_blk, lane_blk,
                   pl.BlockSpec((SUBLANES, LANES), lambda i: (0, 0))],
        out_shape=[jax.ShapeDtypeStruct((t, d), F32),
                   jax.ShapeDtypeStruct((t * ROW_TILES, LANES), F32),
                   jax.ShapeDtypeStruct((t, LANES), jnp.int32),
                   jax.ShapeDtypeStruct((t, LANES), F32),
                   jax.ShapeDtypeStruct((t, LANES), jnp.int32),
                   jax.ShapeDtypeStruct((SUBLANES, LANES), jnp.int32)],
        scratch_shapes=[pltpu.VMEM((SUBLANES, LANES), F32)],
        compiler_params=_cparams(("arbitrary",)),
        name="outproj",
    )(ya, yb, yc, x2d, w_bf, g2, b2, wr128, br128)


def _rc_rows(row, n=1):
    return pl.ds(pl.multiple_of(row * ROW_TILES, ROW_TILES), n * ROW_TILES)


def _dispatch_kernel(pos_ref, cnt_ref, pst_ref, pad_ref, nblk_ref, x_hbm, xs_hbm, buf, zbuf, sem_in, sem_out, sem_z):
    i = pl.program_id(0)
    n = pl.num_programs(0)
    tm = DSP_TM
    nslot = buf.shape[0]

    def in_copy(blk, slot):
        return pltpu.make_async_copy(x_hbm.at[_rc_rows(blk * tm, tm)], buf.at[slot], sem_in.at[slot])

    def row_out(slot, t, p):
        return pltpu.make_async_copy(buf.at[slot, _rc_rows(t)], xs_hbm.at[_rc_rows(p)], sem_out.at[slot])

    def wait_outs(slot):
        for _ in range(TOP_K):
            pltpu.make_async_copy(buf.at[slot], xs_hbm.at[_rc_rows(0, tm)], sem_out.at[slot]).wait()

    slot = lax.rem(i, nslot)

    @pl.when(i == 0)
    def _first_in():
        in_copy(0, 0).start()

    @pl.when(i >= 2)
    def _free_slot():
        wait_outs(lax.rem(i + 1, nslot))

    @pl.when(i + 1 < n)
    def _next_in():
        in_copy(i + 1, lax.rem(i + 1, nslot)).start()

    in_copy(i, slot).wait()
    base = i * (tm * TOP_K)

    def issue(t, _):
        for k in range(TOP_K):
            row_out(slot, t, pos_ref[base + t * TOP_K + k]).start()
        return 0

    lax.fori_loop(0, tm, issue, 0, unroll=2)

    @pl.when(i == n - 1)
    def _finish():
        wait_outs(lax.rem(i + 2, nslot))
        wait_outs(slot)
        zbuf[...] = jnp.zeros(zbuf.shape, zbuf.dtype)

        def pad_rows(e, _):
            cnt = cnt_ref[e]
            first = pst_ref[e] + cnt
            npad = pad_ref[e] - cnt

            def zero_row(q):
                return pltpu.make_async_copy(zbuf.at[_rc_rows(0)], xs_hbm.at[_rc_rows(first + q)], sem_z)

            def start(q, c):
                zero_row(q).start()
                return c

            def wait(q, c):
                zero_row(q).wait()
                return c

            lax.fori_loop(0, npad, start, 0)
            lax.fori_loop(0, npad, wait, 0)
            return 0

        lax.fori_loop(0, N_EXPERTS, pad_rows, 0)

        def tail_block(bk, _):
            cp = pltpu.make_async_copy(zbuf, xs_hbm.at[_rc_rows(bk * MOE_PAD, MOE_PAD)], sem_z)
            cp.start()
            cp.wait()
            return 0

        lax.fori_loop(nblk_ref[0], xs_hbm.shape[0] // (MOE_PAD * ROW_TILES), tail_block, 0)


def _dispatch(pos, counts, pad_start, padded, nblk, x1rc, rows):
    t = x1rc.shape[0] // ROW_TILES
    tm = DSP_TM
    return pl.pallas_call(
        _dispatch_kernel,
        grid_spec=pltpu.PrefetchScalarGridSpec(
            num_scalar_prefetch=5,
            grid=(t // tm,),
            in_specs=[pl.BlockSpec(memory_space=pl.ANY)],
            out_specs=pl.BlockSpec(memory_space=pl.ANY),
            scratch_shapes=[pltpu.VMEM((3, tm * ROW_TILES, LANES), F32),
                            pltpu.VMEM((MOE_PAD * ROW_TILES, LANES), F32),
                            pltpu.SemaphoreType.DMA((3,)),
                            pltpu.SemaphoreType.DMA((3,)),
                            pltpu.SemaphoreType.DMA(())]),
        out_shape=jax.ShapeDtypeStruct((rows * ROW_TILES, LANES), F32),
        compiler_params=_cparams(("arbitrary",)),
        name="dispatch",
    )(pos, counts, pad_start, padded, nblk, x1rc)


def _moe_kernel(sbe_ref, sbr_ref, sbn_ref, nblk_ref, xs_hbm, wg_ref, wu_ref, bg_ref, bu_ref, wd_ref, bd_ref, y_hbm,
                xbuf, acc, stage, wg_bf, wu_bf, wd_bf, sem):
    sb = pl.program_id(0)
    j = pl.program_id(1)
    nsb = pl.num_programs(0)
    nj = pl.num_programs(1)
    n = sbn_ref[sb]
    row0 = sbr_ref[sb]
    nchunk = n // MOE_PAD

    def chunk_rows(c):
        return pl.ds(pl.multiple_of(c * MOE_PAD, MOE_PAD), MOE_PAD)

    def chunk_in(c, slot):
        return pltpu.make_async_copy(xs_hbm.at[_rc_rows(row0 + c * MOE_PAD, MOE_PAD)], stage.at[slot], sem.at[slot])

    def chunk_out(c, slot):
        return pltpu.make_async_copy(stage.at[slot], y_hbm.at[_rc_rows(row0 + c * MOE_PAD, MOE_PAD)], sem.at[slot])

    def for_chunks(fn):
        def body(c, _):
            fn(c)
            return 0
        lax.fori_loop(0, nchunk, body, 0)

    @pl.when((j == 0) & (n > 0))
    def _load_rows():
        chunk_in(0, 0).start()

        def load(c):
            slot = lax.rem(c, 2)

            @pl.when(c + 1 < nchunk)
            def _prefetch():
                chunk_in(c + 1, 1 - slot).start()

            chunk_in(c, slot).wait()
            xbuf[chunk_rows(c), :] = _load_rc(stage.at[slot], MOE_PAD).astype(BF16)

        for_chunks(load)

    @pl.when(n > 0)
    def _cast_weights():
        wg_bf[...] = wg_ref[...].astype(BF16)
        wu_bf[...] = wu_ref[...].astype(BF16)
        wd_bf[...] = wd_ref[...].astype(BF16)

    def ffn_chunk(c, first):
        rs = chunk_rows(c)
        xr = xbuf[rs, :]
        gt = jnp.dot(xr, wg_bf[...], preferred_element_type=F32) + bg_ref[...]
        up = jnp.dot(xr, wu_bf[...], preferred_element_type=F32) + bu_ref[...]
        gt = jnp.minimum(gt, SWIGLU_LIMIT)
        up = jnp.clip(up, -SWIGLU_LIMIT, SWIGLU_LIMIT)
        act = (up + 1.0) * gt * jax.nn.sigmoid(SWIGLU_ALPHA * gt)
        contrib = jnp.dot(act.astype(BF16), wd_bf[...], preferred_element_type=F32)
        if first:
            acc[rs, :] = contrib + bd_ref[...]
        else:
            acc[rs, :] = acc[rs, :] + contrib

    @pl.when(j == 0)
    def _first_tile():
        for_chunks(lambda c: ffn_chunk(c, True))

    @pl.when(j > 0)
    def _next_tiles():
        for_chunks(lambda c: ffn_chunk(c, False))

    @pl.when((j == nj - 1) & (n > 0))
    def _store_rows():
        def store(c):
            slot = lax.rem(c, 2)

            @pl.when(c >= 2)
            def _slot_free():
                chunk_out(c - 2, slot).wait()

            _store_rc(stage.at[slot], acc[chunk_rows(c), :])
            chunk_out(c, slot).start()

        for_chunks(store)

        @pl.when(nchunk >= 2)
        def _drain_prev():
            chunk_out(nchunk - 2, lax.rem(nchunk, 2)).wait()

        chunk_out(nchunk - 1, lax.rem(nchunk - 1, 2)).wait()

    @pl.when((sb == nsb - 1) & (j == nj - 1))
    def _zero_tail():
        stage[0] = jnp.zeros(stage.shape[1:], stage.dtype)

        def tail_block(bk, _):
            cp = pltpu.make_async_copy(stage.at[0], y_hbm.at[_rc_rows(bk * MOE_PAD, MOE_PAD)], sem.at[0])
            cp.start()
            cp.wait()
            return 0

        lax.fori_loop(nblk_ref[0], y_hbm.shape[0] // (MOE_PAD * ROW_TILES), tail_block, 0)


def _moe(layer, sb_e, sb_row0, sb_n, nblk, xs, w_gate_up, b_gate_up, w_down, b_down):
    rows = xs.shape[0] // ROW_TILES
    d = D_MODEL
    dff = w_down.shape[2]
    tf = MOE_TF
    nj = dff // tf
    nsb = sb_e.shape[0]
    bgu = b_gate_up.reshape(DEPTH, N_EXPERTS, 1, 2 * dff)
    bdn = b_down.reshape(DEPTH, N_EXPERTS, 1, d)

    def jeff(sb, j, n_ref):
        return jnp.where(n_ref[sb] > 0, j, nj - 1)

    gate_map = lambda sb, j, e, r, n, nb: (layer, e[sb], 0, jeff(sb, j, n))
    up_map = lambda sb, j, e, r, n, nb: (layer, e[sb], 0, nj + jeff(sb, j, n))
    down_map = lambda sb, j, e, r, n, nb: (layer, e[sb], jeff(sb, j, n), 0)
    bd_map = lambda sb, j, e, r, n, nb: (layer, e[sb], 0, 0)
    return pl.pallas_call(
        _moe_kernel,
        grid_spec=pltpu.PrefetchScalarGridSpec(
            num_scalar_prefetch=4,
            grid=(nsb, nj),
            in_specs=[pl.BlockSpec(memory_space=pl.ANY),
                      pl.BlockSpec((None, None, d, tf), gate_map),
                      pl.BlockSpec((None, None, d, tf), up_map),
                      pl.BlockSpec((None, None, 1, tf), gate_map),
                      pl.BlockSpec((None, None, 1, tf), up_map),
                      pl.BlockSpec((None, None, tf, d), down_map),
                      pl.BlockSpec((None, None, 1, d), bd_map)],
            out_specs=pl.BlockSpec(memory_space=pl.ANY),
            scratch_shapes=[pltpu.VMEM((MOE_TMAX, d), BF16),
                            pltpu.VMEM((MOE_TMAX, d), F32),
                            pltpu.VMEM((2, MOE_PAD * ROW_TILES, LANES), F32),
                            pltpu.VMEM((d, tf), BF16),
                            pltpu.VMEM((d, tf), BF16),
                            pltpu.VMEM((tf, d), BF16),
                            pltpu.SemaphoreType.DMA((2,))]),
        out_shape=jax.ShapeDtypeStruct((rows * ROW_TILES, LANES), F32),
        compiler_params=_cparams(("arbitrary", "arbitrary")),
        name="moe",
    )(sb_e, sb_row0, sb_n, nblk, xs, w_gate_up, w_gate_up, bgu, bgu, w_down, bdn)


def _combine_kernel(pos_ref, y_hbm, gate_ref, x1_ref, g_ref, b_ref, o_ref, buf, sem):
    i = pl.program_id(0)
    n = pl.num_programs(0)
    tm = CMB_TM
    slot = lax.rem(i, 2)

    def issue(blk, dst_slot):
        base = blk * (tm * TOP_K)

        def body(t, _):
            for k in range(TOP_K):
                p = pos_ref[base + t * TOP_K + k]
                pltpu.make_async_copy(y_hbm.at[_rc_rows(p)], buf.at[dst_slot, k, _rc_rows(t)], sem.at[dst_slot]).start()
            return 0

        lax.fori_loop(0, tm, body, 0, unroll=2)

    @pl.when(i == 0)
    def _first():
        issue(0, 0)

    @pl.when(i + 1 < n)
    def _next():
        issue(i + 1, 1 - slot)

    for k in range(TOP_K):
        pltpu.make_async_copy(y_hbm.at[_rc_rows(0, tm)], buf.at[slot, k], sem.at[slot]).wait()
    gate = gate_ref[...]
    ffn = gate[:, 0:1] * _load_rc(buf.at[slot, 0], tm)
    for k in range(1, TOP_K):
        ffn = ffn + gate[:, k:k + 1] * _load_rc(buf.at[slot, k], tm)
    o_ref[...] = _layer_norm(DN_ALPHA * x1_ref[...] + ffn, g_ref[...], b_ref[...])


def _combine(pos_flat, y, gate, x1, g, b):
    t, d = x1.shape
    tm = CMB_TM
    g2, b2 = g.reshape(1, d), b.reshape(1, d)
    return pl.pallas_call(
        _combine_kernel,
        grid_spec=pltpu.PrefetchScalarGridSpec(
            num_scalar_prefetch=1,
            grid=(t // tm,),
            in_specs=[pl.BlockSpec(memory_space=pl.ANY),
                      pl.BlockSpec((tm, LANES), lambda i, p: (i, 0)),
                      pl.BlockSpec((tm, d), lambda i, p: (i, 0)),
                      pl.BlockSpec((1, d), lambda i, p: (0, 0)),
                      pl.BlockSpec((1, d), lambda i, p: (0, 0))],
            out_specs=pl.BlockSpec((tm, d), lambda i, p: (i, 0)),
            scratch_shapes=[pltpu.VMEM((2, TOP_K, tm * ROW_TILES, LANES), F32), pltpu.SemaphoreType.DMA((2,))]),
        out_shape=jax.ShapeDtypeStruct((t, d), F32),
        compiler_params=_cparams(("arbitrary",)),
        name="combine",
    )(pos_flat, y, gate, x1, g2, b2)


def _swap_halves(w):
    half = w.shape[-1] // 2
    return jnp.concatenate([w[..., half:], w[..., :half]], axis=-1)


def _layout_w_in(w):
    d = w.shape[0]
    widths = (512, 128, 128, 256, 256, 512, 512, 8, 512, 256, 64)
    offs = [0]
    for wd in widths:
        offs.append(offs[-1] + wd)
    a_q, a_k, a_v, m_q, m_k, m_v, m_o, m_if, c_q, c_kv, c_kr = [w[:, offs[i]:offs[i + 1]] for i in range(len(widths))]
    z = lambda n: jnp.zeros((d, n), w.dtype)
    out = jnp.concatenate([a_q, m_v, m_o, c_q, m_q, m_k, c_kv, a_k, a_v,
                           c_kr, z(LANES - 64), m_if, z(LANES - 8), _swap_halves(c_kr), z(LANES - 64)], axis=1)
    return out.astype(BF16)


def _layout_w_uq(w):
    r = w.shape[0]
    w3 = w.reshape(r, C_HEADS, C_NOPE_DIM + C_ROPE_DIM)
    nope, rope = w3[..., :C_NOPE_DIM], w3[..., C_NOPE_DIM:]
    z = jnp.zeros((r, C_HEADS, LANES - C_ROPE_DIM), w.dtype)
    wa = jnp.concatenate([nope, rope, z], axis=-1).reshape(r, -1)
    wb = jnp.concatenate([_swap_halves(rope), z], axis=-1).reshape(r, -1)
    return wa.astype(BF16), wb.astype(BF16)


def _layout_w_ukv(w):
    r = w.shape[0]
    w3 = w.reshape(r, C_HEADS, C_NOPE_DIM + C_V_DIM)
    return jnp.concatenate([w3[..., :C_NOPE_DIM].reshape(r, -1), w3[..., C_NOPE_DIM:].reshape(r, -1)],
                           axis=-1).astype(BF16)


def _rope_tables(seq):
    dim = C_ROPE_DIM
    inv = 1.0 / (ROPE_THETA ** (jnp.arange(0, dim, 2, dtype=F32) / dim))
    ang = jnp.arange(seq, dtype=F32)[:, None] * inv[None, :]
    cos, sin = jnp.cos(ang), jnp.sin(ang)
    z = jnp.zeros((seq, LANES - dim), F32)
    return jnp.concatenate([cos, cos, z], axis=-1), jnp.concatenate([-sin, sin, z], axis=-1)


def _route(top_e, rank, counts, nsb):
    padded = (counts + MOE_PAD - 1) // MOE_PAD * MOE_PAD
    pad_end = jnp.cumsum(padded)
    pad_start = pad_end - padded
    is_e = top_e[:, :, None] == jnp.arange(N_EXPERTS, dtype=jnp.int32)[None, None, :]
    pos = (jnp.sum(jnp.where(is_e, pad_start[None, None, :], 0), axis=-1) + rank).reshape(-1).astype(jnp.int32)
    nblk = (pad_end[-1:] // MOE_PAD).astype(jnp.int32)
    nsb_e = (padded + MOE_TMAX - 1) // MOE_TMAX
    sb_end = jnp.cumsum(nsb_e)
    sb_start = sb_end - nsb_e
    sb_idx = jnp.arange(nsb, dtype=jnp.int32)
    n_valid = sb_end[-1]
    sb_eff = jnp.minimum(sb_idx, n_valid - 1)
    sb_e = jnp.searchsorted(sb_end, sb_eff, side="right").astype(jnp.int32)
    part = sb_eff - sb_start[sb_e]
    sb_row0 = pad_start[sb_e] + part * MOE_TMAX
    sb_n = jnp.where(sb_idx < n_valid, jnp.minimum(padded[sb_e] - part * MOE_TMAX, MOE_TMAX), 0)
    tables = (counts, pad_start.astype(jnp.int32), padded.astype(jnp.int32), nblk)
    return pos, tables, sb_e, sb_row0.astype(jnp.int32), sb_n.astype(jnp.int32)


def kernel(x, w_in, conv_w, conv_b, m_gate_b, m_norm_g, sinks, q_norm_g, w_uq, kv_norm_g, w_ukv, w_out,
           ln1_g, ln1_b, w_router, b_router, w_gate_up, b_gate_up, w_down, b_down, ln2_g, ln2_b):
    b, s, d = x.shape
    t = b * s
    n_assign = t * TOP_K
    rows = (n_assign // MOE_PAD + N_EXPERTS) * MOE_PAD
    nsb = N_EXPERTS + rows // MOE_TMAX
    cos128, sin128 = _rope_tables(s)
    xt = x.reshape(t, d)
    for l in range(DEPTH):
        w_in_bf = _layout_w_in(w_in[l])
        wqa, wqb = _layout_w_uq(w_uq[l])
        wkv = _layout_w_ukv(w_ukv[l])
        w_out_bf = w_out[l].astype(BF16)
        wr128 = jnp.zeros((d, LANES), F32).at[:, :N_EXPERTS].set(w_router[l])
        br128 = jnp.zeros((1, LANES), F32).at[0, :N_EXPERTS].set(b_router[l])

        proj3 = _proj(xt, w_in_bf).reshape(b, s, N_PROJ)
        y_a = _swa(proj3, sinks[l])
        y_b = _mlstm(proj3, conv_w[l], conv_b[l], m_gate_b[l], m_norm_g[l])
        qc, kc, vc = _mla_prep(proj3, q_norm_g[l], kv_norm_g[l], wqa, wqb, wkv, cos128, sin128)
        y_c = _flash(qc, kc, vc)
        x1, x1rc, top_e, gate, rank, cnt = _outproj(y_a.reshape(t, -1), y_b.reshape(t, -1), y_c.reshape(t, -1), xt,
                                                    w_out_bf, ln1_g[l], ln1_b[l], wr128, br128)
        pos, tables, sb_e, sb_row0, sb_n = _route(top_e[:, :TOP_K], rank[:, :TOP_K], cnt[0, :N_EXPERTS], nsb)
        xs = _dispatch(pos, *tables, x1rc, rows)
        y = _moe(l, sb_e, sb_row0, sb_n, tables[3], xs, w_gate_up, b_gate_up, w_down, b_down)
        xt = _combine(pos, y, gate, x1, ln2_g[l], ln2_b[l])
    return xt.reshape(b, s, d)
```

```python
import functools

import jax
import jax.numpy as jnp
from jax import lax
from jax.experimental import pallas as pl
from jax.experimental.pallas import tpu as pltpu

F32 = jnp.float32
BF16 = jnp.bfloat16
NEG_INF = float("-inf")

D_MODEL = 2048
DEPTH = 2
SWA_HEADS, SWA_KV_HEADS, SWA_HEAD_DIM, SWA_WINDOW = 8, 2, 64, 128
M_HEADS, M_QK_DIM, M_V_DIM, M_CONV = 4, 64, 128, 4
C_HEADS, C_NOPE_DIM, C_ROPE_DIM, C_V_DIM = 8, 128, 64, 128
C_Q_LORA, C_KV_LORA = 512, 256
ROPE_THETA = 10000.0
N_EXPERTS, TOP_K = 32, 4
SWIGLU_LIMIT, SWIGLU_ALPHA = 7.0, 1.702
DN_ALPHA = (2 * DEPTH) ** 0.25
LN_EPS, RMS_EPS = 1e-5, 1e-6

LANES = 128
SUBLANES = 8
VMEM_LIMIT = 56 * 1024 * 1024

P_AQ, P_MV, P_MO, P_CQ, P_MQK, P_CKV, P_AK, P_AV, P_CKR, P_MIF, P_CKRS = (
    0, 512, 1024, 1536, 2048, 2560, 2816, 2944, 3072, 3200, 3328)
N_PROJ = 3456

PROJ_TM, PROJ_TN = 512, 1152
M_CHUNK = 128
MLA_TM = 512
FLASH_BQ = 512
OUT_TM = 512
MOE_PAD = 256
MOE_TMAX = 1536
MOE_TF = 256
CMB_TM = 64
DSP_TM = 64

ROW_TILES = D_MODEL // LANES


def _cparams(sem, vmem=VMEM_LIMIT):
    return pltpu.CompilerParams(dimension_semantics=sem, vmem_limit_bytes=vmem)


def _store_rc(ref, val):
    n = val.shape[0]
    x = jnp.stack([val[:, s * LANES:(s + 1) * LANES] for s in range(ROW_TILES)], axis=0)
    ref[...] = pltpu.einshape("stl->tsl", x).reshape(n * ROW_TILES, LANES)


def _load_rc(ref, n):
    x = pltpu.einshape("tsl->stl", ref[...].reshape(n, ROW_TILES, LANES))
    return jnp.concatenate([x[s] for s in range(ROW_TILES)], axis=-1)


def _proj_kernel(x_ref, w_ref, o_ref):
    o_ref[...] = jnp.dot(x_ref[...].astype(BF16), w_ref[...], preferred_element_type=F32)


def _proj(x2d, w_bf):
    t, d = x2d.shape
    n = w_bf.shape[1]
    return pl.pallas_call(
        _proj_kernel,
        grid=(t // PROJ_TM, n // PROJ_TN),
        in_specs=[pl.BlockSpec((PROJ_TM, d), lambda i, j: (i, 0)),
                  pl.BlockSpec((d, PROJ_TN), lambda i, j: (0, j))],
        out_specs=pl.BlockSpec((PROJ_TM, PROJ_TN), lambda i, j: (i, j)),
        out_shape=jax.ShapeDtypeStruct((t, n), F32),
        compiler_params=_cparams(("parallel", "arbitrary")),
        name="proj",
    )(x2d, w_bf)


def _swa_kernel(sinks_ref, q_ref, kc_ref, kp_ref, vc_ref, vp_ref, o_ref):
    i = pl.program_id(1)
    w = SWA_WINDOW
    dh = SWA_HEAD_DIM
    grp = SWA_HEADS // SWA_KV_HEADS
    q = q_ref[0]
    qi = lax.broadcasted_iota(jnp.int32, (w, 2 * w), 0)
    kj = lax.broadcasted_iota(jnp.int32, (w, 2 * w), 1)
    rel = qi + w - kj
    has_prev = jnp.where(i > 0, 0, w)
    mask = (rel >= 0) & (rel < w) & (kj >= has_prev)
    outs = []
    for kvh in range(SWA_KV_HEADS):
        sl = slice(kvh * dh, (kvh + 1) * dh)
        k_cat = jnp.concatenate([kp_ref[0][:, sl], kc_ref[0][:, sl]], axis=0).astype(BF16)
        v_cat = jnp.concatenate([vp_ref[0][:, sl], vc_ref[0][:, sl]], axis=0).astype(BF16)
        for g in range(grp):
            h = kvh * grp + g
            qh = (q[:, h * dh:(h + 1) * dh] * (dh ** -0.5)).astype(BF16)
            s = lax.dot_general(qh, k_cat, (((1,), (1,)), ((), ())), preferred_element_type=F32)
            s = jnp.where(mask, s, NEG_INF)
            sink = sinks_ref[h]
            m = jnp.maximum(jnp.max(s, axis=-1, keepdims=True), sink)
            p = jnp.exp(s - m)
            den = jnp.sum(p, axis=-1, keepdims=True) + jnp.exp(sink - m)
            o = jnp.dot(p.astype(BF16), v_cat, preferred_element_type=F32)
            outs.append(o / den)
    o_ref[0] = jnp.concatenate(outs, axis=-1).astype(o_ref.dtype)


def _swa(proj3, sinks):
    b, s, _ = proj3.shape
    w = SWA_WINDOW
    nb = s // w
    kw = SWA_KV_HEADS * SWA_HEAD_DIM
    qw = SWA_HEADS * SWA_HEAD_DIM
    cur = lambda col: (lambda bi, i: (bi, i, col))
    prev = lambda col: (lambda bi, i: (bi, jnp.maximum(i - 1, 0), col))
    return pl.pallas_call(
        _swa_kernel,
        grid=(b, nb),
        in_specs=[pl.BlockSpec(memory_space=pltpu.SMEM),
                  pl.BlockSpec((1, w, qw), cur(P_AQ // qw)),
                  pl.BlockSpec((1, w, kw), cur(P_AK // kw)),
                  pl.BlockSpec((1, w, kw), prev(P_AK // kw)),
                  pl.BlockSpec((1, w, kw), cur(P_AV // kw)),
                  pl.BlockSpec((1, w, kw), prev(P_AV // kw))],
        out_specs=pl.BlockSpec((1, w, qw), lambda bi, i: (bi, i, 0)),
        out_shape=jax.ShapeDtypeStruct((b, s, qw), BF16),
        compiler_params=_cparams(("parallel", "arbitrary")),
        name="swa",
    )(sinks, proj3, proj3, proj3, proj3, proj3)


def _mlstm_kernel(qk_ref, v_ref, og_ref, g_ref, cw_ref, cb_ref, gb_ref, ng_ref, y_ref, xbuf, ct_ref, m_ref):
    c = pl.program_id(0)
    nb = qk_ref.shape[0]
    ln = M_CHUNK
    dqk, dv, nh = M_QK_DIM, M_V_DIM, M_HEADS
    halo = SUBLANES

    @pl.when(c == 0)
    def _init():
        xbuf[:, 0:halo, :] = jnp.zeros((nb, halo, xbuf.shape[2]), F32)
        ct_ref[...] = jnp.zeros(ct_ref.shape, F32)
        m_ref[...] = jnp.zeros(m_ref.shape, F32)

    row = lax.broadcasted_iota(jnp.int32, (ln, ln), 0)
    col = lax.broadcasted_iota(jnp.int32, (ln, ln), 1)
    tril = row >= col
    trilf = jnp.where(tril, 1.0, 0.0).astype(F32)
    ones_blk = jnp.where(lax.broadcasted_iota(jnp.int32, (ln, dv), 1) == 0, 1.0, 0.0).astype(F32)

    for b in range(nb):
        xbuf[b, halo:halo + ln, :] = qk_ref[b]
        conv = cb_ref[...]
        for j in range(M_CONV):
            off = halo - (M_CONV - 1) + j
            conv = conv + cw_ref[j:j + 1, :] * xbuf[b, off:off + ln, :]
        qk = conv * jax.nn.sigmoid(conv)
        xbuf[b, 0:halo, :] = qk_ref[b, ln - halo:ln, :]

        g = g_ref[b] + gb_ref[...]
        lf = jnp.minimum(g, 0.0) - jnp.log1p(jnp.exp(-jnp.abs(g)))
        cum = jnp.dot(trilf, lf, preferred_element_type=F32, precision=lax.Precision.HIGHEST)
        cum_t = cum.T
        g_t = g.T
        vv = v_ref[b]
        hs = []
        for h in range(nh):
            chain = b * nh + h
            q = (qk[:, h * dqk:(h + 1) * dqk] * (dqk ** -0.5)).astype(BF16)
            k = qk[:, nh * dqk + h * dqk: nh * dqk + (h + 1) * dqk].astype(BF16)
            v_ext = jnp.concatenate([vv[:, h * dv:(h + 1) * dv], ones_blk], axis=-1)
            bc_col = cum[:, nh + h:nh + h + 1]
            bc_row = cum_t[nh + h:nh + h + 1, :]
            i_col = g[:, h:h + 1]
            i_row = g_t[h:h + 1, :]
            m_prev = m_ref[chain:chain + 1, 0:1]
            dmat = jnp.where(tril, bc_col - bc_row + i_row, NEG_INF)
            m_inter = bc_col + m_prev
            m_j = jnp.maximum(m_inter, jnp.max(dmat, axis=-1, keepdims=True))
            w_intra = jnp.exp(dmat - m_j)
            w_inter = jnp.exp(m_inter - m_j)
            s = lax.dot_general(q, k, (((1,), (1,)), ((), ())), preferred_element_type=F32) * w_intra
            ct = ct_ref[chain]
            num_ext = (jnp.dot(s.astype(BF16), v_ext.astype(BF16), preferred_element_type=F32)
                       + w_inter * jnp.dot(q, ct.astype(BF16), preferred_element_type=F32))
            num = num_ext[:, :dv]
            nq = num_ext[:, dv:dv + 1]
            den = jnp.maximum(jnp.abs(nq), jnp.exp(-m_j))
            hs.append(num / den)
            m_new = m_j[ln - 1:ln, :]
            bc_last = bc_col[ln - 1:ln, :]
            w_s = jnp.exp(bc_last - bc_col + i_col - m_new)
            w_c = jnp.exp(bc_last + m_prev - m_new)
            upd = lax.dot_general(k, (w_s * v_ext).astype(BF16), (((0,), (0,)), ((), ())),
                                  preferred_element_type=F32)
            ct_ref[chain] = w_c * ct + upd
            m_ref[chain:chain + 1, :] = jnp.broadcast_to(m_new, (1, m_ref.shape[1]))
        og = og_ref[b]
        outs = []
        for h in range(nh):
            seg = jax.nn.sigmoid(og[:, h * dv:(h + 1) * dv]) * hs[h]
            mu = jnp.mean(seg, axis=-1, keepdims=True)
            cen = seg - mu
            var = jnp.mean(cen * cen, axis=-1, keepdims=True)
            outs.append(cen * lax.rsqrt(var + LN_EPS) * ng_ref[:, h * dv:(h + 1) * dv])
        y_ref[b] = jnp.concatenate(outs, axis=-1).astype(y_ref.dtype)


def _mlstm(proj3, conv_w, conv_b, gate_b, norm_g):
    b, s, _ = proj3.shape
    ln = M_CHUNK
    wq = 2 * M_HEADS * M_QK_DIM
    wv = M_HEADS * M_V_DIM
    blk = lambda width, off: pl.BlockSpec((b, ln, width), lambda c: (0, c, off // width))
    full = lambda a: pl.BlockSpec(a.shape, lambda c: (0,) * a.ndim)
    gate_b128 = jnp.zeros((1, LANES), F32).at[0, :2 * M_HEADS].set(gate_b)
    conv_b2 = conv_b.reshape(1, wq)
    norm_g2 = norm_g.reshape(1, wv)
    return pl.pallas_call(
        _mlstm_kernel,
        grid=(s // ln,),
        in_specs=[blk(wq, P_MQK), blk(wv, P_MV), blk(wv, P_MO), blk(LANES, P_MIF),
                  full(conv_w), full(conv_b2), full(gate_b128), full(norm_g2)],
        out_specs=pl.BlockSpec((b, ln, wv), lambda c: (0, c, 0)),
        out_shape=jax.ShapeDtypeStruct((b, s, wv), BF16),
        scratch_shapes=[pltpu.VMEM((b, SUBLANES + ln, wq), F32),
                        pltpu.VMEM((b * M_HEADS, M_QK_DIM, 2 * M_V_DIM), F32),
                        pltpu.VMEM((b * M_HEADS, LANES), F32)],
        compiler_params=_cparams(("arbitrary",)),
        name="mlstm",
    )(proj3, proj3, proj3, proj3, conv_w, conv_b2, gate_b128, norm_g2)


def _mla_prep_kernel(cq_ref, ckv_ref, kr_ref, krs_ref, qg_ref, kvg_ref, wqa_ref, wqb_ref, wkv_ref,
                     cos_ref, sin_ref, q_out, k_out, v_out):
    nd, hd = C_NOPE_DIM, C_HEADS
    scale = (C_NOPE_DIM + C_ROPE_DIM) ** -0.5
    cos = cos_ref[...]
    sin = sin_ref[...]

    cq = cq_ref[0]
    qn = (cq * lax.rsqrt(jnp.mean(cq * cq, axis=-1, keepdims=True) + RMS_EPS) * qg_ref[...]).astype(BF16)
    qa = jnp.dot(qn, wqa_ref[...], preferred_element_type=F32)
    qb = jnp.dot(qn, wqb_ref[...], preferred_element_type=F32)
    for h in range(hd):
        nope = qa[:, h * 2 * nd: h * 2 * nd + nd]
        rope = qa[:, h * 2 * nd + nd:(h + 1) * 2 * nd] * cos + qb[:, h * nd:(h + 1) * nd] * sin
        q_out[0, h] = (jnp.concatenate([nope, rope], axis=-1) * scale).astype(q_out.dtype)

    ckv = ckv_ref[0]
    kvn = (ckv * lax.rsqrt(jnp.mean(ckv * ckv, axis=-1, keepdims=True) + RMS_EPS) * kvg_ref[...]).astype(BF16)
    kv = jnp.dot(kvn, wkv_ref[...], preferred_element_type=F32)
    kr = kr_ref[0] * cos + krs_ref[0] * sin
    for h in range(hd):
        k_out[0, h] = jnp.concatenate([kv[:, h * nd:(h + 1) * nd], kr], axis=-1).astype(k_out.dtype)
        v_out[0, h] = kv[:, hd * nd + h * nd: hd * nd + (h + 1) * nd].astype(v_out.dtype)


def _mla_prep(proj3, q_norm_g, kv_norm_g, wqa, wqb, wkv, cos128, sin128):
    b, s, _ = proj3.shape
    tm = MLA_TM
    hd, nd = C_HEADS, C_NOPE_DIM
    blk = lambda width, off: pl.BlockSpec((1, tm, width), lambda bi, i: (bi, i, off // width))
    full = lambda a: pl.BlockSpec(a.shape, lambda bi, i: (0,) * a.ndim)
    tab = pl.BlockSpec((tm, LANES), lambda bi, i: (i, 0))
    qg = q_norm_g.reshape(1, -1)
    kvg = kv_norm_g.reshape(1, -1)
    head_out = lambda width: pl.BlockSpec((1, hd, tm, width), lambda bi, i: (bi, 0, i, 0))
    return pl.pallas_call(
        _mla_prep_kernel,
        grid=(b, s // tm),
        in_specs=[blk(C_Q_LORA, P_CQ), blk(C_KV_LORA, P_CKV), blk(LANES, P_CKR), blk(LANES, P_CKRS),
                  full(qg), full(kvg), full(wqa), full(wqb), full(wkv), tab, tab],
        out_specs=[head_out(2 * nd), head_out(2 * nd), head_out(C_V_DIM)],
        out_shape=[jax.ShapeDtypeStruct((b, hd, s, 2 * nd), BF16),
                   jax.ShapeDtypeStruct((b, hd, s, 2 * nd), BF16),
                   jax.ShapeDtypeStruct((b, hd, s, C_V_DIM), BF16)],
        compiler_params=_cparams(("parallel", "arbitrary")),
        name="mla_prep",
    )(proj3, proj3, proj3, proj3, qg, kvg, wqa, wqb, wkv, cos128, sin128)


def _flash_kernel(q_ref, k_ref, v_ref, o_ref):
    qi = pl.program_id(2)
    bq = FLASH_BQ
    q = q_ref[0, 0]
    dv = v_ref.shape[3]

    def step(j, carry, diag):
        m, l, acc = carry
        start = pl.multiple_of(j * bq, bq)
        ks = k_ref[0, 0, pl.ds(start, bq), :]
        vs = v_ref[0, 0, pl.ds(start, bq), :]
        s = lax.dot_general(q, ks, (((1,), (1,)), ((), ())), preferred_element_type=F32)
        if diag:
            row = lax.broadcasted_iota(jnp.int32, (bq, bq), 0)
            col = lax.broadcasted_iota(jnp.int32, (bq, bq), 1)
            s = jnp.where(row >= col, s, NEG_INF)
        m_new = jnp.maximum(m, jnp.max(s, axis=-1, keepdims=True))
        alpha = jnp.exp(m - m_new)
        p = jnp.exp(s - m_new)
        l = alpha * l + jnp.sum(p, axis=-1, keepdims=True)
        acc = alpha * acc + jnp.dot(p.astype(BF16), vs, preferred_element_type=F32)
        return m_new, l, acc

    init = (jnp.full((bq, 1), NEG_INF, F32), jnp.zeros((bq, 1), F32), jnp.zeros((bq, dv), F32))
    carry = lax.fori_loop(0, qi, lambda j, cr: step(j, cr, False), init)
    _, l, acc = step(qi, carry, True)
    o_ref[0] = (acc / l).astype(o_ref.dtype)


def _flash(qc, kc, vc):
    b, hd, s, dk = qc.shape
    dv = vc.shape[3]
    bq = FLASH_BQ
    return pl.pallas_call(
        _flash_kernel,
        grid=(b, hd, s // bq),
        in_specs=[pl.BlockSpec((1, 1, bq, dk), lambda bi, h, i: (bi, h, i, 0)),
                  pl.BlockSpec((1, 1, s, dk), lambda bi, h, i: (bi, h, 0, 0)),
                  pl.BlockSpec((1, 1, s, dv), lambda bi, h, i: (bi, h, 0, 0))],
        out_specs=pl.BlockSpec((1, bq, dv), lambda bi, h, i: (bi, i, h)),
        out_shape=jax.ShapeDtypeStruct((b, s, hd * dv), BF16),
        compiler_params=_cparams(("parallel", "parallel", "arbitrary")),
        name="flash",
    )(qc, kc, vc)


def _layer_norm(z, g, b):
    mu = jnp.mean(z, axis=-1, keepdims=True)
    cen = z - mu
    var = jnp.mean(cen * cen, axis=-1, keepdims=True)
    return cen * lax.rsqrt(var + LN_EPS) * g + b


def _outproj_kernel(ya_ref, yb_ref, yc_ref, x_ref, w_ref, g_ref, b_ref, wr_ref, br_ref,
                    x1_ref, x1rc_ref, e_ref, gate_ref, rank_ref, cnt_ref, cnt_sc):
    i = pl.program_id(0)
    tm = x_ref.shape[0]
    wa, wb = ya_ref.shape[1], yb_ref.shape[1]
    mix = jnp.dot(ya_ref[...], w_ref[0:wa, :], preferred_element_type=F32)
    mix = mix + jnp.dot(yb_ref[...], w_ref[wa:wa + wb, :], preferred_element_type=F32)
    mix = mix + jnp.dot(yc_ref[...], w_ref[wa + wb:, :], preferred_element_type=F32)
    x1 = _layer_norm(DN_ALPHA * x_ref[...] + mix, g_ref[...], b_ref[...])
    x1_ref[...] = x1
    _store_rc(x1rc_ref, x1)

    logits = jnp.dot(x1.astype(BF16), wr_ref[...], preferred_element_type=F32) + br_ref[...]
    lane = lax.broadcasted_iota(jnp.int32, logits.shape, 1)
    logits = jnp.where(lane < N_EXPERTS, logits, NEG_INF)
    e_out = jnp.zeros(logits.shape, jnp.int32)
    p_out = jnp.zeros(logits.shape, F32)
    top = None
    den = None
    onehots = []
    for r in range(TOP_K):
        mx = jnp.max(logits, axis=-1, keepdims=True)
        idx = jnp.min(jnp.where(logits == mx, lane, LANES), axis=-1, keepdims=True)
        if r == 0:
            top = mx
        p = jnp.exp(mx - top)
        den = p if r == 0 else den + p
        sel = lane == idx
        onehots.append(jnp.where(sel, 1.0, 0.0).astype(F32))
        e_out = jnp.where(lane == r, idx, e_out)
        p_out = jnp.where(lane == r, p, p_out)
        logits = jnp.where(sel, NEG_INF, logits)
    e_ref[...] = e_out
    gate_ref[...] = p_out / den

    @pl.when(i == 0)
    def _init():
        cnt_sc[...] = jnp.zeros(cnt_sc.shape, F32)

    oh_sum = onehots[0] + onehots[1] + onehots[2] + onehots[3]
    row = lax.broadcasted_iota(jnp.int32, (tm, tm), 0)
    col = lax.broadcasted_iota(jnp.int32, (tm, tm), 1)
    before = jnp.where(row > col, 1.0, 0.0).astype(BF16)
    base = jnp.dot(before, oh_sum.astype(BF16), preferred_element_type=F32) + cnt_sc[0:1, :]
    rank_out = jnp.zeros(logits.shape, jnp.int32)
    for r in range(TOP_K):
        rk = jnp.sum(onehots[r] * base, axis=-1, keepdims=True)
        rank_out = jnp.where(lane == r, rk.astype(jnp.int32), rank_out)
    rank_ref[...] = rank_out
    total = cnt_sc[0:1, :] + jnp.sum(oh_sum, axis=0, keepdims=True)
    cnt_sc[0:1, :] = total
    cnt_ref[...] = jnp.broadcast_to(total, cnt_ref.shape).astype(jnp.int32)


def _outproj(ya, yb, yc, x2d, w_bf, g, b, wr128, br128):
    t, d = x2d.shape
    tm = OUT_TM
    rows = lambda a: pl.BlockSpec((tm, a.shape[1]), lambda i: (i, 0))
    full = lambda a: pl.BlockSpec(a.shape, lambda i: (0,) * a.ndim, pipeline_mode=pl.Buffered(1))
    g2, b2 = g.reshape(1, d), b.reshape(1, d)
    lane_blk = pl.BlockSpec((tm, LANES), lambda i: (i, 0))
    return pl.pallas_call(
        _outproj_kernel,
        grid=(t // tm,),
        in_specs=[rows(ya), rows(yb), rows(yc), rows(x2d), full(w_bf), full(g2), full(b2), full(wr128), full(br128)],
        out_specs=[pl.BlockSpec((tm, d), lambda i: (i, 0)),
                   pl.BlockSpec((tm * ROW_TILES, LANES), lambda i: (i, 0)),
                   lane_blk, lane_blk, lane_blk,
                   pl.BlockSpec((SUBLANES, LANES), lambda i: (0, 0))],
        out_shape=[jax.ShapeDtypeStruct((t, d), F32),
                   jax.ShapeDtypeStruct((t * ROW_TILES, LANES), F32),
                   jax.ShapeDtypeStruct((t, LANES), jnp.int32),
                   jax.ShapeDtypeStruct((t, LANES), F32),
                   jax.ShapeDtypeStruct((t, LANES), jnp.int32),
                   jax.ShapeDtypeStruct((SUBLANES, LANES), jnp.int32)],
        scratch_shapes=[pltpu.VMEM((SUBLANES, LANES), F32)],
        compiler_params=_cparams(("arbitrary",)),
        name="outproj",
    )(ya, yb, yc, x2d, w_bf, g2, b2, wr128, br128)


def _rc_rows(row, n=1):
    return pl.ds(pl.multiple_of(row * ROW_TILES, ROW_TILES), n * ROW_TILES)


def _dispatch_kernel(pos_ref, cnt_ref, pst_ref, pad_ref, nblk_ref, x_hbm, xs_hbm, buf, zbuf, sem_in, sem_out, sem_z):
    i = pl.program_id(0)
    n = pl.num_programs(0)
    tm = DSP_TM
    nslot = buf.shape[0]

    def in_copy(blk, slot):
        return pltpu.make_async_copy(x_hbm.at[_rc_rows(blk * tm, tm)], buf.at[slot], sem_in.at[slot])

    def row_out(slot, t, p):
        return pltpu.make_async_copy(buf.at[slot, _rc_rows(t)], xs_hbm.at[_rc_rows(p)], sem_out.at[slot])

    def wait_outs(slot):
        for _ in range(TOP_K):
            pltpu.make_async_copy(buf.at[slot], xs_hbm.at[_rc_rows(0, tm)], sem_out.at[slot]).wait()

    slot = lax.rem(i, nslot)

    @pl.when(i == 0)
    def _first_in():
        in_copy(0, 0).start()

    @pl.when(i >= 2)
    def _free_slot():
        wait_outs(lax.rem(i + 1, nslot))

    @pl.when(i + 1 < n)
    def _next_in():
        in_copy(i + 1, lax.rem(i + 1, nslot)).start()

    in_copy(i, slot).wait()
    base = i * (tm * TOP_K)

    def issue(t, _):
        for k in range(TOP_K):
            row_out(slot, t, pos_ref[base + t * TOP_K + k]).start()
        return 0

    lax.fori_loop(0, tm, issue, 0, unroll=2)

    @pl.when(i == n - 1)
    def _finish():
        wait_outs(lax.rem(i + 2, nslot))
        wait_outs(slot)
        zbuf[...] = jnp.zeros(zbuf.shape, zbuf.dtype)

        def pad_rows(e, _):
            cnt = cnt_ref[e]
            first = pst_ref[e] + cnt
            npad = pad_ref[e] - cnt

            def zero_row(q):
                return pltpu.make_async_copy(zbuf.at[_rc_rows(0)], xs_hbm.at[_rc_rows(first + q)], sem_z)

            def start(q, c):
                zero_row(q).start()
                return c

            def wait(q, c):
                zero_row(q).wait()
                return c

            lax.fori_loop(0, npad, start, 0)
            lax.fori_loop(0, npad, wait, 0)
            return 0

        lax.fori_loop(0, N_EXPERTS, pad_rows, 0)

        def tail_block(bk, _):
            cp = pltpu.make_async_copy(zbuf, xs_hbm.at[_rc_rows(bk * MOE_PAD, MOE_PAD)], sem_z)
            cp.start()
            cp.wait()
            return 0

        lax.fori_loop(nblk_ref[0], xs_hbm.shape[0] // (MOE_PAD * ROW_TILES), tail_block, 0)


def _dispatch(pos, counts, pad_start, padded, nblk, x1rc, rows):
    t = x1rc.shape[0] // ROW_TILES
    tm = DSP_TM
    return pl.pallas_call(
        _dispatch_kernel,
        grid_spec=pltpu.PrefetchScalarGridSpec(
            num_scalar_prefetch=5,
            grid=(t // tm,),
            in_specs=[pl.BlockSpec(memory_space=pl.ANY)],
            out_specs=pl.BlockSpec(memory_space=pl.ANY),
            scratch_shapes=[pltpu.VMEM((3, tm * ROW_TILES, LANES), F32),
                            pltpu.VMEM((MOE_PAD * ROW_TILES, LANES), F32),
                            pltpu.SemaphoreType.DMA((3,)),
                            pltpu.SemaphoreType.DMA((3,)),
                            pltpu.SemaphoreType.DMA(())]),
        out_shape=jax.ShapeDtypeStruct((rows * ROW_TILES, LANES), F32),
        compiler_params=_cparams(("arbitrary",)),
        name="dispatch",
    )(pos, counts, pad_start, padded, nblk, x1rc)


def _moe_kernel(sbe_ref, sbr_ref, sbn_ref, nblk_ref, xs_hbm, wg_ref, wu_ref, bg_ref, bu_ref, wd_ref, bd_ref, y_hbm,
                xbuf, acc, stage_in, stage_out, wg_bf, wu_bf, wd_bf, sem_in, sem_out):
    sb = pl.program_id(0)
    j = pl.program_id(1)
    nsb = pl.num_programs(0)
    nj = pl.num_programs(1)
    n = sbn_ref[sb]
    row0 = sbr_ref[sb]
    nchunk = n // MOE_PAD
    cur = lax.rem(sb, 2)
    big = 2 * MOE_PAD

    def chunk_rows(c):
        return pl.ds(pl.multiple_of(c * MOE_PAD, MOE_PAD), MOE_PAD)

    def chunk_in(first_row, c):
        return pltpu.make_async_copy(xs_hbm.at[_rc_rows(first_row + c * MOE_PAD, MOE_PAD)], stage_in, sem_in)

    def chunk_out(c, slot):
        return pltpu.make_async_copy(stage_out.at[slot], y_hbm.at[_rc_rows(row0 + c * MOE_PAD, MOE_PAD)],
                                     sem_out.at[slot])

    def wait_outs(count):
        @pl.when(count >= 2)
        def _older():
            chunk_out(0, lax.rem(count, 2)).wait()

        @pl.when(count >= 1)
        def _newest():
            chunk_out(0, lax.rem(count + 1, 2)).wait()

    def convert_in(slot, c):
        xbuf[slot, chunk_rows(c), :] = _load_rc(stage_in, MOE_PAD).astype(BF16)

    @pl.when((sb == 0) & (j == 0))
    def _load_first():
        def load(c, _):
            chunk_in(row0, c).start()
            chunk_in(row0, c).wait()
            convert_in(0, c)
            return 0

        lax.fori_loop(0, nchunk, load, 0)

    n_next = sbn_ref[sb + 1]
    fetch_next = j * MOE_PAD < n_next

    @pl.when(fetch_next)
    def _start_next():
        chunk_in(sbr_ref[sb + 1], j).start()

    @pl.when((j == 1) & (sb > 0))
    def _drain_prev_outs():
        wait_outs(sbn_ref[jnp.maximum(sb - 1, 0)] // MOE_PAD)

    @pl.when((j == 0) & (n > 0))
    def _init_acc():
        def init(c, _):
            acc[chunk_rows(c), :] = jnp.broadcast_to(bd_ref[...], (MOE_PAD, acc.shape[1]))
            return 0

        lax.fori_loop(0, nchunk, init, 0)

    def ffn(start, size, wg, wu, wd):
        rs = pl.ds(start, size)
        xr = xbuf[cur, rs, :]
        gt = jnp.dot(xr, wg, preferred_element_type=F32) + bg_ref[...]
        up = jnp.dot(xr, wu, preferred_element_type=F32) + bu_ref[...]
        gt = jnp.minimum(gt, SWIGLU_LIMIT)
        up = jnp.clip(up, -SWIGLU_LIMIT, SWIGLU_LIMIT)
        act = (up + 1.0) * gt * jax.nn.sigmoid(SWIGLU_ALPHA * gt)
        acc[rs, :] = acc[rs, :] + jnp.dot(act.astype(BF16), wd, preferred_element_type=F32)

    def first_chunk(size):
        wg = wg_ref[...].astype(BF16)
        wu = wu_ref[...].astype(BF16)
        wd = wd_ref[...].astype(BF16)
        wg_bf[...] = wg
        wu_bf[...] = wu
        wd_bf[...] = wd
        ffn(0, size, wg, wu, wd)

    @pl.when(n >= big)
    def _rows_big():
        first_chunk(big)

        def body(c, _):
            ffn(pl.multiple_of(c * big, big), big, wg_bf[...], wu_bf[...], wd_bf[...])
            return 0

        lax.fori_loop(1, n // big, body, 0)

        @pl.when(lax.rem(n, big) != 0)
        def _odd_chunk():
            ffn(pl.multiple_of((n // big) * big, big), MOE_PAD, wg_bf[...], wu_bf[...], wd_bf[...])

    @pl.when((n > 0) & (n < big))
    def _rows_small():
        first_chunk(MOE_PAD)

    @pl.when((j == nj - 1) & (n > 0))
    def _store_rows():
        def store(c, _):
            slot = lax.rem(c, 2)

            @pl.when(c >= 2)
            def _slot_free():
                chunk_out(c - 2, slot).wait()

            _store_rc(stage_out.at[slot], acc[chunk_rows(c), :])
            chunk_out(c, slot).start()
            return 0

        lax.fori_loop(0, nchunk, store, 0)

        @pl.when(sb == nsb - 1)
        def _no_later_step():
            wait_outs(nchunk)

    @pl.when(fetch_next)
    def _finish_next():
        chunk_in(sbr_ref[sb + 1], j).wait()
        convert_in(1 - cur, j)

    @pl.when((sb == nsb - 1) & (j == nj - 1))
    def _zero_tail():
        stage_out[0] = jnp.zeros(stage_out.shape[1:], stage_out.dtype)

        def tail_block(bk, _):
            cp = pltpu.make_async_copy(stage_out.at[0], y_hbm.at[_rc_rows(bk * MOE_PAD, MOE_PAD)], sem_out.at[0])
            cp.start()
            cp.wait()
            return 0

        lax.fori_loop(nblk_ref[0], y_hbm.shape[0] // (MOE_PAD * ROW_TILES), tail_block, 0)


def _moe(layer, sb_e, sb_row0, sb_n, nblk, xs, w_gate_up, b_gate_up, w_down, b_down):
    rows = xs.shape[0] // ROW_TILES
    d = D_MODEL
    dff = w_down.shape[2]
    tf = MOE_TF
    nj = dff // tf
    nsb = sb_e.shape[0] - 1
    assert nj >= 2 and MOE_TMAX // MOE_PAD <= nj and MOE_TMAX % (2 * MOE_PAD) == 0
    bgu = b_gate_up.reshape(DEPTH, N_EXPERTS, 1, 2 * dff)
    bdn = b_down.reshape(DEPTH, N_EXPERTS, 1, d)

    def jeff(sb, j, n_ref):
        return jnp.where(n_ref[sb] > 0, j, nj - 1)

    gate_map = lambda sb, j, e, r, n, nb: (layer, e[sb], 0, jeff(sb, j, n))
    up_map = lambda sb, j, e, r, n, nb: (layer, e[sb], 0, nj + jeff(sb, j, n))
    down_map = lambda sb, j, e, r, n, nb: (layer, e[sb], jeff(sb, j, n), 0)
    bd_map = lambda sb, j, e, r, n, nb: (layer, e[sb], 0, 0)
    return pl.pallas_call(
        _moe_kernel,
        grid_spec=pltpu.PrefetchScalarGridSpec(
            num_scalar_prefetch=4,
            grid=(nsb, nj),
            in_specs=[pl.BlockSpec(memory_space=pl.ANY),
                      pl.BlockSpec((None, None, d, tf), gate_map),
                      pl.BlockSpec((None, None, d, tf), up_map),
                      pl.BlockSpec((None, None, 1, tf), gate_map),
                      pl.BlockSpec((None, None, 1, tf), up_map),
                      pl.BlockSpec((None, None, tf, d), down_map),
                      pl.BlockSpec((None, None, 1, d), bd_map)],
            out_specs=pl.BlockSpec(memory_space=pl.ANY),
            scratch_shapes=[pltpu.VMEM((2, MOE_TMAX, d), BF16),
                            pltpu.VMEM((MOE_TMAX, d), F32),
                            pltpu.VMEM((MOE_PAD * ROW_TILES, LANES), F32),
                            pltpu.VMEM((2, MOE_PAD * ROW_TILES, LANES), F32),
                            pltpu.VMEM((d, tf), BF16),
                            pltpu.VMEM((d, tf), BF16),
                            pltpu.VMEM((tf, d), BF16),
                            pltpu.SemaphoreType.DMA(()),
                            pltpu.SemaphoreType.DMA((2,))]),
        out_shape=jax.ShapeDtypeStruct((rows * ROW_TILES, LANES), F32),
        compiler_params=_cparams(("arbitrary", "arbitrary")),
        name="moe",
    )(sb_e, sb_row0, sb_n, nblk, xs, w_gate_up, w_gate_up, bgu, bgu, w_down, bdn)


def _combine_kernel(pos_ref, y_hbm, gate_ref, x1_ref, g_ref, b_ref, o_ref, buf, sem):
    i = pl.program_id(0)
    n = pl.num_programs(0)
    tm = CMB_TM
    slot = lax.rem(i, 2)

    def issue(blk, dst_slot):
        base = blk * (tm * TOP_K)

        def body(t, _):
            for k in range(TOP_K):
                p = pos_ref[base + t * TOP_K + k]
                pltpu.make_async_copy(y_hbm.at[_rc_rows(p)], buf.at[dst_slot, k, _rc_rows(t)], sem.at[dst_slot]).start()
            return 0

        lax.fori_loop(0, tm, body, 0, unroll=2)

    @pl.when(i == 0)
    def _first():
        issue(0, 0)

    @pl.when(i + 1 < n)
    def _next():
        issue(i + 1, 1 - slot)

    for k in range(TOP_K):
        pltpu.make_async_copy(y_hbm.at[_rc_rows(0, tm)], buf.at[slot, k], sem.at[slot]).wait()
    gate = gate_ref[...]
    ffn = gate[:, 0:1] * _load_rc(buf.at[slot, 0], tm)
    for k in range(1, TOP_K):
        ffn = ffn + gate[:, k:k + 1] * _load_rc(buf.at[slot, k], tm)
    o_ref[...] = _layer_norm(DN_ALPHA * x1_ref[...] + ffn, g_ref[...], b_ref[...])


def _combine(pos_flat, y, gate, x1, g, b):
    t, d = x1.shape
    tm = CMB_TM
    g2, b2 = g.reshape(1, d), b.reshape(1, d)
    return pl.pallas_call(
        _combine_kernel,
        grid_spec=pltpu.PrefetchScalarGridSpec(
            num_scalar_prefetch=1,
            grid=(t // tm,),
            in_specs=[pl.BlockSpec(memory_space=pl.ANY),
                      pl.BlockSpec((tm, LANES), lambda i, p: (i, 0)),
                      pl.BlockSpec((tm, d), lambda i, p: (i, 0)),
                      pl.BlockSpec((1, d), lambda i, p: (0, 0)),
                      pl.BlockSpec((1, d), lambda i, p: (0, 0))],
            out_specs=pl.BlockSpec((tm, d), lambda i, p: (i, 0)),
            scratch_shapes=[pltpu.VMEM((2, TOP_K, tm * ROW_TILES, LANES), F32), pltpu.SemaphoreType.DMA((2,))]),
        out_shape=jax.ShapeDtypeStruct((t, d), F32),
        compiler_params=_cparams(("arbitrary",)),
        name="combine",
    )(pos_flat, y, gate, x1, g2, b2)


def _swap_halves(w):
    half = w.shape[-1] // 2
    return jnp.concatenate([w[..., half:], w[..., :half]], axis=-1)


def _layout_w_in(w):
    d = w.shape[0]
    widths = (512, 128, 128, 256, 256, 512, 512, 8, 512, 256, 64)
    offs = [0]
    for wd in widths:
        offs.append(offs[-1] + wd)
    a_q, a_k, a_v, m_q, m_k, m_v, m_o, m_if, c_q, c_kv, c_kr = [w[:, offs[i]:offs[i + 1]] for i in range(len(widths))]
    z = lambda n: jnp.zeros((d, n), w.dtype)
    out = jnp.concatenate([a_q, m_v, m_o, c_q, m_q, m_k, c_kv, a_k, a_v,
                           c_kr, z(LANES - 64), m_if, z(LANES - 8), _swap_halves(c_kr), z(LANES - 64)], axis=1)
    return out.astype(BF16)


def _layout_w_uq(w):
    r = w.shape[0]
    w3 = w.reshape(r, C_HEADS, C_NOPE_DIM + C_ROPE_DIM)
    nope, rope = w3[..., :C_NOPE_DIM], w3[..., C_NOPE_DIM:]
    z = jnp.zeros((r, C_HEADS, LANES - C_ROPE_DIM), w.dtype)
    wa = jnp.concatenate([nope, rope, z], axis=-1).reshape(r, -1)
    wb = jnp.concatenate([_swap_halves(rope), z], axis=-1).reshape(r, -1)
    return wa.astype(BF16), wb.astype(BF16)


def _layout_w_ukv(w):
    r = w.shape[0]
    w3 = w.reshape(r, C_HEADS, C_NOPE_DIM + C_V_DIM)
    return jnp.concatenate([w3[..., :C_NOPE_DIM].reshape(r, -1), w3[..., C_NOPE_DIM:].reshape(r, -1)],
                           axis=-1).astype(BF16)


def _rope_tables(seq):
    dim = C_ROPE_DIM
    inv = 1.0 / (ROPE_THETA ** (jnp.arange(0, dim, 2, dtype=F32) / dim))
    ang = jnp.arange(seq, dtype=F32)[:, None] * inv[None, :]
    cos, sin = jnp.cos(ang), jnp.sin(ang)
    z = jnp.zeros((seq, LANES - dim), F32)
    return jnp.concatenate([cos, cos, z], axis=-1), jnp.concatenate([-sin, sin, z], axis=-1)


def _route(top_e, rank, counts, nsb):
    padded = (counts + MOE_PAD - 1) // MOE_PAD * MOE_PAD
    pad_end = jnp.cumsum(padded)
    pad_start = pad_end - padded
    is_e = top_e[:, :, None] == jnp.arange(N_EXPERTS, dtype=jnp.int32)[None, None, :]
    pos = (jnp.sum(jnp.where(is_e, pad_start[None, None, :], 0), axis=-1) + rank).reshape(-1).astype(jnp.int32)
    nblk = (pad_end[-1:] // MOE_PAD).astype(jnp.int32)
    nsb_e = (padded + MOE_TMAX - 1) // MOE_TMAX
    sb_end = jnp.cumsum(nsb_e)
    sb_start = sb_end - nsb_e
    sb_idx = jnp.arange(nsb + 1, dtype=jnp.int32)
    n_valid = sb_end[-1]
    sb_eff = jnp.minimum(sb_idx, n_valid - 1)
    sb_e = jnp.searchsorted(sb_end, sb_eff, side="right").astype(jnp.int32)
    part = sb_eff - sb_start[sb_e]
    sb_row0 = pad_start[sb_e] + part * MOE_TMAX
    sb_n = jnp.where(sb_idx < n_valid, jnp.minimum(padded[sb_e] - part * MOE_TMAX, MOE_TMAX), 0)
    tables = (counts, pad_start.astype(jnp.int32), padded.astype(jnp.int32), nblk)
    return pos, tables, sb_e, sb_row0.astype(jnp.int32), sb_n.astype(jnp.int32)


def kernel(x, w_in, conv_w, conv_b, m_gate_b, m_norm_g, sinks, q_norm_g, w_uq, kv_norm_g, w_ukv, w_out,
           ln1_g, ln1_b, w_router, b_router, w_gate_up, b_gate_up, w_down, b_down, ln2_g, ln2_b):
    b, s, d = x.shape
    t = b * s
    n_assign = t * TOP_K
    rows = (n_assign // MOE_PAD + N_EXPERTS) * MOE_PAD
    nsb = N_EXPERTS + rows // MOE_TMAX
    cos128, sin128 = _rope_tables(s)
    xt = x.reshape(t, d)
    for l in range(DEPTH):
        w_in_bf = _layout_w_in(w_in[l])
        wqa, wqb = _layout_w_uq(w_uq[l])
        wkv = _layout_w_ukv(w_ukv[l])
        w_out_bf = w_out[l].astype(BF16)
        wr128 = jnp.zeros((d, LANES), BF16).at[:, :N_EXPERTS].set(w_router[l].astype(BF16))
        br128 = jnp.zeros((1, LANES), F32).at[0, :N_EXPERTS].set(b_router[l])

        proj3 = _proj(xt, w_in_bf).reshape(b, s, N_PROJ)
        y_a = _swa(proj3, sinks[l])
        y_b = _mlstm(proj3, conv_w[l], conv_b[l], m_gate_b[l], m_norm_g[l])
        qc, kc, vc = _mla_prep(proj3, q_norm_g[l], kv_norm_g[l], wqa, wqb, wkv, cos128, sin128)
        y_c = _flash(qc, kc, vc)
        x1, x1rc, top_e, gate, rank, cnt = _outproj(y_a.reshape(t, -1), y_b.reshape(t, -1), y_c.reshape(t, -1), xt,
                                                    w_out_bf, ln1_g[l], ln1_b[l], wr128, br128)
        pos, tables, sb_e, sb_row0, sb_n = _route(top_e[:, :TOP_K], rank[:, :TOP_K], cnt[0, :N_EXPERTS], nsb)
        xs = _dispatch(pos, *tables, x1rc, rows)
        y = _moe(l, sb_e, sb_row0, sb_n, tables[3], xs, w_gate_up, b_gate_up, w_down, b_down)
        xt = _combine(pos, y, gate, x1, ln2_g[l], ln2_b[l])
    return xt.reshape(b, s, d)
```

```python
import functools

import jax
import jax.numpy as jnp
from jax import lax
from jax.experimental import pallas as pl
from jax.experimental.pallas import tpu as pltpu

F32 = jnp.float32
BF16 = jnp.bfloat16
NEG_INF = float("-inf")
LOG2_E = 1.4426950408889634

D_MODEL = 2048
DEPTH = 2
SWA_HEADS, SWA_KV_HEADS, SWA_HEAD_DIM, SWA_WINDOW = 8, 2, 64, 128
M_HEADS, M_QK_DIM, M_V_DIM, M_CONV = 4, 64, 128, 4
C_HEADS, C_NOPE_DIM, C_ROPE_DIM, C_V_DIM = 8, 128, 64, 128
C_Q_LORA, C_KV_LORA = 512, 256
ROPE_THETA = 10000.0
N_EXPERTS, TOP_K = 32, 4
SWIGLU_LIMIT, SWIGLU_ALPHA = 7.0, 1.702
DN_ALPHA = (2 * DEPTH) ** 0.25
LN_EPS, RMS_EPS = 1e-5, 1e-6

LANES = 128
SUBLANES = 8
VMEM_LIMIT = 56 * 1024 * 1024

P_AQ, P_MV, P_MO, P_CQ, P_MQK, P_CKV, P_AK, P_AV, P_CKR, P_MIF, P_CKRS = (
    0, 512, 1024, 1536, 2048, 2560, 2816, 2944, 3072, 3200, 3328)
N_PROJ = 3456

PROJ_TM, PROJ_TN = 512, 1152
M_CHUNK = 128
MLA_TM = 512
FLASH_BQ = 512
FLASH_HEADS = 4
OUT_TM = 512
MOE_PAD = 256
MOE_TMAX = 1536
MOE_TF = 256
CMB_TM = 64
DSP_TM = 64

ROW_TILES = D_MODEL // 2 // LANES
U32 = jnp.uint32
HI_MASK = 0xFFFF0000


def _cparams(sem, vmem=VMEM_LIMIT):
    return pltpu.CompilerParams(dimension_semantics=sem, vmem_limit_bytes=vmem)


def _bf16_bits(x):
    return lax.bitcast_convert_type(x.astype(BF16).astype(F32), U32)


def _store_rc(ref, val):
    n, d = val.shape
    words = (_bf16_bits(val[:, :d // 2]) >> 16) | (_bf16_bits(val[:, d // 2:]) & U32(HI_MASK))
    x = jnp.stack([words[:, s * LANES:(s + 1) * LANES] for s in range(ROW_TILES)], axis=0)
    ref[...] = pltpu.einshape("stl->tsl", x).reshape(n * ROW_TILES, LANES)


def _load_rc(ref, n):
    x = pltpu.einshape("tsl->stl", ref[...].reshape(n, ROW_TILES, LANES))
    words = jnp.concatenate([x[s] for s in range(ROW_TILES)], axis=-1)
    lo = lax.bitcast_convert_type(words << 16, F32)
    hi = lax.bitcast_convert_type(words & U32(HI_MASK), F32)
    return lo, hi


def _proj_kernel(x_ref, w_ref, o_ref):
    o_ref[...] = jnp.dot(x_ref[...].astype(BF16), w_ref[...], preferred_element_type=F32)


def _proj(x2d, w_bf):
    t, d = x2d.shape
    n = w_bf.shape[1]
    return pl.pallas_call(
        _proj_kernel,
        grid=(t // PROJ_TM, n // PROJ_TN),
        in_specs=[pl.BlockSpec((PROJ_TM, d), lambda i, j: (i, 0)),
                  pl.BlockSpec((d, PROJ_TN), lambda i, j: (0, j))],
        out_specs=pl.BlockSpec((PROJ_TM, PROJ_TN), lambda i, j: (i, j)),
        out_shape=jax.ShapeDtypeStruct((t, n), F32),
        compiler_params=_cparams(("parallel", "arbitrary")),
        name="proj",
    )(x2d, w_bf)


def _swa_kernel(sinks_ref, q_ref, kc_ref, kp_ref, vc_ref, vp_ref, o_ref):
    i = pl.program_id(1)
    w = SWA_WINDOW
    dh = SWA_HEAD_DIM
    grp = SWA_HEADS // SWA_KV_HEADS
    q = q_ref[0]
    qi = lax.broadcasted_iota(jnp.int32, (w, 2 * w), 0)
    kj = lax.broadcasted_iota(jnp.int32, (w, 2 * w), 1)
    rel = qi + w - kj
    has_prev = jnp.where(i > 0, 0, w)
    mask = (rel >= 0) & (rel < w) & (kj >= has_prev)
    outs = []
    for kvh in range(SWA_KV_HEADS):
        sl = slice(kvh * dh, (kvh + 1) * dh)
        k_cat = jnp.concatenate([kp_ref[0][:, sl], kc_ref[0][:, sl]], axis=0).astype(BF16)
        v_cat = jnp.concatenate([vp_ref[0][:, sl], vc_ref[0][:, sl]], axis=0).astype(BF16)
        for g in range(grp):
            h = kvh * grp + g
            qh = (q[:, h * dh:(h + 1) * dh] * (dh ** -0.5)).astype(BF16)
            s = lax.dot_general(qh, k_cat, (((1,), (1,)), ((), ())), preferred_element_type=F32)
            s = jnp.where(mask, s, NEG_INF)
            sink = sinks_ref[h]
            m = jnp.maximum(jnp.max(s, axis=-1, keepdims=True), sink)
            p = jnp.exp(s - m)
            den = jnp.sum(p, axis=-1, keepdims=True) + jnp.exp(sink - m)
            o = jnp.dot(p.astype(BF16), v_cat, preferred_element_type=F32)
            outs.append(o / den)
    o_ref[0] = jnp.concatenate(outs, axis=-1).astype(o_ref.dtype)


def _swa(proj3, sinks):
    b, s, _ = proj3.shape
    w = SWA_WINDOW
    nb = s // w
    kw = SWA_KV_HEADS * SWA_HEAD_DIM
    qw = SWA_HEADS * SWA_HEAD_DIM
    cur = lambda col: (lambda bi, i: (bi, i, col))
    prev = lambda col: (lambda bi, i: (bi, jnp.maximum(i - 1, 0), col))
    return pl.pallas_call(
        _swa_kernel,
        grid=(b, nb),
        in_specs=[pl.BlockSpec(memory_space=pltpu.SMEM),
                  pl.BlockSpec((1, w, qw), cur(P_AQ // qw)),
                  pl.BlockSpec((1, w, kw), cur(P_AK // kw)),
                  pl.BlockSpec((1, w, kw), prev(P_AK // kw)),
                  pl.BlockSpec((1, w, kw), cur(P_AV // kw)),
                  pl.BlockSpec((1, w, kw), prev(P_AV // kw))],
        out_specs=pl.BlockSpec((1, w, qw), lambda bi, i: (bi, i, 0)),
        out_shape=jax.ShapeDtypeStruct((b, s, qw), BF16),
        compiler_params=_cparams(("parallel", "arbitrary")),
        name="swa",
    )(sinks, proj3, proj3, proj3, proj3, proj3)


def _mlstm_kernel(qk_ref, v_ref, og_ref, g_ref, cw_ref, cb_ref, gb_ref, ng_ref, y_ref, xbuf, ct_ref, m_ref):
    c = pl.program_id(0)
    nb = qk_ref.shape[0]
    ln = M_CHUNK
    dqk, dv, nh = M_QK_DIM, M_V_DIM, M_HEADS
    halo = SUBLANES

    @pl.when(c == 0)
    def _init():
        xbuf[:, 0:halo, :] = jnp.zeros((nb, halo, xbuf.shape[2]), F32)
        ct_ref[...] = jnp.zeros(ct_ref.shape, F32)
        m_ref[...] = jnp.zeros(m_ref.shape, F32)

    row = lax.broadcasted_iota(jnp.int32, (ln, ln), 0)
    col = lax.broadcasted_iota(jnp.int32, (ln, ln), 1)
    tril = row >= col
    trilf = jnp.where(tril, 1.0, 0.0).astype(F32)
    ones_blk = jnp.where(lax.broadcasted_iota(jnp.int32, (ln, dv), 1) == 0, 1.0, 0.0).astype(F32)

    for b in range(nb):
        xbuf[b, halo:halo + ln, :] = qk_ref[b]
        conv = cb_ref[...]
        for j in range(M_CONV):
            off = halo - (M_CONV - 1) + j
            conv = conv + cw_ref[j:j + 1, :] * xbuf[b, off:off + ln, :]
        qk = conv * jax.nn.sigmoid(conv)
        xbuf[b, 0:halo, :] = qk_ref[b, ln - halo:ln, :]

        g = g_ref[b] + gb_ref[...]
        lf = jnp.minimum(g, 0.0) - jnp.log1p(jnp.exp(-jnp.abs(g)))
        cum = jnp.dot(trilf, lf, preferred_element_type=F32, precision=lax.Precision.HIGHEST)
        cum_t = cum.T
        g_t = g.T
        vv = v_ref[b]
        hs = []
        for h in range(nh):
            chain = b * nh + h
            q = (qk[:, h * dqk:(h + 1) * dqk] * (dqk ** -0.5)).astype(BF16)
            k = qk[:, nh * dqk + h * dqk: nh * dqk + (h + 1) * dqk].astype(BF16)
            v_ext = jnp.concatenate([vv[:, h * dv:(h + 1) * dv], ones_blk], axis=-1)
            bc_col = cum[:, nh + h:nh + h + 1]
            bc_row = cum_t[nh + h:nh + h + 1, :]
            i_col = g[:, h:h + 1]
            i_row = g_t[h:h + 1, :]
            m_prev = m_ref[chain:chain + 1, 0:1]
            dmat = jnp.where(tril, bc_col - bc_row + i_row, NEG_INF)
            m_inter = bc_col + m_prev
            m_j = jnp.maximum(m_inter, jnp.max(dmat, axis=-1, keepdims=True))
            w_intra = jnp.exp(dmat - m_j)
            w_inter = jnp.exp(m_inter - m_j)
            s = lax.dot_general(q, k, (((1,), (1,)), ((), ())), preferred_element_type=F32) * w_intra
            ct = ct_ref[chain]
            num_ext = (jnp.dot(s.astype(BF16), v_ext.astype(BF16), preferred_element_type=F32)
                       + w_inter * jnp.dot(q, ct.astype(BF16), preferred_element_type=F32))
            num = num_ext[:, :dv]
            nq = num_ext[:, dv:dv + 1]
            den = jnp.maximum(jnp.abs(nq), jnp.exp(-m_j))
            hs.append(num / den)
            m_new = m_j[ln - 1:ln, :]
            bc_last = bc_col[ln - 1:ln, :]
            w_s = jnp.exp(bc_last - bc_col + i_col - m_new)
            w_c = jnp.exp(bc_last + m_prev - m_new)
            upd = lax.dot_general(k, (w_s * v_ext).astype(BF16), (((0,), (0,)), ((), ())),
                                  preferred_element_type=F32)
            ct_ref[chain] = w_c * ct + upd
            m_ref[chain:chain + 1, :] = jnp.broadcast_to(m_new, (1, m_ref.shape[1]))
        og = og_ref[b]
        outs = []
        for h in range(nh):
            seg = jax.nn.sigmoid(og[:, h * dv:(h + 1) * dv]) * hs[h]
            mu = jnp.mean(seg, axis=-1, keepdims=True)
            cen = seg - mu
            var = jnp.mean(cen * cen, axis=-1, keepdims=True)
            outs.append(cen * lax.rsqrt(var + LN_EPS) * ng_ref[:, h * dv:(h + 1) * dv])
        y_ref[b] = jnp.concatenate(outs, axis=-1).astype(y_ref.dtype)


def _mlstm(proj3, conv_w, conv_b, gate_b, norm_g):
    b, s, _ = proj3.shape
    ln = M_CHUNK
    wq = 2 * M_HEADS * M_QK_DIM
    wv = M_HEADS * M_V_DIM
    blk = lambda width, off: pl.BlockSpec((b, ln, width), lambda c: (0, c, off // width))
    full = lambda a: pl.BlockSpec(a.shape, lambda c: (0,) * a.ndim)
    gate_b128 = jnp.zeros((1, LANES), F32).at[0, :2 * M_HEADS].set(gate_b)
    conv_b2 = conv_b.reshape(1, wq)
    norm_g2 = norm_g.reshape(1, wv)
    return pl.pallas_call(
        _mlstm_kernel,
        grid=(s // ln,),
        in_specs=[blk(wq, P_MQK), blk(wv, P_MV), blk(wv, P_MO), blk(LANES, P_MIF),
                  full(conv_w), full(conv_b2), full(gate_b128), full(norm_g2)],
        out_specs=pl.BlockSpec((b, ln, wv), lambda c: (0, c, 0)),
        out_shape=jax.ShapeDtypeStruct((b, s, wv), BF16),
        scratch_shapes=[pltpu.VMEM((b, SUBLANES + ln, wq), F32),
                        pltpu.VMEM((b * M_HEADS, M_QK_DIM, 2 * M_V_DIM), F32),
                        pltpu.VMEM((b * M_HEADS, LANES), F32)],
        compiler_params=_cparams(("arbitrary",)),
        name="mlstm",
    )(proj3, proj3, proj3, proj3, conv_w, conv_b2, gate_b128, norm_g2)


def _mla_prep_kernel(cq_ref, ckv_ref, kr_ref, krs_ref, qg_ref, kvg_ref, wqa_ref, wqb_ref, wkv_ref,
                     cos_ref, sin_ref, q_out, k_out, v_out):
    nd, hd = C_NOPE_DIM, C_HEADS
    scale = (C_NOPE_DIM + C_ROPE_DIM) ** -0.5 * LOG2_E
    cos = cos_ref[...]
    sin = sin_ref[...]

    cq = cq_ref[0]
    qn = (cq * lax.rsqrt(jnp.mean(cq * cq, axis=-1, keepdims=True) + RMS_EPS) * qg_ref[...]).astype(BF16)
    qa = jnp.dot(qn, wqa_ref[...], preferred_element_type=F32)
    qb = jnp.dot(qn, wqb_ref[...], preferred_element_type=F32)
    for h in range(hd):
        nope = qa[:, h * 2 * nd: h * 2 * nd + nd]
        rope = qa[:, h * 2 * nd + nd:(h + 1) * 2 * nd] * cos + qb[:, h * nd:(h + 1) * nd] * sin
        q_out[0, h] = (jnp.concatenate([nope, rope], axis=-1) * scale).astype(q_out.dtype)

    ckv = ckv_ref[0]
    kvn = (ckv * lax.rsqrt(jnp.mean(ckv * ckv, axis=-1, keepdims=True) + RMS_EPS) * kvg_ref[...]).astype(BF16)
    kv = jnp.dot(kvn, wkv_ref[...], preferred_element_type=F32)
    kr = kr_ref[0] * cos + krs_ref[0] * sin
    ones_blk = jnp.where(lax.broadcasted_iota(jnp.int32, kr.shape, 1) == 0, 1.0, 0.0).astype(F32)
    for h in range(hd):
        k_out[0, h] = jnp.concatenate([kv[:, h * nd:(h + 1) * nd], kr], axis=-1).astype(k_out.dtype)
        v_h = kv[:, hd * nd + h * nd: hd * nd + (h + 1) * nd]
        v_out[0, h] = jnp.concatenate([v_h, ones_blk], axis=-1).astype(v_out.dtype)


def _mla_prep(proj3, q_norm_g, kv_norm_g, wqa, wqb, wkv, cos128, sin128):
    b, s, _ = proj3.shape
    tm = MLA_TM
    hd, nd = C_HEADS, C_NOPE_DIM
    blk = lambda width, off: pl.BlockSpec((1, tm, width), lambda bi, i: (bi, i, off // width))
    full = lambda a: pl.BlockSpec(a.shape, lambda bi, i: (0,) * a.ndim)
    tab = pl.BlockSpec((tm, LANES), lambda bi, i: (i, 0))
    qg = q_norm_g.reshape(1, -1)
    kvg = kv_norm_g.reshape(1, -1)
    head_out = lambda width: pl.BlockSpec((1, hd, tm, width), lambda bi, i: (bi, 0, i, 0))
    return pl.pallas_call(
        _mla_prep_kernel,
        grid=(b, s // tm),
        in_specs=[blk(C_Q_LORA, P_CQ), blk(C_KV_LORA, P_CKV), blk(LANES, P_CKR), blk(LANES, P_CKRS),
                  full(qg), full(kvg), full(wqa), full(wqb), full(wkv), tab, tab],
        out_specs=[head_out(2 * nd), head_out(2 * nd), head_out(C_V_DIM + LANES)],
        out_shape=[jax.ShapeDtypeStruct((b, hd, s, 2 * nd), BF16),
                   jax.ShapeDtypeStruct((b, hd, s, 2 * nd), BF16),
                   jax.ShapeDtypeStruct((b, hd, s, C_V_DIM + LANES), BF16)],
        compiler_params=_cparams(("parallel", "arbitrary")),
        name="mla_prep",
    )(proj3, proj3, proj3, proj3, qg, kvg, wqa, wqb, wkv, cos128, sin128)


def _flash_kernel(q_ref, k_ref, v_ref, o_ref):
    qi = pl.program_id(2)
    bq = FLASH_BQ
    nh = q_ref.shape[1]
    dv = C_V_DIM

    def step(j, carries, diag):
        start = pl.multiple_of(j * bq, bq)
        out = []
        for h in range(nh):
            m, acc = carries[h]
            ks = k_ref[0, h, pl.ds(start, bq), :]
            vs = v_ref[0, h, pl.ds(start, bq), :]
            s = lax.dot_general(q_ref[0, h], ks, (((1,), (1,)), ((), ())), preferred_element_type=F32)
            if diag:
                row = lax.broadcasted_iota(jnp.int32, (bq, bq), 0)
                col = lax.broadcasted_iota(jnp.int32, (bq, bq), 1)
                s = jnp.where(row >= col, s, NEG_INF)
            m_new = jnp.maximum(m, jnp.max(s, axis=-1, keepdims=True))
            p = jnp.exp2(s - m_new)
            acc = jnp.exp2(m - m_new) * acc + jnp.dot(p.astype(BF16), vs, preferred_element_type=F32)
            out.append((m_new, acc))
        return tuple(out)

    init = tuple((jnp.full((bq, 1), NEG_INF, F32), jnp.zeros((bq, v_ref.shape[3]), F32)) for _ in range(nh))
    carries = lax.fori_loop(0, qi, lambda j, cr: step(j, cr, False), init)
    carries = step(qi, carries, True)
    o_ref[0] = jnp.concatenate([acc[:, :dv] / acc[:, dv:dv + 1] for _, acc in carries], axis=-1).astype(o_ref.dtype)


def _flash(qc, kc, vc):
    b, hd, s, dk = qc.shape
    dv = C_V_DIM
    bq = FLASH_BQ
    nh = FLASH_HEADS
    return pl.pallas_call(
        _flash_kernel,
        grid=(b, hd // nh, s // bq),
        in_specs=[pl.BlockSpec((1, nh, bq, dk), lambda bi, h, i: (bi, h, i, 0)),
                  pl.BlockSpec((1, nh, s, dk), lambda bi, h, i: (bi, h, 0, 0)),
                  pl.BlockSpec((1, nh, s, vc.shape[3]), lambda bi, h, i: (bi, h, 0, 0))],
        out_specs=pl.BlockSpec((1, bq, nh * dv), lambda bi, h, i: (bi, i, h)),
        out_shape=jax.ShapeDtypeStruct((b, s, hd * dv), BF16),
        compiler_params=_cparams(("parallel", "parallel", "arbitrary")),
        name="flash",
    )(qc, kc, vc)


def _layer_norm(z, g, b):
    mu = jnp.mean(z, axis=-1, keepdims=True)
    cen = z - mu
    var = jnp.mean(cen * cen, axis=-1, keepdims=True)
    return cen * lax.rsqrt(var + LN_EPS) * g + b


def _outproj_kernel(ya_ref, yb_ref, yc_ref, x_ref, w_ref, g_ref, b_ref, wr_ref, br_ref,
                    x1_ref, x1rc_ref, e_ref, gate_ref, rank_ref, cnt_ref, cnt_sc):
    i = pl.program_id(0)
    tm = x_ref.shape[0]
    wa, wb = ya_ref.shape[1], yb_ref.shape[1]
    mix = jnp.dot(ya_ref[...], w_ref[0:wa, :], preferred_element_type=F32)
    mix = mix + jnp.dot(yb_ref[...], w_ref[wa:wa + wb, :], preferred_element_type=F32)
    mix = mix + jnp.dot(yc_ref[...], w_ref[wa + wb:, :], preferred_element_type=F32)
    x1 = _layer_norm(DN_ALPHA * x_ref[...] + mix, g_ref[...], b_ref[...])
    x1_ref[...] = x1
    _store_rc(x1rc_ref, x1)

    logits = jnp.dot(x1.astype(BF16), wr_ref[...], preferred_element_type=F32) + br_ref[...]
    lane = lax.broadcasted_iota(jnp.int32, logits.shape, 1)
    logits = jnp.where(lane < N_EXPERTS, logits, NEG_INF)
    e_out = jnp.zeros(logits.shape, jnp.int32)
    p_out = jnp.zeros(logits.shape, F32)
    top = None
    den = None
    onehots = []
    for r in range(TOP_K):
        mx = jnp.max(logits, axis=-1, keepdims=True)
        idx = jnp.min(jnp.where(logits == mx, lane, LANES), axis=-1, keepdims=True)
        if r == 0:
            top = mx
        p = jnp.exp(mx - top)
        den = p if r == 0 else den + p
        sel = lane == idx
        onehots.append(jnp.where(sel, 1.0, 0.0).astype(F32))
        e_out = jnp.where(lane == r, idx, e_out)
        p_out = jnp.where(lane == r, p, p_out)
        logits = jnp.where(sel, NEG_INF, logits)
    e_ref[...] = e_out
    gate_ref[...] = p_out / den

    @pl.when(i == 0)
    def _init():
        cnt_sc[...] = jnp.zeros(cnt_sc.shape, F32)

    oh_sum = onehots[0] + onehots[1] + onehots[2] + onehots[3]
    row = lax.broadcasted_iota(jnp.int32, (tm, tm), 0)
    col = lax.broadcasted_iota(jnp.int32, (tm, tm), 1)
    before = jnp.where(row > col, 1.0, 0.0).astype(BF16)
    base = jnp.dot(before, oh_sum.astype(BF16), preferred_element_type=F32) + cnt_sc[0:1, :]
    rank_out = jnp.zeros(logits.shape, jnp.int32)
    for r in range(TOP_K):
        rk = jnp.sum(onehots[r] * base, axis=-1, keepdims=True)
        rank_out = jnp.where(lane == r, rk.astype(jnp.int32), rank_out)
    rank_ref[...] = rank_out
    total = cnt_sc[0:1, :] + jnp.sum(oh_sum, axis=0, keepdims=True)
    cnt_sc[0:1, :] = total
    cnt_ref[...] = jnp.broadcast_to(total, cnt_ref.shape).astype(jnp.int32)


def _outproj(ya, yb, yc, x2d, w_bf, g, b, wr128, br128):
    t, d = x2d.shape
    tm = OUT_TM
    rows = lambda a: pl.BlockSpec((tm, a.shape[1]), lambda i: (i, 0))
    full = lambda a: pl.BlockSpec(a.shape, lambda i: (0,) * a.ndim, pipeline_mode=pl.Buffered(1))
    g2, b2 = g.reshape(1, d), b.reshape(1, d)
    lane_blk = pl.BlockSpec((tm, LANES), lambda i: (i, 0))
    return pl.pallas_call(
        _outproj_kernel,
        grid=(t // tm,),
        in_specs=[rows(ya), rows(yb), rows(yc), rows(x2d), full(w_bf), full(g2), full(b2), full(wr128), full(br128)],
        out_specs=[pl.BlockSpec((tm, d), lambda i: (i, 0)),
                   pl.BlockSpec((tm * ROW_TILES, LANES), lambda i: (i, 0)),
                   lane_blk, lane_blk, lane_blk,
                   pl.BlockSpec((SUBLANES, LANES), lambda i: (0, 0))],
        out_shape=[jax.ShapeDtypeStruct((t, d), F32),
                   jax.ShapeDtypeStruct((t * ROW_TILES, LANES), U32),
                   jax.ShapeDtypeStruct((t, LANES), jnp.int32),
                   jax.ShapeDtypeStruct((t, LANES), F32),
                   jax.ShapeDtypeStruct((t, LANES), jnp.int32),
                   jax.ShapeDtypeStruct((SUBLANES, LANES), jnp.int32)],
        scratch_shapes=[pltpu.VMEM((SUBLANES, LANES), F32)],
        compiler_params=_cparams(("arbitrary",)),
        name="outproj",
    )(ya, yb, yc, x2d, w_bf, g2, b2, wr128, br128)


def _rc_rows(row, n=1):
    return pl.ds(pl.multiple_of(row * ROW_TILES, ROW_TILES), n * ROW_TILES)


def _dispatch_kernel(pos_ref, cnt_ref, pst_ref, pad_ref, nblk_ref, x_hbm, xs_hbm, buf, zbuf, sem_in, sem_out, sem_z):
    i = pl.program_id(0)
    n = pl.num_programs(0)
    tm = DSP_TM
    nslot = buf.shape[0]

    def in_copy(blk, slot):
        return pltpu.make_async_copy(x_hbm.at[_rc_rows(blk * tm, tm)], buf.at[slot], sem_in.at[slot])

    def row_out(slot, t, p):
        return pltpu.make_async_copy(buf.at[slot, _rc_rows(t)], xs_hbm.at[_rc_rows(p)], sem_out.at[slot])

    def wait_outs(slot):
        for _ in range(TOP_K):
            pltpu.make_async_copy(buf.at[slot], xs_hbm.at[_rc_rows(0, tm)], sem_out.at[slot]).wait()

    slot = lax.rem(i, nslot)

    @pl.when(i == 0)
    def _first_in():
        in_copy(0, 0).start()

    @pl.when(i >= 2)
    def _free_slot():
        wait_outs(lax.rem(i + 1, nslot))

    @pl.when(i + 1 < n)
    def _next_in():
        in_copy(i + 1, lax.rem(i + 1, nslot)).start()

    in_copy(i, slot).wait()
    base = i * (tm * TOP_K)

    def issue(t, _):
        for k in range(TOP_K):
            row_out(slot, t, pos_ref[base + t * TOP_K + k]).start()
        return 0

    lax.fori_loop(0, tm, issue, 0, unroll=2)

    @pl.when(i == n - 1)
    def _finish():
        wait_outs(lax.rem(i + 2, nslot))
        wait_outs(slot)
        zbuf[...] = jnp.zeros(zbuf.shape, zbuf.dtype)

        def pad_rows(e, _):
            cnt = cnt_ref[e]
            first = pst_ref[e] + cnt
            npad = pad_ref[e] - cnt

            def zero_row(q):
                return pltpu.make_async_copy(zbuf.at[_rc_rows(0)], xs_hbm.at[_rc_rows(first + q)], sem_z)

            def start(q, c):
                zero_row(q).start()
                return c

            def wait(q, c):
                zero_row(q).wait()
                return c

            lax.fori_loop(0, npad, start, 0)
            lax.fori_loop(0, npad, wait, 0)
            return 0

        lax.fori_loop(0, N_EXPERTS, pad_rows, 0)

        def tail_block(bk, _):
            cp = pltpu.make_async_copy(zbuf, xs_hbm.at[_rc_rows(bk * MOE_PAD, MOE_PAD)], sem_z)
            cp.start()
            cp.wait()
            return 0

        lax.fori_loop(nblk_ref[0], xs_hbm.shape[0] // (MOE_PAD * ROW_TILES), tail_block, 0)


def _dispatch(pos, counts, pad_start, padded, nblk, x1rc, rows):
    t = x1rc.shape[0] // ROW_TILES
    tm = DSP_TM
    return pl.pallas_call(
        _dispatch_kernel,
        grid_spec=pltpu.PrefetchScalarGridSpec(
            num_scalar_prefetch=5,
            grid=(t // tm,),
            in_specs=[pl.BlockSpec(memory_space=pl.ANY)],
            out_specs=pl.BlockSpec(memory_space=pl.ANY),
            scratch_shapes=[pltpu.VMEM((3, tm * ROW_TILES, LANES), U32),
                            pltpu.VMEM((MOE_PAD * ROW_TILES, LANES), U32),
                            pltpu.SemaphoreType.DMA((3,)),
                            pltpu.SemaphoreType.DMA((3,)),
                            pltpu.SemaphoreType.DMA(())]),
        out_shape=jax.ShapeDtypeStruct((rows * ROW_TILES, LANES), U32),
        compiler_params=_cparams(("arbitrary",)),
        name="dispatch",
    )(pos, counts, pad_start, padded, nblk, x1rc)


def _moe_kernel(sbe_ref, sbr_ref, sbn_ref, nblk_ref, xs_hbm, wg_ref, wu_ref, bg_ref, bu_ref, wd_ref, bd_ref, y_hbm,
                xbuf, acc, stage_in, stage_out, wg_bf, wu_bf, wd_bf, sem_in, sem_out):
    sb = pl.program_id(0)
    j = pl.program_id(1)
    nsb = pl.num_programs(0)
    nj = pl.num_programs(1)
    n = sbn_ref[sb]
    row0 = sbr_ref[sb]
    nchunk = n // MOE_PAD
    cur = lax.rem(sb, 2)
    big = 2 * MOE_PAD

    def chunk_rows(c):
        return pl.ds(pl.multiple_of(c * MOE_PAD, MOE_PAD), MOE_PAD)

    def chunk_in(first_row, c):
        return pltpu.make_async_copy(xs_hbm.at[_rc_rows(first_row + c * MOE_PAD, MOE_PAD)], stage_in, sem_in)

    def chunk_out(c, slot):
        return pltpu.make_async_copy(stage_out.at[slot], y_hbm.at[_rc_rows(row0 + c * MOE_PAD, MOE_PAD)],
                                     sem_out.at[slot])

    def wait_outs(count):
        @pl.when(count >= 2)
        def _older():
            chunk_out(0, lax.rem(count, 2)).wait()

        @pl.when(count >= 1)
        def _newest():
            chunk_out(0, lax.rem(count + 1, 2)).wait()

    def convert_in(slot, c):
        lo, hi = _load_rc(stage_in, MOE_PAD)
        xbuf[slot, chunk_rows(c), :] = jnp.concatenate([lo.astype(BF16), hi.astype(BF16)], axis=-1)

    @pl.when((sb == 0) & (j == 0))
    def _load_first():
        def load(c, _):
            chunk_in(row0, c).start()
            chunk_in(row0, c).wait()
            convert_in(0, c)
            return 0

        lax.fori_loop(0, nchunk, load, 0)

    n_next = sbn_ref[sb + 1]
    fetch_next = j * MOE_PAD < n_next

    @pl.when(fetch_next)
    def _start_next():
        chunk_in(sbr_ref[sb + 1], j).start()

    @pl.when((j == 1) & (sb > 0))
    def _drain_prev_outs():
        wait_outs(sbn_ref[jnp.maximum(sb - 1, 0)] // MOE_PAD)

    @pl.when((j == 0) & (n > 0))
    def _init_acc():
        def init(c, _):
            acc[chunk_rows(c), :] = jnp.broadcast_to(bd_ref[...], (MOE_PAD, acc.shape[1]))
            return 0

        lax.fori_loop(0, nchunk, init, 0)

    def ffn(start, size, wg, wu, wd):
        rs = pl.ds(start, size)
        xr = xbuf[cur, rs, :]
        gt = jnp.dot(xr, wg, preferred_element_type=F32) + bg_ref[...]
        up = jnp.dot(xr, wu, preferred_element_type=F32) + bu_ref[...]
        gt = jnp.minimum(gt, SWIGLU_LIMIT)
        up = jnp.clip(up, -SWIGLU_LIMIT, SWIGLU_LIMIT)
        act = (up + 1.0) * gt * jax.nn.sigmoid(SWIGLU_ALPHA * gt)
        acc[rs, :] = acc[rs, :] + jnp.dot(act.astype(BF16), wd, preferred_element_type=F32)

    def first_chunk(size):
        wg = wg_ref[...].astype(BF16)
        wu = wu_ref[...].astype(BF16)
        wd = wd_ref[...].astype(BF16)
        wg_bf[...] = wg
        wu_bf[...] = wu
        wd_bf[...] = wd
        ffn(0, size, wg, wu, wd)

    @pl.when(n >= big)
    def _rows_big():
        first_chunk(big)

        def body(c, _):
            ffn(pl.multiple_of(c * big, big), big, wg_bf[...], wu_bf[...], wd_bf[...])
            return 0

        lax.fori_loop(1, n // big, body, 0)

        @pl.when(lax.rem(n, big) != 0)
        def _odd_chunk():
            ffn(pl.multiple_of((n // big) * big, big), MOE_PAD, wg_bf[...], wu_bf[...], wd_bf[...])

    @pl.when((n > 0) & (n < big))
    def _rows_small():
        first_chunk(MOE_PAD)

    @pl.when((j == nj - 1) & (n > 0))
    def _store_rows():
        def store(c, _):
            slot = lax.rem(c, 2)

            @pl.when(c >= 2)
            def _slot_free():
                chunk_out(c - 2, slot).wait()

            _store_rc(stage_out.at[slot], acc[chunk_rows(c), :])
            chunk_out(c, slot).start()
            return 0

        lax.fori_loop(0, nchunk, store, 0)

        @pl.when(sb == nsb - 1)
        def _no_later_step():
            wait_outs(nchunk)

    @pl.when(fetch_next)
    def _finish_next():
        chunk_in(sbr_ref[sb + 1], j).wait()
        convert_in(1 - cur, j)

    @pl.when((sb == nsb - 1) & (j == nj - 1))
    def _zero_tail():
        stage_out[0] = jnp.zeros(stage_out.shape[1:], stage_out.dtype)

        def tail_block(bk, _):
            cp = pltpu.make_async_copy(stage_out.at[0], y_hbm.at[_rc_rows(bk * MOE_PAD, MOE_PAD)], sem_out.at[0])
            cp.start()
            cp.wait()
            return 0

        lax.fori_loop(nblk_ref[0], y_hbm.shape[0] // (MOE_PAD * ROW_TILES), tail_block, 0)


def _moe(layer, sb_e, sb_row0, sb_n, nblk, xs, w_gate_up, b_gate_up, w_down, b_down):
    rows = xs.shape[0] // ROW_TILES
    d = D_MODEL
    dff = w_down.shape[2]
    tf = MOE_TF
    nj = dff // tf
    nsb = sb_e.shape[0] - 1
    assert nj >= 2 and MOE_TMAX // MOE_PAD <= nj and MOE_TMAX % (2 * MOE_PAD) == 0
    bgu = b_gate_up.reshape(DEPTH, N_EXPERTS, 1, 2 * dff)
    bdn = b_down.reshape(DEPTH, N_EXPERTS, 1, d)

    def jeff(sb, j, n_ref):
        return jnp.where(n_ref[sb] > 0, j, nj - 1)

    gate_map = lambda sb, j, e, r, n, nb: (layer, e[sb], 0, jeff(sb, j, n))
    up_map = lambda sb, j, e, r, n, nb: (layer, e[sb], 0, nj + jeff(sb, j, n))
    down_map = lambda sb, j, e, r, n, nb: (layer, e[sb], jeff(sb, j, n), 0)
    bd_map = lambda sb, j, e, r, n, nb: (layer, e[sb], 0, 0)
    return pl.pallas_call(
        _moe_kernel,
        grid_spec=pltpu.PrefetchScalarGridSpec(
            num_scalar_prefetch=4,
            grid=(nsb, nj),
            in_specs=[pl.BlockSpec(memory_space=pl.ANY),
                      pl.BlockSpec((None, None, d, tf), gate_map),
                      pl.BlockSpec((None, None, d, tf), up_map),
                      pl.BlockSpec((None, None, 1, tf), gate_map),
                      pl.BlockSpec((None, None, 1, tf), up_map),
                      pl.BlockSpec((None, None, tf, d), down_map),
                      pl.BlockSpec((None, None, 1, d), bd_map)],
            out_specs=pl.BlockSpec(memory_space=pl.ANY),
            scratch_shapes=[pltpu.VMEM((2, MOE_TMAX, d), BF16),
                            pltpu.VMEM((MOE_TMAX, d), F32),
                            pltpu.VMEM((MOE_PAD * ROW_TILES, LANES), U32),
                            pltpu.VMEM((2, MOE_PAD * ROW_TILES, LANES), U32),
                            pltpu.VMEM((d, tf), BF16),
                            pltpu.VMEM((d, tf), BF16),
                            pltpu.VMEM((tf, d), BF16),
                            pltpu.SemaphoreType.DMA(()),
                            pltpu.SemaphoreType.DMA((2,))]),
        out_shape=jax.ShapeDtypeStruct((rows * ROW_TILES, LANES), U32),
        compiler_params=_cparams(("arbitrary", "arbitrary")),
        name="moe",
    )(sb_e, sb_row0, sb_n, nblk, xs, w_gate_up, w_gate_up, bgu, bgu, w_down, bdn)


def _combine_kernel(pos_ref, y_hbm, gate_ref, x1_ref, g_ref, b_ref, o_ref, buf, sem):
    i = pl.program_id(0)
    n = pl.num_programs(0)
    tm = CMB_TM
    slot = lax.rem(i, 2)

    def issue(blk, dst_slot):
        base = blk * (tm * TOP_K)

        def body(t, _):
            for k in range(TOP_K):
                p = pos_ref[base + t * TOP_K + k]
                pltpu.make_async_copy(y_hbm.at[_rc_rows(p)], buf.at[dst_slot, k, _rc_rows(t)], sem.at[dst_slot]).start()
            return 0

        lax.fori_loop(0, tm, body, 0, unroll=2)

    @pl.when(i == 0)
    def _first():
        issue(0, 0)

    @pl.when(i + 1 < n)
    def _next():
        issue(i + 1, 1 - slot)

    for k in range(TOP_K):
        pltpu.make_async_copy(y_hbm.at[_rc_rows(0, tm)], buf.at[slot, k], sem.at[slot]).wait()
    gate = gate_ref[...]
    ffn_lo, ffn_hi = None, None
    for k in range(TOP_K):
        lo, hi = _load_rc(buf.at[slot, k], tm)
        gk = gate[:, k:k + 1]
        ffn_lo = gk * lo if k == 0 else ffn_lo + gk * lo
        ffn_hi = gk * hi if k == 0 else ffn_hi + gk * hi
    ffn = jnp.concatenate([ffn_lo, ffn_hi], axis=-1)
    o_ref[...] = _layer_norm(DN_ALPHA * x1_ref[...] + ffn, g_ref[...], b_ref[...])


def _combine(pos_flat, y, gate, x1, g, b):
    t, d = x1.shape
    tm = CMB_TM
    g2, b2 = g.reshape(1, d), b.reshape(1, d)
    return pl.pallas_call(
        _combine_kernel,
        grid_spec=pltpu.PrefetchScalarGridSpec(
            num_scalar_prefetch=1,
            grid=(t // tm,),
            in_specs=[pl.BlockSpec(memory_space=pl.ANY),
                      pl.BlockSpec((tm, LANES), lambda i, p: (i, 0)),
                      pl.BlockSpec((tm, d), lambda i, p: (i, 0)),
                      pl.BlockSpec((1, d), lambda i, p: (0, 0)),
                      pl.BlockSpec((1, d), lambda i, p: (0, 0))],
            out_specs=pl.BlockSpec((tm, d), lambda i, p: (i, 0)),
            scratch_shapes=[pltpu.VMEM((2, TOP_K, tm * ROW_TILES, LANES), U32), pltpu.SemaphoreType.DMA((2,))]),
        out_shape=jax.ShapeDtypeStruct((t, d), F32),
        compiler_params=_cparams(("arbitrary",)),
        name="combine",
    )(pos_flat, y, gate, x1, g2, b2)


def _swap_halves(w):
    half = w.shape[-1] // 2
    return jnp.concatenate([w[..., half:], w[..., :half]], axis=-1)


def _layout_w_in(w):
    d = w.shape[0]
    widths = (512, 128, 128, 256, 256, 512, 512, 8, 512, 256, 64)
    offs = [0]
    for wd in widths:
        offs.append(offs[-1] + wd)
    a_q, a_k, a_v, m_q, m_k, m_v, m_o, m_if, c_q, c_kv, c_kr = [w[:, offs[i]:offs[i + 1]] for i in range(len(widths))]
    z = lambda n: jnp.zeros((d, n), w.dtype)
    out = jnp.concatenate([a_q, m_v, m_o, c_q, m_q, m_k, c_kv, a_k, a_v,
                           c_kr, z(LANES - 64), m_if, z(LANES - 8), _swap_halves(c_kr), z(LANES - 64)], axis=1)
    return out.astype(BF16)


def _layout_w_uq(w):
    r = w.shape[0]
    w3 = w.reshape(r, C_HEADS, C_NOPE_DIM + C_ROPE_DIM)
    nope, rope = w3[..., :C_NOPE_DIM], w3[..., C_NOPE_DIM:]
    z = jnp.zeros((r, C_HEADS, LANES - C_ROPE_DIM), w.dtype)
    wa = jnp.concatenate([nope, rope, z], axis=-1).reshape(r, -1)
    wb = jnp.concatenate([_swap_halves(rope), z], axis=-1).reshape(r, -1)
    return wa.astype(BF16), wb.astype(BF16)


def _layout_w_ukv(w):
    r = w.shape[0]
    w3 = w.reshape(r, C_HEADS, C_NOPE_DIM + C_V_DIM)
    return jnp.concatenate([w3[..., :C_NOPE_DIM].reshape(r, -1), w3[..., C_NOPE_DIM:].reshape(r, -1)],
                           axis=-1).astype(BF16)


def _rope_tables(seq):
    dim = C_ROPE_DIM
    inv = 1.0 / (ROPE_THETA ** (jnp.arange(0, dim, 2, dtype=F32) / dim))
    ang = jnp.arange(seq, dtype=F32)[:, None] * inv[None, :]
    cos, sin = jnp.cos(ang), jnp.sin(ang)
    z = jnp.zeros((seq, LANES - dim), F32)
    return jnp.concatenate([cos, cos, z], axis=-1), jnp.concatenate([-sin, sin, z], axis=-1)


def _route(top_e, rank, counts, nsb):
    padded = (counts + MOE_PAD - 1) // MOE_PAD * MOE_PAD
    pad_end = jnp.cumsum(padded)
    pad_start = pad_end - padded
    is_e = top_e[:, :, None] == jnp.arange(N_EXPERTS, dtype=jnp.int32)[None, None, :]
    pos = (jnp.sum(jnp.where(is_e, pad_start[None, None, :], 0), axis=-1) + rank).reshape(-1).astype(jnp.int32)
    nblk = (pad_end[-1:] // MOE_PAD).astype(jnp.int32)
    nsb_e = (padded + MOE_TMAX - 1) // MOE_TMAX
    sb_end = jnp.cumsum(nsb_e)
    sb_start = sb_end - nsb_e
    sb_idx = jnp.arange(nsb + 1, dtype=jnp.int32)
    n_valid = sb_end[-1]
    sb_eff = jnp.minimum(sb_idx, n_valid - 1)
    sb_e = jnp.searchsorted(sb_end, sb_eff, side="right").astype(jnp.int32)
    part = sb_eff - sb_start[sb_e]
    sb_row0 = pad_start[sb_e] + part * MOE_TMAX
    sb_n = jnp.where(sb_idx < n_valid, jnp.minimum(padded[sb_e] - part * MOE_TMAX, MOE_TMAX), 0)
    tables = (counts, pad_start.astype(jnp.int32), padded.astype(jnp.int32), nblk)
    return pos, tables, sb_e, sb_row0.astype(jnp.int32), sb_n.astype(jnp.int32)


def kernel(x, w_in, conv_w, conv_b, m_gate_b, m_norm_g, sinks, q_norm_g, w_uq, kv_norm_g, w_ukv, w_out,
           ln1_g, ln1_b, w_router, b_router, w_gate_up, b_gate_up, w_down, b_down, ln2_g, ln2_b):
    b, s, d = x.shape
    t = b * s
    n_assign = t * TOP_K
    rows = (n_assign // MOE_PAD + N_EXPERTS) * MOE_PAD
    nsb = N_EXPERTS + rows // MOE_TMAX
    cos128, sin128 = _rope_tables(s)
    xt = x.reshape(t, d)
    for l in range(DEPTH):
        w_in_bf = _layout_w_in(w_in[l])
        wqa, wqb = _layout_w_uq(w_uq[l])
        wkv = _layout_w_ukv(w_ukv[l])
        w_out_bf = w_out[l].astype(BF16)
        wr128 = jnp.zeros((d, LANES), BF16).at[:, :N_EXPERTS].set(w_router[l].astype(BF16))
        br128 = jnp.zeros((1, LANES), F32).at[0, :N_EXPERTS].set(b_router[l])

        proj3 = _proj(xt, w_in_bf).reshape(b, s, N_PROJ)
        y_a = _swa(proj3, sinks[l])
        y_b = _mlstm(proj3, conv_w[l], conv_b[l], m_gate_b[l], m_norm_g[l])
        qc, kc, vc = _mla_prep(proj3, q_norm_g[l], kv_norm_g[l], wqa, wqb, wkv, cos128, sin128)
        y_c = _flash(qc, kc, vc)
        x1, x1rc, top_e, gate, rank, cnt = _outproj(y_a.reshape(t, -1), y_b.reshape(t, -1), y_c.reshape(t, -1), xt,
                                                    w_out_bf, ln1_g[l], ln1_b[l], wr128, br128)
        pos, tables, sb_e, sb_row0, sb_n = _route(top_e[:, :TOP_K], rank[:, :TOP_K], cnt[0, :N_EXPERTS], nsb)
        xs = _dispatch(pos, *tables, x1rc, rows)
        y = _moe(l, sb_e, sb_row0, sb_n, tables[3], xs, w_gate_up, b_gate_up, w_down, b_down)
        xt = _combine(pos, y, gate, x1, ln2_g[l], ln2_b[l])
    return xt.reshape(b, s, d)
```

```python
import functools

import jax
import jax.numpy as jnp
from jax import lax
from jax.experimental import pallas as pl
from jax.experimental.pallas import tpu as pltpu

F32 = jnp.float32
BF16 = jnp.bfloat16
NEG_INF = float("-inf")
LOG2_E = 1.4426950408889634

D_MODEL = 2048
DEPTH = 2
SWA_HEADS, SWA_KV_HEADS, SWA_HEAD_DIM, SWA_WINDOW = 8, 2, 64, 128
M_HEADS, M_QK_DIM, M_V_DIM, M_CONV = 4, 64, 128, 4
C_HEADS, C_NOPE_DIM, C_ROPE_DIM, C_V_DIM = 8, 128, 64, 128
C_Q_LORA, C_KV_LORA = 512, 256
ROPE_THETA = 10000.0
N_EXPERTS, TOP_K = 32, 4
SWIGLU_LIMIT, SWIGLU_ALPHA = 7.0, 1.702
DN_ALPHA = (2 * DEPTH) ** 0.25
LN_EPS, RMS_EPS = 1e-5, 1e-6

LANES = 128
SUBLANES = 8
VMEM_LIMIT = 56 * 1024 * 1024

P_AQ, P_MV, P_MO, P_CQ, P_MQK, P_CKV, P_AK, P_AV, P_CKR, P_MIF, P_CKRS = (
    0, 512, 1024, 1536, 2048, 2560, 2816, 2944, 3072, 3200, 3328)
N_PROJ = 3456

PROJ_TM, PROJ_TN = 1024, 1152
M_CHUNK = 128
MLA_TM = 512
FLASH_BQ = 512
FLASH_HEADS = 4
OUT_TM = 512
MOE_PAD = 256
MOE_TMAX = 1536
MOE_TF = 256
CMB_TM = 64
DSP_TM = 64

ROW_TILES = D_MODEL // 2 // LANES
U32 = jnp.uint32
HI_MASK = 0xFFFF0000


def _cparams(sem, vmem=VMEM_LIMIT):
    return pltpu.CompilerParams(dimension_semantics=sem, vmem_limit_bytes=vmem)


def _bf16_bits(x):
    return lax.bitcast_convert_type(x.astype(BF16).astype(F32), U32)


def _store_rc(ref, val):
    n, d = val.shape
    words = (_bf16_bits(val[:, :d // 2]) >> 16) | (_bf16_bits(val[:, d // 2:]) & U32(HI_MASK))
    x = jnp.stack([words[:, s * LANES:(s + 1) * LANES] for s in range(ROW_TILES)], axis=0)
    ref[...] = pltpu.einshape("stl->tsl", x).reshape(n * ROW_TILES, LANES)


def _load_rc(ref, n):
    x = pltpu.einshape("tsl->stl", ref[...].reshape(n, ROW_TILES, LANES))
    words = jnp.concatenate([x[s] for s in range(ROW_TILES)], axis=-1)
    lo = lax.bitcast_convert_type(words << 16, F32)
    hi = lax.bitcast_convert_type(words & U32(HI_MASK), F32)
    return lo, hi


def _proj_kernel(x_ref, w_ref, o_ref):
    o_ref[...] = jnp.dot(x_ref[...].astype(BF16), w_ref[...], preferred_element_type=F32)


def _proj(x2d, w_bf):
    t, d = x2d.shape
    n = w_bf.shape[1]
    return pl.pallas_call(
        _proj_kernel,
        grid=(t // PROJ_TM, n // PROJ_TN),
        in_specs=[pl.BlockSpec((PROJ_TM, d), lambda i, j: (i, 0)),
                  pl.BlockSpec((d, PROJ_TN), lambda i, j: (0, j))],
        out_specs=pl.BlockSpec((PROJ_TM, PROJ_TN), lambda i, j: (i, j)),
        out_shape=jax.ShapeDtypeStruct((t, n), F32),
        compiler_params=_cparams(("parallel", "arbitrary")),
        name="proj",
    )(x2d, w_bf)


def _swa_kernel(sinks_ref, q_ref, kc_ref, kp_ref, vc_ref, vp_ref, o_ref):
    i = pl.program_id(1)
    w = SWA_WINDOW
    dh = SWA_HEAD_DIM
    grp = SWA_HEADS // SWA_KV_HEADS
    q = q_ref[0]
    qi = lax.broadcasted_iota(jnp.int32, (w, 2 * w), 0)
    kj = lax.broadcasted_iota(jnp.int32, (w, 2 * w), 1)
    rel = qi + w - kj
    has_prev = jnp.where(i > 0, 0, w)
    mask = (rel >= 0) & (rel < w) & (kj >= has_prev)
    outs = []
    for kvh in range(SWA_KV_HEADS):
        sl = slice(kvh * dh, (kvh + 1) * dh)
        k_cat = jnp.concatenate([kp_ref[0][:, sl], kc_ref[0][:, sl]], axis=0).astype(BF16)
        v_cat = jnp.concatenate([vp_ref[0][:, sl], vc_ref[0][:, sl]], axis=0).astype(BF16)
        for g in range(grp):
            h = kvh * grp + g
            qh = (q[:, h * dh:(h + 1) * dh] * (dh ** -0.5)).astype(BF16)
            s = lax.dot_general(qh, k_cat, (((1,), (1,)), ((), ())), preferred_element_type=F32)
            s = jnp.where(mask, s, NEG_INF)
            sink = sinks_ref[h]
            m = jnp.maximum(jnp.max(s, axis=-1, keepdims=True), sink)
            p = jnp.exp(s - m)
            den = jnp.sum(p, axis=-1, keepdims=True) + jnp.exp(sink - m)
            o = jnp.dot(p.astype(BF16), v_cat, preferred_element_type=F32)
            outs.append(o / den)
    o_ref[0] = jnp.concatenate(outs, axis=-1).astype(o_ref.dtype)


def _swa(proj3, sinks):
    b, s, _ = proj3.shape
    w = SWA_WINDOW
    nb = s // w
    kw = SWA_KV_HEADS * SWA_HEAD_DIM
    qw = SWA_HEADS * SWA_HEAD_DIM
    cur = lambda col: (lambda bi, i: (bi, i, col))
    prev = lambda col: (lambda bi, i: (bi, jnp.maximum(i - 1, 0), col))
    return pl.pallas_call(
        _swa_kernel,
        grid=(b, nb),
        in_specs=[pl.BlockSpec(memory_space=pltpu.SMEM),
                  pl.BlockSpec((1, w, qw), cur(P_AQ // qw)),
                  pl.BlockSpec((1, w, kw), cur(P_AK // kw)),
                  pl.BlockSpec((1, w, kw), prev(P_AK // kw)),
                  pl.BlockSpec((1, w, kw), cur(P_AV // kw)),
                  pl.BlockSpec((1, w, kw), prev(P_AV // kw))],
        out_specs=pl.BlockSpec((1, w, qw), lambda bi, i: (bi, i, 0)),
        out_shape=jax.ShapeDtypeStruct((b, s, qw), BF16),
        compiler_params=_cparams(("parallel", "arbitrary")),
        name="swa",
    )(sinks, proj3, proj3, proj3, proj3, proj3)


def _mlstm_kernel(qk_ref, v_ref, og_ref, g_ref, cw_ref, cb_ref, gb_ref, ng_ref, y_ref, xbuf, ct_ref, m_ref):
    c = pl.program_id(0)
    nb = qk_ref.shape[0]
    ln = M_CHUNK
    dqk, dv, nh = M_QK_DIM, M_V_DIM, M_HEADS
    halo = SUBLANES

    @pl.when(c == 0)
    def _init():
        xbuf[:, 0:halo, :] = jnp.zeros((nb, halo, xbuf.shape[2]), F32)
        ct_ref[...] = jnp.zeros(ct_ref.shape, F32)
        m_ref[...] = jnp.zeros(m_ref.shape, F32)

    row = lax.broadcasted_iota(jnp.int32, (ln, ln), 0)
    col = lax.broadcasted_iota(jnp.int32, (ln, ln), 1)
    tril = row >= col
    trilf = jnp.where(tril, 1.0, 0.0).astype(F32)
    ones_blk = jnp.where(lax.broadcasted_iota(jnp.int32, (ln, dv), 1) == 0, 1.0, 0.0).astype(F32)

    for b in range(nb):
        xbuf[b, halo:halo + ln, :] = qk_ref[b]
        conv = cb_ref[...]
        for j in range(M_CONV):
            off = halo - (M_CONV - 1) + j
            conv = conv + cw_ref[j:j + 1, :] * xbuf[b, off:off + ln, :]
        qk = conv * jax.nn.sigmoid(conv)
        xbuf[b, 0:halo, :] = qk_ref[b, ln - halo:ln, :]

        g = g_ref[b] + gb_ref[...]
        lf = jnp.minimum(g, 0.0) - jnp.log1p(jnp.exp(-jnp.abs(g)))
        cum = jnp.dot(trilf, lf, preferred_element_type=F32, precision=lax.Precision.HIGHEST)
        cum_t = cum.T
        g_t = g.T
        vv = v_ref[b]
        hs = []
        for h in range(nh):
            chain = b * nh + h
            q = (qk[:, h * dqk:(h + 1) * dqk] * (dqk ** -0.5)).astype(BF16)
            k = qk[:, nh * dqk + h * dqk: nh * dqk + (h + 1) * dqk].astype(BF16)
            v_ext = jnp.concatenate([vv[:, h * dv:(h + 1) * dv], ones_blk], axis=-1)
            bc_col = cum[:, nh + h:nh + h + 1]
            bc_row = cum_t[nh + h:nh + h + 1, :]
            i_col = g[:, h:h + 1]
            i_row = g_t[h:h + 1, :]
            m_prev = m_ref[chain:chain + 1, 0:1]
            dmat = jnp.where(tril, bc_col - bc_row + i_row, NEG_INF)
            m_inter = bc_col + m_prev
            m_j = jnp.maximum(m_inter, jnp.max(dmat, axis=-1, keepdims=True))
            w_intra = jnp.exp(dmat - m_j)
            w_inter = jnp.exp(m_inter - m_j)
            s = lax.dot_general(q, k, (((1,), (1,)), ((), ())), preferred_element_type=F32) * w_intra
            ct = ct_ref[chain]
            num_ext = (jnp.dot(s.astype(BF16), v_ext.astype(BF16), preferred_element_type=F32)
                       + w_inter * jnp.dot(q, ct.astype(BF16), preferred_element_type=F32))
            num = num_ext[:, :dv]
            nq = num_ext[:, dv:dv + 1]
            den = jnp.maximum(jnp.abs(nq), jnp.exp(-m_j))
            hs.append(num / den)
            m_new = m_j[ln - 1:ln, :]
            bc_last = bc_col[ln - 1:ln, :]
            w_s = jnp.exp(bc_last - bc_col + i_col - m_new)
            w_c = jnp.exp(bc_last + m_prev - m_new)
            upd = lax.dot_general(k, (w_s * v_ext).astype(BF16), (((0,), (0,)), ((), ())),
                                  preferred_element_type=F32)
            ct_ref[chain] = w_c * ct + upd
            m_ref[chain:chain + 1, :] = jnp.broadcast_to(m_new, (1, m_ref.shape[1]))
        og = og_ref[b]
        outs = []
        for h in range(nh):
            seg = jax.nn.sigmoid(og[:, h * dv:(h + 1) * dv]) * hs[h]
            mu = jnp.mean(seg, axis=-1, keepdims=True)
            cen = seg - mu
            var = jnp.mean(cen * cen, axis=-1, keepdims=True)
            outs.append(cen * lax.rsqrt(var + LN_EPS) * ng_ref[:, h * dv:(h + 1) * dv])
        y_ref[b] = jnp.concatenate(outs, axis=-1).astype(y_ref.dtype)


def _mlstm(proj3, conv_w, conv_b, gate_b, norm_g):
    b, s, _ = proj3.shape
    ln = M_CHUNK
    wq = 2 * M_HEADS * M_QK_DIM
    wv = M_HEADS * M_V_DIM
    blk = lambda width, off: pl.BlockSpec((b, ln, width), lambda c: (0, c, off // width))
    full = lambda a: pl.BlockSpec(a.shape, lambda c: (0,) * a.ndim)
    gate_b128 = jnp.zeros((1, LANES), F32).at[0, :2 * M_HEADS].set(gate_b)
    conv_b2 = conv_b.reshape(1, wq)
    norm_g2 = norm_g.reshape(1, wv)
    return pl.pallas_call(
        _mlstm_kernel,
        grid=(s // ln,),
        in_specs=[blk(wq, P_MQK), blk(wv, P_MV), blk(wv, P_MO), blk(LANES, P_MIF),
                  full(conv_w), full(conv_b2), full(gate_b128), full(norm_g2)],
        out_specs=pl.BlockSpec((b, ln, wv), lambda c: (0, c, 0)),
        out_shape=jax.ShapeDtypeStruct((b, s, wv), BF16),
        scratch_shapes=[pltpu.VMEM((b, SUBLANES + ln, wq), F32),
                        pltpu.VMEM((b * M_HEADS, M_QK_DIM, 2 * M_V_DIM), F32),
                        pltpu.VMEM((b * M_HEADS, LANES), F32)],
        compiler_params=_cparams(("arbitrary",)),
        name="mlstm",
    )(proj3, proj3, proj3, proj3, conv_w, conv_b2, gate_b128, norm_g2)


def _mla_prep_kernel(cq_ref, ckv_ref, kr_ref, krs_ref, qg_ref, kvg_ref, wqa_ref, wqb_ref, wkv_ref,
                     cos_ref, sin_ref, q_out, k_out, v_out):
    nd, hd = C_NOPE_DIM, C_HEADS
    scale = (C_NOPE_DIM + C_ROPE_DIM) ** -0.5 * LOG2_E
    cos = cos_ref[...]
    sin = sin_ref[...]

    cq = cq_ref[0]
    qn = (cq * lax.rsqrt(jnp.mean(cq * cq, axis=-1, keepdims=True) + RMS_EPS) * qg_ref[...]).astype(BF16)
    qa = jnp.dot(qn, wqa_ref[...], preferred_element_type=F32)
    qb = jnp.dot(qn, wqb_ref[...], preferred_element_type=F32)
    for h in range(hd):
        nope = qa[:, h * 2 * nd: h * 2 * nd + nd]
        rope = qa[:, h * 2 * nd + nd:(h + 1) * 2 * nd] * cos + qb[:, h * nd:(h + 1) * nd] * sin
        q_out[0, h] = (jnp.concatenate([nope, rope], axis=-1) * scale).astype(q_out.dtype)

    ckv = ckv_ref[0]
    kvn = (ckv * lax.rsqrt(jnp.mean(ckv * ckv, axis=-1, keepdims=True) + RMS_EPS) * kvg_ref[...]).astype(BF16)
    kv = jnp.dot(kvn, wkv_ref[...], preferred_element_type=F32)
    kr = kr_ref[0] * cos + krs_ref[0] * sin
    ones_blk = jnp.where(lax.broadcasted_iota(jnp.int32, kr.shape, 1) == 0, 1.0, 0.0).astype(F32)
    for h in range(hd):
        k_out[0, h] = jnp.concatenate([kv[:, h * nd:(h + 1) * nd], kr], axis=-1).astype(k_out.dtype)
        v_h = kv[:, hd * nd + h * nd: hd * nd + (h + 1) * nd]
        v_out[0, h] = jnp.concatenate([v_h, ones_blk], axis=-1).astype(v_out.dtype)


def _mla_prep(proj3, q_norm_g, kv_norm_g, wqa, wqb, wkv, cos128, sin128):
    b, s, _ = proj3.shape
    tm = MLA_TM
    hd, nd = C_HEADS, C_NOPE_DIM
    blk = lambda width, off: pl.BlockSpec((1, tm, width), lambda bi, i: (bi, i, off // width))
    full = lambda a: pl.BlockSpec(a.shape, lambda bi, i: (0,) * a.ndim)
    tab = pl.BlockSpec((tm, LANES), lambda bi, i: (i, 0))
    qg = q_norm_g.reshape(1, -1)
    kvg = kv_norm_g.reshape(1, -1)
    head_out = lambda width: pl.BlockSpec((1, hd, tm, width), lambda bi, i: (bi, 0, i, 0))
    return pl.pallas_call(
        _mla_prep_kernel,
        grid=(b, s // tm),
        in_specs=[blk(C_Q_LORA, P_CQ), blk(C_KV_LORA, P_CKV), blk(LANES, P_CKR), blk(LANES, P_CKRS),
                  full(qg), full(kvg), full(wqa), full(wqb), full(wkv), tab, tab],
        out_specs=[head_out(2 * nd), head_out(2 * nd), head_out(C_V_DIM + LANES)],
        out_shape=[jax.ShapeDtypeStruct((b, hd, s, 2 * nd), BF16),
                   jax.ShapeDtypeStruct((b, hd, s, 2 * nd), BF16),
                   jax.ShapeDtypeStruct((b, hd, s, C_V_DIM + LANES), BF16)],
        compiler_params=_cparams(("parallel", "arbitrary")),
        name="mla_prep",
    )(proj3, proj3, proj3, proj3, qg, kvg, wqa, wqb, wkv, cos128, sin128)


def _flash_kernel(q_ref, k_ref, v_ref, o_ref):
    qi = pl.program_id(2)
    bq = FLASH_BQ
    nh = q_ref.shape[1]
    dv = C_V_DIM

    def step(j, carries, diag):
        start = pl.multiple_of(j * bq, bq)
        out = []
        for h in range(nh):
            m, acc = carries[h]
            ks = k_ref[0, h, pl.ds(start, bq), :]
            vs = v_ref[0, h, pl.ds(start, bq), :]
            s = lax.dot_general(q_ref[0, h], ks, (((1,), (1,)), ((), ())), preferred_element_type=F32)
            if diag:
                row = lax.broadcasted_iota(jnp.int32, (bq, bq), 0)
                col = lax.broadcasted_iota(jnp.int32, (bq, bq), 1)
                s = jnp.where(row >= col, s, NEG_INF)
            m_new = jnp.maximum(m, jnp.max(s, axis=-1, keepdims=True))
            p = jnp.exp2(s - m_new)
            acc = jnp.exp2(m - m_new) * acc + jnp.dot(p.astype(BF16), vs, preferred_element_type=F32)
            out.append((m_new, acc))
        return tuple(out)

    init = tuple((jnp.full((bq, 1), NEG_INF, F32), jnp.zeros((bq, v_ref.shape[3]), F32)) for _ in range(nh))
    carries = lax.fori_loop(0, qi, lambda j, cr: step(j, cr, False), init)
    carries = step(qi, carries, True)
    o_ref[0] = jnp.concatenate([acc[:, :dv] / acc[:, dv:dv + 1] for _, acc in carries], axis=-1).astype(o_ref.dtype)


def _flash(qc, kc, vc):
    b, hd, s, dk = qc.shape
    dv = C_V_DIM
    bq = FLASH_BQ
    nh = FLASH_HEADS
    return pl.pallas_call(
        _flash_kernel,
        grid=(b, hd // nh, s // bq),
        in_specs=[pl.BlockSpec((1, nh, bq, dk), lambda bi, h, i: (bi, h, i, 0)),
                  pl.BlockSpec((1, nh, s, dk), lambda bi, h, i: (bi, h, 0, 0)),
                  pl.BlockSpec((1, nh, s, vc.shape[3]), lambda bi, h, i: (bi, h, 0, 0))],
        out_specs=pl.BlockSpec((1, bq, nh * dv), lambda bi, h, i: (bi, i, h)),
        out_shape=jax.ShapeDtypeStruct((b, s, hd * dv), BF16),
        compiler_params=_cparams(("parallel", "parallel", "arbitrary")),
        name="flash",
    )(qc, kc, vc)


def _layer_norm(z, g, b):
    mu = jnp.mean(z, axis=-1, keepdims=True)
    cen = z - mu
    var = jnp.mean(cen * cen, axis=-1, keepdims=True)
    return cen * lax.rsqrt(var + LN_EPS) * g + b


def _outproj_kernel(ya_ref, yb_ref, yc_ref, x_ref, w_ref, g_ref, b_ref, wr_ref, br_ref,
                    x1_ref, x1rc_ref, e_ref, gate_ref, rank_ref, cnt_ref, cnt_sc):
    i = pl.program_id(0)
    tm = x_ref.shape[0]
    wa, wb = ya_ref.shape[1], yb_ref.shape[1]
    mix = jnp.dot(ya_ref[...], w_ref[0:wa, :], preferred_element_type=F32)
    mix = mix + jnp.dot(yb_ref[...], w_ref[wa:wa + wb, :], preferred_element_type=F32)
    mix = mix + jnp.dot(yc_ref[...], w_ref[wa + wb:, :], preferred_element_type=F32)
    x1 = _layer_norm(DN_ALPHA * x_ref[...] + mix, g_ref[...], b_ref[...])
    x1_ref[...] = x1
    _store_rc(x1rc_ref, x1)

    logits = jnp.dot(x1.astype(BF16), wr_ref[...], preferred_element_type=F32) + br_ref[...]
    lane = lax.broadcasted_iota(jnp.int32, logits.shape, 1)
    logits = jnp.where(lane < N_EXPERTS, logits, NEG_INF)
    e_out = jnp.zeros(logits.shape, jnp.int32)
    p_out = jnp.zeros(logits.shape, F32)
    top = None
    den = None
    onehots = []
    for r in range(TOP_K):
        mx = jnp.max(logits, axis=-1, keepdims=True)
        idx = jnp.min(jnp.where(logits == mx, lane, LANES), axis=-1, keepdims=True)
        if r == 0:
            top = mx
        p = jnp.exp(mx - top)
        den = p if r == 0 else den + p
        sel = lane == idx
        onehots.append(jnp.where(sel, 1.0, 0.0).astype(F32))
        e_out = jnp.where(lane == r, idx, e_out)
        p_out = jnp.where(lane == r, p, p_out)
        logits = jnp.where(sel, NEG_INF, logits)
    e_ref[...] = e_out
    gate_ref[...] = p_out / den

    @pl.when(i == 0)
    def _init():
        cnt_sc[...] = jnp.zeros(cnt_sc.shape, F32)

    oh_sum = onehots[0] + onehots[1] + onehots[2] + onehots[3]
    row = lax.broadcasted_iota(jnp.int32, (tm, tm), 0)
    col = lax.broadcasted_iota(jnp.int32, (tm, tm), 1)
    before = jnp.where(row > col, 1.0, 0.0).astype(BF16)
    base = jnp.dot(before, oh_sum.astype(BF16), preferred_element_type=F32) + cnt_sc[0:1, :]
    rank_out = jnp.zeros(logits.shape, jnp.int32)
    for r in range(TOP_K):
        rk = jnp.sum(onehots[r] * base, axis=-1, keepdims=True)
        rank_out = jnp.where(lane == r, rk.astype(jnp.int32), rank_out)
    rank_ref[...] = rank_out
    total = cnt_sc[0:1, :] + jnp.sum(oh_sum, axis=0, keepdims=True)
    cnt_sc[0:1, :] = total
    cnt_ref[...] = jnp.broadcast_to(total, cnt_ref.shape).astype(jnp.int32)


def _outproj(ya, yb, yc, x2d, w_bf, g, b, wr128, br128):
    t, d = x2d.shape
    tm = OUT_TM
    rows = lambda a: pl.BlockSpec((tm, a.shape[1]), lambda i: (i, 0))
    full = lambda a: pl.BlockSpec(a.shape, lambda i: (0,) * a.ndim, pipeline_mode=pl.Buffered(1))
    g2, b2 = g.reshape(1, d), b.reshape(1, d)
    lane_blk = pl.BlockSpec((tm, LANES), lambda i: (i, 0))
    return pl.pallas_call(
        _outproj_kernel,
        grid=(t // tm,),
        in_specs=[rows(ya), rows(yb), rows(yc), rows(x2d), full(w_bf), full(g2), full(b2), full(wr128), full(br128)],
        out_specs=[pl.BlockSpec((tm, d), lambda i: (i, 0)),
                   pl.BlockSpec((tm * ROW_TILES, LANES), lambda i: (i, 0)),
                   lane_blk, lane_blk, lane_blk,
                   pl.BlockSpec((SUBLANES, LANES), lambda i: (0, 0))],
        out_shape=[jax.ShapeDtypeStruct((t, d), F32),
                   jax.ShapeDtypeStruct((t * ROW_TILES, LANES), U32),
                   jax.ShapeDtypeStruct((t, LANES), jnp.int32),
                   jax.ShapeDtypeStruct((t, LANES), F32),
                   jax.ShapeDtypeStruct((t, LANES), jnp.int32),
                   jax.ShapeDtypeStruct((SUBLANES, LANES), jnp.int32)],
        scratch_shapes=[pltpu.VMEM((SUBLANES, LANES), F32)],
        compiler_params=_cparams(("arbitrary",)),
        name="outproj",
    )(ya, yb, yc, x2d, w_bf, g2, b2, wr128, br128)


def _rc_rows(row, n=1):
    return pl.ds(pl.multiple_of(row * ROW_TILES, ROW_TILES), n * ROW_TILES)


def _dispatch_kernel(pos_ref, cnt_ref, pst_ref, pad_ref, nblk_ref, x_hbm, xs_hbm, buf, zbuf, sem_in, sem_out, sem_z):
    i = pl.program_id(0)
    n = pl.num_programs(0)
    tm = DSP_TM
    nslot = buf.shape[0]

    def in_copy(blk, slot):
        return pltpu.make_async_copy(x_hbm.at[_rc_rows(blk * tm, tm)], buf.at[slot], sem_in.at[slot])

    def row_out(slot, t, p):
        return pltpu.make_async_copy(buf.at[slot, _rc_rows(t)], xs_hbm.at[_rc_rows(p)], sem_out.at[slot])

    def wait_outs(slot):
        for _ in range(TOP_K):
            pltpu.make_async_copy(buf.at[slot], xs_hbm.at[_rc_rows(0, tm)], sem_out.at[slot]).wait()

    slot = lax.rem(i, nslot)

    @pl.when(i == 0)
    def _first_in():
        in_copy(0, 0).start()

    @pl.when(i >= 2)
    def _free_slot():
        wait_outs(lax.rem(i + 1, nslot))

    @pl.when(i + 1 < n)
    def _next_in():
        in_copy(i + 1, lax.rem(i + 1, nslot)).start()

    in_copy(i, slot).wait()
    base = i * (tm * TOP_K)

    def issue(t, _):
        for k in range(TOP_K):
            row_out(slot, t, pos_ref[base + t * TOP_K + k]).start()
        return 0

    lax.fori_loop(0, tm, issue, 0, unroll=2)

    @pl.when(i == n - 1)
    def _finish():
        wait_outs(lax.rem(i + 2, nslot))
        wait_outs(slot)
        zbuf[...] = jnp.zeros(zbuf.shape, zbuf.dtype)

        def pad_rows(e, _):
            cnt = cnt_ref[e]
            first = pst_ref[e] + cnt
            npad = pad_ref[e] - cnt

            def zero_row(q):
                return pltpu.make_async_copy(zbuf.at[_rc_rows(0)], xs_hbm.at[_rc_rows(first + q)], sem_z)

            def start(q, c):
                zero_row(q).start()
                return c

            def wait(q, c):
                zero_row(q).wait()
                return c

            lax.fori_loop(0, npad, start, 0)
            lax.fori_loop(0, npad, wait, 0)
            return 0

        lax.fori_loop(0, N_EXPERTS, pad_rows, 0)

        def tail_block(bk, _):
            cp = pltpu.make_async_copy(zbuf, xs_hbm.at[_rc_rows(bk * MOE_PAD, MOE_PAD)], sem_z)
            cp.start()
            cp.wait()
            return 0

        lax.fori_loop(nblk_ref[0], xs_hbm.shape[0] // (MOE_PAD * ROW_TILES), tail_block, 0)


def _dispatch(pos, counts, pad_start, padded, nblk, x1rc, rows):
    t = x1rc.shape[0] // ROW_TILES
    tm = DSP_TM
    return pl.pallas_call(
        _dispatch_kernel,
        grid_spec=pltpu.PrefetchScalarGridSpec(
            num_scalar_prefetch=5,
            grid=(t // tm,),
            in_specs=[pl.BlockSpec(memory_space=pl.ANY)],
            out_specs=pl.BlockSpec(memory_space=pl.ANY),
            scratch_shapes=[pltpu.VMEM((3, tm * ROW_TILES, LANES), U32),
                            pltpu.VMEM((MOE_PAD * ROW_TILES, LANES), U32),
                            pltpu.SemaphoreType.DMA((3,)),
                            pltpu.SemaphoreType.DMA((3,)),
                            pltpu.SemaphoreType.DMA(())]),
        out_shape=jax.ShapeDtypeStruct((rows * ROW_TILES, LANES), U32),
        compiler_params=_cparams(("arbitrary",)),
        name="dispatch",
    )(pos, counts, pad_start, padded, nblk, x1rc)


def _moe_kernel(layer, sbe_ref, sbr_ref, sbn_ref, nblk_ref, xs_hbm, wgu_hbm, wdn_hbm, bgu_ref, bd_ref, y_hbm,
                xbuf, acc, stage_in, stage_out, wg_f, wu_f, wd_f, wg_bf, wu_bf, wd_bf, sem_in, sem_out, sem_w):
    sb = pl.program_id(0)
    nsb = pl.num_programs(0)
    tf = wg_f.shape[2]
    dff = wdn_hbm.shape[2]
    nj = dff // tf
    n = sbn_ref[sb]
    row0 = sbr_ref[sb]
    nchunk = n // MOE_PAD
    cur = lax.rem(sb, 2)
    big, huge = 2 * MOE_PAD, 4 * MOE_PAD

    def weight_copies(sb_idx, j, slot):
        e = sbe_ref[sb_idx]
        cols = pl.ds(pl.multiple_of(j * tf, tf), tf)
        up_cols = pl.ds(pl.multiple_of(dff + j * tf, tf), tf)
        return (pltpu.make_async_copy(wgu_hbm.at[layer, e, :, cols], wg_f.at[slot], sem_w.at[slot]),
                pltpu.make_async_copy(wgu_hbm.at[layer, e, :, up_cols], wu_f.at[slot], sem_w.at[slot]),
                pltpu.make_async_copy(wdn_hbm.at[layer, e, cols, :], wd_f.at[slot], sem_w.at[slot]))

    def chunk_rows(c):
        return pl.ds(pl.multiple_of(c * MOE_PAD, MOE_PAD), MOE_PAD)

    def chunk_in(first_row, c):
        return pltpu.make_async_copy(xs_hbm.at[_rc_rows(first_row + c * MOE_PAD, MOE_PAD)], stage_in, sem_in)

    def chunk_out(c, slot):
        return pltpu.make_async_copy(stage_out.at[slot], y_hbm.at[_rc_rows(row0 + c * MOE_PAD, MOE_PAD)],
                                     sem_out.at[slot])

    def wait_outs(count):
        @pl.when(count >= 2)
        def _older():
            chunk_out(0, lax.rem(count, 2)).wait()

        @pl.when(count >= 1)
        def _newest():
            chunk_out(0, lax.rem(count + 1, 2)).wait()

    def convert_in(slot, c):
        lo, hi = _load_rc(stage_in, MOE_PAD)
        xbuf[slot, chunk_rows(c), :] = jnp.concatenate([lo.astype(BF16), hi.astype(BF16)], axis=-1)

    @pl.when(sb == 0)
    def _load_first():
        for cp in weight_copies(0, 0, 0):
            cp.start()

        def load(c, _):
            chunk_in(row0, c).start()
            chunk_in(row0, c).wait()
            convert_in(0, c)
            return 0

        lax.fori_loop(0, nchunk, load, 0)

    n_next = sbn_ref[sb + 1]

    def ffn(j, start, size, wg, wu, wd):
        rs = pl.ds(start, size)
        xr = xbuf[cur, rs, :]
        cols = pl.ds(pl.multiple_of(j * tf, tf), tf)
        up_cols = pl.ds(pl.multiple_of(dff + j * tf, tf), tf)
        gt = jnp.dot(xr, wg, preferred_element_type=F32) + bgu_ref[:, cols]
        up = jnp.dot(xr, wu, preferred_element_type=F32) + bgu_ref[:, up_cols]
        gt = jnp.minimum(gt, SWIGLU_LIMIT)
        up = jnp.clip(up, -SWIGLU_LIMIT, SWIGLU_LIMIT)
        act = ((up + 1.0) * gt * jax.nn.sigmoid(SWIGLU_ALPHA * gt)).astype(BF16)
        half = acc.shape[1] // 2
        for cs in (slice(0, half), slice(half, 2 * half)):
            acc[rs, cs] = acc[rs, cs] + jnp.dot(act, wd[:, cs], preferred_element_type=F32)

    def first_chunk(j, wslot, size):
        wg = wg_f[wslot].astype(BF16)
        wu = wu_f[wslot].astype(BF16)
        wd = wd_f[wslot].astype(BF16)
        wg_bf[...] = wg
        wu_bf[...] = wu
        wd_bf[...] = wd
        ffn(j, 0, size, wg, wu, wd)

    def hidden_tile(j, _):
        wslot = lax.rem(j, 2)
        for cp in weight_copies(sb, j, wslot):
            cp.wait()

        @pl.when(j + 1 < nj)
        def _next_tile():
            for cp in weight_copies(sb, j + 1, 1 - wslot):
                cp.start()

        @pl.when((j + 1 == nj) & (n_next > 0))
        def _next_super_block():
            for cp in weight_copies(sb + 1, 0, 0):
                cp.start()

        fetch_next = j * MOE_PAD < n_next

        @pl.when(fetch_next)
        def _start_next():
            chunk_in(sbr_ref[sb + 1], j).start()

        @pl.when((j == 1) & (sb > 0))
        def _drain_prev_outs():
            wait_outs(sbn_ref[jnp.maximum(sb - 1, 0)] // MOE_PAD)

        @pl.when(j == 0)
        def _init_acc():
            def init(c, _):
                acc[chunk_rows(c), :] = jnp.broadcast_to(bd_ref[...], (MOE_PAD, acc.shape[1]))
                return 0

            lax.fori_loop(0, nchunk, init, 0)

        def rest(start, size):
            ffn(j, start, size, wg_bf[...], wu_bf[...], wd_bf[...])

        @pl.when(n >= huge)
        def _rows_huge():
            first_chunk(j, wslot, huge)

            @pl.when(n - huge >= big)
            def _then_big():
                rest(huge, big)

            @pl.when(lax.rem(n, big) != 0)
            def _then_pad():
                rest(pl.multiple_of(n - MOE_PAD, MOE_PAD), MOE_PAD)

        @pl.when((n >= big) & (n < huge))
        def _rows_big():
            first_chunk(j, wslot, big)

            @pl.when(n > big)
            def _then_pad():
                rest(big, MOE_PAD)

        @pl.when(n < big)
        def _rows_small():
            first_chunk(j, wslot, MOE_PAD)

        @pl.when(j == nj - 1)
        def _store_rows():
            def store(c, _):
                slot = lax.rem(c, 2)

                @pl.when(c >= 2)
                def _slot_free():
                    chunk_out(c - 2, slot).wait()

                _store_rc(stage_out.at[slot], acc[chunk_rows(c), :])
                chunk_out(c, slot).start()
                return 0

            lax.fori_loop(0, nchunk, store, 0)

            @pl.when((sb == nsb - 1) | (n_next == 0))
            def _no_later_tile():
                wait_outs(nchunk)

        @pl.when(fetch_next)
        def _finish_next():
            chunk_in(sbr_ref[sb + 1], j).wait()
            convert_in(1 - cur, j)

        return 0

    @pl.when(n > 0)
    def _super_block():
        lax.fori_loop(0, nj, hidden_tile, 0)

    @pl.when(sb == nsb - 1)
    def _zero_tail():
        stage_out[0] = jnp.zeros(stage_out.shape[1:], stage_out.dtype)

        def tail_block(bk, _):
            cp = pltpu.make_async_copy(stage_out.at[0], y_hbm.at[_rc_rows(bk * MOE_PAD, MOE_PAD)], sem_out.at[0])
            cp.start()
            cp.wait()
            return 0

        lax.fori_loop(nblk_ref[0], y_hbm.shape[0] // (MOE_PAD * ROW_TILES), tail_block, 0)


def _moe(layer, sb_e, sb_row0, sb_n, nblk, xs, w_gate_up, b_gate_up, w_down, b_down):
    rows = xs.shape[0] // ROW_TILES
    d = D_MODEL
    dff = w_down.shape[2]
    tf = MOE_TF
    nj = dff // tf
    nsb = sb_e.shape[0] - 1
    assert nj % 2 == 0 and MOE_TMAX // MOE_PAD <= nj and MOE_TMAX == 6 * MOE_PAD
    bgu = b_gate_up.reshape(DEPTH, N_EXPERTS, 1, 2 * dff)
    bdn = b_down.reshape(DEPTH, N_EXPERTS, 1, d)
    expert_row = lambda width: pl.BlockSpec((None, None, 1, width), lambda sb, e, r, n, nb: (layer, e[sb], 0, 0))
    return pl.pallas_call(
        functools.partial(_moe_kernel, layer),
        grid_spec=pltpu.PrefetchScalarGridSpec(
            num_scalar_prefetch=4,
            grid=(nsb,),
            in_specs=[pl.BlockSpec(memory_space=pl.ANY),
                      pl.BlockSpec(memory_space=pl.ANY),
                      pl.BlockSpec(memory_space=pl.ANY),
                      expert_row(2 * dff),
                      expert_row(d)],
            out_specs=pl.BlockSpec(memory_space=pl.ANY),
            scratch_shapes=[pltpu.VMEM((2, MOE_TMAX, d), BF16),
                            pltpu.VMEM((MOE_TMAX, d), F32),
                            pltpu.VMEM((MOE_PAD * ROW_TILES, LANES), U32),
                            pltpu.VMEM((2, MOE_PAD * ROW_TILES, LANES), U32),
                            pltpu.VMEM((2, d, tf), F32),
                            pltpu.VMEM((2, d, tf), F32),
                            pltpu.VMEM((2, tf, d), F32),
                            pltpu.VMEM((d, tf), BF16),
                            pltpu.VMEM((d, tf), BF16),
                            pltpu.VMEM((tf, d), BF16),
                            pltpu.SemaphoreType.DMA(()),
                            pltpu.SemaphoreType.DMA((2,)),
                            pltpu.SemaphoreType.DMA((2,))]),
        out_shape=jax.ShapeDtypeStruct((rows * ROW_TILES, LANES), U32),
        compiler_params=_cparams(("arbitrary",)),
        name="moe",
    )(sb_e, sb_row0, sb_n, nblk, xs, w_gate_up, w_down, bgu, bdn)


def _combine_kernel(pos_ref, y_hbm, gate_ref, x1_ref, g_ref, b_ref, o_ref, buf, sem):
    i = pl.program_id(0)
    n = pl.num_programs(0)
    tm = CMB_TM
    slot = lax.rem(i, 2)

    def issue(blk, dst_slot):
        base = blk * (tm * TOP_K)

        def body(t, _):
            for k in range(TOP_K):
                p = pos_ref[base + t * TOP_K + k]
                pltpu.make_async_copy(y_hbm.at[_rc_rows(p)], buf.at[dst_slot, k, _rc_rows(t)], sem.at[dst_slot]).start()
            return 0

        lax.fori_loop(0, tm, body, 0, unroll=2)

    @pl.when(i == 0)
    def _first():
        issue(0, 0)

    @pl.when(i + 1 < n)
    def _next():
        issue(i + 1, 1 - slot)

    for k in range(TOP_K):
        pltpu.make_async_copy(y_hbm.at[_rc_rows(0, tm)], buf.at[slot, k], sem.at[slot]).wait()
    gate = gate_ref[...]
    ffn_lo, ffn_hi = None, None
    for k in range(TOP_K):
        lo, hi = _load_rc(buf.at[slot, k], tm)
        gk = gate[:, k:k + 1]
        ffn_lo = gk * lo if k == 0 else ffn_lo + gk * lo
        ffn_hi = gk * hi if k == 0 else ffn_hi + gk * hi
    ffn = jnp.concatenate([ffn_lo, ffn_hi], axis=-1)
    o_ref[...] = _layer_norm(DN_ALPHA * x1_ref[...] + ffn, g_ref[...], b_ref[...])


def _combine(pos_flat, y, gate, x1, g, b):
    t, d = x1.shape
    tm = CMB_TM
    g2, b2 = g.reshape(1, d), b.reshape(1, d)
    return pl.pallas_call(
        _combine_kernel,
        grid_spec=pltpu.PrefetchScalarGridSpec(
            num_scalar_prefetch=1,
            grid=(t // tm,),
            in_specs=[pl.BlockSpec(memory_space=pl.ANY),
                      pl.BlockSpec((tm, LANES), lambda i, p: (i, 0)),
                      pl.BlockSpec((tm, d), lambda i, p: (i, 0)),
                      pl.BlockSpec((1, d), lambda i, p: (0, 0)),
                      pl.BlockSpec((1, d), lambda i, p: (0, 0))],
            out_specs=pl.BlockSpec((tm, d), lambda i, p: (i, 0)),
            scratch_shapes=[pltpu.VMEM((2, TOP_K, tm * ROW_TILES, LANES), U32), pltpu.SemaphoreType.DMA((2,))]),
        out_shape=jax.ShapeDtypeStruct((t, d), F32),
        compiler_params=_cparams(("arbitrary",)),
        name="combine",
    )(pos_flat, y, gate, x1, g2, b2)


def _swap_halves(w):
    half = w.shape[-1] // 2
    return jnp.concatenate([w[..., half:], w[..., :half]], axis=-1)


def _layout_w_in(w):
    d = w.shape[0]
    widths = (512, 128, 128, 256, 256, 512, 512, 8, 512, 256, 64)
    offs = [0]
    for wd in widths:
        offs.append(offs[-1] + wd)
    a_q, a_k, a_v, m_q, m_k, m_v, m_o, m_if, c_q, c_kv, c_kr = [w[:, offs[i]:offs[i + 1]] for i in range(len(widths))]
    z = lambda n: jnp.zeros((d, n), w.dtype)
    out = jnp.concatenate([a_q, m_v, m_o, c_q, m_q, m_k, c_kv, a_k, a_v,
                           c_kr, z(LANES - 64), m_if, z(LANES - 8), _swap_halves(c_kr), z(LANES - 64)], axis=1)
    return out.astype(BF16)


def _layout_w_uq(w):
    r = w.shape[0]
    w3 = w.reshape(r, C_HEADS, C_NOPE_DIM + C_ROPE_DIM)
    nope, rope = w3[..., :C_NOPE_DIM], w3[..., C_NOPE_DIM:]
    z = jnp.zeros((r, C_HEADS, LANES - C_ROPE_DIM), w.dtype)
    wa = jnp.concatenate([nope, rope, z], axis=-1).reshape(r, -1)
    wb = jnp.concatenate([_swap_halves(rope), z], axis=-1).reshape(r, -1)
    return wa.astype(BF16), wb.astype(BF16)


def _layout_w_ukv(w):
    r = w.shape[0]
    w3 = w.reshape(r, C_HEADS, C_NOPE_DIM + C_V_DIM)
    return jnp.concatenate([w3[..., :C_NOPE_DIM].reshape(r, -1), w3[..., C_NOPE_DIM:].reshape(r, -1)],
                           axis=-1).astype(BF16)


def _rope_tables(seq):
    dim = C_ROPE_DIM
    inv = 1.0 / (ROPE_THETA ** (jnp.arange(0, dim, 2, dtype=F32) / dim))
    ang = jnp.arange(seq, dtype=F32)[:, None] * inv[None, :]
    cos, sin = jnp.cos(ang), jnp.sin(ang)
    z = jnp.zeros((seq, LANES - dim), F32)
    return jnp.concatenate([cos, cos, z], axis=-1), jnp.concatenate([-sin, sin, z], axis=-1)


def _route(top_e, rank, counts, nsb):
    padded = (counts + MOE_PAD - 1) // MOE_PAD * MOE_PAD
    pad_end = jnp.cumsum(padded)
    pad_start = pad_end - padded
    is_e = top_e[:, :, None] == jnp.arange(N_EXPERTS, dtype=jnp.int32)[None, None, :]
    pos = (jnp.sum(jnp.where(is_e, pad_start[None, None, :], 0), axis=-1) + rank).reshape(-1).astype(jnp.int32)
    nblk = (pad_end[-1:] // MOE_PAD).astype(jnp.int32)
    nsb_e = (padded + MOE_TMAX - 1) // MOE_TMAX
    sb_end = jnp.cumsum(nsb_e)
    sb_start = sb_end - nsb_e
    sb_idx = jnp.arange(nsb + 1, dtype=jnp.int32)
    n_valid = sb_end[-1]
    sb_eff = jnp.minimum(sb_idx, n_valid - 1)
    sb_e = jnp.searchsorted(sb_end, sb_eff, side="right").astype(jnp.int32)
    part = sb_eff - sb_start[sb_e]
    sb_row0 = pad_start[sb_e] + part * MOE_TMAX
    sb_n = jnp.where(sb_idx < n_valid, jnp.minimum(padded[sb_e] - part * MOE_TMAX, MOE_TMAX), 0)
    tables = (counts, pad_start.astype(jnp.int32), padded.astype(jnp.int32), nblk)
    return pos, tables, sb_e, sb_row0.astype(jnp.int32), sb_n.astype(jnp.int32)


def kernel(x, w_in, conv_w, conv_b, m_gate_b, m_norm_g, sinks, q_norm_g, w_uq, kv_norm_g, w_ukv, w_out,
           ln1_g, ln1_b, w_router, b_router, w_gate_up, b_gate_up, w_down, b_down, ln2_g, ln2_b):
    b, s, d = x.shape
    t = b * s
    n_assign = t * TOP_K
    rows = (n_assign // MOE_PAD + N_EXPERTS) * MOE_PAD
    nsb = N_EXPERTS + rows // MOE_TMAX
    cos128, sin128 = _rope_tables(s)
    xt = x.reshape(t, d)
    for l in range(DEPTH):
        w_in_bf = _layout_w_in(w_in[l])
        wqa, wqb = _layout_w_uq(w_uq[l])
        wkv = _layout_w_ukv(w_ukv[l])
        w_out_bf = w_out[l].astype(BF16)
        wr128 = jnp.zeros((d, LANES), BF16).at[:, :N_EXPERTS].set(w_router[l].astype(BF16))
        br128 = jnp.zeros((1, LANES), F32).at[0, :N_EXPERTS].set(b_router[l])

        proj3 = _proj(xt, w_in_bf).reshape(b, s, N_PROJ)
        y_a = _swa(proj3, sinks[l])
        y_b = _mlstm(proj3, conv_w[l], conv_b[l], m_gate_b[l], m_norm_g[l])
        qc, kc, vc = _mla_prep(proj3, q_norm_g[l], kv_norm_g[l], wqa, wqb, wkv, cos128, sin128)
        y_c = _flash(qc, kc, vc)
        x1, x1rc, top_e, gate, rank, cnt = _outproj(y_a.reshape(t, -1), y_b.reshape(t, -1), y_c.reshape(t, -1), xt,
                                                    w_out_bf, ln1_g[l], ln1_b[l], wr128, br128)
        pos, tables, sb_e, sb_row0, sb_n = _route(top_e[:, :TOP_K], rank[:, :TOP_K], cnt[0, :N_EXPERTS], nsb)
        xs = _dispatch(pos, *tables, x1rc, rows)
        y = _moe(l, sb_e, sb_row0, sb_n, tables[3], xs, w_gate_up, b_gate_up, w_down, b_down)
        xt = _combine(pos, y, gate, x1, ln2_g[l], ln2_b[l])
    return xt.reshape(b, s, d)
```

```python
import functools

import jax
import jax.numpy as jnp
from jax import lax
from jax.experimental import pallas as pl
from jax.experimental.pallas import tpu as pltpu

F32 = jnp.float32
BF16 = jnp.bfloat16
NEG_INF = float("-inf")
LOG2_E = 1.4426950408889634

D_MODEL = 2048
DEPTH = 2
SWA_HEADS, SWA_KV_HEADS, SWA_HEAD_DIM, SWA_WINDOW = 8, 2, 64, 128
M_HEADS, M_QK_DIM, M_V_DIM, M_CONV = 4, 64, 128, 4
C_HEADS, C_NOPE_DIM, C_ROPE_DIM, C_V_DIM = 8, 128, 64, 128
C_Q_LORA, C_KV_LORA = 512, 256
ROPE_THETA = 10000.0
N_EXPERTS, TOP_K = 32, 4
SWIGLU_LIMIT, SWIGLU_ALPHA = 7.0, 1.702
DN_ALPHA = (2 * DEPTH) ** 0.25
LN_EPS, RMS_EPS = 1e-5, 1e-6

LANES = 128
SUBLANES = 8
VMEM_LIMIT = 56 * 1024 * 1024

P_AQ, P_MV, P_MO, P_CQ, P_MQK, P_CKV, P_AK, P_AV, P_CKR, P_MIF, P_CKRS = (
    0, 512, 1024, 1536, 2048, 2560, 2816, 2944, 3072, 3200, 3328)
N_PROJ = 3456

PROJ_TM, PROJ_TN = 1024, 1152
M_CHUNK = 128
MLA_TM = 512
FLASH_BQ = 1024
FLASH_HEADS = 4
OUT_TM = 512
MOE_PAD = 256
MOE_TMAX = 1536
MOE_TF = 256
CMB_TM = 64
DSP_TM = 64

ROW_TILES = D_MODEL // 2 // LANES
U32 = jnp.uint32
HI_MASK = 0xFFFF0000


def _cparams(sem, vmem=VMEM_LIMIT):
    return pltpu.CompilerParams(dimension_semantics=sem, vmem_limit_bytes=vmem)


def _bf16_bits(x):
    return lax.bitcast_convert_type(x.astype(BF16).astype(F32), U32)


def _store_rc(ref, val):
    n, d = val.shape
    words = (_bf16_bits(val[:, :d // 2]) >> 16) | (_bf16_bits(val[:, d // 2:]) & U32(HI_MASK))
    x = jnp.stack([words[:, s * LANES:(s + 1) * LANES] for s in range(ROW_TILES)], axis=0)
    ref[...] = pltpu.einshape("stl->tsl", x).reshape(n * ROW_TILES, LANES)


def _load_rc(ref, n):
    x = pltpu.einshape("tsl->stl", ref[...].reshape(n, ROW_TILES, LANES))
    words = jnp.concatenate([x[s] for s in range(ROW_TILES)], axis=-1)
    lo = lax.bitcast_convert_type(words << 16, F32)
    hi = lax.bitcast_convert_type(words & U32(HI_MASK), F32)
    return lo, hi


def _proj_kernel(x_ref, w_ref, o_ref, g_ref):
    j = pl.program_id(1)
    acc = jnp.dot(x_ref[...].astype(BF16), w_ref[...], preferred_element_type=F32)
    o_ref[...] = acc.astype(o_ref.dtype)

    @pl.when(j == P_MIF // PROJ_TN)
    def _gates():
        off = P_MIF % PROJ_TN
        g_ref[...] = acc[:, off:off + LANES]


def _proj(x2d, w_bf):
    t, d = x2d.shape
    n = w_bf.shape[1]
    return pl.pallas_call(
        _proj_kernel,
        grid=(t // PROJ_TM, n // PROJ_TN),
        in_specs=[pl.BlockSpec((PROJ_TM, d), lambda i, j: (i, 0)),
                  pl.BlockSpec((d, PROJ_TN), lambda i, j: (0, j))],
        out_specs=[pl.BlockSpec((PROJ_TM, PROJ_TN), lambda i, j: (i, j)),
                   pl.BlockSpec((PROJ_TM, LANES), lambda i, j: (i, 0))],
        out_shape=[jax.ShapeDtypeStruct((t, n), BF16), jax.ShapeDtypeStruct((t, LANES), F32)],
        compiler_params=_cparams(("parallel", "arbitrary")),
        name="proj",
    )(x2d, w_bf)


def _swa_kernel(sinks_ref, q_ref, kc_ref, kp_ref, vc_ref, vp_ref, o_ref):
    i = pl.program_id(1)
    w = SWA_WINDOW
    dh = SWA_HEAD_DIM
    grp = SWA_HEADS // SWA_KV_HEADS
    q = q_ref[0]
    qi = lax.broadcasted_iota(jnp.int32, (w, 2 * w), 0)
    kj = lax.broadcasted_iota(jnp.int32, (w, 2 * w), 1)
    rel = qi + w - kj
    has_prev = jnp.where(i > 0, 0, w)
    mask = (rel >= 0) & (rel < w) & (kj >= has_prev)
    outs = []
    for kvh in range(SWA_KV_HEADS):
        sl = slice(kvh * dh, (kvh + 1) * dh)
        k_cat = jnp.concatenate([kp_ref[0][:, sl], kc_ref[0][:, sl]], axis=0).astype(BF16)
        v_cat = jnp.concatenate([vp_ref[0][:, sl], vc_ref[0][:, sl]], axis=0).astype(BF16)
        for g in range(grp):
            h = kvh * grp + g
            qh = (q[:, h * dh:(h + 1) * dh] * (dh ** -0.5)).astype(BF16)
            s = lax.dot_general(qh, k_cat, (((1,), (1,)), ((), ())), preferred_element_type=F32)
            s = jnp.where(mask, s, NEG_INF)
            sink = sinks_ref[h]
            m = jnp.maximum(jnp.max(s, axis=-1, keepdims=True), sink)
            p = jnp.exp(s - m)
            den = jnp.sum(p, axis=-1, keepdims=True) + jnp.exp(sink - m)
            o = jnp.dot(p.astype(BF16), v_cat, preferred_element_type=F32)
            outs.append(o / den)
    o_ref[0] = jnp.concatenate(outs, axis=-1).astype(o_ref.dtype)


def _swa(proj3, sinks):
    b, s, _ = proj3.shape
    w = SWA_WINDOW
    nb = s // w
    kw = SWA_KV_HEADS * SWA_HEAD_DIM
    qw = SWA_HEADS * SWA_HEAD_DIM
    cur = lambda col: (lambda bi, i: (bi, i, col))
    prev = lambda col: (lambda bi, i: (bi, jnp.maximum(i - 1, 0), col))
    return pl.pallas_call(
        _swa_kernel,
        grid=(b, nb),
        in_specs=[pl.BlockSpec(memory_space=pltpu.SMEM),
                  pl.BlockSpec((1, w, qw), cur(P_AQ // qw)),
                  pl.BlockSpec((1, w, kw), cur(P_AK // kw)),
                  pl.BlockSpec((1, w, kw), prev(P_AK // kw)),
                  pl.BlockSpec((1, w, kw), cur(P_AV // kw)),
                  pl.BlockSpec((1, w, kw), prev(P_AV // kw))],
        out_specs=pl.BlockSpec((1, w, qw), lambda bi, i: (bi, i, 0)),
        out_shape=jax.ShapeDtypeStruct((b, s, qw), BF16),
        compiler_params=_cparams(("parallel", "arbitrary")),
        name="swa",
    )(sinks, proj3, proj3, proj3, proj3, proj3)


def _mlstm_kernel(qk_ref, v_ref, og_ref, g_ref, cw_ref, cb_ref, gb_ref, ng_ref, y_ref, xbuf, ct_ref, m_ref):
    c = pl.program_id(0)
    nb = qk_ref.shape[0]
    ln = M_CHUNK
    dqk, dv, nh = M_QK_DIM, M_V_DIM, M_HEADS
    halo = SUBLANES

    @pl.when(c == 0)
    def _init():
        xbuf[:, 0:halo, :] = jnp.zeros((nb, halo, xbuf.shape[2]), F32)
        ct_ref[...] = jnp.zeros(ct_ref.shape, F32)
        m_ref[...] = jnp.zeros(m_ref.shape, F32)

    row = lax.broadcasted_iota(jnp.int32, (ln, ln), 0)
    col = lax.broadcasted_iota(jnp.int32, (ln, ln), 1)
    tril = row >= col
    trilf = jnp.where(tril, 1.0, 0.0).astype(F32)
    ones_blk = jnp.where(lax.broadcasted_iota(jnp.int32, (ln, dv), 1) == 0, 1.0, 0.0).astype(F32)

    for b in range(nb):
        xbuf[b, halo:halo + ln, :] = qk_ref[b].astype(F32)
        conv = cb_ref[...]
        for j in range(M_CONV):
            off = halo - (M_CONV - 1) + j
            conv = conv + cw_ref[j:j + 1, :] * xbuf[b, off:off + ln, :]
        qk = conv * jax.nn.sigmoid(conv)
        xbuf[b, 0:halo, :] = xbuf[b, ln:ln + halo, :]

        g = g_ref[b] + gb_ref[...]
        lf = jnp.minimum(g, 0.0) - jnp.log1p(jnp.exp(-jnp.abs(g)))
        cum = jnp.dot(trilf, lf, preferred_element_type=F32, precision=lax.Precision.HIGHEST)
        cum_t = cum.T
        g_t = g.T
        vv = v_ref[b].astype(F32)
        hs = []
        for h in range(nh):
            chain = b * nh + h
            q = (qk[:, h * dqk:(h + 1) * dqk] * (dqk ** -0.5)).astype(BF16)
            k = qk[:, nh * dqk + h * dqk: nh * dqk + (h + 1) * dqk].astype(BF16)
            v_ext = jnp.concatenate([vv[:, h * dv:(h + 1) * dv], ones_blk], axis=-1)
            bc_col = cum[:, nh + h:nh + h + 1]
            bc_row = cum_t[nh + h:nh + h + 1, :]
            i_col = g[:, h:h + 1]
            i_row = g_t[h:h + 1, :]
            m_prev = m_ref[chain:chain + 1, 0:1]
            dmat = jnp.where(tril, bc_col - bc_row + i_row, NEG_INF)
            m_inter = bc_col + m_prev
            m_j = jnp.maximum(m_inter, jnp.max(dmat, axis=-1, keepdims=True))
            w_intra = jnp.exp(dmat - m_j)
            w_inter = jnp.exp(m_inter - m_j)
            s = lax.dot_general(q, k, (((1,), (1,)), ((), ())), preferred_element_type=F32) * w_intra
            ct = ct_ref[chain]
            num_ext = (jnp.dot(s.astype(BF16), v_ext.astype(BF16), preferred_element_type=F32)
                       + w_inter * jnp.dot(q, ct.astype(BF16), preferred_element_type=F32))
            num = num_ext[:, :dv]
            nq = num_ext[:, dv:dv + 1]
            den = jnp.maximum(jnp.abs(nq), jnp.exp(-m_j))
            hs.append(num / den)
            m_new = m_j[ln - 1:ln, :]
            bc_last = bc_col[ln - 1:ln, :]
            w_s = jnp.exp(bc_last - bc_col + i_col - m_new)
            w_c = jnp.exp(bc_last + m_prev - m_new)
            upd = lax.dot_general(k, (w_s * v_ext).astype(BF16), (((0,), (0,)), ((), ())),
                                  preferred_element_type=F32)
            ct_ref[chain] = w_c * ct + upd
            m_ref[chain:chain + 1, :] = jnp.broadcast_to(m_new, (1, m_ref.shape[1]))
        og = og_ref[b].astype(F32)
        outs = []
        for h in range(nh):
            seg = jax.nn.sigmoid(og[:, h * dv:(h + 1) * dv]) * hs[h]
            mu = jnp.mean(seg, axis=-1, keepdims=True)
            cen = seg - mu
            var = jnp.mean(cen * cen, axis=-1, keepdims=True)
            outs.append(cen * lax.rsqrt(var + LN_EPS) * ng_ref[:, h * dv:(h + 1) * dv])
        y_ref[b] = jnp.concatenate(outs, axis=-1).astype(y_ref.dtype)


def _mlstm(proj3, gates3, conv_w, conv_b, gate_b, norm_g):
    b, s, _ = proj3.shape
    ln = M_CHUNK
    wq = 2 * M_HEADS * M_QK_DIM
    wv = M_HEADS * M_V_DIM
    blk = lambda width, off: pl.BlockSpec((b, ln, width), lambda c: (0, c, off // width))
    full = lambda a: pl.BlockSpec(a.shape, lambda c: (0,) * a.ndim)
    gate_b128 = jnp.zeros((1, LANES), F32).at[0, :2 * M_HEADS].set(gate_b)
    conv_b2 = conv_b.reshape(1, wq)
    norm_g2 = norm_g.reshape(1, wv)
    return pl.pallas_call(
        _mlstm_kernel,
        grid=(s // ln,),
        in_specs=[blk(wq, P_MQK), blk(wv, P_MV), blk(wv, P_MO), blk(LANES, 0),
                  full(conv_w), full(conv_b2), full(gate_b128), full(norm_g2)],
        out_specs=pl.BlockSpec((b, ln, wv), lambda c: (0, c, 0)),
        out_shape=jax.ShapeDtypeStruct((b, s, wv), BF16),
        scratch_shapes=[pltpu.VMEM((b, SUBLANES + ln, wq), F32),
                        pltpu.VMEM((b * M_HEADS, M_QK_DIM, 2 * M_V_DIM), F32),
                        pltpu.VMEM((b * M_HEADS, LANES), F32)],
        compiler_params=_cparams(("arbitrary",)),
        name="mlstm",
    )(proj3, proj3, proj3, gates3, conv_w, conv_b2, gate_b128, norm_g2)


def _mla_prep_kernel(cq_ref, ckv_ref, kr_ref, krs_ref, qg_ref, kvg_ref, wqa_ref, wqb_ref, wkv_ref,
                     cos_ref, sin_ref, q_out, k_out, v_out):
    nd, hd = C_NOPE_DIM, C_HEADS
    scale = (C_NOPE_DIM + C_ROPE_DIM) ** -0.5 * LOG2_E
    cos = cos_ref[...]
    sin = sin_ref[...]

    cq = cq_ref[0].astype(F32)
    qn = (cq * lax.rsqrt(jnp.mean(cq * cq, axis=-1, keepdims=True) + RMS_EPS) * qg_ref[...]).astype(BF16)
    qa = jnp.dot(qn, wqa_ref[...], preferred_element_type=F32)
    qb = jnp.dot(qn, wqb_ref[...], preferred_element_type=F32)
    for h in range(hd):
        nope = qa[:, h * 2 * nd: h * 2 * nd + nd]
        rope = qa[:, h * 2 * nd + nd:(h + 1) * 2 * nd] * cos + qb[:, h * nd:(h + 1) * nd] * sin
        q_out[0, h] = (jnp.concatenate([nope, rope], axis=-1) * scale).astype(q_out.dtype)

    ckv = ckv_ref[0].astype(F32)
    kvn = (ckv * lax.rsqrt(jnp.mean(ckv * ckv, axis=-1, keepdims=True) + RMS_EPS) * kvg_ref[...]).astype(BF16)
    kv = jnp.dot(kvn, wkv_ref[...], preferred_element_type=F32)
    kr = kr_ref[0].astype(F32) * cos + krs_ref[0].astype(F32) * sin
    ones_blk = jnp.where(lax.broadcasted_iota(jnp.int32, kr.shape, 1) == 0, 1.0, 0.0).astype(F32)
    for h in range(hd):
        k_out[0, h] = jnp.concatenate([kv[:, h * nd:(h + 1) * nd], kr], axis=-1).astype(k_out.dtype)
        v_h = kv[:, hd * nd + h * nd: hd * nd + (h + 1) * nd]
        v_out[0, h] = jnp.concatenate([v_h, ones_blk], axis=-1).astype(v_out.dtype)


def _mla_prep(proj3, q_norm_g, kv_norm_g, wqa, wqb, wkv, cos128, sin128):
    b, s, _ = proj3.shape
    tm = MLA_TM
    hd, nd = C_HEADS, C_NOPE_DIM
    blk = lambda width, off: pl.BlockSpec((1, tm, width), lambda bi, i: (bi, i, off // width))
    full = lambda a: pl.BlockSpec(a.shape, lambda bi, i: (0,) * a.ndim)
    tab = pl.BlockSpec((tm, LANES), lambda bi, i: (i, 0))
    qg = q_norm_g.reshape(1, -1)
    kvg = kv_norm_g.reshape(1, -1)
    head_out = lambda width: pl.BlockSpec((1, hd, tm, width), lambda bi, i: (bi, 0, i, 0))
    return pl.pallas_call(
        _mla_prep_kernel,
        grid=(b, s // tm),
        in_specs=[blk(C_Q_LORA, P_CQ), blk(C_KV_LORA, P_CKV), blk(LANES, P_CKR), blk(LANES, P_CKRS),
                  full(qg), full(kvg), full(wqa), full(wqb), full(wkv), tab, tab],
        out_specs=[head_out(2 * nd), head_out(2 * nd), head_out(C_V_DIM + LANES)],
        out_shape=[jax.ShapeDtypeStruct((b, hd, s, 2 * nd), BF16),
                   jax.ShapeDtypeStruct((b, hd, s, 2 * nd), BF16),
                   jax.ShapeDtypeStruct((b, hd, s, C_V_DIM + LANES), BF16)],
        compiler_params=_cparams(("parallel", "arbitrary")),
        name="mla_prep",
    )(proj3, proj3, proj3, proj3, qg, kvg, wqa, wqb, wkv, cos128, sin128)


def _flash_kernel(q_ref, k_ref, v_ref, o_ref):
    qi = pl.program_id(2)
    bq = FLASH_BQ
    nh = q_ref.shape[1]
    dv = C_V_DIM

    def attend(q, h, key0, nkeys, m, acc, causal):
        ks = k_ref[0, h, pl.ds(key0, nkeys), :]
        vs = v_ref[0, h, pl.ds(key0, nkeys), :]
        s = lax.dot_general(q, ks, (((1,), (1,)), ((), ())), preferred_element_type=F32)
        if causal:
            row = lax.broadcasted_iota(jnp.int32, s.shape, 0)
            col = lax.broadcasted_iota(jnp.int32, s.shape, 1)
            s = jnp.where(row >= col, s, NEG_INF)
        m_new = jnp.maximum(m, jnp.max(s, axis=-1, keepdims=True))
        p = jnp.exp2(s - m_new)
        acc = jnp.exp2(m - m_new) * acc + jnp.dot(p.astype(BF16), vs, preferred_element_type=F32)
        return m_new, acc

    def full_step(j, carries):
        start = pl.multiple_of(j * bq, bq)
        return tuple(attend(q_ref[0, h], h, start, bq, *carries[h], False) for h in range(nh))

    init = tuple((jnp.full((bq, 1), NEG_INF, F32), jnp.zeros((bq, v_ref.shape[3]), F32)) for _ in range(nh))
    carries = lax.fori_loop(0, qi, full_step, init)

    half = bq // 2
    diag0 = pl.multiple_of(qi * bq, bq)
    for h in range(nh):
        m, acc = attend(q_ref[0, h], h, diag0, half, *carries[h], True)
        m_lo, acc_lo = attend(q_ref[0, h, half:, :], h, diag0 + half, half, m[half:], acc[half:], True)
        cols = slice(h * dv, (h + 1) * dv)
        o_ref[0, :half, cols] = (acc[:half, :dv] / acc[:half, dv:dv + 1]).astype(o_ref.dtype)
        o_ref[0, half:, cols] = (acc_lo[:, :dv] / acc_lo[:, dv:dv + 1]).astype(o_ref.dtype)


def _flash(qc, kc, vc):
    b, hd, s, dk = qc.shape
    dv = C_V_DIM
    bq = FLASH_BQ
    nh = FLASH_HEADS
    return pl.pallas_call(
        _flash_kernel,
        grid=(b, hd // nh, s // bq),
        in_specs=[pl.BlockSpec((1, nh, bq, dk), lambda bi, h, i: (bi, h, i, 0)),
                  pl.BlockSpec((1, nh, s, dk), lambda bi, h, i: (bi, h, 0, 0), pipeline_mode=pl.Buffered(1)),
                  pl.BlockSpec((1, nh, s, vc.shape[3]), lambda bi, h, i: (bi, h, 0, 0),
                               pipeline_mode=pl.Buffered(1))],
        out_specs=pl.BlockSpec((1, bq, nh * dv), lambda bi, h, i: (bi, i, h)),
        out_shape=jax.ShapeDtypeStruct((b, s, hd * dv), BF16),
        compiler_params=_cparams(("parallel", "parallel", "arbitrary")),
        name="flash",
    )(qc, kc, vc)


def _layer_norm(z, g, b):
    mu = jnp.mean(z, axis=-1, keepdims=True)
    cen = z - mu
    var = jnp.mean(cen * cen, axis=-1, keepdims=True)
    return cen * lax.rsqrt(var + LN_EPS) * g + b


def _outproj_kernel(ya_ref, yb_ref, yc_ref, x_ref, w_ref, g_ref, b_ref, wr_ref, br_ref,
                    x1_ref, x1rc_ref, e_ref, gate_ref, rank_ref, cnt_ref, cnt_sc):
    i = pl.program_id(0)
    tm = x_ref.shape[0]
    wa, wb = ya_ref.shape[1], yb_ref.shape[1]
    mix = jnp.dot(ya_ref[...], w_ref[0:wa, :], preferred_element_type=F32)
    mix = mix + jnp.dot(yb_ref[...], w_ref[wa:wa + wb, :], preferred_element_type=F32)
    mix = mix + jnp.dot(yc_ref[...], w_ref[wa + wb:, :], preferred_element_type=F32)
    x1 = _layer_norm(DN_ALPHA * x_ref[...] + mix, g_ref[...], b_ref[...])
    x1_ref[...] = x1
    _store_rc(x1rc_ref, x1)

    logits = jnp.dot(x1.astype(BF16), wr_ref[...], preferred_element_type=F32) + br_ref[...]
    lane = lax.broadcasted_iota(jnp.int32, logits.shape, 1)
    logits = jnp.where(lane < N_EXPERTS, logits, NEG_INF)
    e_out = jnp.zeros(logits.shape, jnp.int32)
    p_out = jnp.zeros(logits.shape, F32)
    top = None
    den = None
    onehots = []
    for r in range(TOP_K):
        mx = jnp.max(logits, axis=-1, keepdims=True)
        idx = jnp.min(jnp.where(logits == mx, lane, LANES), axis=-1, keepdims=True)
        if r == 0:
            top = mx
        p = jnp.exp(mx - top)
        den = p if r == 0 else den + p
        sel = lane == idx
        onehots.append(jnp.where(sel, 1.0, 0.0).astype(F32))
        e_out = jnp.where(lane == r, idx, e_out)
        p_out = jnp.where(lane == r, p, p_out)
        logits = jnp.where(sel, NEG_INF, logits)
    e_ref[...] = e_out
    gate_ref[...] = p_out / den

    @pl.when(i == 0)
    def _init():
        cnt_sc[...] = jnp.zeros(cnt_sc.shape, F32)

    oh_sum = onehots[0] + onehots[1] + onehots[2] + onehots[3]
    row = lax.broadcasted_iota(jnp.int32, (tm, tm), 0)
    col = lax.broadcasted_iota(jnp.int32, (tm, tm), 1)
    before = jnp.where(row > col, 1.0, 0.0).astype(BF16)
    base = jnp.dot(before, oh_sum.astype(BF16), preferred_element_type=F32) + cnt_sc[0:1, :]
    rank_out = jnp.zeros(logits.shape, jnp.int32)
    for r in range(TOP_K):
        rk = jnp.sum(onehots[r] * base, axis=-1, keepdims=True)
        rank_out = jnp.where(lane == r, rk.astype(jnp.int32), rank_out)
    rank_ref[...] = rank_out
    total = cnt_sc[0:1, :] + jnp.sum(oh_sum, axis=0, keepdims=True)
    cnt_sc[0:1, :] = total
    cnt_ref[...] = jnp.broadcast_to(total, cnt_ref.shape).astype(jnp.int32)


def _outproj(ya, yb, yc, x2d, w_bf, g, b, wr128, br128):
    t, d = x2d.shape
    tm = OUT_TM
    rows = lambda a: pl.BlockSpec((tm, a.shape[1]), lambda i: (i, 0))
    full = lambda a: pl.BlockSpec(a.shape, lambda i: (0,) * a.ndim, pipeline_mode=pl.Buffered(1))
    g2, b2 = g.reshape(1, d), b.reshape(1, d)
    lane_blk = pl.BlockSpec((tm, LANES), lambda i: (i, 0))
    return pl.pallas_call(
        _outproj_kernel,
        grid=(t // tm,),
        in_specs=[rows(ya), rows(yb), rows(yc), rows(x2d), full(w_bf), full(g2), full(b2), full(wr128), full(br128)],
        out_specs=[pl.BlockSpec((tm, d), lambda i: (i, 0)),
                   pl.BlockSpec((tm * ROW_TILES, LANES), lambda i: (i, 0)),
                   lane_blk, lane_blk, lane_blk,
                   pl.BlockSpec((SUBLANES, LANES), lambda i: (0, 0))],
        out_shape=[jax.ShapeDtypeStruct((t, d), F32),
                   jax.ShapeDtypeStruct((t * ROW_TILES, LANES), U32),
                   jax.ShapeDtypeStruct((t, LANES), jnp.int32),
                   jax.ShapeDtypeStruct((t, LANES), F32),
                   jax.ShapeDtypeStruct((t, LANES), jnp.int32),
                   jax.ShapeDtypeStruct((SUBLANES, LANES), jnp.int32)],
        scratch_shapes=[pltpu.VMEM((SUBLANES, LANES), F32)],
        compiler_params=_cparams(("arbitrary",)),
        name="outproj",
    )(ya, yb, yc, x2d, w_bf, g2, b2, wr128, br128)


def _rc_rows(row, n=1):
    return pl.ds(pl.multiple_of(row * ROW_TILES, ROW_TILES), n * ROW_TILES)


def _dispatch_kernel(pos_ref, cnt_ref, pst_ref, pad_ref, nblk_ref, x_hbm, xs_hbm, buf, zbuf, sem_in, sem_out, sem_z):
    i = pl.program_id(0)
    n = pl.num_programs(0)
    tm = DSP_TM
    nslot = buf.shape[0]

    def in_copy(blk, slot):
        return pltpu.make_async_copy(x_hbm.at[_rc_rows(blk * tm, tm)], buf.at[slot], sem_in.at[slot])

    def row_out(slot, t, p):
        return pltpu.make_async_copy(buf.at[slot, _rc_rows(t)], xs_hbm.at[_rc_rows(p)], sem_out.at[slot])

    def wait_outs(slot):
        for _ in range(TOP_K):
            pltpu.make_async_copy(buf.at[slot], xs_hbm.at[_rc_rows(0, tm)], sem_out.at[slot]).wait()

    slot = lax.rem(i, nslot)

    @pl.when(i == 0)
    def _first_in():
        in_copy(0, 0).start()

    @pl.when(i >= 2)
    def _free_slot():
        wait_outs(lax.rem(i + 1, nslot))

    @pl.when(i + 1 < n)
    def _next_in():
        in_copy(i + 1, lax.rem(i + 1, nslot)).start()

    in_copy(i, slot).wait()
    base = i * (tm * TOP_K)

    def issue(t, _):
        for k in range(TOP_K):
            row_out(slot, t, pos_ref[base + t * TOP_K + k]).start()
        return 0

    lax.fori_loop(0, tm, issue, 0, unroll=2)

    @pl.when(i == n - 1)
    def _finish():
        wait_outs(lax.rem(i + 2, nslot))
        wait_outs(slot)
        zbuf[...] = jnp.zeros(zbuf.shape, zbuf.dtype)

        def pad_rows(e, _):
            cnt = cnt_ref[e]
            first = pst_ref[e] + cnt
            npad = pad_ref[e] - cnt

            def zero_row(q):
                return pltpu.make_async_copy(zbuf.at[_rc_rows(0)], xs_hbm.at[_rc_rows(first + q)], sem_z)

            def start(q, c):
                zero_row(q).start()
                return c

            def wait(q, c):
                zero_row(q).wait()
                return c

            lax.fori_loop(0, npad, start, 0)
            lax.fori_loop(0, npad, wait, 0)
            return 0

        lax.fori_loop(0, N_EXPERTS, pad_rows, 0)

        def tail_block(bk, _):
            cp = pltpu.make_async_copy(zbuf, xs_hbm.at[_rc_rows(bk * MOE_PAD, MOE_PAD)], sem_z)
            cp.start()
            cp.wait()
            return 0

        lax.fori_loop(nblk_ref[0], xs_hbm.shape[0] // (MOE_PAD * ROW_TILES), tail_block, 0)


def _dispatch(pos, counts, pad_start, padded, nblk, x1rc, rows):
    t = x1rc.shape[0] // ROW_TILES
    tm = DSP_TM
    return pl.pallas_call(
        _dispatch_kernel,
        grid_spec=pltpu.PrefetchScalarGridSpec(
            num_scalar_prefetch=5,
            grid=(t // tm,),
            in_specs=[pl.BlockSpec(memory_space=pl.ANY)],
            out_specs=pl.BlockSpec(memory_space=pl.ANY),
            scratch_shapes=[pltpu.VMEM((3, tm * ROW_TILES, LANES), U32),
                            pltpu.VMEM((MOE_PAD * ROW_TILES, LANES), U32),
                            pltpu.SemaphoreType.DMA((3,)),
                            pltpu.SemaphoreType.DMA((3,)),
                            pltpu.SemaphoreType.DMA(())]),
        out_shape=jax.ShapeDtypeStruct((rows * ROW_TILES, LANES), U32),
        compiler_params=_cparams(("arbitrary",)),
        name="dispatch",
    )(pos, counts, pad_start, padded, nblk, x1rc)


def _moe_kernel(layer, sbe_ref, sbr_ref, sbn_ref, nblk_ref, xs_hbm, wgu_hbm, wdn_hbm, bgu_ref, bd_ref, y_hbm,
                xbuf, acc, stage_in, stage_out, wg_f, wu_f, wd_f, wg_bf, wu_bf, wd_bf, sem_in, sem_out, sem_w):
    sb = pl.program_id(0)
    nsb = pl.num_programs(0)
    tf = wg_f.shape[2]
    dff = wdn_hbm.shape[2]
    nj = dff // tf
    n = sbn_ref[sb]
    row0 = sbr_ref[sb]
    nchunk = n // MOE_PAD
    cur = lax.rem(sb, 2)
    big, huge = 2 * MOE_PAD, 4 * MOE_PAD

    def weight_copies(sb_idx, j, slot):
        e = sbe_ref[sb_idx]
        cols = pl.ds(pl.multiple_of(j * tf, tf), tf)
        up_cols = pl.ds(pl.multiple_of(dff + j * tf, tf), tf)
        return (pltpu.make_async_copy(wgu_hbm.at[layer, e, :, cols], wg_f.at[slot], sem_w.at[slot]),
                pltpu.make_async_copy(wgu_hbm.at[layer, e, :, up_cols], wu_f.at[slot], sem_w.at[slot]),
                pltpu.make_async_copy(wdn_hbm.at[layer, e, cols, :], wd_f.at[slot], sem_w.at[slot]))

    def chunk_rows(c):
        return pl.ds(pl.multiple_of(c * MOE_PAD, MOE_PAD), MOE_PAD)

    def chunk_in(first_row, c):
        return pltpu.make_async_copy(xs_hbm.at[_rc_rows(first_row + c * MOE_PAD, MOE_PAD)], stage_in, sem_in)

    def chunk_out(c, slot):
        return pltpu.make_async_copy(stage_out.at[slot], y_hbm.at[_rc_rows(row0 + c * MOE_PAD, MOE_PAD)],
                                     sem_out.at[slot])

    def wait_outs(count):
        @pl.when(count >= 2)
        def _older():
            chunk_out(0, lax.rem(count, 2)).wait()

        @pl.when(count >= 1)
        def _newest():
            chunk_out(0, lax.rem(count + 1, 2)).wait()

    def convert_in(slot, c):
        lo, hi = _load_rc(stage_in, MOE_PAD)
        xbuf[slot, chunk_rows(c), :] = jnp.concatenate([lo.astype(BF16), hi.astype(BF16)], axis=-1)

    @pl.when(sb == 0)
    def _load_first():
        for cp in weight_copies(0, 0, 0):
            cp.start()

        def load(c, _):
            chunk_in(row0, c).start()
            chunk_in(row0, c).wait()
            convert_in(0, c)
            return 0

        lax.fori_loop(0, nchunk, load, 0)

    n_next = sbn_ref[sb + 1]

    def ffn(j, start, size, wg, wu, wd):
        rs = pl.ds(start, size)
        xr = xbuf[cur, rs, :]
        cols = pl.ds(pl.multiple_of(j * tf, tf), tf)
        up_cols = pl.ds(pl.multiple_of(dff + j * tf, tf), tf)
        gt = jnp.dot(xr, wg, preferred_element_type=F32) + bgu_ref[:, cols]
        up = jnp.dot(xr, wu, preferred_element_type=F32) + bgu_ref[:, up_cols]
        gt = jnp.minimum(gt, SWIGLU_LIMIT)
        up = jnp.clip(up, -SWIGLU_LIMIT, SWIGLU_LIMIT)
        act = ((up + 1.0) * gt * jax.nn.sigmoid(SWIGLU_ALPHA * gt)).astype(BF16)
        half = acc.shape[1] // 2
        for cs in (slice(0, half), slice(half, 2 * half)):
            acc[rs, cs] = acc[rs, cs] + jnp.dot(act, wd[:, cs], preferred_element_type=F32)

    def first_chunk(j, wslot, size):
        wg = wg_f[wslot].astype(BF16)
        wu = wu_f[wslot].astype(BF16)
        wd = wd_f[wslot].astype(BF16)
        wg_bf[...] = wg
        wu_bf[...] = wu
        wd_bf[...] = wd
        ffn(j, 0, size, wg, wu, wd)

    def hidden_tile(j, _):
        wslot = lax.rem(j, 2)
        for cp in weight_copies(sb, j, wslot):
            cp.wait()

        @pl.when(j + 1 < nj)
        def _next_tile():
            for cp in weight_copies(sb, j + 1, 1 - wslot):
                cp.start()

        @pl.when((j + 1 == nj) & (n_next > 0))
        def _next_super_block():
            for cp in weight_copies(sb + 1, 0, 0):
                cp.start()

        fetch_next = j * MOE_PAD < n_next

        @pl.when(fetch_next)
        def _start_next():
            chunk_in(sbr_ref[sb + 1], j).start()

        @pl.when((j == 1) & (sb > 0))
        def _drain_prev_outs():
            wait_outs(sbn_ref[jnp.maximum(sb - 1, 0)] // MOE_PAD)

        @pl.when(j == 0)
        def _init_acc():
            def init(c, _):
                acc[chunk_rows(c), :] = jnp.broadcast_to(bd_ref[...], (MOE_PAD, acc.shape[1]))
                return 0

            lax.fori_loop(0, nchunk, init, 0)

        def rest(start, size):
            ffn(j, start, size, wg_bf[...], wu_bf[...], wd_bf[...])

        @pl.when(n >= huge)
        def _rows_huge():
            first_chunk(j, wslot, huge)

            @pl.when(n - huge >= big)
            def _then_big():
                rest(huge, big)

            @pl.when(lax.rem(n, big) != 0)
            def _then_pad():
                rest(pl.multiple_of(n - MOE_PAD, MOE_PAD), MOE_PAD)

        @pl.when((n >= big) & (n < huge))
        def _rows_big():
            first_chunk(j, wslot, big)

            @pl.when(n > big)
            def _then_pad():
                rest(big, MOE_PAD)

        @pl.when(n < big)
        def _rows_small():
            first_chunk(j, wslot, MOE_PAD)

        @pl.when(j == nj - 1)
        def _store_rows():
            def store(c, _):
                slot = lax.rem(c, 2)

                @pl.when(c >= 2)
                def _slot_free():
                    chunk_out(c - 2, slot).wait()

                _store_rc(stage_out.at[slot], acc[chunk_rows(c), :])
                chunk_out(c, slot).start()
                return 0

            lax.fori_loop(0, nchunk, store, 0)

            @pl.when((sb == nsb - 1) | (n_next == 0))
            def _no_later_tile():
                wait_outs(nchunk)

        @pl.when(fetch_next)
        def _finish_next():
            chunk_in(sbr_ref[sb + 1], j).wait()
            convert_in(1 - cur, j)

        return 0

    @pl.when(n > 0)
    def _super_block():
        lax.fori_loop(0, nj, hidden_tile, 0)

    @pl.when(sb == nsb - 1)
    def _zero_tail():
        stage_out[0] = jnp.zeros(stage_out.shape[1:], stage_out.dtype)

        def tail_block(bk, _):
            cp = pltpu.make_async_copy(stage_out.at[0], y_hbm.at[_rc_rows(bk * MOE_PAD, MOE_PAD)], sem_out.at[0])
            cp.start()
            cp.wait()
            return 0

        lax.fori_loop(nblk_ref[0], y_hbm.shape[0] // (MOE_PAD * ROW_TILES), tail_block, 0)


def _moe(layer, sb_e, sb_row0, sb_n, nblk, xs, w_gate_up, b_gate_up, w_down, b_down):
    rows = xs.shape[0] // ROW_TILES
    d = D_MODEL
    dff = w_down.shape[2]
    tf = MOE_TF
    nj = dff // tf
    nsb = sb_e.shape[0] - 1
    assert nj % 2 == 0 and MOE_TMAX // MOE_PAD <= nj and MOE_TMAX == 6 * MOE_PAD
    bgu = b_gate_up.reshape(DEPTH, N_EXPERTS, 1, 2 * dff)
    bdn = b_down.reshape(DEPTH, N_EXPERTS, 1, d)
    expert_row = lambda width: pl.BlockSpec((None, None, 1, width), lambda sb, e, r, n, nb: (layer, e[sb], 0, 0))
    return pl.pallas_call(
        functools.partial(_moe_kernel, layer),
        grid_spec=pltpu.PrefetchScalarGridSpec(
            num_scalar_prefetch=4,
            grid=(nsb,),
            in_specs=[pl.BlockSpec(memory_space=pl.ANY),
                      pl.BlockSpec(memory_space=pl.ANY),
                      pl.BlockSpec(memory_space=pl.ANY),
                      expert_row(2 * dff),
                      expert_row(d)],
            out_specs=pl.BlockSpec(memory_space=pl.ANY),
            scratch_shapes=[pltpu.VMEM((2, MOE_TMAX, d), BF16),
                            pltpu.VMEM((MOE_TMAX, d), F32),
                            pltpu.VMEM((MOE_PAD * ROW_TILES, LANES), U32),
                            pltpu.VMEM((2, MOE_PAD * ROW_TILES, LANES), U32),
                            pltpu.VMEM((2, d, tf), F32),
                            pltpu.VMEM((2, d, tf), F32),
                            pltpu.VMEM((2, tf, d), F32),
                            pltpu.VMEM((d, tf), BF16),
                            pltpu.VMEM((d, tf), BF16),
                            pltpu.VMEM((tf, d), BF16),
                            pltpu.SemaphoreType.DMA(()),
                            pltpu.SemaphoreType.DMA((2,)),
                            pltpu.SemaphoreType.DMA((2,))]),
        out_shape=jax.ShapeDtypeStruct((rows * ROW_TILES, LANES), U32),
        compiler_params=_cparams(("arbitrary",)),
        name="moe",
    )(sb_e, sb_row0, sb_n, nblk, xs, w_gate_up, w_down, bgu, bdn)


def _combine_kernel(pos_ref, y_hbm, gate_ref, x1_ref, g_ref, b_ref, o_ref, buf, sem):
    i = pl.program_id(0)
    n = pl.num_programs(0)
    tm = CMB_TM
    slot = lax.rem(i, 2)

    def issue(blk, dst_slot):
        base = blk * (tm * TOP_K)

        def body(t, _):
            for k in range(TOP_K):
                p = pos_ref[base + t * TOP_K + k]
                pltpu.make_async_copy(y_hbm.at[_rc_rows(p)], buf.at[dst_slot, k, _rc_rows(t)], sem.at[dst_slot]).start()
            return 0

        lax.fori_loop(0, tm, body, 0, unroll=2)

    @pl.when(i == 0)
    def _first():
        issue(0, 0)

    @pl.when(i + 1 < n)
    def _next():
        issue(i + 1, 1 - slot)

    for k in range(TOP_K):
        pltpu.make_async_copy(y_hbm.at[_rc_rows(0, tm)], buf.at[slot, k], sem.at[slot]).wait()
    gate = gate_ref[...]
    ffn_lo, ffn_hi = None, None
    for k in range(TOP_K):
        lo, hi = _load_rc(buf.at[slot, k], tm)
        gk = gate[:, k:k + 1]
        ffn_lo = gk * lo if k == 0 else ffn_lo + gk * lo
        ffn_hi = gk * hi if k == 0 else ffn_hi + gk * hi
    ffn = jnp.concatenate([ffn_lo, ffn_hi], axis=-1)
    o_ref[...] = _layer_norm(DN_ALPHA * x1_ref[...] + ffn, g_ref[...], b_ref[...])


def _combine(pos_flat, y, gate, x1, g, b):
    t, d = x1.shape
    tm = CMB_TM
    g2, b2 = g.reshape(1, d), b.reshape(1, d)
    return pl.pallas_call(
        _combine_kernel,
        grid_spec=pltpu.PrefetchScalarGridSpec(
            num_scalar_prefetch=1,
            grid=(t // tm,),
            in_specs=[pl.BlockSpec(memory_space=pl.ANY),
                      pl.BlockSpec((tm, LANES), lambda i, p: (i, 0)),
                      pl.BlockSpec((tm, d), lambda i, p: (i, 0)),
                      pl.BlockSpec((1, d), lambda i, p: (0, 0)),
                      pl.BlockSpec((1, d), lambda i, p: (0, 0))],
            out_specs=pl.BlockSpec((tm, d), lambda i, p: (i, 0)),
            scratch_shapes=[pltpu.VMEM((2, TOP_K, tm * ROW_TILES, LANES), U32), pltpu.SemaphoreType.DMA((2,))]),
        out_shape=jax.ShapeDtypeStruct((t, d), F32),
        compiler_params=_cparams(("arbitrary",)),
        name="combine",
    )(pos_flat, y, gate, x1, g2, b2)


def _swap_halves(w):
    half = w.shape[-1] // 2
    return jnp.concatenate([w[..., half:], w[..., :half]], axis=-1)


def _layout_w_in(w):
    d = w.shape[0]
    w = w.astype(BF16)
    widths = (512, 128, 128, 256, 256, 512, 512, 8, 512, 256, 64)
    offs = [0]
    for wd in widths:
        offs.append(offs[-1] + wd)
    a_q, a_k, a_v, m_q, m_k, m_v, m_o, m_if, c_q, c_kv, c_kr = [w[:, offs[i]:offs[i + 1]] for i in range(len(widths))]
    z = lambda n: jnp.zeros((d, n), w.dtype)
    out = jnp.concatenate([a_q, m_v, m_o, c_q, m_q, m_k, c_kv, a_k, a_v,
                           c_kr, z(LANES - 64), m_if, z(LANES - 8), _swap_halves(c_kr), z(LANES - 64)], axis=1)
    return out


def _layout_w_uq(w):
    r = w.shape[0]
    w3 = w.astype(BF16).reshape(r, C_HEADS, C_NOPE_DIM + C_ROPE_DIM)
    nope, rope = w3[..., :C_NOPE_DIM], w3[..., C_NOPE_DIM:]
    z = jnp.zeros((r, C_HEADS, LANES - C_ROPE_DIM), BF16)
    wa = jnp.concatenate([nope, rope, z], axis=-1).reshape(r, -1)
    wb = jnp.concatenate([_swap_halves(rope), z], axis=-1).reshape(r, -1)
    return wa, wb


def _layout_w_ukv(w):
    r = w.shape[0]
    w3 = w.astype(BF16).reshape(r, C_HEADS, C_NOPE_DIM + C_V_DIM)
    return jnp.concatenate([w3[..., :C_NOPE_DIM].reshape(r, -1), w3[..., C_NOPE_DIM:].reshape(r, -1)], axis=-1)


def _rope_tables(seq):
    dim = C_ROPE_DIM
    inv = 1.0 / (ROPE_THETA ** (jnp.arange(0, dim, 2, dtype=F32) / dim))
    ang = jnp.arange(seq, dtype=F32)[:, None] * inv[None, :]
    cos, sin = jnp.cos(ang), jnp.sin(ang)
    z = jnp.zeros((seq, LANES - dim), F32)
    return jnp.concatenate([cos, cos, z], axis=-1), jnp.concatenate([-sin, sin, z], axis=-1)


def _route(top_e, rank, counts, nsb):
    padded = (counts + MOE_PAD - 1) // MOE_PAD * MOE_PAD
    pad_end = jnp.cumsum(padded)
    pad_start = pad_end - padded
    is_e = top_e[:, :, None] == jnp.arange(N_EXPERTS, dtype=jnp.int32)[None, None, :]
    pos = (jnp.sum(jnp.where(is_e, pad_start[None, None, :], 0), axis=-1) + rank).reshape(-1).astype(jnp.int32)
    nblk = (pad_end[-1:] // MOE_PAD).astype(jnp.int32)
    nsb_e = (padded + MOE_TMAX - 1) // MOE_TMAX
    sb_end = jnp.cumsum(nsb_e)
    sb_start = sb_end - nsb_e
    sb_idx = jnp.arange(nsb + 1, dtype=jnp.int32)
    n_valid = sb_end[-1]
    sb_eff = jnp.minimum(sb_idx, n_valid - 1)
    sb_e = jnp.searchsorted(sb_end, sb_eff, side="right").astype(jnp.int32)
    part = sb_eff - sb_start[sb_e]
    sb_row0 = pad_start[sb_e] + part * MOE_TMAX
    sb_n = jnp.where(sb_idx < n_valid, jnp.minimum(padded[sb_e] - part * MOE_TMAX, MOE_TMAX), 0)
    tables = (counts, pad_start.astype(jnp.int32), padded.astype(jnp.int32), nblk)
    return pos, tables, sb_e, sb_row0.astype(jnp.int32), sb_n.astype(jnp.int32)


def kernel(x, w_in, conv_w, conv_b, m_gate_b, m_norm_g, sinks, q_norm_g, w_uq, kv_norm_g, w_ukv, w_out,
           ln1_g, ln1_b, w_router, b_router, w_gate_up, b_gate_up, w_down, b_down, ln2_g, ln2_b):
    b, s, d = x.shape
    t = b * s
    n_assign = t * TOP_K
    rows = (n_assign // MOE_PAD + N_EXPERTS) * MOE_PAD
    nsb = N_EXPERTS + rows // MOE_TMAX
    cos128, sin128 = _rope_tables(s)
    xt = x.reshape(t, d)
    for l in range(DEPTH):
        w_in_bf = _layout_w_in(w_in[l])
        wqa, wqb = _layout_w_uq(w_uq[l])
        wkv = _layout_w_ukv(w_ukv[l])
        w_out_bf = w_out[l].astype(BF16)
        wr128 = jnp.zeros((d, LANES), BF16).at[:, :N_EXPERTS].set(w_router[l].astype(BF16))
        br128 = jnp.zeros((1, LANES), F32).at[0, :N_EXPERTS].set(b_router[l])

        proj, gates = _proj(xt, w_in_bf)
        proj3 = proj.reshape(b, s, N_PROJ)
        y_a = _swa(proj3, sinks[l])
        y_b = _mlstm(proj3, gates.reshape(b, s, LANES), conv_w[l], conv_b[l], m_gate_b[l], m_norm_g[l])
        qc, kc, vc = _mla_prep(proj3, q_norm_g[l], kv_norm_g[l], wqa, wqb, wkv, cos128, sin128)
        y_c = _flash(qc, kc, vc)
        x1, x1rc, top_e, gate, rank, cnt = _outproj(y_a.reshape(t, -1), y_b.reshape(t, -1), y_c.reshape(t, -1), xt,
                                                    w_out_bf, ln1_g[l], ln1_b[l], wr128, br128)
        pos, tables, sb_e, sb_row0, sb_n = _route(top_e[:, :TOP_K], rank[:, :TOP_K], cnt[0, :N_EXPERTS], nsb)
        xs = _dispatch(pos, *tables, x1rc, rows)
        y = _moe(l, sb_e, sb_row0, sb_n, tables[3], xs, w_gate_up, b_gate_up, w_down, b_down)
        xt = _combine(pos, y, gate, x1, ln2_g[l], ln2_b[l])
    return xt.reshape(b, s, d)
```

```python
import functools

import jax
import jax.numpy as jnp
from jax import lax
from jax.experimental import pallas as pl
from jax.experimental.pallas import tpu as pltpu

F32 = jnp.float32
BF16 = jnp.bfloat16
NEG_INF = float("-inf")
LOG2_E = 1.4426950408889634

D_MODEL = 2048
DEPTH = 2
SWA_HEADS, SWA_KV_HEADS, SWA_HEAD_DIM, SWA_WINDOW = 8, 2, 64, 128
M_HEADS, M_QK_DIM, M_V_DIM, M_CONV = 4, 64, 128, 4
C_HEADS, C_NOPE_DIM, C_ROPE_DIM, C_V_DIM = 8, 128, 64, 128
C_Q_LORA, C_KV_LORA = 512, 256
ROPE_THETA = 10000.0
N_EXPERTS, TOP_K = 32, 4
SWIGLU_LIMIT, SWIGLU_ALPHA = 7.0, 1.702
DN_ALPHA = (2 * DEPTH) ** 0.25
LN_EPS, RMS_EPS = 1e-5, 1e-6

LANES = 128
SUBLANES = 8
VMEM_LIMIT = 56 * 1024 * 1024

P_AQ, P_MV, P_MO, P_CQ, P_MQK, P_CKV, P_AK, P_AV, P_CKR, P_MIF, P_CKRS = (
    0, 512, 1024, 1536, 2048, 2560, 2816, 2944, 3072, 3200, 3328)
N_PROJ = 3456

PROJ_TM, PROJ_TN = 1024, 1152
M_CHUNK = 128
MLA_TM = 512
FLASH_BQ = 1024
FLASH_HEADS = 4
OUT_TM = 512
MOE_PAD = 256
MOE_TMAX = 1536
MOE_TF = 256
MOE_W_PARTS = 4
CMB_TM = 64
DSP_TM = 64

ROW_TILES = D_MODEL // 2 // LANES
U32 = jnp.uint32
HI_MASK = 0xFFFF0000


def _cparams(sem, vmem=VMEM_LIMIT):
    return pltpu.CompilerParams(dimension_semantics=sem, vmem_limit_bytes=vmem)


def _bf16_bits(x):
    return lax.bitcast_convert_type(x.astype(BF16).astype(F32), U32)


def _store_rc(ref, val):
    n, d = val.shape
    words = (_bf16_bits(val[:, :d // 2]) >> 16) | (_bf16_bits(val[:, d // 2:]) & U32(HI_MASK))
    x = jnp.stack([words[:, s * LANES:(s + 1) * LANES] for s in range(ROW_TILES)], axis=0)
    ref[...] = pltpu.einshape("stl->tsl", x).reshape(n * ROW_TILES, LANES)


def _load_rc(ref, n):
    x = pltpu.einshape("tsl->stl", ref[...].reshape(n, ROW_TILES, LANES))
    words = jnp.concatenate([x[s] for s in range(ROW_TILES)], axis=-1)
    lo = lax.bitcast_convert_type(words << 16, F32)
    hi = lax.bitcast_convert_type(words & U32(HI_MASK), F32)
    return lo, hi


def _proj_kernel(x_ref, w_ref, o_ref, g_ref):
    j = pl.program_id(1)
    acc = jnp.dot(x_ref[...].astype(BF16), w_ref[...], preferred_element_type=F32)
    o_ref[...] = acc.astype(o_ref.dtype)

    @pl.when(j == P_MIF // PROJ_TN)
    def _gates():
        off = P_MIF % PROJ_TN
        g_ref[...] = acc[:, off:off + LANES]


def _proj(x2d, w_bf):
    t, d = x2d.shape
    n = w_bf.shape[1]
    return pl.pallas_call(
        _proj_kernel,
        grid=(t // PROJ_TM, n // PROJ_TN),
        in_specs=[pl.BlockSpec((PROJ_TM, d), lambda i, j: (i, 0)),
                  pl.BlockSpec((d, PROJ_TN), lambda i, j: (0, j))],
        out_specs=[pl.BlockSpec((PROJ_TM, PROJ_TN), lambda i, j: (i, j)),
                   pl.BlockSpec((PROJ_TM, LANES), lambda i, j: (i, 0))],
        out_shape=[jax.ShapeDtypeStruct((t, n), BF16), jax.ShapeDtypeStruct((t, LANES), F32)],
        compiler_params=_cparams(("parallel", "arbitrary")),
        name="proj",
    )(x2d, w_bf)


def _swa_kernel(sinks_ref, q_ref, kc_ref, kp_ref, vc_ref, vp_ref, o_ref):
    i = pl.program_id(1)
    w = SWA_WINDOW
    dh = SWA_HEAD_DIM
    grp = SWA_HEADS // SWA_KV_HEADS
    q = q_ref[0]
    qi = lax.broadcasted_iota(jnp.int32, (w, 2 * w), 0)
    kj = lax.broadcasted_iota(jnp.int32, (w, 2 * w), 1)
    rel = qi + w - kj
    has_prev = jnp.where(i > 0, 0, w)
    mask = (rel >= 0) & (rel < w) & (kj >= has_prev)
    outs = []
    for kvh in range(SWA_KV_HEADS):
        sl = slice(kvh * dh, (kvh + 1) * dh)
        k_cat = jnp.concatenate([kp_ref[0][:, sl], kc_ref[0][:, sl]], axis=0).astype(BF16)
        v_cat = jnp.concatenate([vp_ref[0][:, sl], vc_ref[0][:, sl]], axis=0).astype(BF16)
        for g in range(grp):
            h = kvh * grp + g
            qh = (q[:, h * dh:(h + 1) * dh] * (dh ** -0.5)).astype(BF16)
            s = lax.dot_general(qh, k_cat, (((1,), (1,)), ((), ())), preferred_element_type=F32)
            s = jnp.where(mask, s, NEG_INF)
            sink = sinks_ref[h]
            m = jnp.maximum(jnp.max(s, axis=-1, keepdims=True), sink)
            p = jnp.exp(s - m)
            den = jnp.sum(p, axis=-1, keepdims=True) + jnp.exp(sink - m)
            o = jnp.dot(p.astype(BF16), v_cat, preferred_element_type=F32)
            outs.append(o / den)
    o_ref[0] = jnp.concatenate(outs, axis=-1).astype(o_ref.dtype)


def _swa(proj3, sinks):
    b, s, _ = proj3.shape
    w = SWA_WINDOW
    nb = s // w
    kw = SWA_KV_HEADS * SWA_HEAD_DIM
    qw = SWA_HEADS * SWA_HEAD_DIM
    cur = lambda col: (lambda bi, i: (bi, i, col))
    prev = lambda col: (lambda bi, i: (bi, jnp.maximum(i - 1, 0), col))
    return pl.pallas_call(
        _swa_kernel,
        grid=(b, nb),
        in_specs=[pl.BlockSpec(memory_space=pltpu.SMEM),
                  pl.BlockSpec((1, w, qw), cur(P_AQ // qw)),
                  pl.BlockSpec((1, w, kw), cur(P_AK // kw)),
                  pl.BlockSpec((1, w, kw), prev(P_AK // kw)),
                  pl.BlockSpec((1, w, kw), cur(P_AV // kw)),
                  pl.BlockSpec((1, w, kw), prev(P_AV // kw))],
        out_specs=pl.BlockSpec((1, w, qw), lambda bi, i: (bi, i, 0)),
        out_shape=jax.ShapeDtypeStruct((b, s, qw), BF16),
        compiler_params=_cparams(("parallel", "arbitrary")),
        name="swa",
    )(sinks, proj3, proj3, proj3, proj3, proj3)


def _mlstm_kernel(qk_ref, v_ref, og_ref, g_ref, cw_ref, cb_ref, gb_ref, ng_ref, y_ref, xbuf, ct_ref, m_ref):
    c = pl.program_id(0)
    nb = qk_ref.shape[0]
    ln = M_CHUNK
    dqk, dv, nh = M_QK_DIM, M_V_DIM, M_HEADS
    halo = SUBLANES

    @pl.when(c == 0)
    def _init():
        xbuf[:, 0:halo, :] = jnp.zeros((nb, halo, xbuf.shape[2]), F32)
        ct_ref[...] = jnp.zeros(ct_ref.shape, F32)
        m_ref[...] = jnp.zeros(m_ref.shape, F32)

    row = lax.broadcasted_iota(jnp.int32, (ln, ln), 0)
    col = lax.broadcasted_iota(jnp.int32, (ln, ln), 1)
    tril = row >= col
    trilf = jnp.where(tril, 1.0, 0.0).astype(F32)
    ones_blk = jnp.where(lax.broadcasted_iota(jnp.int32, (ln, dv), 1) == 0, 1.0, 0.0).astype(F32)

    for b in range(nb):
        xbuf[b, halo:halo + ln, :] = qk_ref[b].astype(F32)
        conv = cb_ref[...]
        for j in range(M_CONV):
            off = halo - (M_CONV - 1) + j
            conv = conv + cw_ref[j:j + 1, :] * xbuf[b, off:off + ln, :]
        qk = conv * jax.nn.sigmoid(conv)
        xbuf[b, 0:halo, :] = xbuf[b, ln:ln + halo, :]

        g = g_ref[b] + gb_ref[...]
        lf = jnp.minimum(g, 0.0) - jnp.log1p(jnp.exp(-jnp.abs(g)))
        cum = jnp.dot(trilf, lf, preferred_element_type=F32, precision=lax.Precision.HIGHEST)
        cum_t = cum.T
        g_t = g.T
        vv = v_ref[b].astype(F32)
        hs = []
        for h in range(nh):
            chain = b * nh + h
            q = (qk[:, h * dqk:(h + 1) * dqk] * (dqk ** -0.5)).astype(BF16)
            k = qk[:, nh * dqk + h * dqk: nh * dqk + (h + 1) * dqk].astype(BF16)
            v_ext = jnp.concatenate([vv[:, h * dv:(h + 1) * dv], ones_blk], axis=-1)
            bc_col = cum[:, nh + h:nh + h + 1]
            bc_row = cum_t[nh + h:nh + h + 1, :]
            i_col = g[:, h:h + 1]
            i_row = g_t[h:h + 1, :]
            m_prev = m_ref[chain:chain + 1, 0:1]
            dmat = jnp.where(tril, bc_col - bc_row + i_row, NEG_INF)
            m_inter = bc_col + m_prev
            m_j = jnp.maximum(m_inter, jnp.max(dmat, axis=-1, keepdims=True))
            w_intra = jnp.exp(dmat - m_j)
            w_inter = jnp.exp(m_inter - m_j)
            s = lax.dot_general(q, k, (((1,), (1,)), ((), ())), preferred_element_type=F32) * w_intra
            ct = ct_ref[chain]
            num_ext = (jnp.dot(s.astype(BF16), v_ext.astype(BF16), preferred_element_type=F32)
                       + w_inter * jnp.dot(q, ct.astype(BF16), preferred_element_type=F32))
            num = num_ext[:, :dv]
            nq = num_ext[:, dv:dv + 1]
            den = jnp.maximum(jnp.abs(nq), jnp.exp(-m_j))
            hs.append(num / den)
            m_new = m_j[ln - 1:ln, :]
            bc_last = bc_col[ln - 1:ln, :]
            w_s = jnp.exp(bc_last - bc_col + i_col - m_new)
            w_c = jnp.exp(bc_last + m_prev - m_new)
            upd = lax.dot_general(k, (w_s * v_ext).astype(BF16), (((0,), (0,)), ((), ())),
                                  preferred_element_type=F32)
            ct_ref[chain] = w_c * ct + upd
            m_ref[chain:chain + 1, :] = jnp.broadcast_to(m_new, (1, m_ref.shape[1]))
        og = og_ref[b].astype(F32)
        outs = []
        for h in range(nh):
            seg = jax.nn.sigmoid(og[:, h * dv:(h + 1) * dv]) * hs[h]
            mu = jnp.mean(seg, axis=-1, keepdims=True)
            cen = seg - mu
            var = jnp.mean(cen * cen, axis=-1, keepdims=True)
            outs.append(cen * lax.rsqrt(var + LN_EPS) * ng_ref[:, h * dv:(h + 1) * dv])
        y_ref[b] = jnp.concatenate(outs, axis=-1).astype(y_ref.dtype)


def _mlstm(proj3, gates3, conv_w, conv_b, gate_b, norm_g):
    b, s, _ = proj3.shape
    ln = M_CHUNK
    wq = 2 * M_HEADS * M_QK_DIM
    wv = M_HEADS * M_V_DIM
    blk = lambda width, off: pl.BlockSpec((b, ln, width), lambda c: (0, c, off // width))
    full = lambda a: pl.BlockSpec(a.shape, lambda c: (0,) * a.ndim)
    gate_b128 = jnp.zeros((1, LANES), F32).at[0, :2 * M_HEADS].set(gate_b)
    conv_b2 = conv_b.reshape(1, wq)
    norm_g2 = norm_g.reshape(1, wv)
    return pl.pallas_call(
        _mlstm_kernel,
        grid=(s // ln,),
        in_specs=[blk(wq, P_MQK), blk(wv, P_MV), blk(wv, P_MO), blk(LANES, 0),
                  full(conv_w), full(conv_b2), full(gate_b128), full(norm_g2)],
        out_specs=pl.BlockSpec((b, ln, wv), lambda c: (0, c, 0)),
        out_shape=jax.ShapeDtypeStruct((b, s, wv), BF16),
        scratch_shapes=[pltpu.VMEM((b, SUBLANES + ln, wq), F32),
                        pltpu.VMEM((b * M_HEADS, M_QK_DIM, 2 * M_V_DIM), F32),
                        pltpu.VMEM((b * M_HEADS, LANES), F32)],
        compiler_params=_cparams(("arbitrary",)),
        name="mlstm",
    )(proj3, proj3, proj3, gates3, conv_w, conv_b2, gate_b128, norm_g2)


def _mla_prep_kernel(cq_ref, ckv_ref, kr_ref, krs_ref, qg_ref, kvg_ref, wqa_ref, wqb_ref, wkv_ref,
                     cos_ref, sin_ref, q_out, k_out, v_out):
    nd, hd = C_NOPE_DIM, C_HEADS
    scale = (C_NOPE_DIM + C_ROPE_DIM) ** -0.5 * LOG2_E
    cos = cos_ref[...]
    sin = sin_ref[...]

    cq = cq_ref[0].astype(F32)
    qn = (cq * lax.rsqrt(jnp.mean(cq * cq, axis=-1, keepdims=True) + RMS_EPS) * qg_ref[...]).astype(BF16)
    qa = jnp.dot(qn, wqa_ref[...], preferred_element_type=F32)
    qb = jnp.dot(qn, wqb_ref[...], preferred_element_type=F32)
    for h in range(hd):
        nope = qa[:, h * 2 * nd: h * 2 * nd + nd]
        rope = qa[:, h * 2 * nd + nd:(h + 1) * 2 * nd] * cos + qb[:, h * nd:(h + 1) * nd] * sin
        q_out[0, h] = (jnp.concatenate([nope, rope], axis=-1) * scale).astype(q_out.dtype)

    ckv = ckv_ref[0].astype(F32)
    kvn = (ckv * lax.rsqrt(jnp.mean(ckv * ckv, axis=-1, keepdims=True) + RMS_EPS) * kvg_ref[...]).astype(BF16)
    kv = jnp.dot(kvn, wkv_ref[...], preferred_element_type=F32)
    kr = kr_ref[0].astype(F32) * cos + krs_ref[0].astype(F32) * sin
    ones_blk = jnp.where(lax.broadcasted_iota(jnp.int32, kr.shape, 1) == 0, 1.0, 0.0).astype(F32)
    for h in range(hd):
        k_out[0, h] = jnp.concatenate([kv[:, h * nd:(h + 1) * nd], kr], axis=-1).astype(k_out.dtype)
        v_h = kv[:, hd * nd + h * nd: hd * nd + (h + 1) * nd]
        v_out[0, h] = jnp.concatenate([v_h, ones_blk], axis=-1).astype(v_out.dtype)


def _mla_prep(proj3, q_norm_g, kv_norm_g, wqa, wqb, wkv, cos128, sin128):
    b, s, _ = proj3.shape
    tm = MLA_TM
    hd, nd = C_HEADS, C_NOPE_DIM
    blk = lambda width, off: pl.BlockSpec((1, tm, width), lambda bi, i: (bi, i, off // width))
    full = lambda a: pl.BlockSpec(a.shape, lambda bi, i: (0,) * a.ndim)
    tab = pl.BlockSpec((tm, LANES), lambda bi, i: (i, 0))
    qg = q_norm_g.reshape(1, -1)
    kvg = kv_norm_g.reshape(1, -1)
    head_out = lambda width: pl.BlockSpec((1, hd, tm, width), lambda bi, i: (bi, 0, i, 0))
    return pl.pallas_call(
        _mla_prep_kernel,
        grid=(b, s // tm),
        in_specs=[blk(C_Q_LORA, P_CQ), blk(C_KV_LORA, P_CKV), blk(LANES, P_CKR), blk(LANES, P_CKRS),
                  full(qg), full(kvg), full(wqa), full(wqb), full(wkv), tab, tab],
        out_specs=[head_out(2 * nd), head_out(2 * nd), head_out(C_V_DIM + LANES)],
        out_shape=[jax.ShapeDtypeStruct((b, hd, s, 2 * nd), BF16),
                   jax.ShapeDtypeStruct((b, hd, s, 2 * nd), BF16),
                   jax.ShapeDtypeStruct((b, hd, s, C_V_DIM + LANES), BF16)],
        compiler_params=_cparams(("parallel", "arbitrary")),
        name="mla_prep",
    )(proj3, proj3, proj3, proj3, qg, kvg, wqa, wqb, wkv, cos128, sin128)


def _flash_kernel(q_ref, k_ref, v_ref, o_ref):
    qi = pl.program_id(2)
    bq = FLASH_BQ
    nh = q_ref.shape[1]
    dv = C_V_DIM

    def attend(q, h, key0, nkeys, m, acc, causal):
        ks = k_ref[0, h, pl.ds(key0, nkeys), :]
        vs = v_ref[0, h, pl.ds(key0, nkeys), :]
        s = lax.dot_general(q, ks, (((1,), (1,)), ((), ())), preferred_element_type=F32)
        if causal:
            row = lax.broadcasted_iota(jnp.int32, s.shape, 0)
            col = lax.broadcasted_iota(jnp.int32, s.shape, 1)
            s = jnp.where(row >= col, s, NEG_INF)
        m_new = jnp.maximum(m, jnp.max(s, axis=-1, keepdims=True))
        p = jnp.exp2(s - m_new)
        acc = jnp.exp2(m - m_new) * acc + jnp.dot(p.astype(BF16), vs, preferred_element_type=F32)
        return m_new, acc

    def full_step(j, carries):
        start = pl.multiple_of(j * bq, bq)
        return tuple(attend(q_ref[0, h], h, start, bq, *carries[h], False) for h in range(nh))

    init = tuple((jnp.full((bq, 1), NEG_INF, F32), jnp.zeros((bq, v_ref.shape[3]), F32)) for _ in range(nh))
    carries = lax.fori_loop(0, qi, full_step, init)

    half = bq // 2
    diag0 = pl.multiple_of(qi * bq, bq)
    for h in range(nh):
        m, acc = attend(q_ref[0, h], h, diag0, half, *carries[h], True)
        m_lo, acc_lo = attend(q_ref[0, h, half:, :], h, diag0 + half, half, m[half:], acc[half:], True)
        cols = slice(h * dv, (h + 1) * dv)
        o_ref[0, :half, cols] = (acc[:half, :dv] / acc[:half, dv:dv + 1]).astype(o_ref.dtype)
        o_ref[0, half:, cols] = (acc_lo[:, :dv] / acc_lo[:, dv:dv + 1]).astype(o_ref.dtype)


def _flash(qc, kc, vc):
    b, hd, s, dk = qc.shape
    dv = C_V_DIM
    bq = FLASH_BQ
    nh = FLASH_HEADS
    return pl.pallas_call(
        _flash_kernel,
        grid=(b, hd // nh, s // bq),
        in_specs=[pl.BlockSpec((1, nh, bq, dk), lambda bi, h, i: (bi, h, i, 0)),
                  pl.BlockSpec((1, nh, s, dk), lambda bi, h, i: (bi, h, 0, 0), pipeline_mode=pl.Buffered(1)),
                  pl.BlockSpec((1, nh, s, vc.shape[3]), lambda bi, h, i: (bi, h, 0, 0),
                               pipeline_mode=pl.Buffered(1))],
        out_specs=pl.BlockSpec((1, bq, nh * dv), lambda bi, h, i: (bi, i, h)),
        out_shape=jax.ShapeDtypeStruct((b, s, hd * dv), BF16),
        compiler_params=_cparams(("parallel", "parallel", "arbitrary")),
        name="flash",
    )(qc, kc, vc)


def _layer_norm(z, g, b):
    mu = jnp.mean(z, axis=-1, keepdims=True)
    cen = z - mu
    var = jnp.mean(cen * cen, axis=-1, keepdims=True)
    return cen * lax.rsqrt(var + LN_EPS) * g + b


def _outproj_kernel(ya_ref, yb_ref, yc_ref, x_ref, w_ref, g_ref, b_ref, wr_ref, br_ref,
                    x1_ref, x1rc_ref, e_ref, gate_ref, rank_ref, cnt_ref, cnt_sc):
    i = pl.program_id(0)
    tm = x_ref.shape[0]
    wa, wb = ya_ref.shape[1], yb_ref.shape[1]

    @pl.when(i == 0)
    def _init():
        cnt_sc[...] = jnp.zeros(cnt_sc.shape, F32)

    mix = jnp.dot(ya_ref[...], w_ref[0:wa, :], preferred_element_type=F32)
    mix = mix + jnp.dot(yb_ref[...], w_ref[wa:wa + wb, :], preferred_element_type=F32)
    mix = mix + jnp.dot(yc_ref[...], w_ref[wa + wb:, :], preferred_element_type=F32)
    x1 = _layer_norm(DN_ALPHA * x_ref[...] + mix, g_ref[...], b_ref[...])
    x1_ref[...] = x1
    _store_rc(x1rc_ref, x1)

    logits = jnp.dot(x1.astype(BF16), wr_ref[...], preferred_element_type=F32) + br_ref[...]
    lane = lax.broadcasted_iota(jnp.int32, logits.shape, 1)
    logits = jnp.where(lane < N_EXPERTS, logits, NEG_INF)
    e_out = jnp.zeros(logits.shape, jnp.int32)
    p_out = jnp.zeros(logits.shape, F32)
    top = None
    den = None
    onehots = []
    for r in range(TOP_K):
        mx = jnp.max(logits, axis=-1, keepdims=True)
        idx = jnp.min(jnp.where(logits == mx, lane, LANES), axis=-1, keepdims=True)
        if r == 0:
            top = mx
        p = jnp.exp(mx - top)
        den = p if r == 0 else den + p
        sel = lane == idx
        onehots.append(jnp.where(sel, 1.0, 0.0).astype(F32))
        e_out = jnp.where(lane == r, idx, e_out)
        p_out = jnp.where(lane == r, p, p_out)
        logits = jnp.where(sel, NEG_INF, logits)
    e_ref[...] = e_out
    gate_ref[...] = p_out / den

    oh_sum = onehots[0] + onehots[1] + onehots[2] + onehots[3]
    row = lax.broadcasted_iota(jnp.int32, (tm, tm), 0)
    col = lax.broadcasted_iota(jnp.int32, (tm, tm), 1)
    before = jnp.where(row > col, 1.0, 0.0).astype(BF16)
    base = jnp.dot(before, oh_sum.astype(BF16), preferred_element_type=F32) + cnt_sc[0:1, :]
    rank_out = jnp.zeros(logits.shape, jnp.int32)
    for r in range(TOP_K):
        rk = jnp.sum(onehots[r] * base, axis=-1, keepdims=True)
        rank_out = jnp.where(lane == r, rk.astype(jnp.int32), rank_out)
    rank_ref[...] = rank_out
    total = cnt_sc[0:1, :] + jnp.sum(oh_sum, axis=0, keepdims=True)
    cnt_sc[0:1, :] = total
    cnt_ref[...] = jnp.broadcast_to(total, cnt_ref.shape).astype(jnp.int32)


def _outproj(ya, yb, yc, x2d, w_bf, g, b, wr128, br128):
    t, d = x2d.shape
    tm = OUT_TM
    rows = lambda a: pl.BlockSpec((tm, a.shape[1]), lambda i: (i, 0))
    full = lambda a: pl.BlockSpec(a.shape, lambda i: (0,) * a.ndim, pipeline_mode=pl.Buffered(1))
    g2, b2 = g.reshape(1, d), b.reshape(1, d)
    lane_blk = pl.BlockSpec((tm, LANES), lambda i: (i, 0))
    return pl.pallas_call(
        _outproj_kernel,
        grid=(t // tm,),
        in_specs=[rows(ya), rows(yb), rows(yc), rows(x2d), full(w_bf), full(g2), full(b2), full(wr128), full(br128)],
        out_specs=[pl.BlockSpec((tm, d), lambda i: (i, 0)),
                   pl.BlockSpec((tm * ROW_TILES, LANES), lambda i: (i, 0)),
                   lane_blk, lane_blk, lane_blk,
                   pl.BlockSpec((SUBLANES, LANES), lambda i: (0, 0))],
        out_shape=[jax.ShapeDtypeStruct((t, d), F32),
                   jax.ShapeDtypeStruct((t * ROW_TILES, LANES), U32),
                   jax.ShapeDtypeStruct((t, LANES), jnp.int32),
                   jax.ShapeDtypeStruct((t, LANES), F32),
                   jax.ShapeDtypeStruct((t, LANES), jnp.int32),
                   jax.ShapeDtypeStruct((SUBLANES, LANES), jnp.int32)],
        scratch_shapes=[pltpu.VMEM((SUBLANES, LANES), F32)],
        compiler_params=_cparams(("arbitrary",)),
        name="outproj",
    )(ya, yb, yc, x2d, w_bf, g2, b2, wr128, br128)


def _rc_rows(row, n=1):
    return pl.ds(pl.multiple_of(row * ROW_TILES, ROW_TILES), n * ROW_TILES)


def _dispatch_kernel(pos_ref, cnt_ref, pst_ref, pad_ref, nblk_ref, x_hbm, xs_hbm, buf, zbuf, sem_in, sem_out, sem_z):
    i = pl.program_id(0)
    n = pl.num_programs(0)
    tm = DSP_TM
    nslot = buf.shape[0]

    def in_copy(blk, slot):
        return pltpu.make_async_copy(x_hbm.at[_rc_rows(blk * tm, tm)], buf.at[slot], sem_in.at[slot])

    def row_out(slot, t, p):
        return pltpu.make_async_copy(buf.at[slot, _rc_rows(t)], xs_hbm.at[_rc_rows(p)], sem_out.at[slot])

    def wait_outs(slot):
        for _ in range(TOP_K):
            pltpu.make_async_copy(buf.at[slot], xs_hbm.at[_rc_rows(0, tm)], sem_out.at[slot]).wait()

    slot = lax.rem(i, nslot)

    @pl.when(i == 0)
    def _first_in():
        in_copy(0, 0).start()

    @pl.when(i >= 2)
    def _free_slot():
        wait_outs(lax.rem(i + 1, nslot))

    @pl.when(i + 1 < n)
    def _next_in():
        in_copy(i + 1, lax.rem(i + 1, nslot)).start()

    in_copy(i, slot).wait()
    base = i * (tm * TOP_K)

    def issue(t, _):
        for k in range(TOP_K):
            row_out(slot, t, pos_ref[base + t * TOP_K + k]).start()
        return 0

    lax.fori_loop(0, tm, issue, 0, unroll=2)

    @pl.when(i == n - 1)
    def _finish():
        wait_outs(lax.rem(i + 2, nslot))
        wait_outs(slot)
        zbuf[...] = jnp.zeros(zbuf.shape, zbuf.dtype)

        def pad_rows(e, _):
            cnt = cnt_ref[e]
            first = pst_ref[e] + cnt
            npad = pad_ref[e] - cnt

            def zero_row(q):
                return pltpu.make_async_copy(zbuf.at[_rc_rows(0)], xs_hbm.at[_rc_rows(first + q)], sem_z)

            def start(q, c):
                zero_row(q).start()
                return c

            def wait(q, c):
                zero_row(q).wait()
                return c

            lax.fori_loop(0, npad, start, 0)
            lax.fori_loop(0, npad, wait, 0)
            return 0

        lax.fori_loop(0, N_EXPERTS, pad_rows, 0)

        def tail_block(bk, _):
            cp = pltpu.make_async_copy(zbuf, xs_hbm.at[_rc_rows(bk * MOE_PAD, MOE_PAD)], sem_z)
            cp.start()
            cp.wait()
            return 0

        lax.fori_loop(nblk_ref[0], xs_hbm.shape[0] // (MOE_PAD * ROW_TILES), tail_block, 0)


def _dispatch(pos, counts, pad_start, padded, nblk, x1rc, rows):
    t = x1rc.shape[0] // ROW_TILES
    tm = DSP_TM
    return pl.pallas_call(
        _dispatch_kernel,
        grid_spec=pltpu.PrefetchScalarGridSpec(
            num_scalar_prefetch=5,
            grid=(t // tm,),
            in_specs=[pl.BlockSpec(memory_space=pl.ANY)],
            out_specs=pl.BlockSpec(memory_space=pl.ANY),
            scratch_shapes=[pltpu.VMEM((3, tm * ROW_TILES, LANES), U32),
                            pltpu.VMEM((MOE_PAD * ROW_TILES, LANES), U32),
                            pltpu.SemaphoreType.DMA((3,)),
                            pltpu.SemaphoreType.DMA((3,)),
                            pltpu.SemaphoreType.DMA(())]),
        out_shape=jax.ShapeDtypeStruct((rows * ROW_TILES, LANES), U32),
        compiler_params=_cparams(("arbitrary",)),
        name="dispatch",
    )(pos, counts, pad_start, padded, nblk, x1rc)


def _moe_kernel(layer, sbe_ref, sbr_ref, sbn_ref, nblk_ref, xs_hbm, wgu_hbm, wdn_hbm, bgu_ref, bd_ref, y_hbm,
                xbuf, acc, stage_in, stage_out, wg_f, wu_f, wd_f, wg_bf, wu_bf, wd_bf, sem_in, sem_out, sem_w):
    sb = pl.program_id(0)
    nsb = pl.num_programs(0)
    tf = wg_f.shape[2]
    dff = wdn_hbm.shape[2]
    nj = dff // tf
    n = sbn_ref[sb]
    row0 = sbr_ref[sb]
    nchunk = n // MOE_PAD
    cur = lax.rem(sb, 2)
    big, huge = 2 * MOE_PAD, 4 * MOE_PAD

    def weight_copies(sb_idx, j, slot):
        e = sbe_ref[sb_idx]
        cols = pl.ds(pl.multiple_of(j * tf, tf), tf)
        up_cols = pl.ds(pl.multiple_of(dff + j * tf, tf), tf)
        copies = [pltpu.make_async_copy(wdn_hbm.at[layer, e, cols, :], wd_f.at[slot], sem_w.at[slot])]
        part = wg_f.shape[1] // MOE_W_PARTS
        for p in range(MOE_W_PARTS):
            band = pl.ds(p * part, part)
            copies.append(pltpu.make_async_copy(wgu_hbm.at[layer, e, band, cols], wg_f.at[slot, band], sem_w.at[slot]))
            copies.append(pltpu.make_async_copy(wgu_hbm.at[layer, e, band, up_cols], wu_f.at[slot, band],
                                                sem_w.at[slot]))
        return copies

    def chunk_rows(c):
        return pl.ds(pl.multiple_of(c * MOE_PAD, MOE_PAD), MOE_PAD)

    def chunk_in(first_row, c):
        return pltpu.make_async_copy(xs_hbm.at[_rc_rows(first_row + c * MOE_PAD, MOE_PAD)], stage_in, sem_in)

    def chunk_out(c, slot):
        return pltpu.make_async_copy(stage_out.at[slot], y_hbm.at[_rc_rows(row0 + c * MOE_PAD, MOE_PAD)],
                                     sem_out.at[slot])

    def wait_outs(count):
        @pl.when(count >= 2)
        def _older():
            chunk_out(0, lax.rem(count, 2)).wait()

        @pl.when(count >= 1)
        def _newest():
            chunk_out(0, lax.rem(count + 1, 2)).wait()

    def convert_in(slot, c):
        lo, hi = _load_rc(stage_in, MOE_PAD)
        xbuf[slot, chunk_rows(c), :] = jnp.concatenate([lo.astype(BF16), hi.astype(BF16)], axis=-1)

    @pl.when(sb == 0)
    def _load_first():
        for cp in weight_copies(0, 0, 0):
            cp.start()

        def load(c, _):
            chunk_in(row0, c).start()
            chunk_in(row0, c).wait()
            convert_in(0, c)
            return 0

        lax.fori_loop(0, nchunk, load, 0)

    n_next = sbn_ref[sb + 1]

    def ffn(j, start, size, wg, wu, wd):
        rs = pl.ds(start, size)
        xr = xbuf[cur, rs, :]
        cols = pl.ds(pl.multiple_of(j * tf, tf), tf)
        up_cols = pl.ds(pl.multiple_of(dff + j * tf, tf), tf)
        gt = jnp.dot(xr, wg, preferred_element_type=F32) + bgu_ref[:, cols]
        up = jnp.dot(xr, wu, preferred_element_type=F32) + bgu_ref[:, up_cols]
        gt = jnp.minimum(gt, SWIGLU_LIMIT)
        up = jnp.clip(up, -SWIGLU_LIMIT, SWIGLU_LIMIT)
        act = ((up + 1.0) * gt * jax.nn.sigmoid(SWIGLU_ALPHA * gt)).astype(BF16)
        half = acc.shape[1] // 2
        for cs in (slice(0, half), slice(half, 2 * half)):
            acc[rs, cs] = acc[rs, cs] + jnp.dot(act, wd[:, cs], preferred_element_type=F32)

    def first_chunk(j, wslot, size):
        wg = wg_f[wslot].astype(BF16)
        wu = wu_f[wslot].astype(BF16)
        wd = wd_f[wslot].astype(BF16)
        wg_bf[...] = wg
        wu_bf[...] = wu
        wd_bf[...] = wd
        ffn(j, 0, size, wg, wu, wd)

    def hidden_tile(j, _):
        wslot = lax.rem(j, 2)
        for cp in weight_copies(sb, j, wslot):
            cp.wait()

        @pl.when(j + 1 < nj)
        def _next_tile():
            for cp in weight_copies(sb, j + 1, 1 - wslot):
                cp.start()

        @pl.when((j + 1 == nj) & (n_next > 0))
        def _next_super_block():
            for cp in weight_copies(sb + 1, 0, 0):
                cp.start()

        fetch_next = j * MOE_PAD < n_next

        @pl.when(fetch_next)
        def _start_next():
            chunk_in(sbr_ref[sb + 1], j).start()

        @pl.when((j == 1) & (sb > 0))
        def _drain_prev_outs():
            wait_outs(sbn_ref[jnp.maximum(sb - 1, 0)] // MOE_PAD)

        @pl.when(j == 0)
        def _init_acc():
            def init(c, _):
                acc[chunk_rows(c), :] = jnp.broadcast_to(bd_ref[...], (MOE_PAD, acc.shape[1]))
                return 0

            lax.fori_loop(0, nchunk, init, 0)

        def rest(start, size):
            ffn(j, start, size, wg_bf[...], wu_bf[...], wd_bf[...])

        @pl.when(n >= huge)
        def _rows_huge():
            first_chunk(j, wslot, huge)

            @pl.when(n - huge >= big)
            def _then_big():
                rest(huge, big)

            @pl.when(lax.rem(n, big) != 0)
            def _then_pad():
                rest(pl.multiple_of(n - MOE_PAD, MOE_PAD), MOE_PAD)

        @pl.when((n >= big) & (n < huge))
        def _rows_big():
            first_chunk(j, wslot, big)

            @pl.when(n > big)
            def _then_pad():
                rest(big, MOE_PAD)

        @pl.when(n < big)
        def _rows_small():
            first_chunk(j, wslot, MOE_PAD)

        @pl.when(j == nj - 1)
        def _store_rows():
            def store(c, _):
                slot = lax.rem(c, 2)

                @pl.when(c >= 2)
                def _slot_free():
                    chunk_out(c - 2, slot).wait()

                _store_rc(stage_out.at[slot], acc[chunk_rows(c), :])
                chunk_out(c, slot).start()
                return 0

            lax.fori_loop(0, nchunk, store, 0)

            @pl.when((sb == nsb - 1) | (n_next == 0))
            def _no_later_tile():
                wait_outs(nchunk)

        @pl.when(fetch_next)
        def _finish_next():
            chunk_in(sbr_ref[sb + 1], j).wait()
            convert_in(1 - cur, j)

        return 0

    @pl.when(n > 0)
    def _super_block():
        lax.fori_loop(0, nj, hidden_tile, 0)

    @pl.when(sb == nsb - 1)
    def _zero_tail():
        stage_out[0] = jnp.zeros(stage_out.shape[1:], stage_out.dtype)

        def tail_block(bk, _):
            cp = pltpu.make_async_copy(stage_out.at[0], y_hbm.at[_rc_rows(bk * MOE_PAD, MOE_PAD)], sem_out.at[0])
            cp.start()
            cp.wait()
            return 0

        lax.fori_loop(nblk_ref[0], y_hbm.shape[0] // (MOE_PAD * ROW_TILES), tail_block, 0)


def _moe(layer, sb_e, sb_row0, sb_n, nblk, xs, w_gate_up, b_gate_up, w_down, b_down):
    rows = xs.shape[0] // ROW_TILES
    d = D_MODEL
    dff = w_down.shape[2]
    tf = MOE_TF
    nj = dff // tf
    nsb = sb_e.shape[0] - 1
    assert nj % 2 == 0 and MOE_TMAX // MOE_PAD <= nj and MOE_TMAX == 6 * MOE_PAD
    bgu = b_gate_up.reshape(DEPTH, N_EXPERTS, 1, 2 * dff)
    bdn = b_down.reshape(DEPTH, N_EXPERTS, 1, d)
    expert_row = lambda width: pl.BlockSpec((None, None, 1, width), lambda sb, e, r, n, nb: (layer, e[sb], 0, 0))
    return pl.pallas_call(
        functools.partial(_moe_kernel, layer),
        grid_spec=pltpu.PrefetchScalarGridSpec(
            num_scalar_prefetch=4,
            grid=(nsb,),
            in_specs=[pl.BlockSpec(memory_space=pl.ANY),
                      pl.BlockSpec(memory_space=pl.ANY),
                      pl.BlockSpec(memory_space=pl.ANY),
                      expert_row(2 * dff),
                      expert_row(d)],
            out_specs=pl.BlockSpec(memory_space=pl.ANY),
            scratch_shapes=[pltpu.VMEM((2, MOE_TMAX, d), BF16),
                            pltpu.VMEM((MOE_TMAX, d), F32),
                            pltpu.VMEM((MOE_PAD * ROW_TILES, LANES), U32),
                            pltpu.VMEM((2, MOE_PAD * ROW_TILES, LANES), U32),
                            pltpu.VMEM((2, d, tf), F32),
                            pltpu.VMEM((2, d, tf), F32),
                            pltpu.VMEM((2, tf, d), F32),
                            pltpu.VMEM((d, tf), BF16),
                            pltpu.VMEM((d, tf), BF16),
                            pltpu.VMEM((tf, d), BF16),
                            pltpu.SemaphoreType.DMA(()),
                            pltpu.SemaphoreType.DMA((2,)),
                            pltpu.SemaphoreType.DMA((2,))]),
        out_shape=jax.ShapeDtypeStruct((rows * ROW_TILES, LANES), U32),
        compiler_params=_cparams(("arbitrary",)),
        name="moe",
    )(sb_e, sb_row0, sb_n, nblk, xs, w_gate_up, w_down, bgu, bdn)


def _combine_kernel(pos_ref, y_hbm, gate_ref, x1_ref, g_ref, b_ref, o_ref, buf, sem):
    i = pl.program_id(0)
    n = pl.num_programs(0)
    tm = CMB_TM
    slot = lax.rem(i, 2)

    def issue(blk, dst_slot):
        base = blk * (tm * TOP_K)

        def body(t, _):
            for k in range(TOP_K):
                p = pos_ref[base + t * TOP_K + k]
                pltpu.make_async_copy(y_hbm.at[_rc_rows(p)], buf.at[dst_slot, k, _rc_rows(t)], sem.at[dst_slot]).start()
            return 0

        lax.fori_loop(0, tm, body, 0, unroll=2)

    @pl.when(i == 0)
    def _first():
        issue(0, 0)

    @pl.when(i + 1 < n)
    def _next():
        issue(i + 1, 1 - slot)

    for k in range(TOP_K):
        pltpu.make_async_copy(y_hbm.at[_rc_rows(0, tm)], buf.at[slot, k], sem.at[slot]).wait()
    gate = gate_ref[...]
    ffn_lo, ffn_hi = None, None
    for k in range(TOP_K):
        lo, hi = _load_rc(buf.at[slot, k], tm)
        gk = gate[:, k:k + 1]
        ffn_lo = gk * lo if k == 0 else ffn_lo + gk * lo
        ffn_hi = gk * hi if k == 0 else ffn_hi + gk * hi
    ffn = jnp.concatenate([ffn_lo, ffn_hi], axis=-1)
    o_ref[...] = _layer_norm(DN_ALPHA * x1_ref[...] + ffn, g_ref[...], b_ref[...])


def _combine(pos_flat, y, gate, x1, g, b):
    t, d = x1.shape
    tm = CMB_TM
    g2, b2 = g.reshape(1, d), b.reshape(1, d)
    return pl.pallas_call(
        _combine_kernel,
        grid_spec=pltpu.PrefetchScalarGridSpec(
            num_scalar_prefetch=1,
            grid=(t // tm,),
            in_specs=[pl.BlockSpec(memory_space=pl.ANY),
                      pl.BlockSpec((tm, LANES), lambda i, p: (i, 0)),
                      pl.BlockSpec((tm, d), lambda i, p: (i, 0)),
                      pl.BlockSpec((1, d), lambda i, p: (0, 0)),
                      pl.BlockSpec((1, d), lambda i, p: (0, 0))],
            out_specs=pl.BlockSpec((tm, d), lambda i, p: (i, 0)),
            scratch_shapes=[pltpu.VMEM((2, TOP_K, tm * ROW_TILES, LANES), U32), pltpu.SemaphoreType.DMA((2,))]),
        out_shape=jax.ShapeDtypeStruct((t, d), F32),
        compiler_params=_cparams(("arbitrary",)),
        name="combine",
    )(pos_flat, y, gate, x1, g2, b2)


def _swap_halves(w):
    half = w.shape[-1] // 2
    return jnp.concatenate([w[..., half:], w[..., :half]], axis=-1)


def _w_in_columns(w):
    d = w.shape[0]
    widths = (512, 128, 128, 256, 256, 512, 512, 8, 512, 256, 64)
    offs = [0]
    for wd in widths:
        offs.append(offs[-1] + wd)
    a_q, a_k, a_v, m_q, m_k, m_v, m_o, m_if, c_q, c_kv, c_kr = [w[:, offs[i]:offs[i + 1]] for i in range(len(widths))]
    z = lambda n: jnp.zeros((d, n), w.dtype)
    return jnp.concatenate([a_q, m_v, m_o, c_q, m_q, m_k, c_kv, a_k, a_v,
                            c_kr, z(LANES - 64), m_if, z(LANES - 8), _swap_halves(c_kr), z(LANES - 64)], axis=1)


def _w_in_layout_kernel(w_ref, o_ref):
    o_ref[...] = _w_in_columns(w_ref[...]).astype(o_ref.dtype)


def _layout_w_in(w_in, layer):
    _, d, n = w_in.shape
    tk = 256
    return pl.pallas_call(
        _w_in_layout_kernel,
        grid=(d // tk,),
        in_specs=[pl.BlockSpec((None, tk, n), lambda i: (layer, i, 0))],
        out_specs=pl.BlockSpec((tk, N_PROJ), lambda i: (i, 0)),
        out_shape=jax.ShapeDtypeStruct((d, N_PROJ), BF16),
        compiler_params=_cparams(("parallel",)),
        name="w_in_layout",
    )(w_in)


def _layout_w_uq(w):
    r = w.shape[0]
    w3 = w.astype(BF16).reshape(r, C_HEADS, C_NOPE_DIM + C_ROPE_DIM)
    nope, rope = w3[..., :C_NOPE_DIM], w3[..., C_NOPE_DIM:]
    z = jnp.zeros((r, C_HEADS, LANES - C_ROPE_DIM), BF16)
    wa = jnp.concatenate([nope, rope, z], axis=-1).reshape(r, -1)
    wb = jnp.concatenate([_swap_halves(rope), z], axis=-1).reshape(r, -1)
    return wa, wb


def _layout_w_ukv(w):
    r = w.shape[0]
    w3 = w.astype(BF16).reshape(r, C_HEADS, C_NOPE_DIM + C_V_DIM)
    return jnp.concatenate([w3[..., :C_NOPE_DIM].reshape(r, -1), w3[..., C_NOPE_DIM:].reshape(r, -1)], axis=-1)


def _rope_tables(seq):
    dim = C_ROPE_DIM
    inv = 1.0 / (ROPE_THETA ** (jnp.arange(0, dim, 2, dtype=F32) / dim))
    ang = jnp.arange(seq, dtype=F32)[:, None] * inv[None, :]
    cos, sin = jnp.cos(ang), jnp.sin(ang)
    z = jnp.zeros((seq, LANES - dim), F32)
    return jnp.concatenate([cos, cos, z], axis=-1), jnp.concatenate([-sin, sin, z], axis=-1)


def _route(top_e, rank, counts, nsb):
    padded = (counts + MOE_PAD - 1) // MOE_PAD * MOE_PAD
    pad_end = jnp.cumsum(padded)
    pad_start = pad_end - padded
    is_e = top_e[:, :, None] == jnp.arange(N_EXPERTS, dtype=jnp.int32)[None, None, :]
    pos = (jnp.sum(jnp.where(is_e, pad_start[None, None, :], 0), axis=-1) + rank).reshape(-1).astype(jnp.int32)
    nblk = (pad_end[-1:] // MOE_PAD).astype(jnp.int32)
    nsb_e = (padded + MOE_TMAX - 1) // MOE_TMAX
    sb_end = jnp.cumsum(nsb_e)
    sb_start = sb_end - nsb_e
    sb_idx = jnp.arange(nsb + 1, dtype=jnp.int32)
    n_valid = sb_end[-1]
    sb_eff = jnp.minimum(sb_idx, n_valid - 1)
    sb_e = jnp.searchsorted(sb_end, sb_eff, side="right").astype(jnp.int32)
    part = sb_eff - sb_start[sb_e]
    sb_row0 = pad_start[sb_e] + part * MOE_TMAX
    sb_n = jnp.where(sb_idx < n_valid, jnp.minimum(padded[sb_e] - part * MOE_TMAX, MOE_TMAX), 0)
    tables = (counts, pad_start.astype(jnp.int32), padded.astype(jnp.int32), nblk)
    return pos, tables, sb_e, sb_row0.astype(jnp.int32), sb_n.astype(jnp.int32)


def kernel(x, w_in, conv_w, conv_b, m_gate_b, m_norm_g, sinks, q_norm_g, w_uq, kv_norm_g, w_ukv, w_out,
           ln1_g, ln1_b, w_router, b_router, w_gate_up, b_gate_up, w_down, b_down, ln2_g, ln2_b):
    b, s, d = x.shape
    t = b * s
    n_assign = t * TOP_K
    rows = (n_assign // MOE_PAD + N_EXPERTS) * MOE_PAD
    nsb = N_EXPERTS + rows // MOE_TMAX
    cos128, sin128 = _rope_tables(s)
    xt = x.reshape(t, d)
    for l in range(DEPTH):
        w_in_bf = _layout_w_in(w_in, l)
        wqa, wqb = _layout_w_uq(w_uq[l])
        wkv = _layout_w_ukv(w_ukv[l])
        w_out_bf = w_out[l].astype(BF16)
        wr128 = jnp.zeros((d, LANES), BF16).at[:, :N_EXPERTS].set(w_router[l].astype(BF16))
        br128 = jnp.zeros((1, LANES), F32).at[0, :N_EXPERTS].set(b_router[l])

        proj, gates = _proj(xt, w_in_bf)
        proj3 = proj.reshape(b, s, N_PROJ)
        y_a = _swa(proj3, sinks[l])
        y_b = _mlstm(proj3, gates.reshape(b, s, LANES), conv_w[l], conv_b[l], m_gate_b[l], m_norm_g[l])
        qc, kc, vc = _mla_prep(proj3, q_norm_g[l], kv_norm_g[l], wqa, wqb, wkv, cos128, sin128)
        y_c = _flash(qc, kc, vc)
        x1, x1rc, top_e, gate, rank, cnt = _outproj(y_a.reshape(t, -1), y_b.reshape(t, -1), y_c.reshape(t, -1), xt,
                                                    w_out_bf, ln1_g[l], ln1_b[l], wr128, br128)
        pos, tables, sb_e, sb_row0, sb_n = _route(top_e[:, :TOP_K], rank[:, :TOP_K], cnt[0, :N_EXPERTS], nsb)
        xs = _dispatch(pos, *tables, x1rc, rows)
        y = _moe(l, sb_e, sb_row0, sb_n, tables[3], xs, w_gate_up, b_gate_up, w_down, b_down)
        xt = _combine(pos, y, gate, x1, ln2_g[l], ln2_b[l])
    return xt.reshape(b, s, d)
```

```python
import functools

import jax
import jax.numpy as jnp
from jax import lax
from jax.experimental import pallas as pl
from jax.experimental.pallas import tpu as pltpu

F32 = jnp.float32
BF16 = jnp.bfloat16
NEG_INF = float("-inf")
LOG2_E = 1.4426950408889634

D_MODEL = 2048
DEPTH = 2
SWA_HEADS, SWA_KV_HEADS, SWA_HEAD_DIM, SWA_WINDOW = 8, 2, 64, 128
M_HEADS, M_QK_DIM, M_V_DIM, M_CONV = 4, 64, 128, 4
C_HEADS, C_NOPE_DIM, C_ROPE_DIM, C_V_DIM = 8, 128, 64, 128
C_Q_LORA, C_KV_LORA = 512, 256
ROPE_THETA = 10000.0
N_EXPERTS, TOP_K = 32, 4
SWIGLU_LIMIT, SWIGLU_ALPHA = 7.0, 1.702
DN_ALPHA = (2 * DEPTH) ** 0.25
LN_EPS, RMS_EPS = 1e-5, 1e-6

LANES = 128
SUBLANES = 8
VMEM_LIMIT = 56 * 1024 * 1024

P_AQ, P_MV, P_MO, P_CQ, P_MQK, P_CKV, P_AK, P_AV, P_CKR, P_MIF, P_CKRS = (
    0, 512, 1024, 1536, 2048, 2560, 2816, 2944, 3072, 3200, 3328)
N_PROJ = 3456

PROJ_TM, PROJ_TN = 1024, 1152
M_CHUNK = 128
MLA_TM = 512
FLASH_BQ = 1024
FLASH_HEADS = 4
OUT_TM = 512
MOE_PAD = 256
MOE_TMAX = 1536
MOE_TF = 256
MOE_W_PARTS = 4
CMB_TM = 64
DSP_TM = 64

ROW_TILES = D_MODEL // 2 // LANES
U32 = jnp.uint32
HI_MASK = 0xFFFF0000


def _cparams(sem, vmem=VMEM_LIMIT):
    return pltpu.CompilerParams(dimension_semantics=sem, vmem_limit_bytes=vmem)


def _bf16_bits(x):
    return lax.bitcast_convert_type(x.astype(BF16).astype(F32), U32)


def _store_rc(ref, val):
    n, d = val.shape
    words = (_bf16_bits(val[:, :d // 2]) >> 16) | (_bf16_bits(val[:, d // 2:]) & U32(HI_MASK))
    x = jnp.stack([words[:, s * LANES:(s + 1) * LANES] for s in range(ROW_TILES)], axis=0)
    ref[...] = pltpu.einshape("stl->tsl", x).reshape(n * ROW_TILES, LANES)


def _load_rc(ref, n):
    x = pltpu.einshape("tsl->stl", ref[...].reshape(n, ROW_TILES, LANES))
    words = jnp.concatenate([x[s] for s in range(ROW_TILES)], axis=-1)
    lo = lax.bitcast_convert_type(words << 16, F32)
    hi = lax.bitcast_convert_type(words & U32(HI_MASK), F32)
    return lo, hi


def _proj_kernel(x_ref, w_ref, o_ref, g_ref):
    j = pl.program_id(1)
    acc = jnp.dot(x_ref[...].astype(BF16), w_ref[...], preferred_element_type=F32)
    o_ref[...] = acc.astype(o_ref.dtype)

    @pl.when(j == P_MIF // PROJ_TN)
    def _gates():
        off = P_MIF % PROJ_TN
        g_ref[...] = acc[:, off:off + LANES]


def _proj(x2d, w_bf):
    t, d = x2d.shape
    n = w_bf.shape[1]
    return pl.pallas_call(
        _proj_kernel,
        grid=(t // PROJ_TM, n // PROJ_TN),
        in_specs=[pl.BlockSpec((PROJ_TM, d), lambda i, j: (i, 0)),
                  pl.BlockSpec((d, PROJ_TN), lambda i, j: (0, j))],
        out_specs=[pl.BlockSpec((PROJ_TM, PROJ_TN), lambda i, j: (i, j)),
                   pl.BlockSpec((PROJ_TM, LANES), lambda i, j: (i, 0))],
        out_shape=[jax.ShapeDtypeStruct((t, n), BF16), jax.ShapeDtypeStruct((t, LANES), F32)],
        compiler_params=_cparams(("parallel", "arbitrary")),
        name="proj",
    )(x2d, w_bf)


def _swa_kernel(sinks_ref, q_ref, kc_ref, kp_ref, vc_ref, vp_ref, o_ref):
    i = pl.program_id(1)
    w = SWA_WINDOW
    dh = SWA_HEAD_DIM
    grp = SWA_HEADS // SWA_KV_HEADS
    q = q_ref[0]
    qi = lax.broadcasted_iota(jnp.int32, (w, 2 * w), 0)
    kj = lax.broadcasted_iota(jnp.int32, (w, 2 * w), 1)
    rel = qi + w - kj
    has_prev = jnp.where(i > 0, 0, w)
    mask = (rel >= 0) & (rel < w) & (kj >= has_prev)
    outs = []
    for kvh in range(SWA_KV_HEADS):
        sl = slice(kvh * dh, (kvh + 1) * dh)
        k_cat = jnp.concatenate([kp_ref[0][:, sl], kc_ref[0][:, sl]], axis=0).astype(BF16)
        v_cat = jnp.concatenate([vp_ref[0][:, sl], vc_ref[0][:, sl]], axis=0).astype(BF16)
        for g in range(grp):
            h = kvh * grp + g
            qh = (q[:, h * dh:(h + 1) * dh] * (dh ** -0.5)).astype(BF16)
            s = lax.dot_general(qh, k_cat, (((1,), (1,)), ((), ())), preferred_element_type=F32)
            s = jnp.where(mask, s, NEG_INF)
            sink = sinks_ref[h]
            m = jnp.maximum(jnp.max(s, axis=-1, keepdims=True), sink)
            p = jnp.exp(s - m)
            den = jnp.sum(p, axis=-1, keepdims=True) + jnp.exp(sink - m)
            o = jnp.dot(p.astype(BF16), v_cat, preferred_element_type=F32)
            outs.append(o / den)
    o_ref[0] = jnp.concatenate(outs, axis=-1).astype(o_ref.dtype)


def _swa(proj3, sinks):
    b, s, _ = proj3.shape
    w = SWA_WINDOW
    nb = s // w
    kw = SWA_KV_HEADS * SWA_HEAD_DIM
    qw = SWA_HEADS * SWA_HEAD_DIM
    cur = lambda col: (lambda bi, i: (bi, i, col))
    prev = lambda col: (lambda bi, i: (bi, jnp.maximum(i - 1, 0), col))
    return pl.pallas_call(
        _swa_kernel,
        grid=(b, nb),
        in_specs=[pl.BlockSpec(memory_space=pltpu.SMEM),
                  pl.BlockSpec((1, w, qw), cur(P_AQ // qw)),
                  pl.BlockSpec((1, w, kw), cur(P_AK // kw)),
                  pl.BlockSpec((1, w, kw), prev(P_AK // kw)),
                  pl.BlockSpec((1, w, kw), cur(P_AV // kw)),
                  pl.BlockSpec((1, w, kw), prev(P_AV // kw))],
        out_specs=pl.BlockSpec((1, w, qw), lambda bi, i: (bi, i, 0)),
        out_shape=jax.ShapeDtypeStruct((b, s, qw), BF16),
        compiler_params=_cparams(("parallel", "arbitrary")),
        name="swa",
    )(sinks, proj3, proj3, proj3, proj3, proj3)


def _mlstm_kernel(qk_ref, v_ref, og_ref, g_ref, cw_ref, cb_ref, gb_ref, ng_ref, y_ref, xbuf, ct_ref, m_ref):
    c = pl.program_id(0)
    nb = qk_ref.shape[0]
    ln = M_CHUNK
    dqk, dv, nh = M_QK_DIM, M_V_DIM, M_HEADS
    halo = SUBLANES

    @pl.when(c == 0)
    def _init():
        xbuf[:, 0:halo, :] = jnp.zeros((nb, halo, xbuf.shape[2]), F32)
        ct_ref[...] = jnp.zeros(ct_ref.shape, F32)
        m_ref[...] = jnp.zeros(m_ref.shape, F32)

    row = lax.broadcasted_iota(jnp.int32, (ln, ln), 0)
    col = lax.broadcasted_iota(jnp.int32, (ln, ln), 1)
    tril = row >= col
    trilf = jnp.where(tril, 1.0, 0.0).astype(F32)
    ones_blk = jnp.where(lax.broadcasted_iota(jnp.int32, (ln, dv), 1) == 0, 1.0, 0.0).astype(F32)

    for b in range(nb):
        xbuf[b, halo:halo + ln, :] = qk_ref[b].astype(F32)
        conv = cb_ref[...]
        for j in range(M_CONV):
            off = halo - (M_CONV - 1) + j
            conv = conv + cw_ref[j:j + 1, :] * xbuf[b, off:off + ln, :]
        qk = conv * jax.nn.sigmoid(conv)
        xbuf[b, 0:halo, :] = xbuf[b, ln:ln + halo, :]

        g = g_ref[b] + gb_ref[...]
        lf = jnp.minimum(g, 0.0) - jnp.log1p(jnp.exp(-jnp.abs(g)))
        cum = jnp.dot(trilf, lf, preferred_element_type=F32, precision=lax.Precision.HIGHEST)
        cum_t = cum.T
        g_t = g.T
        vv = v_ref[b].astype(F32)
        hs = []
        for h in range(nh):
            chain = b * nh + h
            q = (qk[:, h * dqk:(h + 1) * dqk] * (dqk ** -0.5)).astype(BF16)
            k = qk[:, nh * dqk + h * dqk: nh * dqk + (h + 1) * dqk].astype(BF16)
            v_ext = jnp.concatenate([vv[:, h * dv:(h + 1) * dv], ones_blk], axis=-1)
            bc_col = cum[:, nh + h:nh + h + 1]
            bc_row = cum_t[nh + h:nh + h + 1, :]
            i_col = g[:, h:h + 1]
            i_row = g_t[h:h + 1, :]
            m_prev = m_ref[chain:chain + 1, 0:1]
            dmat = jnp.where(tril, bc_col - bc_row + i_row, NEG_INF)
            m_inter = bc_col + m_prev
            m_j = jnp.maximum(m_inter, jnp.max(dmat, axis=-1, keepdims=True))
            w_intra = jnp.exp(dmat - m_j)
            w_inter = jnp.exp(m_inter - m_j)
            s = lax.dot_general(q, k, (((1,), (1,)), ((), ())), preferred_element_type=F32) * w_intra
            ct = ct_ref[chain]
            num_ext = (jnp.dot(s.astype(BF16), v_ext.astype(BF16), preferred_element_type=F32)
                       + w_inter * jnp.dot(q, ct.astype(BF16), preferred_element_type=F32))
            num = num_ext[:, :dv]
            nq = num_ext[:, dv:dv + 1]
            den = jnp.maximum(jnp.abs(nq), jnp.exp(-m_j))
            hs.append(num / den)
            m_new = m_j[ln - 1:ln, :]
            bc_last = bc_col[ln - 1:ln, :]
            w_s = jnp.exp(bc_last - bc_col + i_col - m_new)
            w_c = jnp.exp(bc_last + m_prev - m_new)
            upd = lax.dot_general(k, (w_s * v_ext).astype(BF16), (((0,), (0,)), ((), ())),
                                  preferred_element_type=F32)
            ct_ref[chain] = w_c * ct + upd
            m_ref[chain:chain + 1, :] = jnp.broadcast_to(m_new, (1, m_ref.shape[1]))
        og = og_ref[b].astype(F32)
        outs = []
        for h in range(nh):
            seg = jax.nn.sigmoid(og[:, h * dv:(h + 1) * dv]) * hs[h]
            mu = jnp.mean(seg, axis=-1, keepdims=True)
            cen = seg - mu
            var = jnp.mean(cen * cen, axis=-1, keepdims=True)
            outs.append(cen * lax.rsqrt(var + LN_EPS) * ng_ref[:, h * dv:(h + 1) * dv])
        y_ref[b] = jnp.concatenate(outs, axis=-1).astype(y_ref.dtype)


def _mlstm(proj3, gates3, conv_w, conv_b, gate_b, norm_g):
    b, s, _ = proj3.shape
    ln = M_CHUNK
    wq = 2 * M_HEADS * M_QK_DIM
    wv = M_HEADS * M_V_DIM
    blk = lambda width, off: pl.BlockSpec((b, ln, width), lambda c: (0, c, off // width))
    full = lambda a: pl.BlockSpec(a.shape, lambda c: (0,) * a.ndim)
    gate_b128 = jnp.zeros((1, LANES), F32).at[0, :2 * M_HEADS].set(gate_b)
    conv_b2 = conv_b.reshape(1, wq)
    norm_g2 = norm_g.reshape(1, wv)
    return pl.pallas_call(
        _mlstm_kernel,
        grid=(s // ln,),
        in_specs=[blk(wq, P_MQK), blk(wv, P_MV), blk(wv, P_MO), blk(LANES, 0),
                  full(conv_w), full(conv_b2), full(gate_b128), full(norm_g2)],
        out_specs=pl.BlockSpec((b, ln, wv), lambda c: (0, c, 0)),
        out_shape=jax.ShapeDtypeStruct((b, s, wv), BF16),
        scratch_shapes=[pltpu.VMEM((b, SUBLANES + ln, wq), F32),
                        pltpu.VMEM((b * M_HEADS, M_QK_DIM, 2 * M_V_DIM), F32),
                        pltpu.VMEM((b * M_HEADS, LANES), F32)],
        compiler_params=_cparams(("arbitrary",)),
        name="mlstm",
    )(proj3, proj3, proj3, gates3, conv_w, conv_b2, gate_b128, norm_g2)


def _mla_prep_kernel(cq_ref, ckv_ref, kr_ref, krs_ref, qg_ref, kvg_ref, wqa_ref, wqb_ref, wkv_ref,
                     cos_ref, sin_ref, q_out, k_out, v_out):
    nd, hd = C_NOPE_DIM, C_HEADS
    scale = (C_NOPE_DIM + C_ROPE_DIM) ** -0.5 * LOG2_E
    cos = cos_ref[...]
    sin = sin_ref[...]

    cq = cq_ref[0].astype(F32)
    qn = (cq * lax.rsqrt(jnp.mean(cq * cq, axis=-1, keepdims=True) + RMS_EPS) * qg_ref[...]).astype(BF16)
    qa = jnp.dot(qn, wqa_ref[...], preferred_element_type=F32)
    qb = jnp.dot(qn, wqb_ref[...], preferred_element_type=F32)
    for h in range(hd):
        nope = qa[:, h * 2 * nd: h * 2 * nd + nd]
        rope = qa[:, h * 2 * nd + nd:(h + 1) * 2 * nd] * cos + qb[:, h * nd:(h + 1) * nd] * sin
        q_out[0, h] = (jnp.concatenate([nope, rope], axis=-1) * scale).astype(q_out.dtype)

    ckv = ckv_ref[0].astype(F32)
    kvn = (ckv * lax.rsqrt(jnp.mean(ckv * ckv, axis=-1, keepdims=True) + RMS_EPS) * kvg_ref[...]).astype(BF16)
    kv = jnp.dot(kvn, wkv_ref[...], preferred_element_type=F32)
    kr = kr_ref[0].astype(F32) * cos + krs_ref[0].astype(F32) * sin
    ones_blk = jnp.where(lax.broadcasted_iota(jnp.int32, kr.shape, 1) == 0, 1.0, 0.0).astype(F32)
    for h in range(hd):
        k_out[0, h] = jnp.concatenate([kv[:, h * nd:(h + 1) * nd], kr], axis=-1).astype(k_out.dtype)
        v_h = kv[:, hd * nd + h * nd: hd * nd + (h + 1) * nd]
        v_out[0, h] = jnp.concatenate([v_h, ones_blk], axis=-1).astype(v_out.dtype)


def _mla_prep(proj3, q_norm_g, kv_norm_g, wqa, wqb, wkv, cos128, sin128):
    b, s, _ = proj3.shape
    tm = MLA_TM
    hd, nd = C_HEADS, C_NOPE_DIM
    blk = lambda width, off: pl.BlockSpec((1, tm, width), lambda bi, i: (bi, i, off // width))
    full = lambda a: pl.BlockSpec(a.shape, lambda bi, i: (0,) * a.ndim)
    tab = pl.BlockSpec((tm, LANES), lambda bi, i: (i, 0))
    qg = q_norm_g.reshape(1, -1)
    kvg = kv_norm_g.reshape(1, -1)
    head_out = lambda width: pl.BlockSpec((1, hd, tm, width), lambda bi, i: (bi, 0, i, 0))
    return pl.pallas_call(
        _mla_prep_kernel,
        grid=(b, s // tm),
        in_specs=[blk(C_Q_LORA, P_CQ), blk(C_KV_LORA, P_CKV), blk(LANES, P_CKR), blk(LANES, P_CKRS),
                  full(qg), full(kvg), full(wqa), full(wqb), full(wkv), tab, tab],
        out_specs=[head_out(2 * nd), head_out(2 * nd), head_out(C_V_DIM + LANES)],
        out_shape=[jax.ShapeDtypeStruct((b, hd, s, 2 * nd), BF16),
                   jax.ShapeDtypeStruct((b, hd, s, 2 * nd), BF16),
                   jax.ShapeDtypeStruct((b, hd, s, C_V_DIM + LANES), BF16)],
        compiler_params=_cparams(("parallel", "arbitrary")),
        name="mla_prep",
    )(proj3, proj3, proj3, proj3, qg, kvg, wqa, wqb, wkv, cos128, sin128)


def _flash_kernel(q_ref, k_ref, v_ref, o_ref):
    qi = pl.program_id(2)
    bq = FLASH_BQ
    nh = q_ref.shape[1]
    dv = C_V_DIM

    def attend(q, h, key0, nkeys, m, acc, causal):
        ks = k_ref[0, h, pl.ds(key0, nkeys), :]
        vs = v_ref[0, h, pl.ds(key0, nkeys), :]
        s = lax.dot_general(q, ks, (((1,), (1,)), ((), ())), preferred_element_type=F32)
        if causal:
            row = lax.broadcasted_iota(jnp.int32, s.shape, 0)
            col = lax.broadcasted_iota(jnp.int32, s.shape, 1)
            s = jnp.where(row >= col, s, NEG_INF)
        m_new = jnp.maximum(m, jnp.max(s, axis=-1, keepdims=True))
        p = jnp.exp2(s - m_new)
        acc = jnp.exp2(m - m_new) * acc + jnp.dot(p.astype(BF16), vs, preferred_element_type=F32)
        return m_new, acc

    def full_step(j, carries):
        start = pl.multiple_of(j * bq, bq)
        return tuple(attend(q_ref[0, h], h, start, bq, *carries[h], False) for h in range(nh))

    init = tuple((jnp.full((bq, 1), NEG_INF, F32), jnp.zeros((bq, v_ref.shape[3]), F32)) for _ in range(nh))
    carries = lax.fori_loop(0, qi, full_step, init)

    half = bq // 2
    diag0 = pl.multiple_of(qi * bq, bq)
    for h in range(nh):
        m, acc = attend(q_ref[0, h], h, diag0, half, *carries[h], True)
        m_lo, acc_lo = attend(q_ref[0, h, half:, :], h, diag0 + half, half, m[half:], acc[half:], True)
        cols = slice(h * dv, (h + 1) * dv)
        o_ref[0, :half, cols] = (acc[:half, :dv] / acc[:half, dv:dv + 1]).astype(o_ref.dtype)
        o_ref[0, half:, cols] = (acc_lo[:, :dv] / acc_lo[:, dv:dv + 1]).astype(o_ref.dtype)


def _flash(qc, kc, vc):
    b, hd, s, dk = qc.shape
    dv = C_V_DIM
    bq = FLASH_BQ
    nh = FLASH_HEADS
    return pl.pallas_call(
        _flash_kernel,
        grid=(b, hd // nh, s // bq),
        in_specs=[pl.BlockSpec((1, nh, bq, dk), lambda bi, h, i: (bi, h, i, 0)),
                  pl.BlockSpec((1, nh, s, dk), lambda bi, h, i: (bi, h, 0, 0), pipeline_mode=pl.Buffered(1)),
                  pl.BlockSpec((1, nh, s, vc.shape[3]), lambda bi, h, i: (bi, h, 0, 0),
                               pipeline_mode=pl.Buffered(1))],
        out_specs=pl.BlockSpec((1, bq, nh * dv), lambda bi, h, i: (bi, i, h)),
        out_shape=jax.ShapeDtypeStruct((b, s, hd * dv), BF16),
        compiler_params=_cparams(("parallel", "parallel", "arbitrary")),
        name="flash",
    )(qc, kc, vc)


def _layer_norm(z, g, b):
    mu = jnp.mean(z, axis=-1, keepdims=True)
    cen = z - mu
    var = jnp.mean(cen * cen, axis=-1, keepdims=True)
    return cen * lax.rsqrt(var + LN_EPS) * g + b


def _outproj_kernel(ya_ref, yb_ref, yc_ref, x_ref, w_ref, g_ref, b_ref, wr_ref, br_ref,
                    x1_ref, x1rc_ref, e_ref, gate_ref, rank_ref, cnt_ref, cnt_sc):
    i = pl.program_id(0)
    tm = x_ref.shape[0]
    wa, wb = ya_ref.shape[1], yb_ref.shape[1]

    @pl.when(i == 0)
    def _init():
        cnt_sc[...] = jnp.zeros(cnt_sc.shape, F32)

    mix = jnp.dot(ya_ref[...], w_ref[0:wa, :], preferred_element_type=F32)
    mix = mix + jnp.dot(yb_ref[...], w_ref[wa:wa + wb, :], preferred_element_type=F32)
    mix = mix + jnp.dot(yc_ref[...], w_ref[wa + wb:, :], preferred_element_type=F32)
    x1 = _layer_norm(DN_ALPHA * x_ref[...] + mix, g_ref[...], b_ref[...])
    x1_ref[...] = x1
    _store_rc(x1rc_ref, x1)

    logits = jnp.dot(x1.astype(BF16), wr_ref[...], preferred_element_type=F32) + br_ref[...]
    lane = lax.broadcasted_iota(jnp.int32, logits.shape, 1)
    logits = jnp.where(lane < N_EXPERTS, logits, NEG_INF)
    e_out = jnp.zeros(logits.shape, jnp.int32)
    p_out = jnp.zeros(logits.shape, F32)
    top = None
    den = None
    onehots = []
    for r in range(TOP_K):
        mx = jnp.max(logits, axis=-1, keepdims=True)
        idx = jnp.min(jnp.where(logits == mx, lane, LANES), axis=-1, keepdims=True)
        if r == 0:
            top = mx
        p = jnp.exp(mx - top)
        den = p if r == 0 else den + p
        sel = lane == idx
        onehots.append(jnp.where(sel, 1.0, 0.0).astype(F32))
        e_out = jnp.where(lane == r, idx, e_out)
        p_out = jnp.where(lane == r, p, p_out)
        logits = jnp.where(sel, NEG_INF, logits)
    e_ref[...] = e_out
    gate_ref[...] = p_out / den

    oh_sum = onehots[0] + onehots[1] + onehots[2] + onehots[3]
    row = lax.broadcasted_iota(jnp.int32, (tm, tm), 0)
    col = lax.broadcasted_iota(jnp.int32, (tm, tm), 1)
    before = jnp.where(row > col, 1.0, 0.0).astype(BF16)
    base = jnp.dot(before, oh_sum.astype(BF16), preferred_element_type=F32) + cnt_sc[0:1, :]
    rank_out = jnp.zeros(logits.shape, jnp.int32)
    for r in range(TOP_K):
        rk = jnp.sum(onehots[r] * base, axis=-1, keepdims=True)
        rank_out = jnp.where(lane == r, rk.astype(jnp.int32), rank_out)
    rank_ref[...] = rank_out
    total = cnt_sc[0:1, :] + jnp.sum(oh_sum, axis=0, keepdims=True)
    cnt_sc[0:1, :] = total
    cnt_ref[...] = jnp.broadcast_to(total, cnt_ref.shape).astype(jnp.int32)


def _outproj(ya, yb, yc, x2d, w_bf, g, b, wr128, br128):
    t, d = x2d.shape
    tm = OUT_TM
    rows = lambda a: pl.BlockSpec((tm, a.shape[1]), lambda i: (i, 0))
    full = lambda a: pl.BlockSpec(a.shape, lambda i: (0,) * a.ndim, pipeline_mode=pl.Buffered(1))
    g2, b2 = g.reshape(1, d), b.reshape(1, d)
    lane_blk = pl.BlockSpec((tm, LANES), lambda i: (i, 0))
    return pl.pallas_call(
        _outproj_kernel,
        grid=(t // tm,),
        in_specs=[rows(ya), rows(yb), rows(yc), rows(x2d), full(w_bf), full(g2), full(b2), full(wr128), full(br128)],
        out_specs=[pl.BlockSpec((tm, d), lambda i: (i, 0)),
                   pl.BlockSpec((tm * ROW_TILES, LANES), lambda i: (i, 0)),
                   lane_blk, lane_blk, lane_blk,
                   pl.BlockSpec((SUBLANES, LANES), lambda i: (0, 0))],
        out_shape=[jax.ShapeDtypeStruct((t, d), F32),
                   jax.ShapeDtypeStruct((t * ROW_TILES, LANES), U32),
                   jax.ShapeDtypeStruct((t, LANES), jnp.int32),
                   jax.ShapeDtypeStruct((t, LANES), F32),
                   jax.ShapeDtypeStruct((t, LANES), jnp.int32),
                   jax.ShapeDtypeStruct((SUBLANES, LANES), jnp.int32)],
        scratch_shapes=[pltpu.VMEM((SUBLANES, LANES), F32)],
        compiler_params=_cparams(("arbitrary",)),
        name="outproj",
    )(ya, yb, yc, x2d, w_bf, g2, b2, wr128, br128)


def _rc_rows(row, n=1):
    return pl.ds(pl.multiple_of(row * ROW_TILES, ROW_TILES), n * ROW_TILES)


def _dispatch_kernel(pos_ref, cnt_ref, pst_ref, pad_ref, nblk_ref, x_hbm, xs_hbm, buf, zbuf, sem_in, sem_out, sem_z):
    i = pl.program_id(0)
    n = pl.num_programs(0)
    tm = DSP_TM
    nslot = buf.shape[0]

    def in_copy(blk, slot):
        return pltpu.make_async_copy(x_hbm.at[_rc_rows(blk * tm, tm)], buf.at[slot], sem_in.at[slot])

    def row_out(slot, t, p):
        return pltpu.make_async_copy(buf.at[slot, _rc_rows(t)], xs_hbm.at[_rc_rows(p)], sem_out.at[slot])

    def wait_outs(slot):
        for _ in range(TOP_K):
            pltpu.make_async_copy(buf.at[slot], xs_hbm.at[_rc_rows(0, tm)], sem_out.at[slot]).wait()

    slot = lax.rem(i, nslot)

    @pl.when(i == 0)
    def _first_in():
        in_copy(0, 0).start()

    @pl.when(i >= 2)
    def _free_slot():
        wait_outs(lax.rem(i + 1, nslot))

    @pl.when(i + 1 < n)
    def _next_in():
        in_copy(i + 1, lax.rem(i + 1, nslot)).start()

    in_copy(i, slot).wait()
    base = i * (tm * TOP_K)

    def issue(t, _):
        for k in range(TOP_K):
            row_out(slot, t, pos_ref[base + t * TOP_K + k]).start()
        return 0

    lax.fori_loop(0, tm, issue, 0, unroll=2)

    @pl.when(i == n - 1)
    def _finish():
        wait_outs(lax.rem(i + 2, nslot))
        wait_outs(slot)
        zbuf[...] = jnp.zeros(zbuf.shape, zbuf.dtype)

        def pad_rows(e, _):
            cnt = cnt_ref[e]
            first = pst_ref[e] + cnt
            npad = pad_ref[e] - cnt

            def zero_row(q):
                return pltpu.make_async_copy(zbuf.at[_rc_rows(0)], xs_hbm.at[_rc_rows(first + q)], sem_z)

            def start(q, c):
                zero_row(q).start()
                return c

            def wait(q, c):
                zero_row(q).wait()
                return c

            lax.fori_loop(0, npad, start, 0)
            lax.fori_loop(0, npad, wait, 0)
            return 0

        lax.fori_loop(0, N_EXPERTS, pad_rows, 0)

        def tail_block(bk, _):
            cp = pltpu.make_async_copy(zbuf, xs_hbm.at[_rc_rows(bk * MOE_PAD, MOE_PAD)], sem_z)
            cp.start()
            cp.wait()
            return 0

        lax.fori_loop(nblk_ref[0], xs_hbm.shape[0] // (MOE_PAD * ROW_TILES), tail_block, 0)


def _dispatch(pos, counts, pad_start, padded, nblk, x1rc, rows):
    t = x1rc.shape[0] // ROW_TILES
    tm = DSP_TM
    return pl.pallas_call(
        _dispatch_kernel,
        grid_spec=pltpu.PrefetchScalarGridSpec(
            num_scalar_prefetch=5,
            grid=(t // tm,),
            in_specs=[pl.BlockSpec(memory_space=pl.ANY)],
            out_specs=pl.BlockSpec(memory_space=pl.ANY),
            scratch_shapes=[pltpu.VMEM((3, tm * ROW_TILES, LANES), U32),
                            pltpu.VMEM((MOE_PAD * ROW_TILES, LANES), U32),
                            pltpu.SemaphoreType.DMA((3,)),
                            pltpu.SemaphoreType.DMA((3,)),
                            pltpu.SemaphoreType.DMA(())]),
        out_shape=jax.ShapeDtypeStruct((rows * ROW_TILES, LANES), U32),
        compiler_params=_cparams(("arbitrary",)),
        name="dispatch",
    )(pos, counts, pad_start, padded, nblk, x1rc)


def _moe_kernel(layer, sbe_ref, sbr_ref, sbn_ref, nblk_ref, tok_ref, xs_hbm, wgu_hbm, wdn_hbm, bgu_ref, bd_ref, y_hbm,
                xbuf, acc, stage_in, stage_out, wg_f, wu_f, wd_f, wg_bf, wu_bf, wd_bf, sem_in, sem_out, sem_w):
    sb = pl.program_id(0)
    nsb = pl.num_programs(0)
    tf = wg_f.shape[2]
    dff = wdn_hbm.shape[2]
    nj = dff // tf
    n = sbn_ref[sb]
    row0 = sbr_ref[sb]
    nchunk = n // MOE_PAD
    cur = lax.rem(sb, 2)
    big, huge = 2 * MOE_PAD, 4 * MOE_PAD

    def weight_copies(sb_idx, j, slot):
        e = sbe_ref[sb_idx]
        cols = pl.ds(pl.multiple_of(j * tf, tf), tf)
        up_cols = pl.ds(pl.multiple_of(dff + j * tf, tf), tf)
        copies = [pltpu.make_async_copy(wdn_hbm.at[layer, e, cols, :], wd_f.at[slot], sem_w.at[slot])]
        part = wg_f.shape[1] // MOE_W_PARTS
        for p in range(MOE_W_PARTS):
            band = pl.ds(p * part, part)
            copies.append(pltpu.make_async_copy(wgu_hbm.at[layer, e, band, cols], wg_f.at[slot, band], sem_w.at[slot]))
            copies.append(pltpu.make_async_copy(wgu_hbm.at[layer, e, band, up_cols], wu_f.at[slot, band],
                                                sem_w.at[slot]))
        return copies

    def chunk_rows(c):
        return pl.ds(pl.multiple_of(c * MOE_PAD, MOE_PAD), MOE_PAD)

    def start_chunk_in(first_row, c):
        base = first_row + c * MOE_PAD

        def issue(r, _):
            pltpu.make_async_copy(xs_hbm.at[_rc_rows(tok_ref[base + r])], stage_in.at[_rc_rows(r)], sem_in).start()
            return 0

        lax.fori_loop(0, MOE_PAD, issue, 0, unroll=8)

    def wait_chunk_in():
        pltpu.make_async_copy(xs_hbm.at[_rc_rows(0, MOE_PAD)], stage_in, sem_in).wait()

    def chunk_out(c, slot):
        return pltpu.make_async_copy(stage_out.at[slot], y_hbm.at[_rc_rows(row0 + c * MOE_PAD, MOE_PAD)],
                                     sem_out.at[slot])

    def wait_outs(count):
        @pl.when(count >= 2)
        def _older():
            chunk_out(0, lax.rem(count, 2)).wait()

        @pl.when(count >= 1)
        def _newest():
            chunk_out(0, lax.rem(count + 1, 2)).wait()

    def convert_in(slot, c):
        lo, hi = _load_rc(stage_in, MOE_PAD)
        xbuf[slot, chunk_rows(c), :] = jnp.concatenate([lo.astype(BF16), hi.astype(BF16)], axis=-1)

    @pl.when(sb == 0)
    def _load_first():
        for cp in weight_copies(0, 0, 0):
            cp.start()

        def load(c, _):
            start_chunk_in(row0, c)
            wait_chunk_in()
            convert_in(0, c)
            return 0

        lax.fori_loop(0, nchunk, load, 0)

    n_next = sbn_ref[sb + 1]

    def ffn(j, start, size, wg, wu, wd):
        rs = pl.ds(start, size)
        xr = xbuf[cur, rs, :]
        cols = pl.ds(pl.multiple_of(j * tf, tf), tf)
        up_cols = pl.ds(pl.multiple_of(dff + j * tf, tf), tf)
        gt = jnp.dot(xr, wg, preferred_element_type=F32) + bgu_ref[:, cols]
        up = jnp.dot(xr, wu, preferred_element_type=F32) + bgu_ref[:, up_cols]
        gt = jnp.minimum(gt, SWIGLU_LIMIT)
        up = jnp.clip(up, -SWIGLU_LIMIT, SWIGLU_LIMIT)
        act = ((up + 1.0) * gt * jax.nn.sigmoid(SWIGLU_ALPHA * gt)).astype(BF16)
        half = acc.shape[1] // 2
        for cs in (slice(0, half), slice(half, 2 * half)):
            acc[rs, cs] = acc[rs, cs] + jnp.dot(act, wd[:, cs], preferred_element_type=F32)

    def first_chunk(j, wslot, size):
        wg = wg_f[wslot].astype(BF16)
        wu = wu_f[wslot].astype(BF16)
        wd = wd_f[wslot].astype(BF16)
        wg_bf[...] = wg
        wu_bf[...] = wu
        wd_bf[...] = wd
        ffn(j, 0, size, wg, wu, wd)

    def hidden_tile(j, _):
        wslot = lax.rem(j, 2)
        for cp in weight_copies(sb, j, wslot):
            cp.wait()

        @pl.when(j + 1 < nj)
        def _next_tile():
            for cp in weight_copies(sb, j + 1, 1 - wslot):
                cp.start()

        @pl.when((j + 1 == nj) & (n_next > 0))
        def _next_super_block():
            for cp in weight_copies(sb + 1, 0, 0):
                cp.start()

        fetch_next = j * MOE_PAD < n_next

        @pl.when(fetch_next)
        def _start_next():
            start_chunk_in(sbr_ref[sb + 1], j)

        @pl.when((j == 1) & (sb > 0))
        def _drain_prev_outs():
            wait_outs(sbn_ref[jnp.maximum(sb - 1, 0)] // MOE_PAD)

        @pl.when(j == 0)
        def _init_acc():
            def init(c, _):
                acc[chunk_rows(c), :] = jnp.broadcast_to(bd_ref[...], (MOE_PAD, acc.shape[1]))
                return 0

            lax.fori_loop(0, nchunk, init, 0)

        def rest(start, size):
            ffn(j, start, size, wg_bf[...], wu_bf[...], wd_bf[...])

        @pl.when(n >= huge)
        def _rows_huge():
            first_chunk(j, wslot, huge)

            @pl.when(n - huge >= big)
            def _then_big():
                rest(huge, big)

            @pl.when(lax.rem(n, big) != 0)
            def _then_pad():
                rest(pl.multiple_of(n - MOE_PAD, MOE_PAD), MOE_PAD)

        @pl.when((n >= big) & (n < huge))
        def _rows_big():
            first_chunk(j, wslot, big)

            @pl.when(n > big)
            def _then_pad():
                rest(big, MOE_PAD)

        @pl.when(n < big)
        def _rows_small():
            first_chunk(j, wslot, MOE_PAD)

        @pl.when(j == nj - 1)
        def _store_rows():
            def store(c, _):
                slot = lax.rem(c, 2)

                @pl.when(c >= 2)
                def _slot_free():
                    chunk_out(c - 2, slot).wait()

                _store_rc(stage_out.at[slot], acc[chunk_rows(c), :])
                chunk_out(c, slot).start()
                return 0

            lax.fori_loop(0, nchunk, store, 0)

            @pl.when((sb == nsb - 1) | (n_next == 0))
            def _no_later_tile():
                wait_outs(nchunk)

        @pl.when(fetch_next)
        def _finish_next():
            wait_chunk_in()
            convert_in(1 - cur, j)

        return 0

    @pl.when(n > 0)
    def _super_block():
        lax.fori_loop(0, nj, hidden_tile, 0)

    @pl.when(sb == nsb - 1)
    def _zero_tail():
        stage_out[0] = jnp.zeros(stage_out.shape[1:], stage_out.dtype)

        def tail_block(bk, _):
            cp = pltpu.make_async_copy(stage_out.at[0], y_hbm.at[_rc_rows(bk * MOE_PAD, MOE_PAD)], sem_out.at[0])
            cp.start()
            cp.wait()
            return 0

        lax.fori_loop(nblk_ref[0], y_hbm.shape[0] // (MOE_PAD * ROW_TILES), tail_block, 0)


def _moe(layer, sb_e, sb_row0, sb_n, nblk, row_tok, xs, w_gate_up, b_gate_up, w_down, b_down):
    rows = row_tok.shape[0]
    d = D_MODEL
    dff = w_down.shape[2]
    tf = MOE_TF
    nj = dff // tf
    nsb = sb_e.shape[0] - 1
    assert nj % 2 == 0 and MOE_TMAX // MOE_PAD <= nj and MOE_TMAX == 6 * MOE_PAD
    bgu = b_gate_up.reshape(DEPTH, N_EXPERTS, 1, 2 * dff)
    bdn = b_down.reshape(DEPTH, N_EXPERTS, 1, d)
    expert_row = lambda width: pl.BlockSpec((None, None, 1, width), lambda sb, e, r, n, nb, tk: (layer, e[sb], 0, 0))
    return pl.pallas_call(
        functools.partial(_moe_kernel, layer),
        grid_spec=pltpu.PrefetchScalarGridSpec(
            num_scalar_prefetch=5,
            grid=(nsb,),
            in_specs=[pl.BlockSpec(memory_space=pl.ANY),
                      pl.BlockSpec(memory_space=pl.ANY),
                      pl.BlockSpec(memory_space=pl.ANY),
                      expert_row(2 * dff),
                      expert_row(d)],
            out_specs=pl.BlockSpec(memory_space=pl.ANY),
            scratch_shapes=[pltpu.VMEM((2, MOE_TMAX, d), BF16),
                            pltpu.VMEM((MOE_TMAX, d), F32),
                            pltpu.VMEM((MOE_PAD * ROW_TILES, LANES), U32),
                            pltpu.VMEM((2, MOE_PAD * ROW_TILES, LANES), U32),
                            pltpu.VMEM((2, d, tf), F32),
                            pltpu.VMEM((2, d, tf), F32),
                            pltpu.VMEM((2, tf, d), F32),
                            pltpu.VMEM((d, tf), BF16),
                            pltpu.VMEM((d, tf), BF16),
                            pltpu.VMEM((tf, d), BF16),
                            pltpu.SemaphoreType.DMA(()),
                            pltpu.SemaphoreType.DMA((2,)),
                            pltpu.SemaphoreType.DMA((2,))]),
        out_shape=jax.ShapeDtypeStruct((rows * ROW_TILES, LANES), U32),
        compiler_params=_cparams(("arbitrary",)),
        name="moe",
    )(sb_e, sb_row0, sb_n, nblk, row_tok, xs, w_gate_up, w_down, bgu, bdn)


def _combine_kernel(pos_ref, y_hbm, gate_ref, x1_ref, g_ref, b_ref, o_ref, buf, sem):
    i = pl.program_id(0)
    n = pl.num_programs(0)
    tm = CMB_TM
    slot = lax.rem(i, 2)

    def issue(blk, dst_slot):
        base = blk * (tm * TOP_K)

        def body(t, _):
            for k in range(TOP_K):
                p = pos_ref[base + t * TOP_K + k]
                pltpu.make_async_copy(y_hbm.at[_rc_rows(p)], buf.at[dst_slot, k, _rc_rows(t)], sem.at[dst_slot]).start()
            return 0

        lax.fori_loop(0, tm, body, 0, unroll=2)

    @pl.when(i == 0)
    def _first():
        issue(0, 0)

    @pl.when(i + 1 < n)
    def _next():
        issue(i + 1, 1 - slot)

    for k in range(TOP_K):
        pltpu.make_async_copy(y_hbm.at[_rc_rows(0, tm)], buf.at[slot, k], sem.at[slot]).wait()
    gate = gate_ref[...]
    ffn_lo, ffn_hi = None, None
    for k in range(TOP_K):
        lo, hi = _load_rc(buf.at[slot, k], tm)
        gk = gate[:, k:k + 1]
        ffn_lo = gk * lo if k == 0 else ffn_lo + gk * lo
        ffn_hi = gk * hi if k == 0 else ffn_hi + gk * hi
    ffn = jnp.concatenate([ffn_lo, ffn_hi], axis=-1)
    o_ref[...] = _layer_norm(DN_ALPHA * x1_ref[...] + ffn, g_ref[...], b_ref[...])


def _combine(pos_flat, y, gate, x1, g, b):
    t, d = x1.shape
    tm = CMB_TM
    g2, b2 = g.reshape(1, d), b.reshape(1, d)
    return pl.pallas_call(
        _combine_kernel,
        grid_spec=pltpu.PrefetchScalarGridSpec(
            num_scalar_prefetch=1,
            grid=(t // tm,),
            in_specs=[pl.BlockSpec(memory_space=pl.ANY),
                      pl.BlockSpec((tm, LANES), lambda i, p: (i, 0)),
                      pl.BlockSpec((tm, d), lambda i, p: (i, 0)),
                      pl.BlockSpec((1, d), lambda i, p: (0, 0)),
                      pl.BlockSpec((1, d), lambda i, p: (0, 0))],
            out_specs=pl.BlockSpec((tm, d), lambda i, p: (i, 0)),
            scratch_shapes=[pltpu.VMEM((2, TOP_K, tm * ROW_TILES, LANES), U32), pltpu.SemaphoreType.DMA((2,))]),
        out_shape=jax.ShapeDtypeStruct((t, d), F32),
        compiler_params=_cparams(("arbitrary",)),
        name="combine",
    )(pos_flat, y, gate, x1, g2, b2)


def _swap_halves(w):
    half = w.shape[-1] // 2
    return jnp.concatenate([w[..., half:], w[..., :half]], axis=-1)


def _w_in_columns(w):
    d = w.shape[0]
    widths = (512, 128, 128, 256, 256, 512, 512, 8, 512, 256, 64)
    offs = [0]
    for wd in widths:
        offs.append(offs[-1] + wd)
    a_q, a_k, a_v, m_q, m_k, m_v, m_o, m_if, c_q, c_kv, c_kr = [w[:, offs[i]:offs[i + 1]] for i in range(len(widths))]
    z = lambda n: jnp.zeros((d, n), w.dtype)
    return jnp.concatenate([a_q, m_v, m_o, c_q, m_q, m_k, c_kv, a_k, a_v,
                            c_kr, z(LANES - 64), m_if, z(LANES - 8), _swap_halves(c_kr), z(LANES - 64)], axis=1)


def _w_in_layout_kernel(w_ref, o_ref):
    o_ref[...] = _w_in_columns(w_ref[...]).astype(o_ref.dtype)


def _layout_w_in(w_in, layer):
    _, d, n = w_in.shape
    tk = 256
    return pl.pallas_call(
        _w_in_layout_kernel,
        grid=(d // tk,),
        in_specs=[pl.BlockSpec((None, tk, n), lambda i: (layer, i, 0))],
        out_specs=pl.BlockSpec((tk, N_PROJ), lambda i: (i, 0)),
        out_shape=jax.ShapeDtypeStruct((d, N_PROJ), BF16),
        compiler_params=_cparams(("parallel",)),
        name="w_in_layout",
    )(w_in)


def _layout_w_uq(w):
    r = w.shape[0]
    w3 = w.astype(BF16).reshape(r, C_HEADS, C_NOPE_DIM + C_ROPE_DIM)
    nope, rope = w3[..., :C_NOPE_DIM], w3[..., C_NOPE_DIM:]
    z = jnp.zeros((r, C_HEADS, LANES - C_ROPE_DIM), BF16)
    wa = jnp.concatenate([nope, rope, z], axis=-1).reshape(r, -1)
    wb = jnp.concatenate([_swap_halves(rope), z], axis=-1).reshape(r, -1)
    return wa, wb


def _layout_w_ukv(w):
    r = w.shape[0]
    w3 = w.astype(BF16).reshape(r, C_HEADS, C_NOPE_DIM + C_V_DIM)
    return jnp.concatenate([w3[..., :C_NOPE_DIM].reshape(r, -1), w3[..., C_NOPE_DIM:].reshape(r, -1)], axis=-1)


def _rope_tables(seq):
    dim = C_ROPE_DIM
    inv = 1.0 / (ROPE_THETA ** (jnp.arange(0, dim, 2, dtype=F32) / dim))
    ang = jnp.arange(seq, dtype=F32)[:, None] * inv[None, :]
    cos, sin = jnp.cos(ang), jnp.sin(ang)
    z = jnp.zeros((seq, LANES - dim), F32)
    return jnp.concatenate([cos, cos, z], axis=-1), jnp.concatenate([-sin, sin, z], axis=-1)


def _route(top_e, rank, counts, rows, nsb):
    n_assign = top_e.shape[0] * TOP_K
    padded = (counts + MOE_PAD - 1) // MOE_PAD * MOE_PAD
    pad_end = jnp.cumsum(padded)
    pad_start = pad_end - padded
    is_e = top_e[:, :, None] == jnp.arange(N_EXPERTS, dtype=jnp.int32)[None, None, :]
    pos = (jnp.sum(jnp.where(is_e, pad_start[None, None, :], 0), axis=-1) + rank).reshape(-1).astype(jnp.int32)
    nblk = (pad_end[-1:] // MOE_PAD).astype(jnp.int32)
    order = jnp.argsort(top_e.reshape(-1), stable=True).astype(jnp.int32)
    grp_start = jnp.cumsum(counts) - counts
    row = jnp.arange(rows, dtype=jnp.int32)
    row_e = jnp.minimum(jnp.searchsorted(pad_end, row, side="right"), N_EXPERTS - 1)
    in_grp = row - pad_start[row_e]
    real = in_grp < counts[row_e]
    src = jnp.clip(grp_start[row_e] + in_grp, 0, n_assign - 1)
    row_tok = jnp.where(real, order[src] // TOP_K, 0).astype(jnp.int32)
    nsb_e = (padded + MOE_TMAX - 1) // MOE_TMAX
    sb_end = jnp.cumsum(nsb_e)
    sb_start = sb_end - nsb_e
    sb_idx = jnp.arange(nsb + 1, dtype=jnp.int32)
    n_valid = sb_end[-1]
    sb_eff = jnp.minimum(sb_idx, n_valid - 1)
    sb_e = jnp.searchsorted(sb_end, sb_eff, side="right").astype(jnp.int32)
    part = sb_eff - sb_start[sb_e]
    sb_row0 = pad_start[sb_e] + part * MOE_TMAX
    sb_n = jnp.where(sb_idx < n_valid, jnp.minimum(padded[sb_e] - part * MOE_TMAX, MOE_TMAX), 0)
    return pos, row_tok, nblk, sb_e, sb_row0.astype(jnp.int32), sb_n.astype(jnp.int32)


def kernel(x, w_in, conv_w, conv_b, m_gate_b, m_norm_g, sinks, q_norm_g, w_uq, kv_norm_g, w_ukv, w_out,
           ln1_g, ln1_b, w_router, b_router, w_gate_up, b_gate_up, w_down, b_down, ln2_g, ln2_b):
    b, s, d = x.shape
    t = b * s
    n_assign = t * TOP_K
    rows = (n_assign // MOE_PAD + N_EXPERTS) * MOE_PAD
    nsb = N_EXPERTS + rows // MOE_TMAX
    cos128, sin128 = _rope_tables(s)
    xt = x.reshape(t, d)
    for l in range(DEPTH):
        w_in_bf = _layout_w_in(w_in, l)
        wqa, wqb = _layout_w_uq(w_uq[l])
        wkv = _layout_w_ukv(w_ukv[l])
        w_out_bf = w_out[l].astype(BF16)
        wr128 = jnp.zeros((d, LANES), BF16).at[:, :N_EXPERTS].set(w_router[l].astype(BF16))
        br128 = jnp.zeros((1, LANES), F32).at[0, :N_EXPERTS].set(b_router[l])

        proj, gates = _proj(xt, w_in_bf)
        proj3 = proj.reshape(b, s, N_PROJ)
        y_a = _swa(proj3, sinks[l])
        y_b = _mlstm(proj3, gates.reshape(b, s, LANES), conv_w[l], conv_b[l], m_gate_b[l], m_norm_g[l])
        qc, kc, vc = _mla_prep(proj3, q_norm_g[l], kv_norm_g[l], wqa, wqb, wkv, cos128, sin128)
        y_c = _flash(qc, kc, vc)
        x1, x1rc, top_e, gate, rank, cnt = _outproj(y_a.reshape(t, -1), y_b.reshape(t, -1), y_c.reshape(t, -1), xt,
                                                    w_out_bf, ln1_g[l], ln1_b[l], wr128, br128)
        pos, row_tok, nblk, sb_e, sb_row0, sb_n = _route(top_e[:, :TOP_K], rank[:, :TOP_K], cnt[0, :N_EXPERTS],
                                                         rows, nsb)
        y = _moe(l, sb_e, sb_row0, sb_n, nblk, row_tok, x1rc, w_gate_up, b_gate_up, w_down, b_down)
        xt = _combine(pos, y, gate, x1, ln2_g[l], ln2_b[l])
    return xt.reshape(b, s, d)
```

```python
import functools

import jax
import jax.numpy as jnp
from jax import lax
from jax.experimental import pallas as pl
from jax.experimental.pallas import tpu as pltpu

F32 = jnp.float32
BF16 = jnp.bfloat16
NEG_INF = float("-inf")
LOG2_E = 1.4426950408889634

D_MODEL = 2048
DEPTH = 2
SWA_HEADS, SWA_KV_HEADS, SWA_HEAD_DIM, SWA_WINDOW = 8, 2, 64, 128
M_HEADS, M_QK_DIM, M_V_DIM, M_CONV = 4, 64, 128, 4
C_HEADS, C_NOPE_DIM, C_ROPE_DIM, C_V_DIM = 8, 128, 64, 128
C_Q_LORA, C_KV_LORA = 512, 256
ROPE_THETA = 10000.0
N_EXPERTS, TOP_K = 32, 4
SWIGLU_LIMIT, SWIGLU_ALPHA = 7.0, 1.702
DN_ALPHA = (2 * DEPTH) ** 0.25
LN_EPS, RMS_EPS = 1e-5, 1e-6

LANES = 128
SUBLANES = 8
VMEM_LIMIT = 56 * 1024 * 1024

P_AQ, P_MV, P_MO, P_CQ, P_MQK, P_CKV, P_AK, P_AV, P_CKR, P_MIF, P_CKRS = (
    0, 512, 1024, 1536, 2048, 2560, 2816, 2944, 3072, 3200, 3328)
N_PROJ = 3456

PROJ_TM, PROJ_TN = 1024, 1152
M_CHUNK = 128
MLA_TM = 512
FLASH_BQ = 1024
FLASH_HEADS = 4
OUT_TM = 512
MOE_PAD = 256
MOE_TMAX = 1536
MOE_TF = 256
MOE_W_PARTS = 4
CMB_TM = 128
DSP_TM = 128

ROW_TILES = D_MODEL // 2 // LANES
U32 = jnp.uint32
HI_MASK = 0xFFFF0000


def _cparams(sem, vmem=VMEM_LIMIT):
    return pltpu.CompilerParams(dimension_semantics=sem, vmem_limit_bytes=vmem)


def _bf16_bits(x):
    return lax.bitcast_convert_type(x.astype(BF16).astype(F32), U32)


def _store_rc(ref, val):
    n, d = val.shape
    words = (_bf16_bits(val[:, :d // 2]) >> 16) | (_bf16_bits(val[:, d // 2:]) & U32(HI_MASK))
    x = jnp.stack([words[:, s * LANES:(s + 1) * LANES] for s in range(ROW_TILES)], axis=0)
    ref[...] = pltpu.einshape("stl->tsl", x).reshape(n * ROW_TILES, LANES)


def _load_rc(ref, n):
    x = pltpu.einshape("tsl->stl", ref[...].reshape(n, ROW_TILES, LANES))
    words = jnp.concatenate([x[s] for s in range(ROW_TILES)], axis=-1)
    lo = lax.bitcast_convert_type(words << 16, F32)
    hi = lax.bitcast_convert_type(words & U32(HI_MASK), F32)
    return lo, hi


def _proj_kernel(x_ref, w_ref, o_ref, g_ref):
    j = pl.program_id(1)
    acc = jnp.dot(x_ref[...].astype(BF16), w_ref[...], preferred_element_type=F32)
    o_ref[...] = acc.astype(o_ref.dtype)

    @pl.when(j == P_MIF // PROJ_TN)
    def _gates():
        off = P_MIF % PROJ_TN
        g_ref[...] = acc[:, off:off + LANES]


def _proj(x2d, w_bf):
    t, d = x2d.shape
    n = w_bf.shape[1]
    return pl.pallas_call(
        _proj_kernel,
        grid=(t // PROJ_TM, n // PROJ_TN),
        in_specs=[pl.BlockSpec((PROJ_TM, d), lambda i, j: (i, 0)),
                  pl.BlockSpec((d, PROJ_TN), lambda i, j: (0, j))],
        out_specs=[pl.BlockSpec((PROJ_TM, PROJ_TN), lambda i, j: (i, j)),
                   pl.BlockSpec((PROJ_TM, LANES), lambda i, j: (i, 0))],
        out_shape=[jax.ShapeDtypeStruct((t, n), BF16), jax.ShapeDtypeStruct((t, LANES), F32)],
        compiler_params=_cparams(("parallel", "arbitrary")),
        name="proj",
    )(x2d, w_bf)


def _swa_kernel(sinks_ref, q_ref, kc_ref, kp_ref, vc_ref, vp_ref, o_ref):
    i = pl.program_id(1)
    w = SWA_WINDOW
    dh = SWA_HEAD_DIM
    grp = SWA_HEADS // SWA_KV_HEADS
    q = q_ref[0]
    qi = lax.broadcasted_iota(jnp.int32, (w, 2 * w), 0)
    kj = lax.broadcasted_iota(jnp.int32, (w, 2 * w), 1)
    rel = qi + w - kj
    has_prev = jnp.where(i > 0, 0, w)
    mask = (rel >= 0) & (rel < w) & (kj >= has_prev)
    outs = []
    for kvh in range(SWA_KV_HEADS):
        sl = slice(kvh * dh, (kvh + 1) * dh)
        k_cat = jnp.concatenate([kp_ref[0][:, sl], kc_ref[0][:, sl]], axis=0).astype(BF16)
        v_cat = jnp.concatenate([vp_ref[0][:, sl], vc_ref[0][:, sl]], axis=0).astype(BF16)
        for g in range(grp):
            h = kvh * grp + g
            qh = (q[:, h * dh:(h + 1) * dh] * (dh ** -0.5)).astype(BF16)
            s = lax.dot_general(qh, k_cat, (((1,), (1,)), ((), ())), preferred_element_type=F32)
            s = jnp.where(mask, s, NEG_INF)
            sink = sinks_ref[h]
            m = jnp.maximum(jnp.max(s, axis=-1, keepdims=True), sink)
            p = jnp.exp(s - m)
            den = jnp.sum(p, axis=-1, keepdims=True) + jnp.exp(sink - m)
            o = jnp.dot(p.astype(BF16), v_cat, preferred_element_type=F32)
            outs.append(o / den)
    o_ref[0] = jnp.concatenate(outs, axis=-1).astype(o_ref.dtype)


def _swa(proj3, sinks):
    b, s, _ = proj3.shape
    w = SWA_WINDOW
    nb = s // w
    kw = SWA_KV_HEADS * SWA_HEAD_DIM
    qw = SWA_HEADS * SWA_HEAD_DIM
    cur = lambda col: (lambda bi, i: (bi, i, col))
    prev = lambda col: (lambda bi, i: (bi, jnp.maximum(i - 1, 0), col))
    return pl.pallas_call(
        _swa_kernel,
        grid=(b, nb),
        in_specs=[pl.BlockSpec(memory_space=pltpu.SMEM),
                  pl.BlockSpec((1, w, qw), cur(P_AQ // qw)),
                  pl.BlockSpec((1, w, kw), cur(P_AK // kw)),
                  pl.BlockSpec((1, w, kw), prev(P_AK // kw)),
                  pl.BlockSpec((1, w, kw), cur(P_AV // kw)),
                  pl.BlockSpec((1, w, kw), prev(P_AV // kw))],
        out_specs=pl.BlockSpec((1, w, qw), lambda bi, i: (bi, i, 0)),
        out_shape=jax.ShapeDtypeStruct((b, s, qw), BF16),
        compiler_params=_cparams(("parallel", "arbitrary")),
        name="swa",
    )(sinks, proj3, proj3, proj3, proj3, proj3)


def _mlstm_kernel(qk_ref, v_ref, og_ref, g_ref, cw_ref, cb_ref, gb_ref, ng_ref, y_ref, xbuf, ct_ref, m_ref):
    c = pl.program_id(0)
    nb = qk_ref.shape[0]
    ln = M_CHUNK
    dqk, dv, nh = M_QK_DIM, M_V_DIM, M_HEADS
    halo = SUBLANES

    @pl.when(c == 0)
    def _init():
        xbuf[:, 0:halo, :] = jnp.zeros((nb, halo, xbuf.shape[2]), F32)
        ct_ref[...] = jnp.zeros(ct_ref.shape, F32)
        m_ref[...] = jnp.zeros(m_ref.shape, F32)

    row = lax.broadcasted_iota(jnp.int32, (ln, ln), 0)
    col = lax.broadcasted_iota(jnp.int32, (ln, ln), 1)
    tril = row >= col
    trilf = jnp.where(tril, 1.0, 0.0).astype(F32)
    ones_blk = jnp.where(lax.broadcasted_iota(jnp.int32, (ln, dv), 1) == 0, 1.0, 0.0).astype(F32)

    for b in range(nb):
        xbuf[b, halo:halo + ln, :] = qk_ref[b].astype(F32)
        conv = cb_ref[...]
        for j in range(M_CONV):
            off = halo - (M_CONV - 1) + j
            conv = conv + cw_ref[j:j + 1, :] * xbuf[b, off:off + ln, :]
        qk = conv * jax.nn.sigmoid(conv)
        xbuf[b, 0:halo, :] = xbuf[b, ln:ln + halo, :]

        g = g_ref[b] + gb_ref[...]
        lf = jnp.minimum(g, 0.0) - jnp.log1p(jnp.exp(-jnp.abs(g)))
        cum = jnp.dot(trilf, lf, preferred_element_type=F32, precision=lax.Precision.HIGHEST)
        cum_t = cum.T
        g_t = g.T
        vv = v_ref[b].astype(F32)
        hs = []
        for h in range(nh):
            chain = b * nh + h
            q = (qk[:, h * dqk:(h + 1) * dqk] * (dqk ** -0.5)).astype(BF16)
            k = qk[:, nh * dqk + h * dqk: nh * dqk + (h + 1) * dqk].astype(BF16)
            v_ext = jnp.concatenate([vv[:, h * dv:(h + 1) * dv], ones_blk], axis=-1)
            bc_col = cum[:, nh + h:nh + h + 1]
            bc_row = cum_t[nh + h:nh + h + 1, :]
            i_col = g[:, h:h + 1]
            i_row = g_t[h:h + 1, :]
            m_prev = m_ref[chain:chain + 1, 0:1]
            dmat = jnp.where(tril, bc_col - bc_row + i_row, NEG_INF)
            m_inter = bc_col + m_prev
            m_j = jnp.maximum(m_inter, jnp.max(dmat, axis=-1, keepdims=True))
            w_intra = jnp.exp(dmat - m_j)
            w_inter = jnp.exp(m_inter - m_j)
            s = lax.dot_general(q, k, (((1,), (1,)), ((), ())), preferred_element_type=F32) * w_intra
            ct = ct_ref[chain]
            num_ext = (jnp.dot(s.astype(BF16), v_ext.astype(BF16), preferred_element_type=F32)
                       + w_inter * jnp.dot(q, ct.astype(BF16), preferred_element_type=F32))
            num = num_ext[:, :dv]
            nq = num_ext[:, dv:dv + 1]
            den = jnp.maximum(jnp.abs(nq), jnp.exp(-m_j))
            hs.append(num / den)
            m_new = m_j[ln - 1:ln, :]
            bc_last = bc_col[ln - 1:ln, :]
            w_s = jnp.exp(bc_last - bc_col + i_col - m_new)
            w_c = jnp.exp(bc_last + m_prev - m_new)
            upd = lax.dot_general(k, (w_s * v_ext).astype(BF16), (((0,), (0,)), ((), ())),
                                  preferred_element_type=F32)
            ct_ref[chain] = w_c * ct + upd
            m_ref[chain:chain + 1, :] = jnp.broadcast_to(m_new, (1, m_ref.shape[1]))
        og = og_ref[b].astype(F32)
        outs = []
        for h in range(nh):
            seg = jax.nn.sigmoid(og[:, h * dv:(h + 1) * dv]) * hs[h]
            mu = jnp.mean(seg, axis=-1, keepdims=True)
            cen = seg - mu
            var = jnp.mean(cen * cen, axis=-1, keepdims=True)
            outs.append(cen * lax.rsqrt(var + LN_EPS) * ng_ref[:, h * dv:(h + 1) * dv])
        y_ref[b] = jnp.concatenate(outs, axis=-1).astype(y_ref.dtype)


def _mlstm(proj3, gates3, conv_w, conv_b, gate_b, norm_g):
    b, s, _ = proj3.shape
    ln = M_CHUNK
    wq = 2 * M_HEADS * M_QK_DIM
    wv = M_HEADS * M_V_DIM
    blk = lambda width, off: pl.BlockSpec((b, ln, width), lambda c: (0, c, off // width))
    full = lambda a: pl.BlockSpec(a.shape, lambda c: (0,) * a.ndim)
    gate_b128 = jnp.zeros((1, LANES), F32).at[0, :2 * M_HEADS].set(gate_b)
    conv_b2 = conv_b.reshape(1, wq)
    norm_g2 = norm_g.reshape(1, wv)
    return pl.pallas_call(
        _mlstm_kernel,
        grid=(s // ln,),
        in_specs=[blk(wq, P_MQK), blk(wv, P_MV), blk(wv, P_MO), blk(LANES, 0),
                  full(conv_w), full(conv_b2), full(gate_b128), full(norm_g2)],
        out_specs=pl.BlockSpec((b, ln, wv), lambda c: (0, c, 0)),
        out_shape=jax.ShapeDtypeStruct((b, s, wv), BF16),
        scratch_shapes=[pltpu.VMEM((b, SUBLANES + ln, wq), F32),
                        pltpu.VMEM((b * M_HEADS, M_QK_DIM, 2 * M_V_DIM), F32),
                        pltpu.VMEM((b * M_HEADS, LANES), F32)],
        compiler_params=_cparams(("arbitrary",)),
        name="mlstm",
    )(proj3, proj3, proj3, gates3, conv_w, conv_b2, gate_b128, norm_g2)


def _mla_prep_kernel(cq_ref, ckv_ref, kr_ref, krs_ref, qg_ref, kvg_ref, wqa_ref, wqb_ref, wkv_ref,
                     cos_ref, sin_ref, q_out, k_out, v_out):
    nd, hd = C_NOPE_DIM, C_HEADS
    scale = (C_NOPE_DIM + C_ROPE_DIM) ** -0.5 * LOG2_E
    cos = cos_ref[...]
    sin = sin_ref[...]

    cq = cq_ref[0].astype(F32)
    qn = (cq * lax.rsqrt(jnp.mean(cq * cq, axis=-1, keepdims=True) + RMS_EPS) * qg_ref[...]).astype(BF16)
    qa = jnp.dot(qn, wqa_ref[...], preferred_element_type=F32)
    qb = jnp.dot(qn, wqb_ref[...], preferred_element_type=F32)
    for h in range(hd):
        nope = qa[:, h * 2 * nd: h * 2 * nd + nd]
        rope = qa[:, h * 2 * nd + nd:(h + 1) * 2 * nd] * cos + qb[:, h * nd:(h + 1) * nd] * sin
        q_out[0, h] = (jnp.concatenate([nope, rope], axis=-1) * scale).astype(q_out.dtype)

    ckv = ckv_ref[0].astype(F32)
    kvn = (ckv * lax.rsqrt(jnp.mean(ckv * ckv, axis=-1, keepdims=True) + RMS_EPS) * kvg_ref[...]).astype(BF16)
    kv = jnp.dot(kvn, wkv_ref[...], preferred_element_type=F32)
    kr = kr_ref[0].astype(F32) * cos + krs_ref[0].astype(F32) * sin
    ones_blk = jnp.where(lax.broadcasted_iota(jnp.int32, kr.shape, 1) == 0, 1.0, 0.0).astype(F32)
    for h in range(hd):
        k_out[0, h] = jnp.concatenate([kv[:, h * nd:(h + 1) * nd], kr], axis=-1).astype(k_out.dtype)
        v_h = kv[:, hd * nd + h * nd: hd * nd + (h + 1) * nd]
        v_out[0, h] = jnp.concatenate([v_h, ones_blk], axis=-1).astype(v_out.dtype)


def _mla_prep(proj3, q_norm_g, kv_norm_g, wqa, wqb, wkv, cos128, sin128):
    b, s, _ = proj3.shape
    tm = MLA_TM
    hd, nd = C_HEADS, C_NOPE_DIM
    blk = lambda width, off: pl.BlockSpec((1, tm, width), lambda bi, i: (bi, i, off // width))
    full = lambda a: pl.BlockSpec(a.shape, lambda bi, i: (0,) * a.ndim)
    tab = pl.BlockSpec((tm, LANES), lambda bi, i: (i, 0))
    qg = q_norm_g.reshape(1, -1)
    kvg = kv_norm_g.reshape(1, -1)
    head_out = lambda width: pl.BlockSpec((1, hd, tm, width), lambda bi, i: (bi, 0, i, 0))
    return pl.pallas_call(
        _mla_prep_kernel,
        grid=(b, s // tm),
        in_specs=[blk(C_Q_LORA, P_CQ), blk(C_KV_LORA, P_CKV), blk(LANES, P_CKR), blk(LANES, P_CKRS),
                  full(qg), full(kvg), full(wqa), full(wqb), full(wkv), tab, tab],
        out_specs=[head_out(2 * nd), head_out(2 * nd), head_out(C_V_DIM + LANES)],
        out_shape=[jax.ShapeDtypeStruct((b, hd, s, 2 * nd), BF16),
                   jax.ShapeDtypeStruct((b, hd, s, 2 * nd), BF16),
                   jax.ShapeDtypeStruct((b, hd, s, C_V_DIM + LANES), BF16)],
        compiler_params=_cparams(("parallel", "arbitrary")),
        name="mla_prep",
    )(proj3, proj3, proj3, proj3, qg, kvg, wqa, wqb, wkv, cos128, sin128)


def _flash_kernel(q_ref, k_ref, v_ref, o_ref):
    qi = pl.program_id(2)
    bq = FLASH_BQ
    nh = q_ref.shape[1]
    dv = C_V_DIM

    def attend(q, h, key0, nkeys, m, acc, causal):
        ks = k_ref[0, h, pl.ds(key0, nkeys), :]
        vs = v_ref[0, h, pl.ds(key0, nkeys), :]
        s = lax.dot_general(q, ks, (((1,), (1,)), ((), ())), preferred_element_type=F32)
        if causal:
            row = lax.broadcasted_iota(jnp.int32, s.shape, 0)
            col = lax.broadcasted_iota(jnp.int32, s.shape, 1)
            s = jnp.where(row >= col, s, NEG_INF)
        m_new = jnp.maximum(m, jnp.max(s, axis=-1, keepdims=True))
        p = jnp.exp2(s - m_new)
        acc = jnp.exp2(m - m_new) * acc + jnp.dot(p.astype(BF16), vs, preferred_element_type=F32)
        return m_new, acc

    def full_step(j, carries):
        start = pl.multiple_of(j * bq, bq)
        return tuple(attend(q_ref[0, h], h, start, bq, *carries[h], False) for h in range(nh))

    init = tuple((jnp.full((bq, 1), NEG_INF, F32), jnp.zeros((bq, v_ref.shape[3]), F32)) for _ in range(nh))
    carries = lax.fori_loop(0, qi, full_step, init)

    half = bq // 2
    diag0 = pl.multiple_of(qi * bq, bq)
    for h in range(nh):
        m, acc = attend(q_ref[0, h], h, diag0, half, *carries[h], True)
        m_lo, acc_lo = attend(q_ref[0, h, half:, :], h, diag0 + half, half, m[half:], acc[half:], True)
        cols = slice(h * dv, (h + 1) * dv)
        o_ref[0, :half, cols] = (acc[:half, :dv] / acc[:half, dv:dv + 1]).astype(o_ref.dtype)
        o_ref[0, half:, cols] = (acc_lo[:, :dv] / acc_lo[:, dv:dv + 1]).astype(o_ref.dtype)


def _flash(qc, kc, vc):
    b, hd, s, dk = qc.shape
    dv = C_V_DIM
    bq = FLASH_BQ
    nh = FLASH_HEADS
    return pl.pallas_call(
        _flash_kernel,
        grid=(b, hd // nh, s // bq),
        in_specs=[pl.BlockSpec((1, nh, bq, dk), lambda bi, h, i: (bi, h, i, 0)),
                  pl.BlockSpec((1, nh, s, dk), lambda bi, h, i: (bi, h, 0, 0), pipeline_mode=pl.Buffered(1)),
                  pl.BlockSpec((1, nh, s, vc.shape[3]), lambda bi, h, i: (bi, h, 0, 0),
                               pipeline_mode=pl.Buffered(1))],
        out_specs=pl.BlockSpec((1, bq, nh * dv), lambda bi, h, i: (bi, i, h)),
        out_shape=jax.ShapeDtypeStruct((b, s, hd * dv), BF16),
        compiler_params=_cparams(("parallel", "parallel", "arbitrary")),
        name="flash",
    )(qc, kc, vc)


def _layer_norm(z, g, b):
    mu = jnp.mean(z, axis=-1, keepdims=True)
    cen = z - mu
    var = jnp.mean(cen * cen, axis=-1, keepdims=True)
    return cen * lax.rsqrt(var + LN_EPS) * g + b


def _outproj_kernel(ya_ref, yb_ref, yc_ref, x_ref, w_ref, g_ref, b_ref, wr_ref, br_ref,
                    x1_ref, x1rc_ref, e_ref, gate_ref, rank_ref, cnt_ref, cnt_sc):
    i = pl.program_id(0)
    tm = x_ref.shape[0]
    wa, wb = ya_ref.shape[1], yb_ref.shape[1]

    @pl.when(i == 0)
    def _init():
        cnt_sc[...] = jnp.zeros(cnt_sc.shape, F32)

    mix = jnp.dot(ya_ref[...], w_ref[0:wa, :], preferred_element_type=F32)
    mix = mix + jnp.dot(yb_ref[...], w_ref[wa:wa + wb, :], preferred_element_type=F32)
    mix = mix + jnp.dot(yc_ref[...], w_ref[wa + wb:, :], preferred_element_type=F32)
    x1 = _layer_norm(DN_ALPHA * x_ref[...] + mix, g_ref[...], b_ref[...])
    x1_ref[...] = x1
    _store_rc(x1rc_ref, x1)

    logits = jnp.dot(x1.astype(BF16), wr_ref[...], preferred_element_type=F32) + br_ref[...]
    lane = lax.broadcasted_iota(jnp.int32, logits.shape, 1)
    logits = jnp.where(lane < N_EXPERTS, logits, NEG_INF)
    e_out = jnp.zeros(logits.shape, jnp.int32)
    p_out = jnp.zeros(logits.shape, F32)
    top = None
    den = None
    onehots = []
    for r in range(TOP_K):
        mx = jnp.max(logits, axis=-1, keepdims=True)
        idx = jnp.min(jnp.where(logits == mx, lane, LANES), axis=-1, keepdims=True)
        if r == 0:
            top = mx
        p = jnp.exp(mx - top)
        den = p if r == 0 else den + p
        sel = lane == idx
        onehots.append(jnp.where(sel, 1.0, 0.0).astype(F32))
        e_out = jnp.where(lane == r, idx, e_out)
        p_out = jnp.where(lane == r, p, p_out)
        logits = jnp.where(sel, NEG_INF, logits)
    e_ref[...] = e_out
    gate_ref[...] = p_out / den

    oh_sum = onehots[0] + onehots[1] + onehots[2] + onehots[3]
    row = lax.broadcasted_iota(jnp.int32, (tm, tm), 0)
    col = lax.broadcasted_iota(jnp.int32, (tm, tm), 1)
    before = jnp.where(row > col, 1.0, 0.0).astype(BF16)
    base = jnp.dot(before, oh_sum.astype(BF16), preferred_element_type=F32) + cnt_sc[0:1, :]
    rank_out = jnp.zeros(logits.shape, jnp.int32)
    for r in range(TOP_K):
        rk = jnp.sum(onehots[r] * base, axis=-1, keepdims=True)
        rank_out = jnp.where(lane == r, rk.astype(jnp.int32), rank_out)
    rank_ref[...] = rank_out
    total = cnt_sc[0:1, :] + jnp.sum(oh_sum, axis=0, keepdims=True)
    cnt_sc[0:1, :] = total
    cnt_ref[...] = jnp.broadcast_to(total, cnt_ref.shape).astype(jnp.int32)


def _outproj(ya, yb, yc, x2d, w_bf, g, b, wr128, br128):
    t, d = x2d.shape
    tm = OUT_TM
    rows = lambda a: pl.BlockSpec((tm, a.shape[1]), lambda i: (i, 0))
    full = lambda a: pl.BlockSpec(a.shape, lambda i: (0,) * a.ndim, pipeline_mode=pl.Buffered(1))
    g2, b2 = g.reshape(1, d), b.reshape(1, d)
    lane_blk = pl.BlockSpec((tm, LANES), lambda i: (i, 0))
    return pl.pallas_call(
        _outproj_kernel,
        grid=(t // tm,),
        in_specs=[rows(ya), rows(yb), rows(yc), rows(x2d), full(w_bf), full(g2), full(b2), full(wr128), full(br128)],
        out_specs=[pl.BlockSpec((tm, d), lambda i: (i, 0)),
                   pl.BlockSpec((tm * ROW_TILES, LANES), lambda i: (i, 0)),
                   lane_blk, lane_blk, lane_blk,
                   pl.BlockSpec((SUBLANES, LANES), lambda i: (0, 0))],
        out_shape=[jax.ShapeDtypeStruct((t, d), F32),
                   jax.ShapeDtypeStruct((t * ROW_TILES, LANES), U32),
                   jax.ShapeDtypeStruct((t, LANES), jnp.int32),
                   jax.ShapeDtypeStruct((t, LANES), F32),
                   jax.ShapeDtypeStruct((t, LANES), jnp.int32),
                   jax.ShapeDtypeStruct((SUBLANES, LANES), jnp.int32)],
        scratch_shapes=[pltpu.VMEM((SUBLANES, LANES), F32)],
        compiler_params=_cparams(("arbitrary",)),
        name="outproj",
    )(ya, yb, yc, x2d, w_bf, g2, b2, wr128, br128)


def _rc_rows(row, n=1):
    return pl.ds(pl.multiple_of(row * ROW_TILES, ROW_TILES), n * ROW_TILES)


def _dispatch_kernel(pos_ref, cnt_ref, pst_ref, pad_ref, nblk_ref, x_hbm, xs_hbm, buf, zbuf, sem_in, sem_out, sem_z):
    i = pl.program_id(0)
    n = pl.num_programs(0)
    tm = DSP_TM
    nslot = buf.shape[0]

    def in_copy(blk, slot):
        return pltpu.make_async_copy(x_hbm.at[_rc_rows(blk * tm, tm)], buf.at[slot], sem_in.at[slot])

    def row_out(slot, t, p):
        return pltpu.make_async_copy(buf.at[slot, _rc_rows(t)], xs_hbm.at[_rc_rows(p)], sem_out.at[slot])

    def wait_outs(slot):
        for _ in range(TOP_K):
            pltpu.make_async_copy(buf.at[slot], xs_hbm.at[_rc_rows(0, tm)], sem_out.at[slot]).wait()

    slot = lax.rem(i, nslot)

    @pl.when(i == 0)
    def _first_in():
        in_copy(0, 0).start()

    @pl.when(i >= 2)
    def _free_slot():
        wait_outs(lax.rem(i + 1, nslot))

    @pl.when(i + 1 < n)
    def _next_in():
        in_copy(i + 1, lax.rem(i + 1, nslot)).start()

    in_copy(i, slot).wait()
    base = i * (tm * TOP_K)

    def issue(t, _):
        for k in range(TOP_K):
            row_out(slot, t, pos_ref[base + t * TOP_K + k]).start(priority=k % 2)
        return 0

    lax.fori_loop(0, tm, issue, 0, unroll=2)

    @pl.when(i == n - 1)
    def _finish():
        wait_outs(lax.rem(i + 2, nslot))
        wait_outs(slot)
        zbuf[...] = jnp.zeros(zbuf.shape, zbuf.dtype)

        def pad_rows(e, _):
            cnt = cnt_ref[e]
            first = pst_ref[e] + cnt
            npad = pad_ref[e] - cnt

            def zero_row(q):
                return pltpu.make_async_copy(zbuf.at[_rc_rows(0)], xs_hbm.at[_rc_rows(first + q)], sem_z)

            def start(q, c):
                zero_row(q).start()
                return c

            def wait(q, c):
                zero_row(q).wait()
                return c

            lax.fori_loop(0, npad, start, 0)
            lax.fori_loop(0, npad, wait, 0)
            return 0

        lax.fori_loop(0, N_EXPERTS, pad_rows, 0)

        def tail_block(bk, _):
            cp = pltpu.make_async_copy(zbuf, xs_hbm.at[_rc_rows(bk * MOE_PAD, MOE_PAD)], sem_z)
            cp.start()
            cp.wait()
            return 0

        lax.fori_loop(nblk_ref[0], xs_hbm.shape[0] // (MOE_PAD * ROW_TILES), tail_block, 0)


def _dispatch(pos, counts, pad_start, padded, nblk, x1rc, rows):
    t = x1rc.shape[0] // ROW_TILES
    tm = DSP_TM
    return pl.pallas_call(
        _dispatch_kernel,
        grid_spec=pltpu.PrefetchScalarGridSpec(
            num_scalar_prefetch=5,
            grid=(t // tm,),
            in_specs=[pl.BlockSpec(memory_space=pl.ANY)],
            out_specs=pl.BlockSpec(memory_space=pl.ANY),
            scratch_shapes=[pltpu.VMEM((3, tm * ROW_TILES, LANES), U32),
                            pltpu.VMEM((MOE_PAD * ROW_TILES, LANES), U32),
                            pltpu.SemaphoreType.DMA((3,)),
                            pltpu.SemaphoreType.DMA((3,)),
                            pltpu.SemaphoreType.DMA(())]),
        out_shape=jax.ShapeDtypeStruct((rows * ROW_TILES, LANES), U32),
        compiler_params=_cparams(("arbitrary",)),
        name="dispatch",
    )(pos, counts, pad_start, padded, nblk, x1rc)


def _moe_kernel(layer, sbe_ref, sbr_ref, sbn_ref, nblk_ref, xs_hbm, wgu_hbm, wdn_hbm, bgu_ref, bd_ref, y_hbm,
                xbuf, acc, stage_in, stage_out, wg_f, wu_f, wd_f, wg_bf, wu_bf, wd_bf, sem_in, sem_out, sem_w):
    sb = pl.program_id(0)
    nsb = pl.num_programs(0)
    tf = wg_f.shape[2]
    dff = wdn_hbm.shape[2]
    nj = dff // tf
    n = sbn_ref[sb]
    row0 = sbr_ref[sb]
    nchunk = n // MOE_PAD
    cur = lax.rem(sb, 2)
    big, huge = 2 * MOE_PAD, 4 * MOE_PAD

    def weight_copies(sb_idx, j, slot):
        e = sbe_ref[sb_idx]
        cols = pl.ds(pl.multiple_of(j * tf, tf), tf)
        up_cols = pl.ds(pl.multiple_of(dff + j * tf, tf), tf)
        copies = [pltpu.make_async_copy(wdn_hbm.at[layer, e, cols, :], wd_f.at[slot], sem_w.at[slot])]
        part = wg_f.shape[1] // MOE_W_PARTS
        for p in range(MOE_W_PARTS):
            band = pl.ds(p * part, part)
            copies.append(pltpu.make_async_copy(wgu_hbm.at[layer, e, band, cols], wg_f.at[slot, band], sem_w.at[slot]))
            copies.append(pltpu.make_async_copy(wgu_hbm.at[layer, e, band, up_cols], wu_f.at[slot, band],
                                                sem_w.at[slot]))
        return copies

    def chunk_rows(c):
        return pl.ds(pl.multiple_of(c * MOE_PAD, MOE_PAD), MOE_PAD)

    def chunk_in(first_row, c):
        return pltpu.make_async_copy(xs_hbm.at[_rc_rows(first_row + c * MOE_PAD, MOE_PAD)], stage_in, sem_in)

    def chunk_out(c, slot):
        return pltpu.make_async_copy(stage_out.at[slot], y_hbm.at[_rc_rows(row0 + c * MOE_PAD, MOE_PAD)],
                                     sem_out.at[slot])

    def wait_outs(count):
        @pl.when(count >= 2)
        def _older():
            chunk_out(0, lax.rem(count, 2)).wait()

        @pl.when(count >= 1)
        def _newest():
            chunk_out(0, lax.rem(count + 1, 2)).wait()

    def convert_in(slot, c):
        lo, hi = _load_rc(stage_in, MOE_PAD)
        xbuf[slot, chunk_rows(c), :] = jnp.concatenate([lo.astype(BF16), hi.astype(BF16)], axis=-1)

    @pl.when(sb == 0)
    def _load_first():
        for cp in weight_copies(0, 0, 0):
            cp.start()

        def load(c, _):
            chunk_in(row0, c).start()
            chunk_in(row0, c).wait()
            convert_in(0, c)
            return 0

        lax.fori_loop(0, nchunk, load, 0)

    n_next = sbn_ref[sb + 1]

    def ffn(j, start, size, wg, wu, wd):
        rs = pl.ds(start, size)
        xr = xbuf[cur, rs, :]
        cols = pl.ds(pl.multiple_of(j * tf, tf), tf)
        up_cols = pl.ds(pl.multiple_of(dff + j * tf, tf), tf)
        gt = jnp.dot(xr, wg, preferred_element_type=F32) + bgu_ref[:, cols]
        up = jnp.dot(xr, wu, preferred_element_type=F32) + bgu_ref[:, up_cols]
        gt = jnp.minimum(gt, SWIGLU_LIMIT)
        up = jnp.clip(up, -SWIGLU_LIMIT, SWIGLU_LIMIT)
        act = ((up + 1.0) * gt * jax.nn.sigmoid(SWIGLU_ALPHA * gt)).astype(BF16)
        half = acc.shape[1] // 2
        for cs in (slice(0, half), slice(half, 2 * half)):
            acc[rs, cs] = acc[rs, cs] + jnp.dot(act, wd[:, cs], preferred_element_type=F32)

    def first_chunk(j, wslot, size):
        wg = wg_f[wslot].astype(BF16)
        wu = wu_f[wslot].astype(BF16)
        wd = wd_f[wslot].astype(BF16)
        wg_bf[...] = wg
        wu_bf[...] = wu
        wd_bf[...] = wd
        ffn(j, 0, size, wg, wu, wd)

    def hidden_tile(j, _):
        wslot = lax.rem(j, 2)
        for cp in weight_copies(sb, j, wslot):
            cp.wait()

        @pl.when(j + 1 < nj)
        def _next_tile():
            for cp in weight_copies(sb, j + 1, 1 - wslot):
                cp.start()

        @pl.when((j + 1 == nj) & (n_next > 0))
        def _next_super_block():
            for cp in weight_copies(sb + 1, 0, 0):
                cp.start()

        fetch_next = j * MOE_PAD < n_next

        @pl.when(fetch_next)
        def _start_next():
            chunk_in(sbr_ref[sb + 1], j).start()

        @pl.when((j == 1) & (sb > 0))
        def _drain_prev_outs():
            wait_outs(sbn_ref[jnp.maximum(sb - 1, 0)] // MOE_PAD)

        @pl.when(j == 0)
        def _init_acc():
            def init(c, _):
                acc[chunk_rows(c), :] = jnp.broadcast_to(bd_ref[...], (MOE_PAD, acc.shape[1]))
                return 0

            lax.fori_loop(0, nchunk, init, 0)

        def rest(start, size):
            ffn(j, start, size, wg_bf[...], wu_bf[...], wd_bf[...])

        @pl.when(n >= huge)
        def _rows_huge():
            first_chunk(j, wslot, huge)

            @pl.when(n - huge >= big)
            def _then_big():
                rest(huge, big)

            @pl.when(lax.rem(n, big) != 0)
            def _then_pad():
                rest(pl.multiple_of(n - MOE_PAD, MOE_PAD), MOE_PAD)

        @pl.when((n >= big) & (n < huge))
        def _rows_big():
            first_chunk(j, wslot, big)

            @pl.when(n > big)
            def _then_pad():
                rest(big, MOE_PAD)

        @pl.when(n < big)
        def _rows_small():
            first_chunk(j, wslot, MOE_PAD)

        @pl.when(j == nj - 1)
        def _store_rows():
            def store(c, _):
                slot = lax.rem(c, 2)

                @pl.when(c >= 2)
                def _slot_free():
                    chunk_out(c - 2, slot).wait()

                _store_rc(stage_out.at[slot], acc[chunk_rows(c), :])
                chunk_out(c, slot).start()
                return 0

            lax.fori_loop(0, nchunk, store, 0)

            @pl.when((sb == nsb - 1) | (n_next == 0))
            def _no_later_tile():
                wait_outs(nchunk)

        @pl.when(fetch_next)
        def _finish_next():
            chunk_in(sbr_ref[sb + 1], j).wait()
            convert_in(1 - cur, j)

        return 0

    @pl.when(n > 0)
    def _super_block():
        lax.fori_loop(0, nj, hidden_tile, 0)

    @pl.when(sb == nsb - 1)
    def _zero_tail():
        stage_out[0] = jnp.zeros(stage_out.shape[1:], stage_out.dtype)

        def tail_block(bk, _):
            cp = pltpu.make_async_copy(stage_out.at[0], y_hbm.at[_rc_rows(bk * MOE_PAD, MOE_PAD)], sem_out.at[0])
            cp.start()
            cp.wait()
            return 0

        lax.fori_loop(nblk_ref[0], y_hbm.shape[0] // (MOE_PAD * ROW_TILES), tail_block, 0)


def _moe(layer, sb_e, sb_row0, sb_n, nblk, xs, w_gate_up, b_gate_up, w_down, b_down):
    rows = xs.shape[0] // ROW_TILES
    d = D_MODEL
    dff = w_down.shape[2]
    tf = MOE_TF
    nj = dff // tf
    nsb = sb_e.shape[0] - 1
    assert nj % 2 == 0 and MOE_TMAX // MOE_PAD <= nj and MOE_TMAX == 6 * MOE_PAD
    bgu = b_gate_up.reshape(DEPTH, N_EXPERTS, 1, 2 * dff)
    bdn = b_down.reshape(DEPTH, N_EXPERTS, 1, d)
    expert_row = lambda width: pl.BlockSpec((None, None, 1, width), lambda sb, e, r, n, nb: (layer, e[sb], 0, 0))
    return pl.pallas_call(
        functools.partial(_moe_kernel, layer),
        grid_spec=pltpu.PrefetchScalarGridSpec(
            num_scalar_prefetch=4,
            grid=(nsb,),
            in_specs=[pl.BlockSpec(memory_space=pl.ANY),
                      pl.BlockSpec(memory_space=pl.ANY),
                      pl.BlockSpec(memory_space=pl.ANY),
                      expert_row(2 * dff),
                      expert_row(d)],
            out_specs=pl.BlockSpec(memory_space=pl.ANY),
            scratch_shapes=[pltpu.VMEM((2, MOE_TMAX, d), BF16),
                            pltpu.VMEM((MOE_TMAX, d), F32),
                            pltpu.VMEM((MOE_PAD * ROW_TILES, LANES), U32),
                            pltpu.VMEM((2, MOE_PAD * ROW_TILES, LANES), U32),
                            pltpu.VMEM((2, d, tf), F32),
                            pltpu.VMEM((2, d, tf), F32),
                            pltpu.VMEM((2, tf, d), F32),
                            pltpu.VMEM((d, tf), BF16),
                            pltpu.VMEM((d, tf), BF16),
                            pltpu.VMEM((tf, d), BF16),
                            pltpu.SemaphoreType.DMA(()),
                            pltpu.SemaphoreType.DMA((2,)),
                            pltpu.SemaphoreType.DMA((2,))]),
        out_shape=jax.ShapeDtypeStruct((rows * ROW_TILES, LANES), U32),
        compiler_params=_cparams(("arbitrary",)),
        name="moe",
    )(sb_e, sb_row0, sb_n, nblk, xs, w_gate_up, w_down, bgu, bdn)


def _combine_kernel(pos_ref, y_hbm, gate_ref, x1_ref, g_ref, b_ref, o_ref, buf, sem):
    i = pl.program_id(0)
    n = pl.num_programs(0)
    tm = CMB_TM
    slot = lax.rem(i, 2)

    def issue(blk, dst_slot):
        base = blk * (tm * TOP_K)

        def body(t, _):
            for k in range(TOP_K):
                p = pos_ref[base + t * TOP_K + k]
                pltpu.make_async_copy(y_hbm.at[_rc_rows(p)], buf.at[dst_slot, k, _rc_rows(t)],
                                      sem.at[dst_slot]).start(priority=k % 2)
            return 0

        lax.fori_loop(0, tm, body, 0, unroll=2)

    @pl.when(i == 0)
    def _first():
        issue(0, 0)

    @pl.when(i + 1 < n)
    def _next():
        issue(i + 1, 1 - slot)

    for k in range(TOP_K):
        pltpu.make_async_copy(y_hbm.at[_rc_rows(0, tm)], buf.at[slot, k], sem.at[slot]).wait()
    gate = gate_ref[...]
    ffn_lo, ffn_hi = None, None
    for k in range(TOP_K):
        lo, hi = _load_rc(buf.at[slot, k], tm)
        gk = gate[:, k:k + 1]
        ffn_lo = gk * lo if k == 0 else ffn_lo + gk * lo
        ffn_hi = gk * hi if k == 0 else ffn_hi + gk * hi
    ffn = jnp.concatenate([ffn_lo, ffn_hi], axis=-1)
    o_ref[...] = _layer_norm(DN_ALPHA * x1_ref[...] + ffn, g_ref[...], b_ref[...])


def _combine(pos_flat, y, gate, x1, g, b):
    t, d = x1.shape
    tm = CMB_TM
    g2, b2 = g.reshape(1, d), b.reshape(1, d)
    return pl.pallas_call(
        _combine_kernel,
        grid_spec=pltpu.PrefetchScalarGridSpec(
            num_scalar_prefetch=1,
            grid=(t // tm,),
            in_specs=[pl.BlockSpec(memory_space=pl.ANY),
                      pl.BlockSpec((tm, LANES), lambda i, p: (i, 0)),
                      pl.BlockSpec((tm, d), lambda i, p: (i, 0)),
                      pl.BlockSpec((1, d), lambda i, p: (0, 0)),
                      pl.BlockSpec((1, d), lambda i, p: (0, 0))],
            out_specs=pl.BlockSpec((tm, d), lambda i, p: (i, 0)),
            scratch_shapes=[pltpu.VMEM((2, TOP_K, tm * ROW_TILES, LANES), U32), pltpu.SemaphoreType.DMA((2,))]),
        out_shape=jax.ShapeDtypeStruct((t, d), F32),
        compiler_params=_cparams(("arbitrary",)),
        name="combine",
    )(pos_flat, y, gate, x1, g2, b2)


def _swap_halves(w):
    half = w.shape[-1] // 2
    return jnp.concatenate([w[..., half:], w[..., :half]], axis=-1)


def _w_in_columns(w):
    d = w.shape[0]
    widths = (512, 128, 128, 256, 256, 512, 512, 8, 512, 256, 64)
    offs = [0]
    for wd in widths:
        offs.append(offs[-1] + wd)
    a_q, a_k, a_v, m_q, m_k, m_v, m_o, m_if, c_q, c_kv, c_kr = [w[:, offs[i]:offs[i + 1]] for i in range(len(widths))]
    z = lambda n: jnp.zeros((d, n), w.dtype)
    return jnp.concatenate([a_q, m_v, m_o, c_q, m_q, m_k, c_kv, a_k, a_v,
                            c_kr, z(LANES - 64), m_if, z(LANES - 8), _swap_halves(c_kr), z(LANES - 64)], axis=1)


def _w_in_layout_kernel(w_ref, o_ref):
    o_ref[...] = _w_in_columns(w_ref[...]).astype(o_ref.dtype)


def _layout_w_in(w_in, layer):
    _, d, n = w_in.shape
    tk = 256
    return pl.pallas_call(
        _w_in_layout_kernel,
        grid=(d // tk,),
        in_specs=[pl.BlockSpec((None, tk, n), lambda i: (layer, i, 0))],
        out_specs=pl.BlockSpec((tk, N_PROJ), lambda i: (i, 0)),
        out_shape=jax.ShapeDtypeStruct((d, N_PROJ), BF16),
        compiler_params=_cparams(("parallel",)),
        name="w_in_layout",
    )(w_in)


def _layout_w_uq(w):
    r = w.shape[0]
    w3 = w.astype(BF16).reshape(r, C_HEADS, C_NOPE_DIM + C_ROPE_DIM)
    nope, rope = w3[..., :C_NOPE_DIM], w3[..., C_NOPE_DIM:]
    z = jnp.zeros((r, C_HEADS, LANES - C_ROPE_DIM), BF16)
    wa = jnp.concatenate([nope, rope, z], axis=-1).reshape(r, -1)
    wb = jnp.concatenate([_swap_halves(rope), z], axis=-1).reshape(r, -1)
    return wa, wb


def _layout_w_ukv(w):
    r = w.shape[0]
    w3 = w.astype(BF16).reshape(r, C_HEADS, C_NOPE_DIM + C_V_DIM)
    return jnp.concatenate([w3[..., :C_NOPE_DIM].reshape(r, -1), w3[..., C_NOPE_DIM:].reshape(r, -1)], axis=-1)


def _rope_tables(seq):
    dim = C_ROPE_DIM
    inv = 1.0 / (ROPE_THETA ** (jnp.arange(0, dim, 2, dtype=F32) / dim))
    ang = jnp.arange(seq, dtype=F32)[:, None] * inv[None, :]
    cos, sin = jnp.cos(ang), jnp.sin(ang)
    z = jnp.zeros((seq, LANES - dim), F32)
    return jnp.concatenate([cos, cos, z], axis=-1), jnp.concatenate([-sin, sin, z], axis=-1)


def _route(top_e, rank, counts, nsb):
    padded = (counts + MOE_PAD - 1) // MOE_PAD * MOE_PAD
    pad_end = jnp.cumsum(padded)
    pad_start = pad_end - padded
    is_e = top_e[:, :, None] == jnp.arange(N_EXPERTS, dtype=jnp.int32)[None, None, :]
    pos = (jnp.sum(jnp.where(is_e, pad_start[None, None, :], 0), axis=-1) + rank).reshape(-1).astype(jnp.int32)
    nblk = (pad_end[-1:] // MOE_PAD).astype(jnp.int32)
    nsb_e = (padded + MOE_TMAX - 1) // MOE_TMAX
    sb_end = jnp.cumsum(nsb_e)
    sb_start = sb_end - nsb_e
    sb_idx = jnp.arange(nsb + 1, dtype=jnp.int32)
    n_valid = sb_end[-1]
    sb_eff = jnp.minimum(sb_idx, n_valid - 1)
    sb_e = jnp.searchsorted(sb_end, sb_eff, side="right").astype(jnp.int32)
    part = sb_eff - sb_start[sb_e]
    sb_row0 = pad_start[sb_e] + part * MOE_TMAX
    sb_n = jnp.where(sb_idx < n_valid, jnp.minimum(padded[sb_e] - part * MOE_TMAX, MOE_TMAX), 0)
    tables = (counts, pad_start.astype(jnp.int32), padded.astype(jnp.int32), nblk)
    return pos, tables, sb_e, sb_row0.astype(jnp.int32), sb_n.astype(jnp.int32)


def kernel(x, w_in, conv_w, conv_b, m_gate_b, m_norm_g, sinks, q_norm_g, w_uq, kv_norm_g, w_ukv, w_out,
           ln1_g, ln1_b, w_router, b_router, w_gate_up, b_gate_up, w_down, b_down, ln2_g, ln2_b):
    b, s, d = x.shape
    t = b * s
    n_assign = t * TOP_K
    rows = (n_assign // MOE_PAD + N_EXPERTS) * MOE_PAD
    nsb = N_EXPERTS + rows // MOE_TMAX
    cos128, sin128 = _rope_tables(s)
    xt = x.reshape(t, d)
    for l in range(DEPTH):
        w_in_bf = _layout_w_in(w_in, l)
        wqa, wqb = _layout_w_uq(w_uq[l])
        wkv = _layout_w_ukv(w_ukv[l])
        w_out_bf = w_out[l].astype(BF16)
        wr128 = jnp.zeros((d, LANES), BF16).at[:, :N_EXPERTS].set(w_router[l].astype(BF16))
        br128 = jnp.zeros((1, LANES), F32).at[0, :N_EXPERTS].set(b_router[l])

        proj, gates = _proj(xt, w_in_bf)
        proj3 = proj.reshape(b, s, N_PROJ)
        y_a = _swa(proj3, sinks[l])
        y_b = _mlstm(proj3, gates.reshape(b, s, LANES), conv_w[l], conv_b[l], m_gate_b[l], m_norm_g[l])
        qc, kc, vc = _mla_prep(proj3, q_norm_g[l], kv_norm_g[l], wqa, wqb, wkv, cos128, sin128)
        y_c = _flash(qc, kc, vc)
        x1, x1rc, top_e, gate, rank, cnt = _outproj(y_a.reshape(t, -1), y_b.reshape(t, -1), y_c.reshape(t, -1), xt,
                                                    w_out_bf, ln1_g[l], ln1_b[l], wr128, br128)
        pos, tables, sb_e, sb_row0, sb_n = _route(top_e[:, :TOP_K], rank[:, :TOP_K], cnt[0, :N_EXPERTS], nsb)
        xs = _dispatch(pos, *tables, x1rc, rows)
        y = _moe(l, sb_e, sb_row0, sb_n, tables[3], xs, w_gate_up, b_gate_up, w_down, b_down)
        xt = _combine(pos, y, gate, x1, ln2_g[l], ln2_b[l])
    return xt.reshape(b, s, d)
```

```python
import functools

import jax
import jax.numpy as jnp
from jax import lax
from jax.experimental import pallas as pl
from jax.experimental.pallas import tpu as pltpu

F32 = jnp.float32
BF16 = jnp.bfloat16
NEG_INF = float("-inf")
LOG2_E = 1.4426950408889634

D_MODEL = 2048
DEPTH = 2
SWA_HEADS, SWA_KV_HEADS, SWA_HEAD_DIM, SWA_WINDOW = 8, 2, 64, 128
M_HEADS, M_QK_DIM, M_V_DIM, M_CONV = 4, 64, 128, 4
C_HEADS, C_NOPE_DIM, C_ROPE_DIM, C_V_DIM = 8, 128, 64, 128
C_Q_LORA, C_KV_LORA = 512, 256
ROPE_THETA = 10000.0
N_EXPERTS, TOP_K = 32, 4
SWIGLU_LIMIT, SWIGLU_ALPHA = 7.0, 1.702
DN_ALPHA = (2 * DEPTH) ** 0.25
LN_EPS, RMS_EPS = 1e-5, 1e-6

LANES = 128
SUBLANES = 8
VMEM_LIMIT = 56 * 1024 * 1024

P_AQ, P_MV, P_MO, P_CQ, P_MQK, P_CKV, P_AK, P_AV, P_CKR, P_MIF, P_CKRS = (
    0, 512, 1024, 1536, 2048, 2560, 2816, 2944, 3072, 3200, 3328)
N_PROJ = 3456

PROJ_TM, PROJ_TN = 1024, 1152
M_CHUNK = 256
MLA_TM = 512
FLASH_BQ = 1024
FLASH_HEADS = 4
OUT_TM = 512
MOE_PAD = 256
MOE_TMAX = 1536
MOE_TF = 256
MOE_W_PARTS = 4
CMB_TM = 128
DSP_TM = 128

ROW_TILES = D_MODEL // 2 // LANES
U32 = jnp.uint32
HI_MASK = 0xFFFF0000


def _cparams(sem, vmem=VMEM_LIMIT):
    return pltpu.CompilerParams(dimension_semantics=sem, vmem_limit_bytes=vmem)


def _bf16_bits(x):
    return lax.bitcast_convert_type(x.astype(BF16).astype(F32), U32)


def _store_rc(ref, val):
    n, d = val.shape
    words = (_bf16_bits(val[:, :d // 2]) >> 16) | (_bf16_bits(val[:, d // 2:]) & U32(HI_MASK))
    x = jnp.stack([words[:, s * LANES:(s + 1) * LANES] for s in range(ROW_TILES)], axis=0)
    ref[...] = pltpu.einshape("stl->tsl", x).reshape(n * ROW_TILES, LANES)


def _load_rc(ref, n):
    x = pltpu.einshape("tsl->stl", ref[...].reshape(n, ROW_TILES, LANES))
    words = jnp.concatenate([x[s] for s in range(ROW_TILES)], axis=-1)
    lo = lax.bitcast_convert_type(words << 16, F32)
    hi = lax.bitcast_convert_type(words & U32(HI_MASK), F32)
    return lo, hi


def _proj_kernel(x_ref, w_ref, o_ref, g_ref):
    j = pl.program_id(1)
    acc = jnp.dot(x_ref[...].astype(BF16), w_ref[...], preferred_element_type=F32)
    o_ref[...] = acc.astype(o_ref.dtype)

    @pl.when(j == P_MIF // PROJ_TN)
    def _gates():
        off = P_MIF % PROJ_TN
        g_ref[...] = acc[:, off:off + LANES]


def _proj(x2d, w_bf):
    t, d = x2d.shape
    n = w_bf.shape[1]
    return pl.pallas_call(
        _proj_kernel,
        grid=(t // PROJ_TM, n // PROJ_TN),
        in_specs=[pl.BlockSpec((PROJ_TM, d), lambda i, j: (i, 0)),
                  pl.BlockSpec((d, PROJ_TN), lambda i, j: (0, j))],
        out_specs=[pl.BlockSpec((PROJ_TM, PROJ_TN), lambda i, j: (i, j)),
                   pl.BlockSpec((PROJ_TM, LANES), lambda i, j: (i, 0))],
        out_shape=[jax.ShapeDtypeStruct((t, n), BF16), jax.ShapeDtypeStruct((t, LANES), F32)],
        compiler_params=_cparams(("parallel", "arbitrary")),
        name="proj",
    )(x2d, w_bf)


def _swa_kernel(sinks_ref, q_ref, kc_ref, kp_ref, vc_ref, vp_ref, o_ref):
    i = pl.program_id(0)
    w = SWA_WINDOW
    dh = SWA_HEAD_DIM
    grp = SWA_HEADS // SWA_KV_HEADS
    qi = lax.broadcasted_iota(jnp.int32, (w, 2 * w), 0)
    kj = lax.broadcasted_iota(jnp.int32, (w, 2 * w), 1)
    rel = qi + w - kj
    has_prev = jnp.where(i > 0, 0, w)
    mask = (rel >= 0) & (rel < w) & (kj >= has_prev)
    for b in range(q_ref.shape[0]):
        q = q_ref[b]
        outs = []
        for kvh in range(SWA_KV_HEADS):
            sl = slice(kvh * dh, (kvh + 1) * dh)
            k_cat = jnp.concatenate([kp_ref[b][:, sl], kc_ref[b][:, sl]], axis=0).astype(BF16)
            v_cat = jnp.concatenate([vp_ref[b][:, sl], vc_ref[b][:, sl]], axis=0).astype(BF16)
            for g in range(grp):
                h = kvh * grp + g
                qh = (q[:, h * dh:(h + 1) * dh] * (dh ** -0.5)).astype(BF16)
                s = lax.dot_general(qh, k_cat, (((1,), (1,)), ((), ())), preferred_element_type=F32)
                s = jnp.where(mask, s, NEG_INF)
                sink = sinks_ref[h]
                m = jnp.maximum(jnp.max(s, axis=-1, keepdims=True), sink)
                p = jnp.exp(s - m)
                den = jnp.sum(p, axis=-1, keepdims=True) + jnp.exp(sink - m)
                o = jnp.dot(p.astype(BF16), v_cat, preferred_element_type=F32)
                outs.append(o / den)
        o_ref[b] = jnp.concatenate(outs, axis=-1).astype(o_ref.dtype)


def _swa(proj3, sinks):
    b, s, _ = proj3.shape
    w = SWA_WINDOW
    nb = s // w
    kw = SWA_KV_HEADS * SWA_HEAD_DIM
    qw = SWA_HEADS * SWA_HEAD_DIM
    cur = lambda col: (lambda i: (0, i, col))
    prev = lambda col: (lambda i: (0, jnp.maximum(i - 1, 0), col))
    return pl.pallas_call(
        _swa_kernel,
        grid=(nb,),
        in_specs=[pl.BlockSpec(memory_space=pltpu.SMEM),
                  pl.BlockSpec((b, w, qw), cur(P_AQ // qw)),
                  pl.BlockSpec((b, w, kw), cur(P_AK // kw)),
                  pl.BlockSpec((b, w, kw), prev(P_AK // kw)),
                  pl.BlockSpec((b, w, kw), cur(P_AV // kw)),
                  pl.BlockSpec((b, w, kw), prev(P_AV // kw))],
        out_specs=pl.BlockSpec((b, w, qw), lambda i: (0, i, 0)),
        out_shape=jax.ShapeDtypeStruct((b, s, qw), BF16),
        compiler_params=_cparams(("arbitrary",)),
        name="swa",
    )(sinks, proj3, proj3, proj3, proj3, proj3)


def _mlstm_kernel(qk_ref, v_ref, og_ref, g_ref, cw_ref, cb_ref, gb_ref, ng_ref, y_ref, xbuf, ct_ref, m_ref):
    c = pl.program_id(0)
    nb = qk_ref.shape[0]
    ln = M_CHUNK
    dqk, dv, nh = M_QK_DIM, M_V_DIM, M_HEADS
    halo = SUBLANES

    @pl.when(c == 0)
    def _init():
        xbuf[:, 0:halo, :] = jnp.zeros((nb, halo, xbuf.shape[2]), F32)
        ct_ref[...] = jnp.zeros(ct_ref.shape, F32)
        m_ref[...] = jnp.zeros(m_ref.shape, F32)

    row = lax.broadcasted_iota(jnp.int32, (ln, ln), 0)
    col = lax.broadcasted_iota(jnp.int32, (ln, ln), 1)
    tril = row >= col
    trilf = jnp.where(tril, 1.0, 0.0).astype(F32)
    ones_blk = jnp.where(lax.broadcasted_iota(jnp.int32, (ln, dv), 1) == 0, 1.0, 0.0).astype(F32)

    for b in range(nb):
        xbuf[b, halo:halo + ln, :] = qk_ref[b].astype(F32)
        conv = cb_ref[...]
        for j in range(M_CONV):
            off = halo - (M_CONV - 1) + j
            conv = conv + cw_ref[j:j + 1, :] * xbuf[b, off:off + ln, :]
        qk = conv * jax.nn.sigmoid(conv)
        xbuf[b, 0:halo, :] = xbuf[b, ln:ln + halo, :]

        g = g_ref[b] + gb_ref[...]
        lf = jnp.minimum(g, 0.0) - jnp.log1p(jnp.exp(-jnp.abs(g)))
        cum = jnp.dot(trilf, lf, preferred_element_type=F32, precision=lax.Precision.HIGHEST)
        cum_t = cum.T
        g_t = g.T
        vv = v_ref[b].astype(F32)
        hs = []
        for h in range(nh):
            chain = b * nh + h
            q = (qk[:, h * dqk:(h + 1) * dqk] * (dqk ** -0.5)).astype(BF16)
            k = qk[:, nh * dqk + h * dqk: nh * dqk + (h + 1) * dqk].astype(BF16)
            v_ext = jnp.concatenate([vv[:, h * dv:(h + 1) * dv], ones_blk], axis=-1)
            bc_col = cum[:, nh + h:nh + h + 1]
            bc_row = cum_t[nh + h:nh + h + 1, :]
            i_col = g[:, h:h + 1]
            i_row = g_t[h:h + 1, :]
            m_prev = m_ref[chain:chain + 1, 0:1]
            dmat = jnp.where(tril, bc_col - bc_row + i_row, NEG_INF)
            m_inter = bc_col + m_prev
            m_j = jnp.maximum(m_inter, jnp.max(dmat, axis=-1, keepdims=True))
            w_intra = jnp.exp(dmat - m_j)
            w_inter = jnp.exp(m_inter - m_j)
            s = lax.dot_general(q, k, (((1,), (1,)), ((), ())), preferred_element_type=F32) * w_intra
            ct = ct_ref[chain]
            num_ext = (jnp.dot(s.astype(BF16), v_ext.astype(BF16), preferred_element_type=F32)
                       + w_inter * jnp.dot(q, ct.astype(BF16), preferred_element_type=F32))
            num = num_ext[:, :dv]
            nq = num_ext[:, dv:dv + 1]
            den = jnp.maximum(jnp.abs(nq), jnp.exp(-m_j))
            hs.append(num / den)
            m_new = m_j[ln - 1:ln, :]
            bc_last = bc_col[ln - 1:ln, :]
            w_s = jnp.exp(bc_last - bc_col + i_col - m_new)
            w_c = jnp.exp(bc_last + m_prev - m_new)
            upd = lax.dot_general(k, (w_s * v_ext).astype(BF16), (((0,), (0,)), ((), ())),
                                  preferred_element_type=F32)
            ct_ref[chain] = w_c * ct + upd
            m_ref[chain:chain + 1, :] = jnp.broadcast_to(m_new, (1, m_ref.shape[1]))
        og = og_ref[b].astype(F32)
        outs = []
        for h in range(nh):
            seg = jax.nn.sigmoid(og[:, h * dv:(h + 1) * dv]) * hs[h]
            mu = jnp.mean(seg, axis=-1, keepdims=True)
            cen = seg - mu
            var = jnp.mean(cen * cen, axis=-1, keepdims=True)
            outs.append(cen * lax.rsqrt(var + LN_EPS) * ng_ref[:, h * dv:(h + 1) * dv])
        y_ref[b] = jnp.concatenate(outs, axis=-1).astype(y_ref.dtype)


def _mlstm(proj3, gates3, conv_w, conv_b, gate_b, norm_g):
    b, s, _ = proj3.shape
    ln = M_CHUNK
    wq = 2 * M_HEADS * M_QK_DIM
    wv = M_HEADS * M_V_DIM
    blk = lambda width, off: pl.BlockSpec((b, ln, width), lambda c: (0, c, off // width))
    full = lambda a: pl.BlockSpec(a.shape, lambda c: (0,) * a.ndim)
    gate_b128 = jnp.zeros((1, LANES), F32).at[0, :2 * M_HEADS].set(gate_b)
    conv_b2 = conv_b.reshape(1, wq)
    norm_g2 = norm_g.reshape(1, wv)
    return pl.pallas_call(
        _mlstm_kernel,
        grid=(s // ln,),
        in_specs=[blk(wq, P_MQK), blk(wv, P_MV), blk(wv, P_MO), blk(LANES, 0),
                  full(conv_w), full(conv_b2), full(gate_b128), full(norm_g2)],
        out_specs=pl.BlockSpec((b, ln, wv), lambda c: (0, c, 0)),
        out_shape=jax.ShapeDtypeStruct((b, s, wv), BF16),
        scratch_shapes=[pltpu.VMEM((b, SUBLANES + ln, wq), F32),
                        pltpu.VMEM((b * M_HEADS, M_QK_DIM, 2 * M_V_DIM), F32),
                        pltpu.VMEM((b * M_HEADS, LANES), F32)],
        compiler_params=_cparams(("arbitrary",)),
        name="mlstm",
    )(proj3, proj3, proj3, gates3, conv_w, conv_b2, gate_b128, norm_g2)


def _mla_prep_kernel(cq_ref, ckv_ref, kr_ref, krs_ref, qg_ref, kvg_ref, wqa_ref, wqb_ref, wkv_ref,
                     cos_ref, sin_ref, q_out, k_out, v_out):
    nd, hd = C_NOPE_DIM, C_HEADS
    scale = (C_NOPE_DIM + C_ROPE_DIM) ** -0.5 * LOG2_E
    cos = cos_ref[...]
    sin = sin_ref[...]

    cq = cq_ref[0].astype(F32)
    qn = (cq * lax.rsqrt(jnp.mean(cq * cq, axis=-1, keepdims=True) + RMS_EPS) * qg_ref[...]).astype(BF16)
    qa = jnp.dot(qn, wqa_ref[...], preferred_element_type=F32)
    qb = jnp.dot(qn, wqb_ref[...], preferred_element_type=F32)
    for h in range(hd):
        nope = qa[:, h * 2 * nd: h * 2 * nd + nd]
        rope = qa[:, h * 2 * nd + nd:(h + 1) * 2 * nd] * cos + qb[:, h * nd:(h + 1) * nd] * sin
        q_out[0, h] = (jnp.concatenate([nope, rope], axis=-1) * scale).astype(q_out.dtype)

    ckv = ckv_ref[0].astype(F32)
    kvn = (ckv * lax.rsqrt(jnp.mean(ckv * ckv, axis=-1, keepdims=True) + RMS_EPS) * kvg_ref[...]).astype(BF16)
    kv = jnp.dot(kvn, wkv_ref[...], preferred_element_type=F32)
    kr = kr_ref[0].astype(F32) * cos + krs_ref[0].astype(F32) * sin
    ones_blk = jnp.where(lax.broadcasted_iota(jnp.int32, kr.shape, 1) == 0, 1.0, 0.0).astype(F32)
    for h in range(hd):
        k_out[0, h] = jnp.concatenate([kv[:, h * nd:(h + 1) * nd], kr], axis=-1).astype(k_out.dtype)
        v_h = kv[:, hd * nd + h * nd: hd * nd + (h + 1) * nd]
        v_out[0, h] = jnp.concatenate([v_h, ones_blk], axis=-1).astype(v_out.dtype)


def _mla_prep(proj3, q_norm_g, kv_norm_g, wqa, wqb, wkv, cos128, sin128):
    b, s, _ = proj3.shape
    tm = MLA_TM
    hd, nd = C_HEADS, C_NOPE_DIM
    blk = lambda width, off: pl.BlockSpec((1, tm, width), lambda bi, i: (bi, i, off // width))
    full = lambda a: pl.BlockSpec(a.shape, lambda bi, i: (0,) * a.ndim)
    tab = pl.BlockSpec((tm, LANES), lambda bi, i: (i, 0))
    qg = q_norm_g.reshape(1, -1)
    kvg = kv_norm_g.reshape(1, -1)
    head_out = lambda width: pl.BlockSpec((1, hd, tm, width), lambda bi, i: (bi, 0, i, 0))
    return pl.pallas_call(
        _mla_prep_kernel,
        grid=(b, s // tm),
        in_specs=[blk(C_Q_LORA, P_CQ), blk(C_KV_LORA, P_CKV), blk(LANES, P_CKR), blk(LANES, P_CKRS),
                  full(qg), full(kvg), full(wqa), full(wqb), full(wkv), tab, tab],
        out_specs=[head_out(2 * nd), head_out(2 * nd), head_out(C_V_DIM + LANES)],
        out_shape=[jax.ShapeDtypeStruct((b, hd, s, 2 * nd), BF16),
                   jax.ShapeDtypeStruct((b, hd, s, 2 * nd), BF16),
                   jax.ShapeDtypeStruct((b, hd, s, C_V_DIM + LANES), BF16)],
        compiler_params=_cparams(("parallel", "arbitrary")),
        name="mla_prep",
    )(proj3, proj3, proj3, proj3, qg, kvg, wqa, wqb, wkv, cos128, sin128)


def _flash_kernel(q_ref, k_ref, v_ref, o_ref):
    qi = pl.program_id(2)
    bq = FLASH_BQ
    nh = q_ref.shape[1]
    dv = C_V_DIM

    def attend(q, h, key0, nkeys, m, acc, causal):
        ks = k_ref[0, h, pl.ds(key0, nkeys), :]
        vs = v_ref[0, h, pl.ds(key0, nkeys), :]
        s = lax.dot_general(q, ks, (((1,), (1,)), ((), ())), preferred_element_type=F32)
        if causal:
            row = lax.broadcasted_iota(jnp.int32, s.shape, 0)
            col = lax.broadcasted_iota(jnp.int32, s.shape, 1)
            s = jnp.where(row >= col, s, NEG_INF)
        m_new = jnp.maximum(m, jnp.max(s, axis=-1, keepdims=True))
        p = jnp.exp2(s - m_new)
        acc = jnp.exp2(m - m_new) * acc + jnp.dot(p.astype(BF16), vs, preferred_element_type=F32)
        return m_new, acc

    def full_step(j, carries):
        start = pl.multiple_of(j * bq, bq)
        return tuple(attend(q_ref[0, h], h, start, bq, *carries[h], False) for h in range(nh))

    init = tuple((jnp.full((bq, 1), NEG_INF, F32), jnp.zeros((bq, v_ref.shape[3]), F32)) for _ in range(nh))
    carries = lax.fori_loop(0, qi, full_step, init)

    half = bq // 2
    diag0 = pl.multiple_of(qi * bq, bq)
    for h in range(nh):
        m, acc = attend(q_ref[0, h], h, diag0, half, *carries[h], True)
        m_lo, acc_lo = attend(q_ref[0, h, half:, :], h, diag0 + half, half, m[half:], acc[half:], True)
        cols = slice(h * dv, (h + 1) * dv)
        o_ref[0, :half, cols] = (acc[:half, :dv] / acc[:half, dv:dv + 1]).astype(o_ref.dtype)
        o_ref[0, half:, cols] = (acc_lo[:, :dv] / acc_lo[:, dv:dv + 1]).astype(o_ref.dtype)


def _flash(qc, kc, vc):
    b, hd, s, dk = qc.shape
    dv = C_V_DIM
    bq = FLASH_BQ
    nh = FLASH_HEADS
    return pl.pallas_call(
        _flash_kernel,
        grid=(b, hd // nh, s // bq),
        in_specs=[pl.BlockSpec((1, nh, bq, dk), lambda bi, h, i: (bi, h, i, 0)),
                  pl.BlockSpec((1, nh, s, dk), lambda bi, h, i: (bi, h, 0, 0), pipeline_mode=pl.Buffered(1)),
                  pl.BlockSpec((1, nh, s, vc.shape[3]), lambda bi, h, i: (bi, h, 0, 0),
                               pipeline_mode=pl.Buffered(1))],
        out_specs=pl.BlockSpec((1, bq, nh * dv), lambda bi, h, i: (bi, i, h)),
        out_shape=jax.ShapeDtypeStruct((b, s, hd * dv), BF16),
        compiler_params=_cparams(("parallel", "parallel", "arbitrary")),
        name="flash",
    )(qc, kc, vc)


def _layer_norm(z, g, b):
    mu = jnp.mean(z, axis=-1, keepdims=True)
    cen = z - mu
    var = jnp.mean(cen * cen, axis=-1, keepdims=True)
    return cen * lax.rsqrt(var + LN_EPS) * g + b


def _outproj_kernel(ya_ref, yb_ref, yc_ref, x_ref, w_ref, g_ref, b_ref, wr_ref, br_ref,
                    x1_ref, x1rc_ref, e_ref, gate_ref, rank_ref, cnt_ref, cnt_sc):
    i = pl.program_id(0)
    tm = x_ref.shape[0]
    wa, wb = ya_ref.shape[1], yb_ref.shape[1]

    @pl.when(i == 0)
    def _init():
        cnt_sc[...] = jnp.zeros(cnt_sc.shape, F32)

    mix = jnp.dot(ya_ref[...], w_ref[0:wa, :], preferred_element_type=F32)
    mix = mix + jnp.dot(yb_ref[...], w_ref[wa:wa + wb, :], preferred_element_type=F32)
    mix = mix + jnp.dot(yc_ref[...], w_ref[wa + wb:, :], preferred_element_type=F32)
    x1 = _layer_norm(DN_ALPHA * x_ref[...] + mix, g_ref[...], b_ref[...])
    x1_ref[...] = x1
    _store_rc(x1rc_ref, x1)

    logits = jnp.dot(x1.astype(BF16), wr_ref[...], preferred_element_type=F32) + br_ref[...]
    lane = lax.broadcasted_iota(jnp.int32, logits.shape, 1)
    logits = jnp.where(lane < N_EXPERTS, logits, NEG_INF)
    e_out = jnp.zeros(logits.shape, jnp.int32)
    p_out = jnp.zeros(logits.shape, F32)
    top = None
    den = None
    onehots = []
    for r in range(TOP_K):
        mx = jnp.max(logits, axis=-1, keepdims=True)
        idx = jnp.min(jnp.where(logits == mx, lane, LANES), axis=-1, keepdims=True)
        if r == 0:
            top = mx
        p = jnp.exp(mx - top)
        den = p if r == 0 else den + p
        sel = lane == idx
        onehots.append(jnp.where(sel, 1.0, 0.0).astype(F32))
        e_out = jnp.where(lane == r, idx, e_out)
        p_out = jnp.where(lane == r, p, p_out)
        logits = jnp.where(sel, NEG_INF, logits)
    e_ref[...] = e_out
    gate_ref[...] = p_out / den

    oh_sum = onehots[0] + onehots[1] + onehots[2] + onehots[3]
    row = lax.broadcasted_iota(jnp.int32, (tm, tm), 0)
    col = lax.broadcasted_iota(jnp.int32, (tm, tm), 1)
    before = jnp.where(row > col, 1.0, 0.0).astype(BF16)
    base = jnp.dot(before, oh_sum.astype(BF16), preferred_element_type=F32) + cnt_sc[0:1, :]
    rank_out = jnp.zeros(logits.shape, jnp.int32)
    for r in range(TOP_K):
        rk = jnp.sum(onehots[r] * base, axis=-1, keepdims=True)
        rank_out = jnp.where(lane == r, rk.astype(jnp.int32), rank_out)
    rank_ref[...] = rank_out
    total = cnt_sc[0:1, :] + jnp.sum(oh_sum, axis=0, keepdims=True)
    cnt_sc[0:1, :] = total
    cnt_ref[...] = jnp.broadcast_to(total, cnt_ref.shape).astype(jnp.int32)


def _outproj(ya, yb, yc, x2d, w_bf, g, b, wr128, br128):
    t, d = x2d.shape
    tm = OUT_TM
    rows = lambda a: pl.BlockSpec((tm, a.shape[1]), lambda i: (i, 0))
    full = lambda a: pl.BlockSpec(a.shape, lambda i: (0,) * a.ndim, pipeline_mode=pl.Buffered(1))
    g2, b2 = g.reshape(1, d), b.reshape(1, d)
    lane_blk = pl.BlockSpec((tm, LANES), lambda i: (i, 0))
    return pl.pallas_call(
        _outproj_kernel,
        grid=(t // tm,),
        in_specs=[rows(ya), rows(yb), rows(yc), rows(x2d), full(w_bf), full(g2), full(b2), full(wr128), full(br128)],
        out_specs=[pl.BlockSpec((tm, d), lambda i: (i, 0)),
                   pl.BlockSpec((tm * ROW_TILES, LANES), lambda i: (i, 0)),
                   lane_blk, lane_blk, lane_blk,
                   pl.BlockSpec((SUBLANES, LANES), lambda i: (0, 0))],
        out_shape=[jax.ShapeDtypeStruct((t, d), F32),
                   jax.ShapeDtypeStruct((t * ROW_TILES, LANES), U32),
                   jax.ShapeDtypeStruct((t, LANES), jnp.int32),
                   jax.ShapeDtypeStruct((t, LANES), F32),
                   jax.ShapeDtypeStruct((t, LANES), jnp.int32),
                   jax.ShapeDtypeStruct((SUBLANES, LANES), jnp.int32)],
        scratch_shapes=[pltpu.VMEM((SUBLANES, LANES), F32)],
        compiler_params=_cparams(("arbitrary",)),
        name="outproj",
    )(ya, yb, yc, x2d, w_bf, g2, b2, wr128, br128)


def _rc_rows(row, n=1):
    return pl.ds(pl.multiple_of(row * ROW_TILES, ROW_TILES), n * ROW_TILES)


def _dispatch_kernel(pos_ref, cnt_ref, pst_ref, pad_ref, nblk_ref, x_hbm, xs_hbm, buf, zbuf, sem_in, sem_out, sem_z):
    i = pl.program_id(0)
    n = pl.num_programs(0)
    tm = DSP_TM
    nslot = buf.shape[0]

    def in_copy(blk, slot):
        return pltpu.make_async_copy(x_hbm.at[_rc_rows(blk * tm, tm)], buf.at[slot], sem_in.at[slot])

    def row_out(slot, t, p):
        return pltpu.make_async_copy(buf.at[slot, _rc_rows(t)], xs_hbm.at[_rc_rows(p)], sem_out.at[slot])

    def wait_outs(slot):
        for _ in range(TOP_K):
            pltpu.make_async_copy(buf.at[slot], xs_hbm.at[_rc_rows(0, tm)], sem_out.at[slot]).wait()

    slot = lax.rem(i, nslot)

    @pl.when(i == 0)
    def _first_in():
        in_copy(0, 0).start()

    @pl.when(i >= 2)
    def _free_slot():
        wait_outs(lax.rem(i + 1, nslot))

    @pl.when(i + 1 < n)
    def _next_in():
        in_copy(i + 1, lax.rem(i + 1, nslot)).start()

    in_copy(i, slot).wait()
    base = i * (tm * TOP_K)

    def issue(t, _):
        for k in range(TOP_K):
            row_out(slot, t, pos_ref[base + t * TOP_K + k]).start(priority=k % 2)
        return 0

    lax.fori_loop(0, tm, issue, 0, unroll=2)

    @pl.when(i == n - 1)
    def _finish():
        wait_outs(lax.rem(i + 2, nslot))
        wait_outs(slot)
        zbuf[...] = jnp.zeros(zbuf.shape, zbuf.dtype)

        def pad_rows(e, _):
            cnt = cnt_ref[e]
            first = pst_ref[e] + cnt
            npad = pad_ref[e] - cnt

            def zero_row(q):
                return pltpu.make_async_copy(zbuf.at[_rc_rows(0)], xs_hbm.at[_rc_rows(first + q)], sem_z)

            def start(q, c):
                zero_row(q).start()
                return c

            def wait(q, c):
                zero_row(q).wait()
                return c

            lax.fori_loop(0, npad, start, 0)
            lax.fori_loop(0, npad, wait, 0)
            return 0

        lax.fori_loop(0, N_EXPERTS, pad_rows, 0)

        def tail_block(bk, _):
            cp = pltpu.make_async_copy(zbuf, xs_hbm.at[_rc_rows(bk * MOE_PAD, MOE_PAD)], sem_z)
            cp.start()
            cp.wait()
            return 0

        lax.fori_loop(nblk_ref[0], xs_hbm.shape[0] // (MOE_PAD * ROW_TILES), tail_block, 0)


def _dispatch(pos, counts, pad_start, padded, nblk, x1rc, rows):
    t = x1rc.shape[0] // ROW_TILES
    tm = DSP_TM
    return pl.pallas_call(
        _dispatch_kernel,
        grid_spec=pltpu.PrefetchScalarGridSpec(
            num_scalar_prefetch=5,
            grid=(t // tm,),
            in_specs=[pl.BlockSpec(memory_space=pl.ANY)],
            out_specs=pl.BlockSpec(memory_space=pl.ANY),
            scratch_shapes=[pltpu.VMEM((3, tm * ROW_TILES, LANES), U32),
                            pltpu.VMEM((MOE_PAD * ROW_TILES, LANES), U32),
                            pltpu.SemaphoreType.DMA((3,)),
                            pltpu.SemaphoreType.DMA((3,)),
                            pltpu.SemaphoreType.DMA(())]),
        out_shape=jax.ShapeDtypeStruct((rows * ROW_TILES, LANES), U32),
        compiler_params=_cparams(("arbitrary",)),
        name="dispatch",
    )(pos, counts, pad_start, padded, nblk, x1rc)


def _moe_kernel(layer, sbe_ref, sbr_ref, sbn_ref, nblk_ref, xs_hbm, wgu_hbm, wdn_hbm, bgu_ref, bd_ref, y_hbm,
                xbuf, acc, stage_in, stage_out, wg_f, wu_f, wd_f, wg_bf, wu_bf, wd_bf, sem_in, sem_out, sem_w):
    sb = pl.program_id(0)
    nsb = pl.num_programs(0)
    tf = wg_f.shape[2]
    dff = wdn_hbm.shape[2]
    nj = dff // tf
    n = sbn_ref[sb]
    row0 = sbr_ref[sb]
    nchunk = n // MOE_PAD
    cur = lax.rem(sb, 2)
    big, huge = 2 * MOE_PAD, 4 * MOE_PAD

    def weight_copies(sb_idx, j, slot):
        e = sbe_ref[sb_idx]
        cols = pl.ds(pl.multiple_of(j * tf, tf), tf)
        up_cols = pl.ds(pl.multiple_of(dff + j * tf, tf), tf)
        copies = [pltpu.make_async_copy(wdn_hbm.at[layer, e, cols, :], wd_f.at[slot], sem_w.at[slot])]
        part = wg_f.shape[1] // MOE_W_PARTS
        for p in range(MOE_W_PARTS):
            band = pl.ds(p * part, part)
            copies.append(pltpu.make_async_copy(wgu_hbm.at[layer, e, band, cols], wg_f.at[slot, band], sem_w.at[slot]))
            copies.append(pltpu.make_async_copy(wgu_hbm.at[layer, e, band, up_cols], wu_f.at[slot, band],
                                                sem_w.at[slot]))
        return copies

    def chunk_rows(c):
        return pl.ds(pl.multiple_of(c * MOE_PAD, MOE_PAD), MOE_PAD)

    def chunk_in(first_row, c):
        return pltpu.make_async_copy(xs_hbm.at[_rc_rows(first_row + c * MOE_PAD, MOE_PAD)], stage_in, sem_in)

    def chunk_out(c, slot):
        return pltpu.make_async_copy(stage_out.at[slot], y_hbm.at[_rc_rows(row0 + c * MOE_PAD, MOE_PAD)],
                                     sem_out.at[slot])

    def wait_outs(count):
        @pl.when(count >= 2)
        def _older():
            chunk_out(0, lax.rem(count, 2)).wait()

        @pl.when(count >= 1)
        def _newest():
            chunk_out(0, lax.rem(count + 1, 2)).wait()

    def convert_in(slot, c):
        lo, hi = _load_rc(stage_in, MOE_PAD)
        xbuf[slot, chunk_rows(c), :] = jnp.concatenate([lo.astype(BF16), hi.astype(BF16)], axis=-1)

    @pl.when(sb == 0)
    def _load_first():
        for cp in weight_copies(0, 0, 0):
            cp.start()

        def load(c, _):
            chunk_in(row0, c).start()
            chunk_in(row0, c).wait()
            convert_in(0, c)
            return 0

        lax.fori_loop(0, nchunk, load, 0)

    n_next = sbn_ref[sb + 1]

    def ffn(j, start, size, wg, wu, wd):
        rs = pl.ds(start, size)
        xr = xbuf[cur, rs, :]
        cols = pl.ds(pl.multiple_of(j * tf, tf), tf)
        up_cols = pl.ds(pl.multiple_of(dff + j * tf, tf), tf)
        gt = jnp.dot(xr, wg, preferred_element_type=F32) + bgu_ref[:, cols]
        up = jnp.dot(xr, wu, preferred_element_type=F32) + bgu_ref[:, up_cols]
        gt = jnp.minimum(gt, SWIGLU_LIMIT)
        up = jnp.clip(up, -SWIGLU_LIMIT, SWIGLU_LIMIT)
        act = ((up + 1.0) * gt * jax.nn.sigmoid(SWIGLU_ALPHA * gt)).astype(BF16)
        half = acc.shape[1] // 2
        for cs in (slice(0, half), slice(half, 2 * half)):
            acc[rs, cs] = acc[rs, cs] + jnp.dot(act, wd[:, cs], preferred_element_type=F32)

    def first_chunk(j, wslot, size):
        wg = wg_f[wslot].astype(BF16)
        wu = wu_f[wslot].astype(BF16)
        wd = wd_f[wslot].astype(BF16)
        wg_bf[...] = wg
        wu_bf[...] = wu
        wd_bf[...] = wd
        ffn(j, 0, size, wg, wu, wd)

    def hidden_tile(j, _):
        wslot = lax.rem(j, 2)
        for cp in weight_copies(sb, j, wslot):
            cp.wait()

        @pl.when(j + 1 < nj)
        def _next_tile():
            for cp in weight_copies(sb, j + 1, 1 - wslot):
                cp.start()

        @pl.when((j + 1 == nj) & (n_next > 0))
        def _next_super_block():
            for cp in weight_copies(sb + 1, 0, 0):
                cp.start()

        fetch_next = j * MOE_PAD < n_next

        @pl.when(fetch_next)
        def _start_next():
            chunk_in(sbr_ref[sb + 1], j).start()

        @pl.when((j == 1) & (sb > 0))
        def _drain_prev_outs():
            wait_outs(sbn_ref[jnp.maximum(sb - 1, 0)] // MOE_PAD)

        @pl.when(j == 0)
        def _init_acc():
            def init(c, _):
                acc[chunk_rows(c), :] = jnp.broadcast_to(bd_ref[...], (MOE_PAD, acc.shape[1]))
                return 0

            lax.fori_loop(0, nchunk, init, 0)

        def rest(start, size):
            ffn(j, start, size, wg_bf[...], wu_bf[...], wd_bf[...])

        @pl.when(n >= huge)
        def _rows_huge():
            first_chunk(j, wslot, huge)

            @pl.when(n - huge >= big)
            def _then_big():
                rest(huge, big)

            @pl.when(lax.rem(n, big) != 0)
            def _then_pad():
                rest(pl.multiple_of(n - MOE_PAD, MOE_PAD), MOE_PAD)

        @pl.when((n >= big) & (n < huge))
        def _rows_big():
            first_chunk(j, wslot, big)

            @pl.when(n > big)
            def _then_pad():
                rest(big, MOE_PAD)

        @pl.when(n < big)
        def _rows_small():
            first_chunk(j, wslot, MOE_PAD)

        @pl.when(j == nj - 1)
        def _store_rows():
            def store(c, _):
                slot = lax.rem(c, 2)

                @pl.when(c >= 2)
                def _slot_free():
                    chunk_out(c - 2, slot).wait()

                _store_rc(stage_out.at[slot], acc[chunk_rows(c), :])
                chunk_out(c, slot).start()
                return 0

            lax.fori_loop(0, nchunk, store, 0)

            @pl.when((sb == nsb - 1) | (n_next == 0))
            def _no_later_tile():
                wait_outs(nchunk)

        @pl.when(fetch_next)
        def _finish_next():
            chunk_in(sbr_ref[sb + 1], j).wait()
            convert_in(1 - cur, j)

        return 0

    @pl.when(n > 0)
    def _super_block():
        lax.fori_loop(0, nj, hidden_tile, 0)

    @pl.when(sb == nsb - 1)
    def _zero_tail():
        stage_out[0] = jnp.zeros(stage_out.shape[1:], stage_out.dtype)

        def tail_block(bk, _):
            cp = pltpu.make_async_copy(stage_out.at[0], y_hbm.at[_rc_rows(bk * MOE_PAD, MOE_PAD)], sem_out.at[0])
            cp.start()
            cp.wait()
            return 0

        lax.fori_loop(nblk_ref[0], y_hbm.shape[0] // (MOE_PAD * ROW_TILES), tail_block, 0)


def _moe(layer, sb_e, sb_row0, sb_n, nblk, xs, w_gate_up, b_gate_up, w_down, b_down):
    rows = xs.shape[0] // ROW_TILES
    d = D_MODEL
    dff = w_down.shape[2]
    tf = MOE_TF
    nj = dff // tf
    nsb = sb_e.shape[0] - 1
    assert nj % 2 == 0 and MOE_TMAX // MOE_PAD <= nj and MOE_TMAX == 6 * MOE_PAD
    bgu = b_gate_up.reshape(DEPTH, N_EXPERTS, 1, 2 * dff)
    bdn = b_down.reshape(DEPTH, N_EXPERTS, 1, d)
    expert_row = lambda width: pl.BlockSpec((None, None, 1, width), lambda sb, e, r, n, nb: (layer, e[sb], 0, 0))
    return pl.pallas_call(
        functools.partial(_moe_kernel, layer),
        grid_spec=pltpu.PrefetchScalarGridSpec(
            num_scalar_prefetch=4,
            grid=(nsb,),
            in_specs=[pl.BlockSpec(memory_space=pl.ANY),
                      pl.BlockSpec(memory_space=pl.ANY),
                      pl.BlockSpec(memory_space=pl.ANY),
                      expert_row(2 * dff),
                      expert_row(d)],
            out_specs=pl.BlockSpec(memory_space=pl.ANY),
            scratch_shapes=[pltpu.VMEM((2, MOE_TMAX, d), BF16),
                            pltpu.VMEM((MOE_TMAX, d), F32),
                            pltpu.VMEM((MOE_PAD * ROW_TILES, LANES), U32),
                            pltpu.VMEM((2, MOE_PAD * ROW_TILES, LANES), U32),
                            pltpu.VMEM((2, d, tf), F32),
                            pltpu.VMEM((2, d, tf), F32),
                            pltpu.VMEM((2, tf, d), F32),
                            pltpu.VMEM((d, tf), BF16),
                            pltpu.VMEM((d, tf), BF16),
                            pltpu.VMEM((tf, d), BF16),
                            pltpu.SemaphoreType.DMA(()),
                            pltpu.SemaphoreType.DMA((2,)),
                            pltpu.SemaphoreType.DMA((2,))]),
        out_shape=jax.ShapeDtypeStruct((rows * ROW_TILES, LANES), U32),
        compiler_params=_cparams(("arbitrary",)),
        name="moe",
    )(sb_e, sb_row0, sb_n, nblk, xs, w_gate_up, w_down, bgu, bdn)


def _combine_kernel(pos_ref, y_hbm, gate_ref, x1_ref, g_ref, b_ref, o_ref, buf, sem):
    i = pl.program_id(0)
    n = pl.num_programs(0)
    tm = CMB_TM
    slot = lax.rem(i, 2)

    def issue(blk, dst_slot):
        base = blk * (tm * TOP_K)

        def body(t, _):
            for k in range(TOP_K):
                p = pos_ref[base + t * TOP_K + k]
                pltpu.make_async_copy(y_hbm.at[_rc_rows(p)], buf.at[dst_slot, k, _rc_rows(t)],
                                      sem.at[dst_slot]).start(priority=k % 2)
            return 0

        lax.fori_loop(0, tm, body, 0, unroll=2)

    @pl.when(i == 0)
    def _first():
        issue(0, 0)

    @pl.when(i + 1 < n)
    def _next():
        issue(i + 1, 1 - slot)

    for k in range(TOP_K):
        pltpu.make_async_copy(y_hbm.at[_rc_rows(0, tm)], buf.at[slot, k], sem.at[slot]).wait()
    gate = gate_ref[...]
    ffn_lo, ffn_hi = None, None
    for k in range(TOP_K):
        lo, hi = _load_rc(buf.at[slot, k], tm)
        gk = gate[:, k:k + 1]
        ffn_lo = gk * lo if k == 0 else ffn_lo + gk * lo
        ffn_hi = gk * hi if k == 0 else ffn_hi + gk * hi
    ffn = jnp.concatenate([ffn_lo, ffn_hi], axis=-1)
    o_ref[...] = _layer_norm(DN_ALPHA * x1_ref[...] + ffn, g_ref[...], b_ref[...])


def _combine(pos_flat, y, gate, x1, g, b):
    t, d = x1.shape
    tm = CMB_TM
    g2, b2 = g.reshape(1, d), b.reshape(1, d)
    return pl.pallas_call(
        _combine_kernel,
        grid_spec=pltpu.PrefetchScalarGridSpec(
            num_scalar_prefetch=1,
            grid=(t // tm,),
            in_specs=[pl.BlockSpec(memory_space=pl.ANY),
                      pl.BlockSpec((tm, LANES), lambda i, p: (i, 0)),
                      pl.BlockSpec((tm, d), lambda i, p: (i, 0)),
                      pl.BlockSpec((1, d), lambda i, p: (0, 0)),
                      pl.BlockSpec((1, d), lambda i, p: (0, 0))],
            out_specs=pl.BlockSpec((tm, d), lambda i, p: (i, 0)),
            scratch_shapes=[pltpu.VMEM((2, TOP_K, tm * ROW_TILES, LANES), U32), pltpu.SemaphoreType.DMA((2,))]),
        out_shape=jax.ShapeDtypeStruct((t, d), F32),
        compiler_params=_cparams(("arbitrary",)),
        name="combine",
    )(pos_flat, y, gate, x1, g2, b2)


def _swap_halves(w):
    half = w.shape[-1] // 2
    return jnp.concatenate([w[..., half:], w[..., :half]], axis=-1)


def _w_in_columns(w):
    d = w.shape[0]
    widths = (512, 128, 128, 256, 256, 512, 512, 8, 512, 256, 64)
    offs = [0]
    for wd in widths:
        offs.append(offs[-1] + wd)
    a_q, a_k, a_v, m_q, m_k, m_v, m_o, m_if, c_q, c_kv, c_kr = [w[:, offs[i]:offs[i + 1]] for i in range(len(widths))]
    z = lambda n: jnp.zeros((d, n), w.dtype)
    return jnp.concatenate([a_q, m_v, m_o, c_q, m_q, m_k, c_kv, a_k, a_v,
                            c_kr, z(LANES - 64), m_if, z(LANES - 8), _swap_halves(c_kr), z(LANES - 64)], axis=1)


def _w_in_layout_kernel(w_ref, o_ref):
    o_ref[...] = _w_in_columns(w_ref[...]).astype(o_ref.dtype)


def _layout_w_in(w_in, layer):
    _, d, n = w_in.shape
    tk = 256
    return pl.pallas_call(
        _w_in_layout_kernel,
        grid=(d // tk,),
        in_specs=[pl.BlockSpec((None, tk, n), lambda i: (layer, i, 0))],
        out_specs=pl.BlockSpec((tk, N_PROJ), lambda i: (i, 0)),
        out_shape=jax.ShapeDtypeStruct((d, N_PROJ), BF16),
        compiler_params=_cparams(("parallel",)),
        name="w_in_layout",
    )(w_in)


def _layout_w_uq(w):
    r = w.shape[0]
    w3 = w.astype(BF16).reshape(r, C_HEADS, C_NOPE_DIM + C_ROPE_DIM)
    nope, rope = w3[..., :C_NOPE_DIM], w3[..., C_NOPE_DIM:]
    z = jnp.zeros((r, C_HEADS, LANES - C_ROPE_DIM), BF16)
    wa = jnp.concatenate([nope, rope, z], axis=-1).reshape(r, -1)
    wb = jnp.concatenate([_swap_halves(rope), z], axis=-1).reshape(r, -1)
    return wa, wb


def _layout_w_ukv(w):
    r = w.shape[0]
    w3 = w.astype(BF16).reshape(r, C_HEADS, C_NOPE_DIM + C_V_DIM)
    return jnp.concatenate([w3[..., :C_NOPE_DIM].reshape(r, -1), w3[..., C_NOPE_DIM:].reshape(r, -1)], axis=-1)


def _rope_tables(seq):
    dim = C_ROPE_DIM
    inv = 1.0 / (ROPE_THETA ** (jnp.arange(0, dim, 2, dtype=F32) / dim))
    ang = jnp.arange(seq, dtype=F32)[:, None] * inv[None, :]
    cos, sin = jnp.cos(ang), jnp.sin(ang)
    z = jnp.zeros((seq, LANES - dim), F32)
    return jnp.concatenate([cos, cos, z], axis=-1), jnp.concatenate([-sin, sin, z], axis=-1)


def _route(top_e, rank, counts, nsb):
    padded = (counts + MOE_PAD - 1) // MOE_PAD * MOE_PAD
    pad_end = jnp.cumsum(padded)
    pad_start = pad_end - padded
    is_e = top_e[:, :, None] == jnp.arange(N_EXPERTS, dtype=jnp.int32)[None, None, :]
    pos = (jnp.sum(jnp.where(is_e, pad_start[None, None, :], 0), axis=-1) + rank).reshape(-1).astype(jnp.int32)
    nblk = (pad_end[-1:] // MOE_PAD).astype(jnp.int32)
    nsb_e = (padded + MOE_TMAX - 1) // MOE_TMAX
    sb_end = jnp.cumsum(nsb_e)
    sb_start = sb_end - nsb_e
    sb_idx = jnp.arange(nsb + 1, dtype=jnp.int32)
    n_valid = sb_end[-1]
    sb_eff = jnp.minimum(sb_idx, n_valid - 1)
    sb_e = jnp.searchsorted(sb_end, sb_eff, side="right").astype(jnp.int32)
    part = sb_eff - sb_start[sb_e]
    sb_row0 = pad_start[sb_e] + part * MOE_TMAX
    sb_n = jnp.where(sb_idx < n_valid, jnp.minimum(padded[sb_e] - part * MOE_TMAX, MOE_TMAX), 0)
    tables = (counts, pad_start.astype(jnp.int32), padded.astype(jnp.int32), nblk)
    return pos, tables, sb_e, sb_row0.astype(jnp.int32), sb_n.astype(jnp.int32)


def kernel(x, w_in, conv_w, conv_b, m_gate_b, m_norm_g, sinks, q_norm_g, w_uq, kv_norm_g, w_ukv, w_out,
           ln1_g, ln1_b, w_router, b_router, w_gate_up, b_gate_up, w_down, b_down, ln2_g, ln2_b):
    b, s, d = x.shape
    t = b * s
    n_assign = t * TOP_K
    rows = (n_assign // MOE_PAD + N_EXPERTS) * MOE_PAD
    nsb = N_EXPERTS + rows // MOE_TMAX
    cos128, sin128 = _rope_tables(s)
    xt = x.reshape(t, d)
    for l in range(DEPTH):
        w_in_bf = _layout_w_in(w_in, l)
        wqa, wqb = _layout_w_uq(w_uq[l])
        wkv = _layout_w_ukv(w_ukv[l])
        w_out_bf = w_out[l].astype(BF16)
        wr128 = jnp.zeros((d, LANES), BF16).at[:, :N_EXPERTS].set(w_router[l].astype(BF16))
        br128 = jnp.zeros((1, LANES), F32).at[0, :N_EXPERTS].set(b_router[l])

        proj, gates = _proj(xt, w_in_bf)
        proj3 = proj.reshape(b, s, N_PROJ)
        y_a = _swa(proj3, sinks[l])
        y_b = _mlstm(proj3, gates.reshape(b, s, LANES), conv_w[l], conv_b[l], m_gate_b[l], m_norm_g[l])
        qc, kc, vc = _mla_prep(proj3, q_norm_g[l], kv_norm_g[l], wqa, wqb, wkv, cos128, sin128)
        y_c = _flash(qc, kc, vc)
        x1, x1rc, top_e, gate, rank, cnt = _outproj(y_a.reshape(t, -1), y_b.reshape(t, -1), y_c.reshape(t, -1), xt,
                                                    w_out_bf, ln1_g[l], ln1_b[l], wr128, br128)
        pos, tables, sb_e, sb_row0, sb_n = _route(top_e[:, :TOP_K], rank[:, :TOP_K], cnt[0, :N_EXPERTS], nsb)
        xs = _dispatch(pos, *tables, x1rc, rows)
        y = _moe(l, sb_e, sb_row0, sb_n, tables[3], xs, w_gate_up, b_gate_up, w_down, b_down)
        xt = _combine(pos, y, gate, x1, ln2_g[l], ln2_b[l])
    return xt.reshape(b, s, d)
```

```python
import functools

import jax
import jax.numpy as jnp
from jax import lax
from jax.experimental import pallas as pl
from jax.experimental.pallas import tpu as pltpu

F32 = jnp.float32
BF16 = jnp.bfloat16
NEG_INF = float("-inf")
LOG2_E = 1.4426950408889634

D_MODEL = 2048
DEPTH = 2
SWA_HEADS, SWA_KV_HEADS, SWA_HEAD_DIM, SWA_WINDOW = 8, 2, 64, 128
M_HEADS, M_QK_DIM, M_V_DIM, M_CONV = 4, 64, 128, 4
C_HEADS, C_NOPE_DIM, C_ROPE_DIM, C_V_DIM = 8, 128, 64, 128
C_Q_LORA, C_KV_LORA = 512, 256
ROPE_THETA = 10000.0
N_EXPERTS, TOP_K = 32, 4
SWIGLU_LIMIT, SWIGLU_ALPHA = 7.0, 1.702
DN_ALPHA = (2 * DEPTH) ** 0.25
LN_EPS, RMS_EPS = 1e-5, 1e-6

LANES = 128
SUBLANES = 8
VMEM_LIMIT = 56 * 1024 * 1024

P_AQ, P_MV, P_MO, P_CQ, P_MQK, P_CKV, P_AK, P_AV, P_CKR, P_MIF, P_CKRS = (
    0, 512, 1024, 1536, 2048, 2560, 2816, 2944, 3072, 3200, 3328)
N_PROJ = 3456

PROJ_TM, PROJ_TN = 1024, 1152
M_CHUNK = 256
MLA_TM = 512
FLASH_BQ = 1024
FLASH_HEADS = 4
OUT_TM = 512
MOE_PAD = 256
MOE_TMAX = 1536
MOE_TF = 256
MOE_W_PARTS = 4
CMB_TM = 256
DSP_TM = 256

ROW_TILES = D_MODEL // 2 // LANES
U32 = jnp.uint32
HI_MASK = 0xFFFF0000


def _cparams(sem, vmem=VMEM_LIMIT):
    return pltpu.CompilerParams(dimension_semantics=sem, vmem_limit_bytes=vmem)


def _bf16_bits(x):
    return lax.bitcast_convert_type(x.astype(BF16).astype(F32), U32)


def _store_rc(ref, val):
    n, d = val.shape
    words = (_bf16_bits(val[:, :d // 2]) >> 16) | (_bf16_bits(val[:, d // 2:]) & U32(HI_MASK))
    x = jnp.stack([words[:, s * LANES:(s + 1) * LANES] for s in range(ROW_TILES)], axis=0)
    ref[...] = pltpu.einshape("stl->tsl", x).reshape(n * ROW_TILES, LANES)


def _load_rc(ref, n):
    x = pltpu.einshape("tsl->stl", ref[...].reshape(n, ROW_TILES, LANES))
    words = jnp.concatenate([x[s] for s in range(ROW_TILES)], axis=-1)
    lo = lax.bitcast_convert_type(words << 16, F32)
    hi = lax.bitcast_convert_type(words & U32(HI_MASK), F32)
    return lo, hi


def _proj_kernel(x_ref, w_ref, o_ref, g_ref):
    j = pl.program_id(1)
    acc = jnp.dot(x_ref[...].astype(BF16), w_ref[...], preferred_element_type=F32)
    o_ref[...] = acc.astype(o_ref.dtype)

    @pl.when(j == P_MIF // PROJ_TN)
    def _gates():
        off = P_MIF % PROJ_TN
        g_ref[...] = acc[:, off:off + LANES]


def _proj(x2d, w_bf):
    t, d = x2d.shape
    n = w_bf.shape[1]
    return pl.pallas_call(
        _proj_kernel,
        grid=(t // PROJ_TM, n // PROJ_TN),
        in_specs=[pl.BlockSpec((PROJ_TM, d), lambda i, j: (i, 0)),
                  pl.BlockSpec((d, PROJ_TN), lambda i, j: (0, j))],
        out_specs=[pl.BlockSpec((PROJ_TM, PROJ_TN), lambda i, j: (i, j)),
                   pl.BlockSpec((PROJ_TM, LANES), lambda i, j: (i, 0))],
        out_shape=[jax.ShapeDtypeStruct((t, n), BF16), jax.ShapeDtypeStruct((t, LANES), F32)],
        compiler_params=_cparams(("parallel", "arbitrary")),
        name="proj",
    )(x2d, w_bf)


def _swa_kernel(sinks_ref, q_ref, kc_ref, kp_ref, vc_ref, vp_ref, o_ref):
    i = pl.program_id(0)
    w = SWA_WINDOW
    dh = SWA_HEAD_DIM
    grp = SWA_HEADS // SWA_KV_HEADS
    qi = lax.broadcasted_iota(jnp.int32, (w, 2 * w), 0)
    kj = lax.broadcasted_iota(jnp.int32, (w, 2 * w), 1)
    rel = qi + w - kj
    has_prev = jnp.where(i > 0, 0, w)
    mask = (rel >= 0) & (rel < w) & (kj >= has_prev)
    for b in range(q_ref.shape[0]):
        q = q_ref[b]
        outs = []
        for kvh in range(SWA_KV_HEADS):
            sl = slice(kvh * dh, (kvh + 1) * dh)
            k_cat = jnp.concatenate([kp_ref[b][:, sl], kc_ref[b][:, sl]], axis=0).astype(BF16)
            v_cat = jnp.concatenate([vp_ref[b][:, sl], vc_ref[b][:, sl]], axis=0).astype(BF16)
            for g in range(grp):
                h = kvh * grp + g
                qh = (q[:, h * dh:(h + 1) * dh] * (dh ** -0.5)).astype(BF16)
                s = lax.dot_general(qh, k_cat, (((1,), (1,)), ((), ())), preferred_element_type=F32)
                s = jnp.where(mask, s, NEG_INF)
                sink = sinks_ref[h]
                m = jnp.maximum(jnp.max(s, axis=-1, keepdims=True), sink)
                p = jnp.exp(s - m)
                den = jnp.sum(p, axis=-1, keepdims=True) + jnp.exp(sink - m)
                o = jnp.dot(p.astype(BF16), v_cat, preferred_element_type=F32)
                outs.append(o / den)
        o_ref[b] = jnp.concatenate(outs, axis=-1).astype(o_ref.dtype)


def _swa(proj3, sinks):
    b, s, _ = proj3.shape
    w = SWA_WINDOW
    nb = s // w
    kw = SWA_KV_HEADS * SWA_HEAD_DIM
    qw = SWA_HEADS * SWA_HEAD_DIM
    cur = lambda col: (lambda i: (0, i, col))
    prev = lambda col: (lambda i: (0, jnp.maximum(i - 1, 0), col))
    return pl.pallas_call(
        _swa_kernel,
        grid=(nb,),
        in_specs=[pl.BlockSpec(memory_space=pltpu.SMEM),
                  pl.BlockSpec((b, w, qw), cur(P_AQ // qw)),
                  pl.BlockSpec((b, w, kw), cur(P_AK // kw)),
                  pl.BlockSpec((b, w, kw), prev(P_AK // kw)),
                  pl.BlockSpec((b, w, kw), cur(P_AV // kw)),
                  pl.BlockSpec((b, w, kw), prev(P_AV // kw))],
        out_specs=pl.BlockSpec((b, w, qw), lambda i: (0, i, 0)),
        out_shape=jax.ShapeDtypeStruct((b, s, qw), BF16),
        compiler_params=_cparams(("arbitrary",)),
        name="swa",
    )(sinks, proj3, proj3, proj3, proj3, proj3)


def _mlstm_kernel(qk_ref, v_ref, og_ref, g_ref, cw_ref, cb_ref, gb_ref, ng_ref, y_ref, xbuf, ct_ref, m_ref):
    c = pl.program_id(0)
    nb = qk_ref.shape[0]
    ln = M_CHUNK
    dqk, dv, nh = M_QK_DIM, M_V_DIM, M_HEADS
    halo = SUBLANES

    @pl.when(c == 0)
    def _init():
        xbuf[:, 0:halo, :] = jnp.zeros((nb, halo, xbuf.shape[2]), F32)
        ct_ref[...] = jnp.zeros(ct_ref.shape, F32)
        m_ref[...] = jnp.zeros(m_ref.shape, F32)

    row = lax.broadcasted_iota(jnp.int32, (ln, ln), 0)
    col = lax.broadcasted_iota(jnp.int32, (ln, ln), 1)
    tril = row >= col
    trilf = jnp.where(tril, 1.0, 0.0).astype(F32)
    ones_blk = jnp.where(lax.broadcasted_iota(jnp.int32, (ln, dv), 1) == 0, 1.0, 0.0).astype(F32)

    for b in range(nb):
        xbuf[b, halo:halo + ln, :] = qk_ref[b].astype(F32)
        conv = cb_ref[...]
        for j in range(M_CONV):
            off = halo - (M_CONV - 1) + j
            conv = conv + cw_ref[j:j + 1, :] * xbuf[b, off:off + ln, :]
        qk = conv * jax.nn.sigmoid(conv)
        xbuf[b, 0:halo, :] = xbuf[b, ln:ln + halo, :]

        g = g_ref[b] + gb_ref[...]
        lf = jnp.minimum(g, 0.0) - jnp.log1p(jnp.exp(-jnp.abs(g)))
        cum = jnp.dot(trilf, lf, preferred_element_type=F32, precision=lax.Precision.HIGHEST)
        cum_t = cum.T
        g_t = g.T
        vv = v_ref[b].astype(F32)
        hs = []
        for h in range(nh):
            chain = b * nh + h
            q = (qk[:, h * dqk:(h + 1) * dqk] * (dqk ** -0.5)).astype(BF16)
            k = qk[:, nh * dqk + h * dqk: nh * dqk + (h + 1) * dqk].astype(BF16)
            v_ext = jnp.concatenate([vv[:, h * dv:(h + 1) * dv], ones_blk], axis=-1)
            bc_col = cum[:, nh + h:nh + h + 1]
            bc_row = cum_t[nh + h:nh + h + 1, :]
            i_col = g[:, h:h + 1]
            i_row = g_t[h:h + 1, :]
            m_prev = m_ref[chain:chain + 1, 0:1]
            dmat = jnp.where(tril, bc_col - bc_row + i_row, NEG_INF)
            m_inter = bc_col + m_prev
            m_j = jnp.maximum(m_inter, jnp.max(dmat, axis=-1, keepdims=True))
            w_intra = jnp.exp(dmat - m_j)
            w_inter = jnp.exp(m_inter - m_j)
            s = lax.dot_general(q, k, (((1,), (1,)), ((), ())), preferred_element_type=F32) * w_intra
            ct = ct_ref[chain]
            num_ext = (jnp.dot(s.astype(BF16), v_ext.astype(BF16), preferred_element_type=F32)
                       + w_inter * jnp.dot(q, ct.astype(BF16), preferred_element_type=F32))
            num = num_ext[:, :dv]
            nq = num_ext[:, dv:dv + 1]
            den = jnp.maximum(jnp.abs(nq), jnp.exp(-m_j))
            hs.append(num / den)
            m_new = m_j[ln - 1:ln, :]
            bc_last = bc_col[ln - 1:ln, :]
            w_s = jnp.exp(bc_last - bc_col + i_col - m_new)
            w_c = jnp.exp(bc_last + m_prev - m_new)
            upd = lax.dot_general(k, (w_s * v_ext).astype(BF16), (((0,), (0,)), ((), ())),
                                  preferred_element_type=F32)
            ct_ref[chain] = w_c * ct + upd
            m_ref[chain:chain + 1, :] = jnp.broadcast_to(m_new, (1, m_ref.shape[1]))
        og = og_ref[b].astype(F32)
        outs = []
        for h in range(nh):
            seg = jax.nn.sigmoid(og[:, h * dv:(h + 1) * dv]) * hs[h]
            mu = jnp.mean(seg, axis=-1, keepdims=True)
            cen = seg - mu
            var = jnp.mean(cen * cen, axis=-1, keepdims=True)
            outs.append(cen * lax.rsqrt(var + LN_EPS) * ng_ref[:, h * dv:(h + 1) * dv])
        y_ref[b] = jnp.concatenate(outs, axis=-1).astype(y_ref.dtype)


def _mlstm(proj3, gates3, conv_w, conv_b, gate_b, norm_g):
    b, s, _ = proj3.shape
    ln = M_CHUNK
    wq = 2 * M_HEADS * M_QK_DIM
    wv = M_HEADS * M_V_DIM
    blk = lambda width, off: pl.BlockSpec((b, ln, width), lambda c: (0, c, off // width))
    full = lambda a: pl.BlockSpec(a.shape, lambda c: (0,) * a.ndim)
    gate_b128 = jnp.zeros((1, LANES), F32).at[0, :2 * M_HEADS].set(gate_b)
    conv_b2 = conv_b.reshape(1, wq)
    norm_g2 = norm_g.reshape(1, wv)
    return pl.pallas_call(
        _mlstm_kernel,
        grid=(s // ln,),
        in_specs=[blk(wq, P_MQK), blk(wv, P_MV), blk(wv, P_MO), blk(LANES, 0),
                  full(conv_w), full(conv_b2), full(gate_b128), full(norm_g2)],
        out_specs=pl.BlockSpec((b, ln, wv), lambda c: (0, c, 0)),
        out_shape=jax.ShapeDtypeStruct((b, s, wv), BF16),
        scratch_shapes=[pltpu.VMEM((b, SUBLANES + ln, wq), F32),
                        pltpu.VMEM((b * M_HEADS, M_QK_DIM, 2 * M_V_DIM), F32),
                        pltpu.VMEM((b * M_HEADS, LANES), F32)],
        compiler_params=_cparams(("arbitrary",)),
        name="mlstm",
    )(proj3, proj3, proj3, gates3, conv_w, conv_b2, gate_b128, norm_g2)


def _mla_prep_kernel(cq_ref, ckv_ref, kr_ref, krs_ref, qg_ref, kvg_ref, wqa_ref, wqb_ref, wkv_ref,
                     cos_ref, sin_ref, q_out, k_out, v_out):
    nd, hd = C_NOPE_DIM, C_HEADS
    scale = (C_NOPE_DIM + C_ROPE_DIM) ** -0.5 * LOG2_E
    cos = cos_ref[...]
    sin = sin_ref[...]

    cq = cq_ref[0].astype(F32)
    qn = (cq * lax.rsqrt(jnp.mean(cq * cq, axis=-1, keepdims=True) + RMS_EPS) * qg_ref[...]).astype(BF16)
    qa = jnp.dot(qn, wqa_ref[...], preferred_element_type=F32)
    qb = jnp.dot(qn, wqb_ref[...], preferred_element_type=F32)
    for h in range(hd):
        nope = qa[:, h * 2 * nd: h * 2 * nd + nd]
        rope = qa[:, h * 2 * nd + nd:(h + 1) * 2 * nd] * cos + qb[:, h * nd:(h + 1) * nd] * sin
        q_out[0, h] = (jnp.concatenate([nope, rope], axis=-1) * scale).astype(q_out.dtype)

    ckv = ckv_ref[0].astype(F32)
    kvn = (ckv * lax.rsqrt(jnp.mean(ckv * ckv, axis=-1, keepdims=True) + RMS_EPS) * kvg_ref[...]).astype(BF16)
    kv = jnp.dot(kvn, wkv_ref[...], preferred_element_type=F32)
    kr = kr_ref[0].astype(F32) * cos + krs_ref[0].astype(F32) * sin
    ones_blk = jnp.where(lax.broadcasted_iota(jnp.int32, kr.shape, 1) == 0, 1.0, 0.0).astype(F32)
    for h in range(hd):
        k_out[0, h] = jnp.concatenate([kv[:, h * nd:(h + 1) * nd], kr], axis=-1).astype(k_out.dtype)
        v_h = kv[:, hd * nd + h * nd: hd * nd + (h + 1) * nd]
        v_out[0, h] = jnp.concatenate([v_h, ones_blk], axis=-1).astype(v_out.dtype)


def _mla_prep(proj3, q_norm_g, kv_norm_g, wqa, wqb, wkv, cos128, sin128):
    b, s, _ = proj3.shape
    tm = MLA_TM
    hd, nd = C_HEADS, C_NOPE_DIM
    blk = lambda width, off: pl.BlockSpec((1, tm, width), lambda bi, i: (bi, i, off // width))
    full = lambda a: pl.BlockSpec(a.shape, lambda bi, i: (0,) * a.ndim)
    tab = pl.BlockSpec((tm, LANES), lambda bi, i: (i, 0))
    qg = q_norm_g.reshape(1, -1)
    kvg = kv_norm_g.reshape(1, -1)
    head_out = lambda width: pl.BlockSpec((1, hd, tm, width), lambda bi, i: (bi, 0, i, 0))
    return pl.pallas_call(
        _mla_prep_kernel,
        grid=(b, s // tm),
        in_specs=[blk(C_Q_LORA, P_CQ), blk(C_KV_LORA, P_CKV), blk(LANES, P_CKR), blk(LANES, P_CKRS),
                  full(qg), full(kvg), full(wqa), full(wqb), full(wkv), tab, tab],
        out_specs=[head_out(2 * nd), head_out(2 * nd), head_out(C_V_DIM + LANES)],
        out_shape=[jax.ShapeDtypeStruct((b, hd, s, 2 * nd), BF16),
                   jax.ShapeDtypeStruct((b, hd, s, 2 * nd), BF16),
                   jax.ShapeDtypeStruct((b, hd, s, C_V_DIM + LANES), BF16)],
        compiler_params=_cparams(("parallel", "arbitrary")),
        name="mla_prep",
    )(proj3, proj3, proj3, proj3, qg, kvg, wqa, wqb, wkv, cos128, sin128)


def _flash_kernel(q_ref, k_ref, v_ref, o_ref):
    qi = pl.program_id(2)
    bq = FLASH_BQ
    nh = q_ref.shape[1]
    dv = C_V_DIM

    def attend(q, h, key0, nkeys, m, acc, causal):
        ks = k_ref[0, h, pl.ds(key0, nkeys), :]
        vs = v_ref[0, h, pl.ds(key0, nkeys), :]
        s = lax.dot_general(q, ks, (((1,), (1,)), ((), ())), preferred_element_type=F32)
        if causal:
            row = lax.broadcasted_iota(jnp.int32, s.shape, 0)
            col = lax.broadcasted_iota(jnp.int32, s.shape, 1)
            s = jnp.where(row >= col, s, NEG_INF)
        m_new = jnp.maximum(m, jnp.max(s, axis=-1, keepdims=True))
        p = jnp.exp2(s - m_new)
        acc = jnp.exp2(m - m_new) * acc + jnp.dot(p.astype(BF16), vs, preferred_element_type=F32)
        return m_new, acc

    def full_step(j, carries):
        start = pl.multiple_of(j * bq, bq)
        return tuple(attend(q_ref[0, h], h, start, bq, *carries[h], False) for h in range(nh))

    init = tuple((jnp.full((bq, 1), NEG_INF, F32), jnp.zeros((bq, v_ref.shape[3]), F32)) for _ in range(nh))
    carries = lax.fori_loop(0, qi, full_step, init)

    half = bq // 2
    diag0 = pl.multiple_of(qi * bq, bq)
    for h in range(nh):
        m, acc = attend(q_ref[0, h], h, diag0, half, *carries[h], True)
        m_lo, acc_lo = attend(q_ref[0, h, half:, :], h, diag0 + half, half, m[half:], acc[half:], True)
        cols = slice(h * dv, (h + 1) * dv)
        o_ref[0, :half, cols] = (acc[:half, :dv] / acc[:half, dv:dv + 1]).astype(o_ref.dtype)
        o_ref[0, half:, cols] = (acc_lo[:, :dv] / acc_lo[:, dv:dv + 1]).astype(o_ref.dtype)


def _flash(qc, kc, vc):
    b, hd, s, dk = qc.shape
    dv = C_V_DIM
    bq = FLASH_BQ
    nh = FLASH_HEADS
    return pl.pallas_call(
        _flash_kernel,
        grid=(b, hd // nh, s // bq),
        in_specs=[pl.BlockSpec((1, nh, bq, dk), lambda bi, h, i: (bi, h, i, 0)),
                  pl.BlockSpec((1, nh, s, dk), lambda bi, h, i: (bi, h, 0, 0), pipeline_mode=pl.Buffered(1)),
                  pl.BlockSpec((1, nh, s, vc.shape[3]), lambda bi, h, i: (bi, h, 0, 0),
                               pipeline_mode=pl.Buffered(1))],
        out_specs=pl.BlockSpec((1, bq, nh * dv), lambda bi, h, i: (bi, i, h)),
        out_shape=jax.ShapeDtypeStruct((b, s, hd * dv), BF16),
        compiler_params=_cparams(("parallel", "parallel", "arbitrary")),
        name="flash",
    )(qc, kc, vc)


def _layer_norm(z, g, b):
    mu = jnp.mean(z, axis=-1, keepdims=True)
    cen = z - mu
    var = jnp.mean(cen * cen, axis=-1, keepdims=True)
    return cen * lax.rsqrt(var + LN_EPS) * g + b


def _outproj_kernel(ya_ref, yb_ref, yc_ref, x_ref, w_ref, g_ref, b_ref, wr_ref, br_ref,
                    x1_ref, x1rc_ref, e_ref, gate_ref, rank_ref, cnt_ref, cnt_sc):
    i = pl.program_id(0)
    tm = x_ref.shape[0]
    wa, wb = ya_ref.shape[1], yb_ref.shape[1]

    @pl.when(i == 0)
    def _init():
        cnt_sc[...] = jnp.zeros(cnt_sc.shape, F32)

    mix = jnp.dot(ya_ref[...], w_ref[0:wa, :], preferred_element_type=F32)
    mix = mix + jnp.dot(yb_ref[...], w_ref[wa:wa + wb, :], preferred_element_type=F32)
    mix = mix + jnp.dot(yc_ref[...], w_ref[wa + wb:, :], preferred_element_type=F32)
    x1 = _layer_norm(DN_ALPHA * x_ref[...] + mix, g_ref[...], b_ref[...])
    x1_ref[...] = x1
    _store_rc(x1rc_ref, x1)

    logits = jnp.dot(x1.astype(BF16), wr_ref[...], preferred_element_type=F32) + br_ref[...]
    lane = lax.broadcasted_iota(jnp.int32, logits.shape, 1)
    logits = jnp.where(lane < N_EXPERTS, logits, NEG_INF)
    e_out = jnp.zeros(logits.shape, jnp.int32)
    p_out = jnp.zeros(logits.shape, F32)
    top = None
    den = None
    onehots = []
    for r in range(TOP_K):
        mx = jnp.max(logits, axis=-1, keepdims=True)
        idx = jnp.min(jnp.where(logits == mx, lane, LANES), axis=-1, keepdims=True)
        if r == 0:
            top = mx
        p = jnp.exp(mx - top)
        den = p if r == 0 else den + p
        sel = lane == idx
        onehots.append(jnp.where(sel, 1.0, 0.0).astype(F32))
        e_out = jnp.where(lane == r, idx, e_out)
        p_out = jnp.where(lane == r, p, p_out)
        logits = jnp.where(sel, NEG_INF, logits)
    e_ref[...] = e_out
    gate_ref[...] = p_out / den

    oh_sum = onehots[0] + onehots[1] + onehots[2] + onehots[3]
    row = lax.broadcasted_iota(jnp.int32, (tm, tm), 0)
    col = lax.broadcasted_iota(jnp.int32, (tm, tm), 1)
    before = jnp.where(row > col, 1.0, 0.0).astype(BF16)
    base = jnp.dot(before, oh_sum.astype(BF16), preferred_element_type=F32) + cnt_sc[0:1, :]
    rank_out = jnp.zeros(logits.shape, jnp.int32)
    for r in range(TOP_K):
        rk = jnp.sum(onehots[r] * base, axis=-1, keepdims=True)
        rank_out = jnp.where(lane == r, rk.astype(jnp.int32), rank_out)
    rank_ref[...] = rank_out
    total = cnt_sc[0:1, :] + jnp.sum(oh_sum, axis=0, keepdims=True)
    cnt_sc[0:1, :] = total
    cnt_ref[...] = jnp.broadcast_to(total, cnt_ref.shape).astype(jnp.int32)


def _outproj(ya, yb, yc, x2d, w_bf, g, b, wr128, br128):
    t, d = x2d.shape
    tm = OUT_TM
    rows = lambda a: pl.BlockSpec((tm, a.shape[1]), lambda i: (i, 0))
    full = lambda a: pl.BlockSpec(a.shape, lambda i: (0,) * a.ndim, pipeline_mode=pl.Buffered(1))
    g2, b2 = g.reshape(1, d), b.reshape(1, d)
    lane_blk = pl.BlockSpec((tm, LANES), lambda i: (i, 0))
    return pl.pallas_call(
        _outproj_kernel,
        grid=(t // tm,),
        in_specs=[rows(ya), rows(yb), rows(yc), rows(x2d), full(w_bf), full(g2), full(b2), full(wr128), full(br128)],
        out_specs=[pl.BlockSpec((tm, d), lambda i: (i, 0)),
                   pl.BlockSpec((tm * ROW_TILES, LANES), lambda i: (i, 0)),
                   lane_blk, lane_blk, lane_blk,
                   pl.BlockSpec((SUBLANES, LANES), lambda i: (0, 0))],
        out_shape=[jax.ShapeDtypeStruct((t, d), F32),
                   jax.ShapeDtypeStruct((t * ROW_TILES, LANES), U32),
                   jax.ShapeDtypeStruct((t, LANES), jnp.int32),
                   jax.ShapeDtypeStruct((t, LANES), F32),
                   jax.ShapeDtypeStruct((t, LANES), jnp.int32),
                   jax.ShapeDtypeStruct((SUBLANES, LANES), jnp.int32)],
        scratch_shapes=[pltpu.VMEM((SUBLANES, LANES), F32)],
        compiler_params=_cparams(("arbitrary",)),
        name="outproj",
    )(ya, yb, yc, x2d, w_bf, g2, b2, wr128, br128)


def _rc_rows(row, n=1):
    return pl.ds(pl.multiple_of(row * ROW_TILES, ROW_TILES), n * ROW_TILES)


def _dispatch_kernel(pos_ref, cnt_ref, pst_ref, pad_ref, nblk_ref, x_hbm, xs_hbm, buf, zbuf, sem_in, sem_out, sem_z):
    i = pl.program_id(0)
    n = pl.num_programs(0)
    tm = DSP_TM
    nslot = buf.shape[0]

    def in_copy(blk, slot):
        return pltpu.make_async_copy(x_hbm.at[_rc_rows(blk * tm, tm)], buf.at[slot], sem_in.at[slot])

    def row_out(slot, t, p):
        return pltpu.make_async_copy(buf.at[slot, _rc_rows(t)], xs_hbm.at[_rc_rows(p)], sem_out.at[slot])

    def wait_outs(slot):
        for _ in range(TOP_K):
            pltpu.make_async_copy(buf.at[slot], xs_hbm.at[_rc_rows(0, tm)], sem_out.at[slot]).wait()

    slot = lax.rem(i, nslot)

    @pl.when(i == 0)
    def _first_in():
        in_copy(0, 0).start()

    @pl.when(i >= 2)
    def _free_slot():
        wait_outs(lax.rem(i + 1, nslot))

    @pl.when(i + 1 < n)
    def _next_in():
        in_copy(i + 1, lax.rem(i + 1, nslot)).start()

    in_copy(i, slot).wait()
    base = i * (tm * TOP_K)

    def issue(t, _):
        for k in range(TOP_K):
            row_out(slot, t, pos_ref[base + t * TOP_K + k]).start(priority=k % 2)
        return 0

    lax.fori_loop(0, tm, issue, 0, unroll=2)

    @pl.when(i == n - 1)
    def _finish():
        wait_outs(lax.rem(i + 2, nslot))
        wait_outs(slot)
        zbuf[...] = jnp.zeros(zbuf.shape, zbuf.dtype)

        def pad_rows(e, _):
            cnt = cnt_ref[e]
            first = pst_ref[e] + cnt
            npad = pad_ref[e] - cnt

            def zero_row(q):
                return pltpu.make_async_copy(zbuf.at[_rc_rows(0)], xs_hbm.at[_rc_rows(first + q)], sem_z)

            def start(q, c):
                zero_row(q).start()
                return c

            def wait(q, c):
                zero_row(q).wait()
                return c

            lax.fori_loop(0, npad, start, 0)
            lax.fori_loop(0, npad, wait, 0)
            return 0

        lax.fori_loop(0, N_EXPERTS, pad_rows, 0)

        def tail_block(bk, _):
            cp = pltpu.make_async_copy(zbuf, xs_hbm.at[_rc_rows(bk * MOE_PAD, MOE_PAD)], sem_z)
            cp.start()
            cp.wait()
            return 0

        lax.fori_loop(nblk_ref[0], xs_hbm.shape[0] // (MOE_PAD * ROW_TILES), tail_block, 0)


def _dispatch(pos, counts, pad_start, padded, nblk, x1rc, rows):
    t = x1rc.shape[0] // ROW_TILES
    tm = DSP_TM
    return pl.pallas_call(
        _dispatch_kernel,
        grid_spec=pltpu.PrefetchScalarGridSpec(
            num_scalar_prefetch=5,
            grid=(t // tm,),
            in_specs=[pl.BlockSpec(memory_space=pl.ANY)],
            out_specs=pl.BlockSpec(memory_space=pl.ANY),
            scratch_shapes=[pltpu.VMEM((3, tm * ROW_TILES, LANES), U32),
                            pltpu.VMEM((MOE_PAD * ROW_TILES, LANES), U32),
                            pltpu.SemaphoreType.DMA((3,)),
                            pltpu.SemaphoreType.DMA((3,)),
                            pltpu.SemaphoreType.DMA(())]),
        out_shape=jax.ShapeDtypeStruct((rows * ROW_TILES, LANES), U32),
        compiler_params=_cparams(("arbitrary",)),
        name="dispatch",
    )(pos, counts, pad_start, padded, nblk, x1rc)


def _moe_kernel(layer, sbe_ref, sbr_ref, sbn_ref, nblk_ref, xs_hbm, wgu_hbm, wdn_hbm, bgu_ref, bd_ref, y_hbm,
                xbuf, acc, stage_in, stage_out, wg_f, wu_f, wd_f, wg_bf, wu_bf, wd_bf, sem_in, sem_out, sem_w):
    sb = pl.program_id(0)
    nsb = pl.num_programs(0)
    tf = wg_f.shape[2]
    dff = wdn_hbm.shape[2]
    nj = dff // tf
    n = sbn_ref[sb]
    row0 = sbr_ref[sb]
    nchunk = n // MOE_PAD
    cur = lax.rem(sb, 2)
    big, huge = 2 * MOE_PAD, 4 * MOE_PAD

    def weight_copies(sb_idx, j, slot):
        e = sbe_ref[sb_idx]
        cols = pl.ds(pl.multiple_of(j * tf, tf), tf)
        up_cols = pl.ds(pl.multiple_of(dff + j * tf, tf), tf)
        copies = [pltpu.make_async_copy(wdn_hbm.at[layer, e, cols, :], wd_f.at[slot], sem_w.at[slot])]
        part = wg_f.shape[1] // MOE_W_PARTS
        for p in range(MOE_W_PARTS):
            band = pl.ds(p * part, part)
            copies.append(pltpu.make_async_copy(wgu_hbm.at[layer, e, band, cols], wg_f.at[slot, band], sem_w.at[slot]))
            copies.append(pltpu.make_async_copy(wgu_hbm.at[layer, e, band, up_cols], wu_f.at[slot, band],
                                                sem_w.at[slot]))
        return copies

    def chunk_rows(c):
        return pl.ds(pl.multiple_of(c * MOE_PAD, MOE_PAD), MOE_PAD)

    def chunk_in(first_row, c):
        return pltpu.make_async_copy(xs_hbm.at[_rc_rows(first_row + c * MOE_PAD, MOE_PAD)], stage_in, sem_in)

    def chunk_out(c, slot):
        return pltpu.make_async_copy(stage_out.at[slot], y_hbm.at[_rc_rows(row0 + c * MOE_PAD, MOE_PAD)],
                                     sem_out.at[slot])

    def wait_outs(count):
        @pl.when(count >= 2)
        def _older():
            chunk_out(0, lax.rem(count, 2)).wait()

        @pl.when(count >= 1)
        def _newest():
            chunk_out(0, lax.rem(count + 1, 2)).wait()

    def convert_in(slot, c):
        lo, hi = _load_rc(stage_in, MOE_PAD)
        xbuf[slot, chunk_rows(c), :] = jnp.concatenate([lo.astype(BF16), hi.astype(BF16)], axis=-1)

    @pl.when(sb == 0)
    def _load_first():
        for cp in weight_copies(0, 0, 0):
            cp.start()
        acc[...] = jnp.zeros(acc.shape, acc.dtype)

        def load(c, _):
            chunk_in(row0, c).start()
            chunk_in(row0, c).wait()
            convert_in(0, c)
            return 0

        lax.fori_loop(0, nchunk, load, 0)

    n_next = sbn_ref[sb + 1]

    def ffn(j, start, size, wg, wu, wd):
        rs = pl.ds(start, size)
        xr = xbuf[cur, rs, :]
        cols = pl.ds(pl.multiple_of(j * tf, tf), tf)
        up_cols = pl.ds(pl.multiple_of(dff + j * tf, tf), tf)
        gt = jnp.dot(xr, wg, preferred_element_type=F32) + bgu_ref[:, cols]
        up = jnp.dot(xr, wu, preferred_element_type=F32) + bgu_ref[:, up_cols]
        gt = jnp.minimum(gt, SWIGLU_LIMIT)
        up = jnp.clip(up, -SWIGLU_LIMIT, SWIGLU_LIMIT)
        act = ((up + 1.0) * gt * jax.nn.sigmoid(SWIGLU_ALPHA * gt)).astype(BF16)
        half = acc.shape[1] // 2
        for cs in (slice(0, half), slice(half, 2 * half)):
            prev = jnp.where(j == 0, bd_ref[:, cs], acc[rs, cs])
            acc[rs, cs] = prev + jnp.dot(act, wd[:, cs], preferred_element_type=F32)

    def first_chunk(j, wslot, size):
        wg = wg_f[wslot].astype(BF16)
        wu = wu_f[wslot].astype(BF16)
        wd = wd_f[wslot].astype(BF16)
        wg_bf[...] = wg
        wu_bf[...] = wu
        wd_bf[...] = wd
        ffn(j, 0, size, wg, wu, wd)

    def hidden_tile(j, _):
        wslot = lax.rem(j, 2)
        for cp in weight_copies(sb, j, wslot):
            cp.wait()

        @pl.when(j + 1 < nj)
        def _next_tile():
            for cp in weight_copies(sb, j + 1, 1 - wslot):
                cp.start()

        @pl.when((j + 1 == nj) & (n_next > 0))
        def _next_super_block():
            for cp in weight_copies(sb + 1, 0, 0):
                cp.start()

        fetch_next = j * MOE_PAD < n_next

        @pl.when(fetch_next)
        def _start_next():
            chunk_in(sbr_ref[sb + 1], j).start()

        @pl.when((j == 1) & (sb > 0))
        def _drain_prev_outs():
            wait_outs(sbn_ref[jnp.maximum(sb - 1, 0)] // MOE_PAD)

        def rest(start, size):
            ffn(j, start, size, wg_bf[...], wu_bf[...], wd_bf[...])

        @pl.when(n >= huge)
        def _rows_huge():
            first_chunk(j, wslot, huge)

            @pl.when(n - huge >= big)
            def _then_big():
                rest(huge, big)

            @pl.when(lax.rem(n, big) != 0)
            def _then_pad():
                rest(pl.multiple_of(n - MOE_PAD, MOE_PAD), MOE_PAD)

        @pl.when((n >= big) & (n < huge))
        def _rows_big():
            first_chunk(j, wslot, big)

            @pl.when(n > big)
            def _then_pad():
                rest(big, MOE_PAD)

        @pl.when(n < big)
        def _rows_small():
            first_chunk(j, wslot, MOE_PAD)

        @pl.when(j == nj - 1)
        def _store_rows():
            def store(c, _):
                slot = lax.rem(c, 2)

                @pl.when(c >= 2)
                def _slot_free():
                    chunk_out(c - 2, slot).wait()

                _store_rc(stage_out.at[slot], acc[chunk_rows(c), :])
                chunk_out(c, slot).start()
                return 0

            lax.fori_loop(0, nchunk, store, 0)

            @pl.when((sb == nsb - 1) | (n_next == 0))
            def _no_later_tile():
                wait_outs(nchunk)

        @pl.when(fetch_next)
        def _finish_next():
            chunk_in(sbr_ref[sb + 1], j).wait()
            convert_in(1 - cur, j)

        return 0

    @pl.when(n > 0)
    def _super_block():
        lax.fori_loop(0, nj, hidden_tile, 0)

    @pl.when(sb == nsb - 1)
    def _zero_tail():
        stage_out[0] = jnp.zeros(stage_out.shape[1:], stage_out.dtype)

        def tail_block(bk, _):
            cp = pltpu.make_async_copy(stage_out.at[0], y_hbm.at[_rc_rows(bk * MOE_PAD, MOE_PAD)], sem_out.at[0])
            cp.start()
            cp.wait()
            return 0

        lax.fori_loop(nblk_ref[0], y_hbm.shape[0] // (MOE_PAD * ROW_TILES), tail_block, 0)


def _moe(layer, sb_e, sb_row0, sb_n, nblk, xs, w_gate_up, b_gate_up, w_down, b_down):
    rows = xs.shape[0] // ROW_TILES
    d = D_MODEL
    dff = w_down.shape[2]
    tf = MOE_TF
    nj = dff // tf
    nsb = sb_e.shape[0] - 1
    assert nj % 2 == 0 and MOE_TMAX // MOE_PAD <= nj and MOE_TMAX == 6 * MOE_PAD
    bgu = b_gate_up.reshape(DEPTH, N_EXPERTS, 1, 2 * dff)
    bdn = b_down.reshape(DEPTH, N_EXPERTS, 1, d)
    expert_row = lambda width: pl.BlockSpec((None, None, 1, width), lambda sb, e, r, n, nb: (layer, e[sb], 0, 0))
    return pl.pallas_call(
        functools.partial(_moe_kernel, layer),
        grid_spec=pltpu.PrefetchScalarGridSpec(
            num_scalar_prefetch=4,
            grid=(nsb,),
            in_specs=[pl.BlockSpec(memory_space=pl.ANY),
                      pl.BlockSpec(memory_space=pl.ANY),
                      pl.BlockSpec(memory_space=pl.ANY),
                      expert_row(2 * dff),
                      expert_row(d)],
            out_specs=pl.BlockSpec(memory_space=pl.ANY),
            scratch_shapes=[pltpu.VMEM((2, MOE_TMAX, d), BF16),
                            pltpu.VMEM((MOE_TMAX, d), F32),
                            pltpu.VMEM((MOE_PAD * ROW_TILES, LANES), U32),
                            pltpu.VMEM((2, MOE_PAD * ROW_TILES, LANES), U32),
                            pltpu.VMEM((2, d, tf), F32),
                            pltpu.VMEM((2, d, tf), F32),
                            pltpu.VMEM((2, tf, d), F32),
                            pltpu.VMEM((d, tf), BF16),
                            pltpu.VMEM((d, tf), BF16),
                            pltpu.VMEM((tf, d), BF16),
                            pltpu.SemaphoreType.DMA(()),
                            pltpu.SemaphoreType.DMA((2,)),
                            pltpu.SemaphoreType.DMA((2,))]),
        out_shape=jax.ShapeDtypeStruct((rows * ROW_TILES, LANES), U32),
        compiler_params=_cparams(("arbitrary",)),
        name="moe",
    )(sb_e, sb_row0, sb_n, nblk, xs, w_gate_up, w_down, bgu, bdn)


def _combine_kernel(pos_ref, y_hbm, gate_ref, x1_ref, g_ref, b_ref, o_ref, buf, sem):
    i = pl.program_id(0)
    n = pl.num_programs(0)
    tm = CMB_TM
    slot = lax.rem(i, 2)

    def issue(blk, dst_slot):
        base = blk * (tm * TOP_K)

        def body(t, _):
            for k in range(TOP_K):
                p = pos_ref[base + t * TOP_K + k]
                pltpu.make_async_copy(y_hbm.at[_rc_rows(p)], buf.at[dst_slot, k, _rc_rows(t)],
                                      sem.at[dst_slot]).start(priority=k % 2)
            return 0

        lax.fori_loop(0, tm, body, 0, unroll=2)

    @pl.when(i == 0)
    def _first():
        issue(0, 0)

    @pl.when(i + 1 < n)
    def _next():
        issue(i + 1, 1 - slot)

    for k in range(TOP_K):
        pltpu.make_async_copy(y_hbm.at[_rc_rows(0, tm)], buf.at[slot, k], sem.at[slot]).wait()
    gate = gate_ref[...]
    ffn_lo, ffn_hi = None, None
    for k in range(TOP_K):
        lo, hi = _load_rc(buf.at[slot, k], tm)
        gk = gate[:, k:k + 1]
        ffn_lo = gk * lo if k == 0 else ffn_lo + gk * lo
        ffn_hi = gk * hi if k == 0 else ffn_hi + gk * hi
    ffn = jnp.concatenate([ffn_lo, ffn_hi], axis=-1)
    o_ref[...] = _layer_norm(DN_ALPHA * x1_ref[...] + ffn, g_ref[...], b_ref[...])


def _combine(pos_flat, y, gate, x1, g, b):
    t, d = x1.shape
    tm = CMB_TM
    g2, b2 = g.reshape(1, d), b.reshape(1, d)
    return pl.pallas_call(
        _combine_kernel,
        grid_spec=pltpu.PrefetchScalarGridSpec(
            num_scalar_prefetch=1,
            grid=(t // tm,),
            in_specs=[pl.BlockSpec(memory_space=pl.ANY),
                      pl.BlockSpec((tm, LANES), lambda i, p: (i, 0)),
                      pl.BlockSpec((tm, d), lambda i, p: (i, 0)),
                      pl.BlockSpec((1, d), lambda i, p: (0, 0)),
                      pl.BlockSpec((1, d), lambda i, p: (0, 0))],
            out_specs=pl.BlockSpec((tm, d), lambda i, p: (i, 0)),
            scratch_shapes=[pltpu.VMEM((2, TOP_K, tm * ROW_TILES, LANES), U32), pltpu.SemaphoreType.DMA((2,))]),
        out_shape=jax.ShapeDtypeStruct((t, d), F32),
        compiler_params=_cparams(("arbitrary",)),
        name="combine",
    )(pos_flat, y, gate, x1, g2, b2)


def _swap_halves(w):
    half = w.shape[-1] // 2
    return jnp.concatenate([w[..., half:], w[..., :half]], axis=-1)


def _w_in_columns(w):
    d = w.shape[0]
    widths = (512, 128, 128, 256, 256, 512, 512, 8, 512, 256, 64)
    offs = [0]
    for wd in widths:
        offs.append(offs[-1] + wd)
    a_q, a_k, a_v, m_q, m_k, m_v, m_o, m_if, c_q, c_kv, c_kr = [w[:, offs[i]:offs[i + 1]] for i in range(len(widths))]
    z = lambda n: jnp.zeros((d, n), w.dtype)
    return jnp.concatenate([a_q, m_v, m_o, c_q, m_q, m_k, c_kv, a_k, a_v,
                            c_kr, z(LANES - 64), m_if, z(LANES - 8), _swap_halves(c_kr), z(LANES - 64)], axis=1)


def _w_in_layout_kernel(w_ref, o_ref):
    o_ref[...] = _w_in_columns(w_ref[...]).astype(o_ref.dtype)


def _layout_w_in(w_in, layer):
    _, d, n = w_in.shape
    tk = 256
    return pl.pallas_call(
        _w_in_layout_kernel,
        grid=(d // tk,),
        in_specs=[pl.BlockSpec((None, tk, n), lambda i: (layer, i, 0))],
        out_specs=pl.BlockSpec((tk, N_PROJ), lambda i: (i, 0)),
        out_shape=jax.ShapeDtypeStruct((d, N_PROJ), BF16),
        compiler_params=_cparams(("parallel",)),
        name="w_in_layout",
    )(w_in)


def _layout_w_uq(w):
    r = w.shape[0]
    w3 = w.astype(BF16).reshape(r, C_HEADS, C_NOPE_DIM + C_ROPE_DIM)
    nope, rope = w3[..., :C_NOPE_DIM], w3[..., C_NOPE_DIM:]
    z = jnp.zeros((r, C_HEADS, LANES - C_ROPE_DIM), BF16)
    wa = jnp.concatenate([nope, rope, z], axis=-1).reshape(r, -1)
    wb = jnp.concatenate([_swap_halves(rope), z], axis=-1).reshape(r, -1)
    return wa, wb


def _layout_w_ukv(w):
    r = w.shape[0]
    w3 = w.astype(BF16).reshape(r, C_HEADS, C_NOPE_DIM + C_V_DIM)
    return jnp.concatenate([w3[..., :C_NOPE_DIM].reshape(r, -1), w3[..., C_NOPE_DIM:].reshape(r, -1)], axis=-1)


def _rope_tables(seq):
    dim = C_ROPE_DIM
    inv = 1.0 / (ROPE_THETA ** (jnp.arange(0, dim, 2, dtype=F32) / dim))
    ang = jnp.arange(seq, dtype=F32)[:, None] * inv[None, :]
    cos, sin = jnp.cos(ang), jnp.sin(ang)
    z = jnp.zeros((seq, LANES - dim), F32)
    return jnp.concatenate([cos, cos, z], axis=-1), jnp.concatenate([-sin, sin, z], axis=-1)


def _route(top_e, rank, counts, nsb):
    padded = (counts + MOE_PAD - 1) // MOE_PAD * MOE_PAD
    pad_end = jnp.cumsum(padded)
    pad_start = pad_end - padded
    is_e = top_e[:, :, None] == jnp.arange(N_EXPERTS, dtype=jnp.int32)[None, None, :]
    pos = (jnp.sum(jnp.where(is_e, pad_start[None, None, :], 0), axis=-1) + rank).reshape(-1).astype(jnp.int32)
    nblk = (pad_end[-1:] // MOE_PAD).astype(jnp.int32)
    nsb_e = (padded + MOE_TMAX - 1) // MOE_TMAX
    sb_end = jnp.cumsum(nsb_e)
    sb_start = sb_end - nsb_e
    sb_idx = jnp.arange(nsb + 1, dtype=jnp.int32)
    n_valid = sb_end[-1]
    sb_eff = jnp.minimum(sb_idx, n_valid - 1)
    sb_e = jnp.searchsorted(sb_end, sb_eff, side="right").astype(jnp.int32)
    part = sb_eff - sb_start[sb_e]
    sb_row0 = pad_start[sb_e] + part * MOE_TMAX
    sb_n = jnp.where(sb_idx < n_valid, jnp.minimum(padded[sb_e] - part * MOE_TMAX, MOE_TMAX), 0)
    tables = (counts, pad_start.astype(jnp.int32), padded.astype(jnp.int32), nblk)
    return pos, tables, sb_e, sb_row0.astype(jnp.int32), sb_n.astype(jnp.int32)


def kernel(x, w_in, conv_w, conv_b, m_gate_b, m_norm_g, sinks, q_norm_g, w_uq, kv_norm_g, w_ukv, w_out,
           ln1_g, ln1_b, w_router, b_router, w_gate_up, b_gate_up, w_down, b_down, ln2_g, ln2_b):
    b, s, d = x.shape
    t = b * s
    n_assign = t * TOP_K
    rows = (n_assign // MOE_PAD + N_EXPERTS) * MOE_PAD
    nsb = N_EXPERTS + rows // MOE_TMAX
    cos128, sin128 = _rope_tables(s)
    xt = x.reshape(t, d)
    for l in range(DEPTH):
        w_in_bf = _layout_w_in(w_in, l)
        wqa, wqb = _layout_w_uq(w_uq[l])
        wkv = _layout_w_ukv(w_ukv[l])
        w_out_bf = w_out[l].astype(BF16)
        wr128 = jnp.zeros((d, LANES), BF16).at[:, :N_EXPERTS].set(w_router[l].astype(BF16))
        br128 = jnp.zeros((1, LANES), F32).at[0, :N_EXPERTS].set(b_router[l])

        proj, gates = _proj(xt, w_in_bf)
        proj3 = proj.reshape(b, s, N_PROJ)
        y_a = _swa(proj3, sinks[l])
        y_b = _mlstm(proj3, gates.reshape(b, s, LANES), conv_w[l], conv_b[l], m_gate_b[l], m_norm_g[l])
        qc, kc, vc = _mla_prep(proj3, q_norm_g[l], kv_norm_g[l], wqa, wqb, wkv, cos128, sin128)
        y_c = _flash(qc, kc, vc)
        x1, x1rc, top_e, gate, rank, cnt = _outproj(y_a.reshape(t, -1), y_b.reshape(t, -1), y_c.reshape(t, -1), xt,
                                                    w_out_bf, ln1_g[l], ln1_b[l], wr128, br128)
        pos, tables, sb_e, sb_row0, sb_n = _route(top_e[:, :TOP_K], rank[:, :TOP_K], cnt[0, :N_EXPERTS], nsb)
        xs = _dispatch(pos, *tables, x1rc, rows)
        y = _moe(l, sb_e, sb_row0, sb_n, tables[3], xs, w_gate_up, b_gate_up, w_down, b_down)
        xt = _combine(pos, y, gate, x1, ln2_g[l], ln2_b[l])
    return xt.reshape(b, s, d)
```

```python
import functools

import jax
import jax.numpy as jnp
from jax import lax
from jax.experimental import pallas as pl
from jax.experimental.pallas import tpu as pltpu

F32 = jnp.float32
BF16 = jnp.bfloat16
NEG_INF = float("-inf")
LOG2_E = 1.4426950408889634

D_MODEL = 2048
DEPTH = 2
SWA_HEADS, SWA_KV_HEADS, SWA_HEAD_DIM, SWA_WINDOW = 8, 2, 64, 128
M_HEADS, M_QK_DIM, M_V_DIM, M_CONV = 4, 64, 128, 4
C_HEADS, C_NOPE_DIM, C_ROPE_DIM, C_V_DIM = 8, 128, 64, 128
C_Q_LORA, C_KV_LORA = 512, 256
ROPE_THETA = 10000.0
N_EXPERTS, TOP_K = 32, 4
SWIGLU_LIMIT, SWIGLU_ALPHA = 7.0, 1.702
DN_ALPHA = (2 * DEPTH) ** 0.25
LN_EPS, RMS_EPS = 1e-5, 1e-6

LANES = 128
SUBLANES = 8
VMEM_LIMIT = 56 * 1024 * 1024

P_AQ, P_MV, P_MO, P_CQ, P_MQK, P_CKV, P_AK, P_AV, P_CKR, P_MIF, P_CKRS = (
    0, 512, 1024, 1536, 2048, 2560, 2816, 2944, 3072, 3200, 3328)
N_PROJ = 3456

PROJ_TM, PROJ_TN = 1024, 1152
M_CHUNK = 256
MLA_TM = 512
FLASH_BQ = 1024
FLASH_HEADS = 4
OUT_TM = 512
MOE_PAD = 256
MOE_TMAX = 1536
MOE_TF = 256
MOE_W_PARTS = 4
CMB_TM = 256
DSP_TM = 512

ROW_TILES = D_MODEL // 2 // LANES
U32 = jnp.uint32
HI_MASK = 0xFFFF0000


def _cparams(sem, vmem=VMEM_LIMIT):
    return pltpu.CompilerParams(dimension_semantics=sem, vmem_limit_bytes=vmem)


def _bf16_bits(x):
    return lax.bitcast_convert_type(x.astype(BF16).astype(F32), U32)


def _store_rc(ref, val):
    n, d = val.shape
    words = (_bf16_bits(val[:, :d // 2]) >> 16) | (_bf16_bits(val[:, d // 2:]) & U32(HI_MASK))
    x = jnp.stack([words[:, s * LANES:(s + 1) * LANES] for s in range(ROW_TILES)], axis=0)
    ref[...] = pltpu.einshape("stl->tsl", x).reshape(n * ROW_TILES, LANES)


def _load_rc(ref, n):
    x = pltpu.einshape("tsl->stl", ref[...].reshape(n, ROW_TILES, LANES))
    words = jnp.concatenate([x[s] for s in range(ROW_TILES)], axis=-1)
    lo = lax.bitcast_convert_type(words << 16, F32)
    hi = lax.bitcast_convert_type(words & U32(HI_MASK), F32)
    return lo, hi


def _proj_kernel(x_ref, w_ref, o_ref, g_ref):
    j = pl.program_id(1)
    acc = jnp.dot(x_ref[...].astype(BF16), w_ref[...], preferred_element_type=F32)
    o_ref[...] = acc.astype(o_ref.dtype)

    @pl.when(j == P_MIF // PROJ_TN)
    def _gates():
        off = P_MIF % PROJ_TN
        g_ref[...] = acc[:, off:off + LANES]


def _proj(x2d, w_bf):
    t, d = x2d.shape
    n = w_bf.shape[1]
    return pl.pallas_call(
        _proj_kernel,
        grid=(t // PROJ_TM, n // PROJ_TN),
        in_specs=[pl.BlockSpec((PROJ_TM, d), lambda i, j: (i, 0)),
                  pl.BlockSpec((d, PROJ_TN), lambda i, j: (0, j))],
        out_specs=[pl.BlockSpec((PROJ_TM, PROJ_TN), lambda i, j: (i, j)),
                   pl.BlockSpec((PROJ_TM, LANES), lambda i, j: (i, 0))],
        out_shape=[jax.ShapeDtypeStruct((t, n), BF16), jax.ShapeDtypeStruct((t, LANES), F32)],
        compiler_params=_cparams(("parallel", "arbitrary")),
        name="proj",
    )(x2d, w_bf)


def _swa_kernel(sinks_ref, q_ref, kc_ref, kp_ref, vc_ref, vp_ref, o_ref):
    i = pl.program_id(0)
    w = SWA_WINDOW
    dh = SWA_HEAD_DIM
    grp = SWA_HEADS // SWA_KV_HEADS
    qi = lax.broadcasted_iota(jnp.int32, (w, 2 * w), 0)
    kj = lax.broadcasted_iota(jnp.int32, (w, 2 * w), 1)
    rel = qi + w - kj
    has_prev = jnp.where(i > 0, 0, w)
    mask = (rel >= 0) & (rel < w) & (kj >= has_prev)
    for b in range(q_ref.shape[0]):
        q = q_ref[b]
        outs = []
        for kvh in range(SWA_KV_HEADS):
            sl = slice(kvh * dh, (kvh + 1) * dh)
            k_cat = jnp.concatenate([kp_ref[b][:, sl], kc_ref[b][:, sl]], axis=0).astype(BF16)
            v_cat = jnp.concatenate([vp_ref[b][:, sl], vc_ref[b][:, sl]], axis=0).astype(BF16)
            for g in range(grp):
                h = kvh * grp + g
                qh = (q[:, h * dh:(h + 1) * dh] * (dh ** -0.5)).astype(BF16)
                s = lax.dot_general(qh, k_cat, (((1,), (1,)), ((), ())), preferred_element_type=F32)
                s = jnp.where(mask, s, NEG_INF)
                sink = sinks_ref[h]
                m = jnp.maximum(jnp.max(s, axis=-1, keepdims=True), sink)
                p = jnp.exp(s - m)
                den = jnp.sum(p, axis=-1, keepdims=True) + jnp.exp(sink - m)
                o = jnp.dot(p.astype(BF16), v_cat, preferred_element_type=F32)
                outs.append(o / den)
        o_ref[b] = jnp.concatenate(outs, axis=-1).astype(o_ref.dtype)


def _swa(proj3, sinks):
    b, s, _ = proj3.shape
    w = SWA_WINDOW
    nb = s // w
    kw = SWA_KV_HEADS * SWA_HEAD_DIM
    qw = SWA_HEADS * SWA_HEAD_DIM
    cur = lambda col: (lambda i: (0, i, col))
    prev = lambda col: (lambda i: (0, jnp.maximum(i - 1, 0), col))
    return pl.pallas_call(
        _swa_kernel,
        grid=(nb,),
        in_specs=[pl.BlockSpec(memory_space=pltpu.SMEM),
                  pl.BlockSpec((b, w, qw), cur(P_AQ // qw)),
                  pl.BlockSpec((b, w, kw), cur(P_AK // kw)),
                  pl.BlockSpec((b, w, kw), prev(P_AK // kw)),
                  pl.BlockSpec((b, w, kw), cur(P_AV // kw)),
                  pl.BlockSpec((b, w, kw), prev(P_AV // kw))],
        out_specs=pl.BlockSpec((b, w, qw), lambda i: (0, i, 0)),
        out_shape=jax.ShapeDtypeStruct((b, s, qw), BF16),
        compiler_params=_cparams(("arbitrary",)),
        name="swa",
    )(sinks, proj3, proj3, proj3, proj3, proj3)


def _mlstm_kernel(qk_ref, v_ref, og_ref, g_ref, cw_ref, cb_ref, gb_ref, ng_ref, y_ref, xbuf, ct_ref, m_ref):
    c = pl.program_id(0)
    nb = qk_ref.shape[0]
    ln = M_CHUNK
    dqk, dv, nh = M_QK_DIM, M_V_DIM, M_HEADS
    halo = SUBLANES

    @pl.when(c == 0)
    def _init():
        xbuf[:, 0:halo, :] = jnp.zeros((nb, halo, xbuf.shape[2]), F32)
        ct_ref[...] = jnp.zeros(ct_ref.shape, F32)
        m_ref[...] = jnp.zeros(m_ref.shape, F32)

    row = lax.broadcasted_iota(jnp.int32, (ln, ln), 0)
    col = lax.broadcasted_iota(jnp.int32, (ln, ln), 1)
    tril = row >= col
    trilf = jnp.where(tril, 1.0, 0.0).astype(F32)
    ones_blk = jnp.where(lax.broadcasted_iota(jnp.int32, (ln, dv), 1) == 0, 1.0, 0.0).astype(F32)

    for b in range(nb):
        xbuf[b, halo:halo + ln, :] = qk_ref[b].astype(F32)
        conv = cb_ref[...]
        for j in range(M_CONV):
            off = halo - (M_CONV - 1) + j
            conv = conv + cw_ref[j:j + 1, :] * xbuf[b, off:off + ln, :]
        qk = conv * jax.nn.sigmoid(conv)
        xbuf[b, 0:halo, :] = xbuf[b, ln:ln + halo, :]

        g = g_ref[b] + gb_ref[...]
        lf = jnp.minimum(g, 0.0) - jnp.log1p(jnp.exp(-jnp.abs(g)))
        cum = jnp.dot(trilf, lf, preferred_element_type=F32, precision=lax.Precision.HIGHEST)
        cum_t = cum.T
        g_t = g.T
        vv = v_ref[b].astype(F32)
        hs = []
        for h in range(nh):
            chain = b * nh + h
            q = (qk[:, h * dqk:(h + 1) * dqk] * (dqk ** -0.5)).astype(BF16)
            k = qk[:, nh * dqk + h * dqk: nh * dqk + (h + 1) * dqk].astype(BF16)
            v_ext = jnp.concatenate([vv[:, h * dv:(h + 1) * dv], ones_blk], axis=-1)
            bc_col = cum[:, nh + h:nh + h + 1]
            bc_row = cum_t[nh + h:nh + h + 1, :]
            i_col = g[:, h:h + 1]
            i_row = g_t[h:h + 1, :]
            m_prev = m_ref[chain:chain + 1, 0:1]
            dmat = jnp.where(tril, bc_col - bc_row + i_row, NEG_INF)
            m_inter = bc_col + m_prev
            m_j = jnp.maximum(m_inter, jnp.max(dmat, axis=-1, keepdims=True))
            w_intra = jnp.exp(dmat - m_j)
            w_inter = jnp.exp(m_inter - m_j)
            s = lax.dot_general(q, k, (((1,), (1,)), ((), ())), preferred_element_type=F32) * w_intra
            ct = ct_ref[chain]
            num_ext = (jnp.dot(s.astype(BF16), v_ext.astype(BF16), preferred_element_type=F32)
                       + w_inter * jnp.dot(q, ct.astype(BF16), preferred_element_type=F32))
            num = num_ext[:, :dv]
            nq = num_ext[:, dv:dv + 1]
            den = jnp.maximum(jnp.abs(nq), jnp.exp(-m_j))
            hs.append(num / den)
            m_new = m_j[ln - 1:ln, :]
            bc_last = bc_col[ln - 1:ln, :]
            w_s = jnp.exp(bc_last - bc_col + i_col - m_new)
            w_c = jnp.exp(bc_last + m_prev - m_new)
            upd = lax.dot_general(k, (w_s * v_ext).astype(BF16), (((0,), (0,)), ((), ())),
                                  preferred_element_type=F32)
            ct_ref[chain] = w_c * ct + upd
            m_ref[chain:chain + 1, :] = jnp.broadcast_to(m_new, (1, m_ref.shape[1]))
        og = og_ref[b].astype(F32)
        outs = []
        for h in range(nh):
            seg = jax.nn.sigmoid(og[:, h * dv:(h + 1) * dv]) * hs[h]
            mu = jnp.mean(seg, axis=-1, keepdims=True)
            cen = seg - mu
            var = jnp.mean(cen * cen, axis=-1, keepdims=True)
            outs.append(cen * lax.rsqrt(var + LN_EPS) * ng_ref[:, h * dv:(h + 1) * dv])
        y_ref[b] = jnp.concatenate(outs, axis=-1).astype(y_ref.dtype)


def _mlstm(proj3, gates3, conv_w, conv_b, gate_b, norm_g):
    b, s, _ = proj3.shape
    ln = M_CHUNK
    wq = 2 * M_HEADS * M_QK_DIM
    wv = M_HEADS * M_V_DIM
    blk = lambda width, off: pl.BlockSpec((b, ln, width), lambda c: (0, c, off // width))
    full = lambda a: pl.BlockSpec(a.shape, lambda c: (0,) * a.ndim)
    gate_b128 = jnp.zeros((1, LANES), F32).at[0, :2 * M_HEADS].set(gate_b)
    conv_b2 = conv_b.reshape(1, wq)
    norm_g2 = norm_g.reshape(1, wv)
    return pl.pallas_call(
        _mlstm_kernel,
        grid=(s // ln,),
        in_specs=[blk(wq, P_MQK), blk(wv, P_MV), blk(wv, P_MO), blk(LANES, 0),
                  full(conv_w), full(conv_b2), full(gate_b128), full(norm_g2)],
        out_specs=pl.BlockSpec((b, ln, wv), lambda c: (0, c, 0)),
        out_shape=jax.ShapeDtypeStruct((b, s, wv), BF16),
        scratch_shapes=[pltpu.VMEM((b, SUBLANES + ln, wq), F32),
                        pltpu.VMEM((b * M_HEADS, M_QK_DIM, 2 * M_V_DIM), F32),
                        pltpu.VMEM((b * M_HEADS, LANES), F32)],
        compiler_params=_cparams(("arbitrary",)),
        name="mlstm",
    )(proj3, proj3, proj3, gates3, conv_w, conv_b2, gate_b128, norm_g2)


def _mla_prep_kernel(cq_ref, ckv_ref, kr_ref, krs_ref, qg_ref, kvg_ref, wqa_ref, wqb_ref, wkv_ref,
                     cos_ref, sin_ref, q_out, k_out, v_out):
    nd, hd = C_NOPE_DIM, C_HEADS
    scale = (C_NOPE_DIM + C_ROPE_DIM) ** -0.5 * LOG2_E
    cos = cos_ref[...]
    sin = sin_ref[...]

    cq = cq_ref[0].astype(F32)
    qn = (cq * lax.rsqrt(jnp.mean(cq * cq, axis=-1, keepdims=True) + RMS_EPS) * qg_ref[...]).astype(BF16)
    qa = jnp.dot(qn, wqa_ref[...], preferred_element_type=F32)
    qb = jnp.dot(qn, wqb_ref[...], preferred_element_type=F32)
    for h in range(hd):
        nope = qa[:, h * 2 * nd: h * 2 * nd + nd]
        rope = qa[:, h * 2 * nd + nd:(h + 1) * 2 * nd] * cos + qb[:, h * nd:(h + 1) * nd] * sin
        q_out[0, h] = (jnp.concatenate([nope, rope], axis=-1) * scale).astype(q_out.dtype)

    ckv = ckv_ref[0].astype(F32)
    kvn = (ckv * lax.rsqrt(jnp.mean(ckv * ckv, axis=-1, keepdims=True) + RMS_EPS) * kvg_ref[...]).astype(BF16)
    kv = jnp.dot(kvn, wkv_ref[...], preferred_element_type=F32)
    kr = kr_ref[0].astype(F32) * cos + krs_ref[0].astype(F32) * sin
    ones_blk = jnp.where(lax.broadcasted_iota(jnp.int32, kr.shape, 1) == 0, 1.0, 0.0).astype(F32)
    for h in range(hd):
        k_out[0, h] = jnp.concatenate([kv[:, h * nd:(h + 1) * nd], kr], axis=-1).astype(k_out.dtype)
        v_h = kv[:, hd * nd + h * nd: hd * nd + (h + 1) * nd]
        v_out[0, h] = jnp.concatenate([v_h, ones_blk], axis=-1).astype(v_out.dtype)


def _mla_prep(proj3, q_norm_g, kv_norm_g, wqa, wqb, wkv, cos128, sin128):
    b, s, _ = proj3.shape
    tm = MLA_TM
    hd, nd = C_HEADS, C_NOPE_DIM
    blk = lambda width, off: pl.BlockSpec((1, tm, width), lambda bi, i: (bi, i, off // width))
    full = lambda a: pl.BlockSpec(a.shape, lambda bi, i: (0,) * a.ndim)
    tab = pl.BlockSpec((tm, LANES), lambda bi, i: (i, 0))
    qg = q_norm_g.reshape(1, -1)
    kvg = kv_norm_g.reshape(1, -1)
    head_out = lambda width: pl.BlockSpec((1, hd, tm, width), lambda bi, i: (bi, 0, i, 0))
    return pl.pallas_call(
        _mla_prep_kernel,
        grid=(b, s // tm),
        in_specs=[blk(C_Q_LORA, P_CQ), blk(C_KV_LORA, P_CKV), blk(LANES, P_CKR), blk(LANES, P_CKRS),
                  full(qg), full(kvg), full(wqa), full(wqb), full(wkv), tab, tab],
        out_specs=[head_out(2 * nd), head_out(2 * nd), head_out(C_V_DIM + LANES)],
        out_shape=[jax.ShapeDtypeStruct((b, hd, s, 2 * nd), BF16),
                   jax.ShapeDtypeStruct((b, hd, s, 2 * nd), BF16),
                   jax.ShapeDtypeStruct((b, hd, s, C_V_DIM + LANES), BF16)],
        compiler_params=_cparams(("parallel", "arbitrary")),
        name="mla_prep",
    )(proj3, proj3, proj3, proj3, qg, kvg, wqa, wqb, wkv, cos128, sin128)


def _flash_kernel(q_ref, k_ref, v_ref, o_ref):
    qi = pl.program_id(2)
    bq = FLASH_BQ
    nh = q_ref.shape[1]
    dv = C_V_DIM

    def attend(q, h, key0, nkeys, m, acc, causal):
        ks = k_ref[0, h, pl.ds(key0, nkeys), :]
        vs = v_ref[0, h, pl.ds(key0, nkeys), :]
        s = lax.dot_general(q, ks, (((1,), (1,)), ((), ())), preferred_element_type=F32)
        if causal:
            row = lax.broadcasted_iota(jnp.int32, s.shape, 0)
            col = lax.broadcasted_iota(jnp.int32, s.shape, 1)
            s = jnp.where(row >= col, s, NEG_INF)
        m_new = jnp.maximum(m, jnp.max(s, axis=-1, keepdims=True))
        p = jnp.exp2(s - m_new)
        acc = jnp.exp2(m - m_new) * acc + jnp.dot(p.astype(BF16), vs, preferred_element_type=F32)
        return m_new, acc

    def full_step(j, carries):
        start = pl.multiple_of(j * bq, bq)
        return tuple(attend(q_ref[0, h], h, start, bq, *carries[h], False) for h in range(nh))

    init = tuple((jnp.full((bq, 1), NEG_INF, F32), jnp.zeros((bq, v_ref.shape[3]), F32)) for _ in range(nh))
    carries = lax.fori_loop(0, qi, full_step, init)

    half = bq // 2
    diag0 = pl.multiple_of(qi * bq, bq)
    for h in range(nh):
        m, acc = attend(q_ref[0, h], h, diag0, half, *carries[h], True)
        m_lo, acc_lo = attend(q_ref[0, h, half:, :], h, diag0 + half, half, m[half:], acc[half:], True)
        cols = slice(h * dv, (h + 1) * dv)
        o_ref[0, :half, cols] = (acc[:half, :dv] / acc[:half, dv:dv + 1]).astype(o_ref.dtype)
        o_ref[0, half:, cols] = (acc_lo[:, :dv] / acc_lo[:, dv:dv + 1]).astype(o_ref.dtype)


def _flash(qc, kc, vc):
    b, hd, s, dk = qc.shape
    dv = C_V_DIM
    bq = FLASH_BQ
    nh = FLASH_HEADS
    return pl.pallas_call(
        _flash_kernel,
        grid=(b, hd // nh, s // bq),
        in_specs=[pl.BlockSpec((1, nh, bq, dk), lambda bi, h, i: (bi, h, i, 0)),
                  pl.BlockSpec((1, nh, s, dk), lambda bi, h, i: (bi, h, 0, 0), pipeline_mode=pl.Buffered(1)),
                  pl.BlockSpec((1, nh, s, vc.shape[3]), lambda bi, h, i: (bi, h, 0, 0),
                               pipeline_mode=pl.Buffered(1))],
        out_specs=pl.BlockSpec((1, bq, nh * dv), lambda bi, h, i: (bi, i, h)),
        out_shape=jax.ShapeDtypeStruct((b, s, hd * dv), BF16),
        compiler_params=_cparams(("parallel", "parallel", "arbitrary")),
        name="flash",
    )(qc, kc, vc)


def _layer_norm(z, g, b):
    mu = jnp.mean(z, axis=-1, keepdims=True)
    cen = z - mu
    var = jnp.mean(cen * cen, axis=-1, keepdims=True)
    return cen * lax.rsqrt(var + LN_EPS) * g + b


def _outproj_kernel(ya_ref, yb_ref, yc_ref, x_ref, w_ref, g_ref, b_ref, wr_ref, br_ref,
                    x1_ref, x1rc_ref, e_ref, gate_ref, rank_ref, cnt_ref, cnt_sc):
    i = pl.program_id(0)
    tm = x_ref.shape[0]
    wa, wb = ya_ref.shape[1], yb_ref.shape[1]

    @pl.when(i == 0)
    def _init():
        cnt_sc[...] = jnp.zeros(cnt_sc.shape, F32)

    mix = jnp.dot(ya_ref[...], w_ref[0:wa, :], preferred_element_type=F32)
    mix = mix + jnp.dot(yb_ref[...], w_ref[wa:wa + wb, :], preferred_element_type=F32)
    mix = mix + jnp.dot(yc_ref[...], w_ref[wa + wb:, :], preferred_element_type=F32)
    x1 = _layer_norm(DN_ALPHA * x_ref[...] + mix, g_ref[...], b_ref[...])
    x1_ref[...] = x1
    _store_rc(x1rc_ref, x1)

    logits = jnp.dot(x1.astype(BF16), wr_ref[...], preferred_element_type=F32) + br_ref[...]
    lane = lax.broadcasted_iota(jnp.int32, logits.shape, 1)
    logits = jnp.where(lane < N_EXPERTS, logits, NEG_INF)
    e_out = jnp.zeros(logits.shape, jnp.int32)
    p_out = jnp.zeros(logits.shape, F32)
    top = None
    den = None
    onehots = []
    for r in range(TOP_K):
        mx = jnp.max(logits, axis=-1, keepdims=True)
        idx = jnp.min(jnp.where(logits == mx, lane, LANES), axis=-1, keepdims=True)
        if r == 0:
            top = mx
        p = jnp.exp(mx - top)
        den = p if r == 0 else den + p
        sel = lane == idx
        onehots.append(jnp.where(sel, 1.0, 0.0).astype(F32))
        e_out = jnp.where(lane == r, idx, e_out)
        p_out = jnp.where(lane == r, p, p_out)
        logits = jnp.where(sel, NEG_INF, logits)
    e_ref[...] = e_out
    gate_ref[...] = p_out / den

    oh_sum = onehots[0] + onehots[1] + onehots[2] + onehots[3]
    row = lax.broadcasted_iota(jnp.int32, (tm, tm), 0)
    col = lax.broadcasted_iota(jnp.int32, (tm, tm), 1)
    before = jnp.where(row > col, 1.0, 0.0).astype(BF16)
    base = jnp.dot(before, oh_sum.astype(BF16), preferred_element_type=F32) + cnt_sc[0:1, :]
    rank_out = jnp.zeros(logits.shape, jnp.int32)
    for r in range(TOP_K):
        rk = jnp.sum(onehots[r] * base, axis=-1, keepdims=True)
        rank_out = jnp.where(lane == r, rk.astype(jnp.int32), rank_out)
    rank_ref[...] = rank_out
    total = cnt_sc[0:1, :] + jnp.sum(oh_sum, axis=0, keepdims=True)
    cnt_sc[0:1, :] = total
    cnt_ref[...] = jnp.broadcast_to(total, cnt_ref.shape).astype(jnp.int32)


def _outproj(ya, yb, yc, x2d, w_bf, g, b, wr128, br128):
    t, d = x2d.shape
    tm = OUT_TM
    rows = lambda a: pl.BlockSpec((tm, a.shape[1]), lambda i: (i, 0))
    full = lambda a: pl.BlockSpec(a.shape, lambda i: (0,) * a.ndim, pipeline_mode=pl.Buffered(1))
    g2, b2 = g.reshape(1, d), b.reshape(1, d)
    lane_blk = pl.BlockSpec((tm, LANES), lambda i: (i, 0))
    return pl.pallas_call(
        _outproj_kernel,
        grid=(t // tm,),
        in_specs=[rows(ya), rows(yb), rows(yc), rows(x2d), full(w_bf), full(g2), full(b2), full(wr128), full(br128)],
        out_specs=[pl.BlockSpec((tm, d), lambda i: (i, 0)),
                   pl.BlockSpec((tm * ROW_TILES, LANES), lambda i: (i, 0)),
                   lane_blk, lane_blk, lane_blk,
                   pl.BlockSpec((SUBLANES, LANES), lambda i: (0, 0))],
        out_shape=[jax.ShapeDtypeStruct((t, d), F32),
                   jax.ShapeDtypeStruct((t * ROW_TILES, LANES), U32),
                   jax.ShapeDtypeStruct((t, LANES), jnp.int32),
                   jax.ShapeDtypeStruct((t, LANES), F32),
                   jax.ShapeDtypeStruct((t, LANES), jnp.int32),
                   jax.ShapeDtypeStruct((SUBLANES, LANES), jnp.int32)],
        scratch_shapes=[pltpu.VMEM((SUBLANES, LANES), F32)],
        compiler_params=_cparams(("arbitrary",)),
        name="outproj",
    )(ya, yb, yc, x2d, w_bf, g2, b2, wr128, br128)


def _rc_rows(row, n=1):
    return pl.ds(pl.multiple_of(row * ROW_TILES, ROW_TILES), n * ROW_TILES)


def _dispatch_kernel(pos_ref, cnt_ref, pst_ref, pad_ref, nblk_ref, x_hbm, xs_hbm, buf, zbuf, sem_in, sem_out, sem_z):
    i = pl.program_id(0)
    n = pl.num_programs(0)
    tm = DSP_TM
    nslot = buf.shape[0]

    def in_copy(blk, slot):
        return pltpu.make_async_copy(x_hbm.at[_rc_rows(blk * tm, tm)], buf.at[slot], sem_in.at[slot])

    def row_out(slot, t, p):
        return pltpu.make_async_copy(buf.at[slot, _rc_rows(t)], xs_hbm.at[_rc_rows(p)], sem_out.at[slot])

    def wait_outs(slot):
        for _ in range(TOP_K):
            pltpu.make_async_copy(buf.at[slot], xs_hbm.at[_rc_rows(0, tm)], sem_out.at[slot]).wait()

    slot = lax.rem(i, nslot)

    @pl.when(i == 0)
    def _first_in():
        in_copy(0, 0).start()

    @pl.when(i >= 2)
    def _free_slot():
        wait_outs(lax.rem(i + 1, nslot))

    @pl.when(i + 1 < n)
    def _next_in():
        in_copy(i + 1, lax.rem(i + 1, nslot)).start()

    in_copy(i, slot).wait()
    base = i * (tm * TOP_K)

    def issue(t, _):
        for k in range(TOP_K):
            row_out(slot, t, pos_ref[base + t * TOP_K + k]).start(priority=k % 2)
        return 0

    lax.fori_loop(0, tm, issue, 0, unroll=2)

    @pl.when(i == n - 1)
    def _finish():
        wait_outs(lax.rem(i + 2, nslot))
        wait_outs(slot)
        zbuf[...] = jnp.zeros(zbuf.shape, zbuf.dtype)

        def pad_rows(e, _):
            cnt = cnt_ref[e]
            first = pst_ref[e] + cnt
            npad = pad_ref[e] - cnt

            def zero_row(q):
                return pltpu.make_async_copy(zbuf.at[_rc_rows(0)], xs_hbm.at[_rc_rows(first + q)], sem_z)

            def start(q, c):
                zero_row(q).start()
                return c

            def wait(q, c):
                zero_row(q).wait()
                return c

            lax.fori_loop(0, npad, start, 0)
            lax.fori_loop(0, npad, wait, 0)
            return 0

        lax.fori_loop(0, N_EXPERTS, pad_rows, 0)

        def tail_block(bk, _):
            cp = pltpu.make_async_copy(zbuf, xs_hbm.at[_rc_rows(bk * MOE_PAD, MOE_PAD)], sem_z)
            cp.start()
            cp.wait()
            return 0

        lax.fori_loop(nblk_ref[0], xs_hbm.shape[0] // (MOE_PAD * ROW_TILES), tail_block, 0)


def _dispatch(pos, counts, pad_start, padded, nblk, x1rc, rows):
    t = x1rc.shape[0] // ROW_TILES
    tm = DSP_TM
    return pl.pallas_call(
        _dispatch_kernel,
        grid_spec=pltpu.PrefetchScalarGridSpec(
            num_scalar_prefetch=5,
            grid=(t // tm,),
            in_specs=[pl.BlockSpec(memory_space=pl.ANY)],
            out_specs=pl.BlockSpec(memory_space=pl.ANY),
            scratch_shapes=[pltpu.VMEM((3, tm * ROW_TILES, LANES), U32),
                            pltpu.VMEM((MOE_PAD * ROW_TILES, LANES), U32),
                            pltpu.SemaphoreType.DMA((3,)),
                            pltpu.SemaphoreType.DMA((3,)),
                            pltpu.SemaphoreType.DMA(())]),
        out_shape=jax.ShapeDtypeStruct((rows * ROW_TILES, LANES), U32),
        compiler_params=_cparams(("arbitrary",)),
        name="dispatch",
    )(pos, counts, pad_start, padded, nblk, x1rc)


def _moe_kernel(layer, sbe_ref, sbr_ref, sbn_ref, nblk_ref, xs_hbm, wgu_hbm, wdn_hbm, bgu_ref, bd_ref, y_hbm,
                xbuf, acc, stage_in, stage_out, wg_f, wu_f, wd_f, wg_bf, wu_bf, wd_bf, sem_in, sem_out, sem_w):
    sb = pl.program_id(0)
    nsb = pl.num_programs(0)
    tf = wg_f.shape[2]
    dff = wdn_hbm.shape[2]
    nj = dff // tf
    n = sbn_ref[sb]
    row0 = sbr_ref[sb]
    nchunk = n // MOE_PAD
    cur = lax.rem(sb, 2)
    big, huge = 2 * MOE_PAD, 4 * MOE_PAD

    def weight_copies(sb_idx, j, slot):
        e = sbe_ref[sb_idx]
        cols = pl.ds(pl.multiple_of(j * tf, tf), tf)
        up_cols = pl.ds(pl.multiple_of(dff + j * tf, tf), tf)
        copies = [pltpu.make_async_copy(wdn_hbm.at[layer, e, cols, :], wd_f.at[slot], sem_w.at[slot])]
        part = wg_f.shape[1] // MOE_W_PARTS
        for p in range(MOE_W_PARTS):
            band = pl.ds(p * part, part)
            copies.append(pltpu.make_async_copy(wgu_hbm.at[layer, e, band, cols], wg_f.at[slot, band], sem_w.at[slot]))
            copies.append(pltpu.make_async_copy(wgu_hbm.at[layer, e, band, up_cols], wu_f.at[slot, band],
                                                sem_w.at[slot]))
        return copies

    def chunk_rows(c):
        return pl.ds(pl.multiple_of(c * MOE_PAD, MOE_PAD), MOE_PAD)

    def chunk_in(first_row, c):
        return pltpu.make_async_copy(xs_hbm.at[_rc_rows(first_row + c * MOE_PAD, MOE_PAD)], stage_in, sem_in)

    def chunk_out(c, slot):
        return pltpu.make_async_copy(stage_out.at[slot], y_hbm.at[_rc_rows(row0 + c * MOE_PAD, MOE_PAD)],
                                     sem_out.at[slot])

    def wait_outs(count):
        @pl.when(count >= 2)
        def _older():
            chunk_out(0, lax.rem(count, 2)).wait()

        @pl.when(count >= 1)
        def _newest():
            chunk_out(0, lax.rem(count + 1, 2)).wait()

    def convert_in(slot, c):
        lo, hi = _load_rc(stage_in, MOE_PAD)
        xbuf[slot, chunk_rows(c), :] = jnp.concatenate([lo.astype(BF16), hi.astype(BF16)], axis=-1)

    @pl.when(sb == 0)
    def _load_first():
        for cp in weight_copies(0, 0, 0):
            cp.start()
        acc[...] = jnp.zeros(acc.shape, acc.dtype)

        def load(c, _):
            chunk_in(row0, c).start()
            chunk_in(row0, c).wait()
            convert_in(0, c)
            return 0

        lax.fori_loop(0, nchunk, load, 0)

    n_next = sbn_ref[sb + 1]

    def ffn(j, start, size, wg, wu, wd):
        rs = pl.ds(start, size)
        xr = xbuf[cur, rs, :]
        cols = pl.ds(pl.multiple_of(j * tf, tf), tf)
        up_cols = pl.ds(pl.multiple_of(dff + j * tf, tf), tf)
        gt = jnp.dot(xr, wg, preferred_element_type=F32) + bgu_ref[:, cols]
        up = jnp.dot(xr, wu, preferred_element_type=F32) + bgu_ref[:, up_cols]
        gt = jnp.minimum(gt, SWIGLU_LIMIT)
        up = jnp.clip(up, -SWIGLU_LIMIT, SWIGLU_LIMIT)
        act = ((up + 1.0) * gt * jax.nn.sigmoid(SWIGLU_ALPHA * gt)).astype(BF16)
        half = acc.shape[1] // 2
        for cs in (slice(0, half), slice(half, 2 * half)):
            prev = jnp.where(j == 0, bd_ref[:, cs], acc[rs, cs])
            acc[rs, cs] = prev + jnp.dot(act, wd[:, cs], preferred_element_type=F32)

    def first_chunk(j, wslot, size):
        wg = wg_f[wslot].astype(BF16)
        wu = wu_f[wslot].astype(BF16)
        wd = wd_f[wslot].astype(BF16)
        wg_bf[...] = wg
        wu_bf[...] = wu
        wd_bf[...] = wd
        ffn(j, 0, size, wg, wu, wd)

    def hidden_tile(j, _):
        wslot = lax.rem(j, 2)
        for cp in weight_copies(sb, j, wslot):
            cp.wait()

        @pl.when(j + 1 < nj)
        def _next_tile():
            for cp in weight_copies(sb, j + 1, 1 - wslot):
                cp.start()

        @pl.when((j + 1 == nj) & (n_next > 0))
        def _next_super_block():
            for cp in weight_copies(sb + 1, 0, 0):
                cp.start()

        fetch_next = j * MOE_PAD < n_next

        @pl.when(fetch_next)
        def _start_next():
            chunk_in(sbr_ref[sb + 1], j).start()

        @pl.when((j == 1) & (sb > 0))
        def _drain_prev_outs():
            wait_outs(sbn_ref[jnp.maximum(sb - 1, 0)] // MOE_PAD)

        def rest(start, size):
            ffn(j, start, size, wg_bf[...], wu_bf[...], wd_bf[...])

        @pl.when(n >= huge)
        def _rows_huge():
            first_chunk(j, wslot, huge)

            @pl.when(n - huge >= big)
            def _then_big():
                rest(huge, big)

            @pl.when(lax.rem(n, big) != 0)
            def _then_pad():
                rest(pl.multiple_of(n - MOE_PAD, MOE_PAD), MOE_PAD)

        @pl.when((n >= big) & (n < huge))
        def _rows_big():
            first_chunk(j, wslot, big)

            @pl.when(n > big)
            def _then_pad():
                rest(big, MOE_PAD)

        @pl.when(n < big)
        def _rows_small():
            first_chunk(j, wslot, MOE_PAD)

        @pl.when(j == nj - 1)
        def _store_rows():
            def store(c, _):
                slot = lax.rem(c, 2)

                @pl.when(c >= 2)
                def _slot_free():
                    chunk_out(c - 2, slot).wait()

                _store_rc(stage_out.at[slot], acc[chunk_rows(c), :])
                chunk_out(c, slot).start()
                return 0

            lax.fori_loop(0, nchunk, store, 0)

            @pl.when((sb == nsb - 1) | (n_next == 0))
            def _no_later_tile():
                wait_outs(nchunk)

        @pl.when(fetch_next)
        def _finish_next():
            chunk_in(sbr_ref[sb + 1], j).wait()
            convert_in(1 - cur, j)

        return 0

    @pl.when(n > 0)
    def _super_block():
        lax.fori_loop(0, nj, hidden_tile, 0)

    @pl.when(sb == nsb - 1)
    def _zero_tail():
        stage_out[0] = jnp.zeros(stage_out.shape[1:], stage_out.dtype)

        def tail_block(bk, _):
            cp = pltpu.make_async_copy(stage_out.at[0], y_hbm.at[_rc_rows(bk * MOE_PAD, MOE_PAD)], sem_out.at[0])
            cp.start()
            cp.wait()
            return 0

        lax.fori_loop(nblk_ref[0], y_hbm.shape[0] // (MOE_PAD * ROW_TILES), tail_block, 0)


def _moe(layer, sb_e, sb_row0, sb_n, nblk, xs, w_gate_up, b_gate_up, w_down, b_down):
    rows = xs.shape[0] // ROW_TILES
    d = D_MODEL
    dff = w_down.shape[2]
    tf = MOE_TF
    nj = dff // tf
    nsb = sb_e.shape[0] - 1
    assert nj % 2 == 0 and MOE_TMAX // MOE_PAD <= nj and MOE_TMAX == 6 * MOE_PAD
    bgu = b_gate_up.reshape(DEPTH, N_EXPERTS, 1, 2 * dff)
    bdn = b_down.reshape(DEPTH, N_EXPERTS, 1, d)
    expert_row = lambda width: pl.BlockSpec((None, None, 1, width), lambda sb, e, r, n, nb: (layer, e[sb], 0, 0))
    return pl.pallas_call(
        functools.partial(_moe_kernel, layer),
        grid_spec=pltpu.PrefetchScalarGridSpec(
            num_scalar_prefetch=4,
            grid=(nsb,),
            in_specs=[pl.BlockSpec(memory_space=pl.ANY),
                      pl.BlockSpec(memory_space=pl.ANY),
                      pl.BlockSpec(memory_space=pl.ANY),
                      expert_row(2 * dff),
                      expert_row(d)],
            out_specs=pl.BlockSpec(memory_space=pl.ANY),
            scratch_shapes=[pltpu.VMEM((2, MOE_TMAX, d), BF16),
                            pltpu.VMEM((MOE_TMAX, d), F32),
                            pltpu.VMEM((MOE_PAD * ROW_TILES, LANES), U32),
                            pltpu.VMEM((2, MOE_PAD * ROW_TILES, LANES), U32),
                            pltpu.VMEM((2, d, tf), F32),
                            pltpu.VMEM((2, d, tf), F32),
                            pltpu.VMEM((2, tf, d), F32),
                            pltpu.VMEM((d, tf), BF16),
                            pltpu.VMEM((d, tf), BF16),
                            pltpu.VMEM((tf, d), BF16),
                            pltpu.SemaphoreType.DMA(()),
                            pltpu.SemaphoreType.DMA((2,)),
                            pltpu.SemaphoreType.DMA((2,))]),
        out_shape=jax.ShapeDtypeStruct((rows * ROW_TILES, LANES), U32),
        compiler_params=_cparams(("arbitrary",)),
        name="moe",
    )(sb_e, sb_row0, sb_n, nblk, xs, w_gate_up, w_down, bgu, bdn)


def _combine_kernel(pos_ref, y_hbm, gate_ref, x1_ref, g_ref, b_ref, o_ref, buf, sem):
    i = pl.program_id(0)
    n = pl.num_programs(0)
    tm = CMB_TM
    slot = lax.rem(i, 2)

    def issue(blk, dst_slot):
        base = blk * (tm * TOP_K)

        def body(t, _):
            for k in range(TOP_K):
                p = pos_ref[base + t * TOP_K + k]
                pltpu.make_async_copy(y_hbm.at[_rc_rows(p)], buf.at[dst_slot, k, _rc_rows(t)],
                                      sem.at[dst_slot]).start(priority=k % 2)
            return 0

        lax.fori_loop(0, tm, body, 0, unroll=2)

    @pl.when(i == 0)
    def _first():
        issue(0, 0)

    @pl.when(i + 1 < n)
    def _next():
        issue(i + 1, 1 - slot)

    for k in range(TOP_K):
        pltpu.make_async_copy(y_hbm.at[_rc_rows(0, tm)], buf.at[slot, k], sem.at[slot]).wait()
    gate = gate_ref[...]
    ffn_lo, ffn_hi = None, None
    for k in range(TOP_K):
        lo, hi = _load_rc(buf.at[slot, k], tm)
        gk = gate[:, k:k + 1]
        ffn_lo = gk * lo if k == 0 else ffn_lo + gk * lo
        ffn_hi = gk * hi if k == 0 else ffn_hi + gk * hi
    ffn = jnp.concatenate([ffn_lo, ffn_hi], axis=-1)
    o_ref[...] = _layer_norm(DN_ALPHA * x1_ref[...] + ffn, g_ref[...], b_ref[...])


def _combine(pos_flat, y, gate, x1, g, b):
    t, d = x1.shape
    tm = CMB_TM
    g2, b2 = g.reshape(1, d), b.reshape(1, d)
    return pl.pallas_call(
        _combine_kernel,
        grid_spec=pltpu.PrefetchScalarGridSpec(
            num_scalar_prefetch=1,
            grid=(t // tm,),
            in_specs=[pl.BlockSpec(memory_space=pl.ANY),
                      pl.BlockSpec((tm, LANES), lambda i, p: (i, 0)),
                      pl.BlockSpec((tm, d), lambda i, p: (i, 0)),
                      pl.BlockSpec((1, d), lambda i, p: (0, 0)),
                      pl.BlockSpec((1, d), lambda i, p: (0, 0))],
            out_specs=pl.BlockSpec((tm, d), lambda i, p: (i, 0)),
            scratch_shapes=[pltpu.VMEM((2, TOP_K, tm * ROW_TILES, LANES), U32), pltpu.SemaphoreType.DMA((2,))]),
        out_shape=jax.ShapeDtypeStruct((t, d), F32),
        compiler_params=_cparams(("arbitrary",)),
        name="combine",
    )(pos_flat, y, gate, x1, g2, b2)


def _swap_halves(w):
    half = w.shape[-1] // 2
    return jnp.concatenate([w[..., half:], w[..., :half]], axis=-1)


def _w_in_columns(w):
    d = w.shape[0]
    widths = (512, 128, 128, 256, 256, 512, 512, 8, 512, 256, 64)
    offs = [0]
    for wd in widths:
        offs.append(offs[-1] + wd)
    a_q, a_k, a_v, m_q, m_k, m_v, m_o, m_if, c_q, c_kv, c_kr = [w[:, offs[i]:offs[i + 1]] for i in range(len(widths))]
    z = lambda n: jnp.zeros((d, n), w.dtype)
    return jnp.concatenate([a_q, m_v, m_o, c_q, m_q, m_k, c_kv, a_k, a_v,
                            c_kr, z(LANES - 64), m_if, z(LANES - 8), _swap_halves(c_kr), z(LANES - 64)], axis=1)


def _w_in_layout_kernel(w_ref, o_ref):
    o_ref[...] = _w_in_columns(w_ref[...]).astype(o_ref.dtype)


def _layout_w_in(w_in, layer):
    _, d, n = w_in.shape
    tk = 256
    return pl.pallas_call(
        _w_in_layout_kernel,
        grid=(d // tk,),
        in_specs=[pl.BlockSpec((None, tk, n), lambda i: (layer, i, 0))],
        out_specs=pl.BlockSpec((tk, N_PROJ), lambda i: (i, 0)),
        out_shape=jax.ShapeDtypeStruct((d, N_PROJ), BF16),
        compiler_params=_cparams(("parallel",)),
        name="w_in_layout",
    )(w_in)


def _layout_w_uq(w):
    r = w.shape[0]
    w3 = w.astype(BF16).reshape(r, C_HEADS, C_NOPE_DIM + C_ROPE_DIM)
    nope, rope = w3[..., :C_NOPE_DIM], w3[..., C_NOPE_DIM:]
    z = jnp.zeros((r, C_HEADS, LANES - C_ROPE_DIM), BF16)
    wa = jnp.concatenate([nope, rope, z], axis=-1).reshape(r, -1)
    wb = jnp.concatenate([_swap_halves(rope), z], axis=-1).reshape(r, -1)
    return wa, wb


def _layout_w_ukv(w):
    r = w.shape[0]
    w3 = w.astype(BF16).reshape(r, C_HEADS, C_NOPE_DIM + C_V_DIM)
    return jnp.concatenate([w3[..., :C_NOPE_DIM].reshape(r, -1), w3[..., C_NOPE_DIM:].reshape(r, -1)], axis=-1)


def _rope_tables(seq):
    dim = C_ROPE_DIM
    inv = 1.0 / (ROPE_THETA ** (jnp.arange(0, dim, 2, dtype=F32) / dim))
    ang = jnp.arange(seq, dtype=F32)[:, None] * inv[None, :]
    cos, sin = jnp.cos(ang), jnp.sin(ang)
    z = jnp.zeros((seq, LANES - dim), F32)
    return jnp.concatenate([cos, cos, z], axis=-1), jnp.concatenate([-sin, sin, z], axis=-1)


def _route(top_e, rank, counts, nsb):
    padded = (counts + MOE_PAD - 1) // MOE_PAD * MOE_PAD
    pad_end = jnp.cumsum(padded)
    pad_start = pad_end - padded
    is_e = top_e[:, :, None] == jnp.arange(N_EXPERTS, dtype=jnp.int32)[None, None, :]
    pos = (jnp.sum(jnp.where(is_e, pad_start[None, None, :], 0), axis=-1) + rank).reshape(-1).astype(jnp.int32)
    nblk = (pad_end[-1:] // MOE_PAD).astype(jnp.int32)
    nsb_e = (padded + MOE_TMAX - 1) // MOE_TMAX
    sb_end = jnp.cumsum(nsb_e)
    sb_start = sb_end - nsb_e
    sb_idx = jnp.arange(nsb + 1, dtype=jnp.int32)
    n_valid = sb_end[-1]
    sb_eff = jnp.minimum(sb_idx, n_valid - 1)
    sb_e = jnp.searchsorted(sb_end, sb_eff, side="right").astype(jnp.int32)
    part = sb_eff - sb_start[sb_e]
    sb_row0 = pad_start[sb_e] + part * MOE_TMAX
    sb_n = jnp.where(sb_idx < n_valid, jnp.minimum(padded[sb_e] - part * MOE_TMAX, MOE_TMAX), 0)
    tables = (counts, pad_start.astype(jnp.int32), padded.astype(jnp.int32), nblk)
    return pos, tables, sb_e, sb_row0.astype(jnp.int32), sb_n.astype(jnp.int32)


def kernel(x, w_in, conv_w, conv_b, m_gate_b, m_norm_g, sinks, q_norm_g, w_uq, kv_norm_g, w_ukv, w_out,
           ln1_g, ln1_b, w_router, b_router, w_gate_up, b_gate_up, w_down, b_down, ln2_g, ln2_b):
    b, s, d = x.shape
    t = b * s
    n_assign = t * TOP_K
    rows = (n_assign // MOE_PAD + N_EXPERTS) * MOE_PAD
    nsb = N_EXPERTS + rows // MOE_TMAX
    cos128, sin128 = _rope_tables(s)
    xt = x.reshape(t, d)
    for l in range(DEPTH):
        w_in_bf = _layout_w_in(w_in, l)
        wqa, wqb = _layout_w_uq(w_uq[l])
        wkv = _layout_w_ukv(w_ukv[l])
        w_out_bf = w_out[l].astype(BF16)
        wr128 = jnp.zeros((d, LANES), BF16).at[:, :N_EXPERTS].set(w_router[l].astype(BF16))
        br128 = jnp.zeros((1, LANES), F32).at[0, :N_EXPERTS].set(b_router[l])

        proj, gates = _proj(xt, w_in_bf)
        proj3 = proj.reshape(b, s, N_PROJ)
        y_a = _swa(proj3, sinks[l])
        y_b = _mlstm(proj3, gates.reshape(b, s, LANES), conv_w[l], conv_b[l], m_gate_b[l], m_norm_g[l])
        qc, kc, vc = _mla_prep(proj3, q_norm_g[l], kv_norm_g[l], wqa, wqb, wkv, cos128, sin128)
        y_c = _flash(qc, kc, vc)
        x1, x1rc, top_e, gate, rank, cnt = _outproj(y_a.reshape(t, -1), y_b.reshape(t, -1), y_c.reshape(t, -1), xt,
                                                    w_out_bf, ln1_g[l], ln1_b[l], wr128, br128)
        pos, tables, sb_e, sb_row0, sb_n = _route(top_e[:, :TOP_K], rank[:, :TOP_K], cnt[0, :N_EXPERTS], nsb)
        xs = _dispatch(pos, *tables, x1rc, rows)
        y = _moe(l, sb_e, sb_row0, sb_n, tables[3], xs, w_gate_up, b_gate_up, w_down, b_down)
        xt = _combine(pos, y, gate, x1, ln2_g[l], ln2_b[l])
    return xt.reshape(b, s, d)
```

```python
import functools

import jax
import jax.numpy as jnp
from jax import lax
from jax.experimental import pallas as pl
from jax.experimental.pallas import tpu as pltpu

F32 = jnp.float32
BF16 = jnp.bfloat16
NEG_INF = float("-inf")
LOG2_E = 1.4426950408889634

D_MODEL = 2048
DEPTH = 2
SWA_HEADS, SWA_KV_HEADS, SWA_HEAD_DIM, SWA_WINDOW = 8, 2, 64, 128
M_HEADS, M_QK_DIM, M_V_DIM, M_CONV = 4, 64, 128, 4
C_HEADS, C_NOPE_DIM, C_ROPE_DIM, C_V_DIM = 8, 128, 64, 128
C_Q_LORA, C_KV_LORA = 512, 256
ROPE_THETA = 10000.0
N_EXPERTS, TOP_K = 32, 4
SWIGLU_LIMIT, SWIGLU_ALPHA = 7.0, 1.702
DN_ALPHA = (2 * DEPTH) ** 0.25
LN_EPS, RMS_EPS = 1e-5, 1e-6

LANES = 128
SUBLANES = 8
VMEM_LIMIT = 56 * 1024 * 1024

P_AQ, P_MV, P_MO, P_CQ, P_MQK, P_CKV, P_AK, P_AV, P_CKR, P_MIF, P_CKRS = (
    0, 512, 1024, 1536, 2048, 2560, 2816, 2944, 3072, 3200, 3328)
N_PROJ = 3456

PROJ_TM, PROJ_TN = 1024, 1152
M_CHUNK = 256
MLA_TM = 512
FLASH_BQ = 1024
FLASH_HEADS = 4
OUT_TM = 512
MOE_PAD = 256
MOE_TMAX = 1536
MOE_TF = 256
MOE_W_PARTS = 4
CMB_TM = 256
DSP_TM = 512

ROW_TILES = D_MODEL // 2 // LANES
U32 = jnp.uint32
HI_MASK = 0xFFFF0000


def _cparams(sem, vmem=VMEM_LIMIT):
    return pltpu.CompilerParams(dimension_semantics=sem, vmem_limit_bytes=vmem)


def _bf16_bits(x):
    return lax.bitcast_convert_type(x.astype(BF16).astype(F32), U32)


def _store_rc(ref, val):
    n, d = val.shape
    words = (_bf16_bits(val[:, :d // 2]) >> 16) | (_bf16_bits(val[:, d // 2:]) & U32(HI_MASK))
    x = jnp.stack([words[:, s * LANES:(s + 1) * LANES] for s in range(ROW_TILES)], axis=0)
    ref[...] = pltpu.einshape("stl->tsl", x).reshape(n * ROW_TILES, LANES)


def _load_rc(ref, n):
    x = pltpu.einshape("tsl->stl", ref[...].reshape(n, ROW_TILES, LANES))
    words = jnp.concatenate([x[s] for s in range(ROW_TILES)], axis=-1)
    lo = lax.bitcast_convert_type(words << 16, F32)
    hi = lax.bitcast_convert_type(words & U32(HI_MASK), F32)
    return lo, hi


def _proj_kernel(x_ref, w_ref, o_ref, g_ref):
    xb = x_ref[...].astype(BF16)
    for jj in range(w_ref.shape[1] // PROJ_TN):
        cols = slice(jj * PROJ_TN, (jj + 1) * PROJ_TN)
        acc = jnp.dot(xb, w_ref[:, cols], preferred_element_type=F32)
        o_ref[:, cols] = acc.astype(o_ref.dtype)
        if jj == P_MIF // PROJ_TN:
            off = P_MIF % PROJ_TN
            g_ref[...] = acc[:, off:off + LANES]


def _proj(x2d, w_bf):
    t, d = x2d.shape
    n = w_bf.shape[1]
    return pl.pallas_call(
        _proj_kernel,
        grid=(t // PROJ_TM,),
        in_specs=[pl.BlockSpec((PROJ_TM, d), lambda i: (i, 0)),
                  pl.BlockSpec((d, n), lambda i: (0, 0), pipeline_mode=pl.Buffered(1))],
        out_specs=[pl.BlockSpec((PROJ_TM, n), lambda i: (i, 0)),
                   pl.BlockSpec((PROJ_TM, LANES), lambda i: (i, 0))],
        out_shape=[jax.ShapeDtypeStruct((t, n), BF16), jax.ShapeDtypeStruct((t, LANES), F32)],
        compiler_params=_cparams(("parallel",)),
        name="proj",
    )(x2d, w_bf)


def _swa_kernel(sinks_ref, q_ref, kc_ref, kp_ref, vc_ref, vp_ref, o_ref):
    i = pl.program_id(0)
    w = SWA_WINDOW
    dh = SWA_HEAD_DIM
    grp = SWA_HEADS // SWA_KV_HEADS
    qi = lax.broadcasted_iota(jnp.int32, (w, 2 * w), 0)
    kj = lax.broadcasted_iota(jnp.int32, (w, 2 * w), 1)
    rel = qi + w - kj
    has_prev = jnp.where(i > 0, 0, w)
    mask = (rel >= 0) & (rel < w) & (kj >= has_prev)
    for b in range(q_ref.shape[0]):
        q = q_ref[b]
        outs = []
        for kvh in range(SWA_KV_HEADS):
            sl = slice(kvh * dh, (kvh + 1) * dh)
            k_cat = jnp.concatenate([kp_ref[b][:, sl], kc_ref[b][:, sl]], axis=0).astype(BF16)
            v_cat = jnp.concatenate([vp_ref[b][:, sl], vc_ref[b][:, sl]], axis=0).astype(BF16)
            for g in range(grp):
                h = kvh * grp + g
                qh = (q[:, h * dh:(h + 1) * dh] * (dh ** -0.5)).astype(BF16)
                s = lax.dot_general(qh, k_cat, (((1,), (1,)), ((), ())), preferred_element_type=F32)
                s = jnp.where(mask, s, NEG_INF)
                sink = sinks_ref[h]
                m = jnp.maximum(jnp.max(s, axis=-1, keepdims=True), sink)
                p = jnp.exp(s - m)
                den = jnp.sum(p, axis=-1, keepdims=True) + jnp.exp(sink - m)
                o = jnp.dot(p.astype(BF16), v_cat, preferred_element_type=F32)
                outs.append(o / den)
        o_ref[b] = jnp.concatenate(outs, axis=-1).astype(o_ref.dtype)


def _swa(proj3, sinks):
    b, s, _ = proj3.shape
    w = SWA_WINDOW
    nb = s // w
    kw = SWA_KV_HEADS * SWA_HEAD_DIM
    qw = SWA_HEADS * SWA_HEAD_DIM
    cur = lambda col: (lambda i: (0, i, col))
    prev = lambda col: (lambda i: (0, jnp.maximum(i - 1, 0), col))
    return pl.pallas_call(
        _swa_kernel,
        grid=(nb,),
        in_specs=[pl.BlockSpec(memory_space=pltpu.SMEM),
                  pl.BlockSpec((b, w, qw), cur(P_AQ // qw)),
                  pl.BlockSpec((b, w, kw), cur(P_AK // kw)),
                  pl.BlockSpec((b, w, kw), prev(P_AK // kw)),
                  pl.BlockSpec((b, w, kw), cur(P_AV // kw)),
                  pl.BlockSpec((b, w, kw), prev(P_AV // kw))],
        out_specs=pl.BlockSpec((b, w, qw), lambda i: (0, i, 0)),
        out_shape=jax.ShapeDtypeStruct((b, s, qw), BF16),
        compiler_params=_cparams(("arbitrary",)),
        name="swa",
    )(sinks, proj3, proj3, proj3, proj3, proj3)


def _mlstm_kernel(qk_ref, v_ref, og_ref, g_ref, cw_ref, cb_ref, gb_ref, ng_ref, y_ref, xbuf, ct_ref, m_ref):
    c = pl.program_id(0)
    nb = qk_ref.shape[0]
    ln = M_CHUNK
    dqk, dv, nh = M_QK_DIM, M_V_DIM, M_HEADS
    halo = SUBLANES

    @pl.when(c == 0)
    def _init():
        xbuf[:, 0:halo, :] = jnp.zeros((nb, halo, xbuf.shape[2]), F32)
        ct_ref[...] = jnp.zeros(ct_ref.shape, F32)
        m_ref[...] = jnp.zeros(m_ref.shape, F32)

    row = lax.broadcasted_iota(jnp.int32, (ln, ln), 0)
    col = lax.broadcasted_iota(jnp.int32, (ln, ln), 1)
    tril = row >= col
    trilf = jnp.where(tril, 1.0, 0.0).astype(F32)
    ones_blk = jnp.where(lax.broadcasted_iota(jnp.int32, (ln, dv), 1) == 0, 1.0, 0.0).astype(F32)

    for b in range(nb):
        xbuf[b, halo:halo + ln, :] = qk_ref[b].astype(F32)
        conv = cb_ref[...]
        for j in range(M_CONV):
            off = halo - (M_CONV - 1) + j
            conv = conv + cw_ref[j:j + 1, :] * xbuf[b, off:off + ln, :]
        qk = conv * jax.nn.sigmoid(conv)
        xbuf[b, 0:halo, :] = xbuf[b, ln:ln + halo, :]

        g = g_ref[b] + gb_ref[...]
        lf = jnp.minimum(g, 0.0) - jnp.log1p(jnp.exp(-jnp.abs(g)))
        cum = jnp.dot(trilf, lf, preferred_element_type=F32, precision=lax.Precision.HIGHEST)
        cum_t = cum.T
        g_t = g.T
        vv = v_ref[b].astype(F32)
        hs = []
        for h in range(nh):
            chain = b * nh + h
            q = (qk[:, h * dqk:(h + 1) * dqk] * (dqk ** -0.5)).astype(BF16)
            k = qk[:, nh * dqk + h * dqk: nh * dqk + (h + 1) * dqk].astype(BF16)
            v_ext = jnp.concatenate([vv[:, h * dv:(h + 1) * dv], ones_blk], axis=-1)
            bc_col = cum[:, nh + h:nh + h + 1]
            bc_row = cum_t[nh + h:nh + h + 1, :]
            i_col = g[:, h:h + 1]
            i_row = g_t[h:h + 1, :]
            m_prev = m_ref[chain:chain + 1, 0:1]
            dmat = jnp.where(tril, bc_col - bc_row + i_row, NEG_INF)
            m_inter = bc_col + m_prev
            m_j = jnp.maximum(m_inter, jnp.max(dmat, axis=-1, keepdims=True))
            w_intra = jnp.exp(dmat - m_j)
            w_inter = jnp.exp(m_inter - m_j)
            s = lax.dot_general(q, k, (((1,), (1,)), ((), ())), preferred_element_type=F32) * w_intra
            ct = ct_ref[chain]
            num_ext = (jnp.dot(s.astype(BF16), v_ext.astype(BF16), preferred_element_type=F32)
                       + w_inter * jnp.dot(q, ct.astype(BF16), preferred_element_type=F32))
            num = num_ext[:, :dv]
            nq = num_ext[:, dv:dv + 1]
            den = jnp.maximum(jnp.abs(nq), jnp.exp(-m_j))
            hs.append(num / den)
            m_new = m_j[ln - 1:ln, :]
            bc_last = bc_col[ln - 1:ln, :]
            w_s = jnp.exp(bc_last - bc_col + i_col - m_new)
            w_c = jnp.exp(bc_last + m_prev - m_new)
            upd = lax.dot_general(k, (w_s * v_ext).astype(BF16), (((0,), (0,)), ((), ())),
                                  preferred_element_type=F32)
            ct_ref[chain] = w_c * ct + upd
            m_ref[chain:chain + 1, :] = jnp.broadcast_to(m_new, (1, m_ref.shape[1]))
        og = og_ref[b].astype(F32)
        outs = []
        for h in range(nh):
            seg = jax.nn.sigmoid(og[:, h * dv:(h + 1) * dv]) * hs[h]
            mu = jnp.mean(seg, axis=-1, keepdims=True)
            cen = seg - mu
            var = jnp.mean(cen * cen, axis=-1, keepdims=True)
            outs.append(cen * lax.rsqrt(var + LN_EPS) * ng_ref[:, h * dv:(h + 1) * dv])
        y_ref[b] = jnp.concatenate(outs, axis=-1).astype(y_ref.dtype)


def _mlstm(proj3, gates3, conv_w, conv_b, gate_b, norm_g):
    b, s, _ = proj3.shape
    ln = M_CHUNK
    wq = 2 * M_HEADS * M_QK_DIM
    wv = M_HEADS * M_V_DIM
    blk = lambda width, off: pl.BlockSpec((b, ln, width), lambda c: (0, c, off // width))
    full = lambda a: pl.BlockSpec(a.shape, lambda c: (0,) * a.ndim)
    gate_b128 = jnp.zeros((1, LANES), F32).at[0, :2 * M_HEADS].set(gate_b)
    conv_b2 = conv_b.reshape(1, wq)
    norm_g2 = norm_g.reshape(1, wv)
    return pl.pallas_call(
        _mlstm_kernel,
        grid=(s // ln,),
        in_specs=[blk(wq, P_MQK), blk(wv, P_MV), blk(wv, P_MO), blk(LANES, 0),
                  full(conv_w), full(conv_b2), full(gate_b128), full(norm_g2)],
        out_specs=pl.BlockSpec((b, ln, wv), lambda c: (0, c, 0)),
        out_shape=jax.ShapeDtypeStruct((b, s, wv), BF16),
        scratch_shapes=[pltpu.VMEM((b, SUBLANES + ln, wq), F32),
                        pltpu.VMEM((b * M_HEADS, M_QK_DIM, 2 * M_V_DIM), F32),
                        pltpu.VMEM((b * M_HEADS, LANES), F32)],
        compiler_params=_cparams(("arbitrary",)),
        name="mlstm",
    )(proj3, proj3, proj3, gates3, conv_w, conv_b2, gate_b128, norm_g2)


def _mla_prep_kernel(cq_ref, ckv_ref, kr_ref, krs_ref, qg_ref, kvg_ref, wqa_ref, wqb_ref, wkv_ref,
                     cos_ref, sin_ref, q_out, k_out, v_out):
    nd, hd = C_NOPE_DIM, C_HEADS
    scale = (C_NOPE_DIM + C_ROPE_DIM) ** -0.5 * LOG2_E
    cos = cos_ref[...]
    sin = sin_ref[...]

    cq = cq_ref[0].astype(F32)
    qn = (cq * lax.rsqrt(jnp.mean(cq * cq, axis=-1, keepdims=True) + RMS_EPS) * qg_ref[...]).astype(BF16)
    qa = jnp.dot(qn, wqa_ref[...], preferred_element_type=F32)
    qb = jnp.dot(qn, wqb_ref[...], preferred_element_type=F32)
    for h in range(hd):
        nope = qa[:, h * 2 * nd: h * 2 * nd + nd]
        rope = qa[:, h * 2 * nd + nd:(h + 1) * 2 * nd] * cos + qb[:, h * nd:(h + 1) * nd] * sin
        q_out[0, h] = (jnp.concatenate([nope, rope], axis=-1) * scale).astype(q_out.dtype)

    ckv = ckv_ref[0].astype(F32)
    kvn = (ckv * lax.rsqrt(jnp.mean(ckv * ckv, axis=-1, keepdims=True) + RMS_EPS) * kvg_ref[...]).astype(BF16)
    kv = jnp.dot(kvn, wkv_ref[...], preferred_element_type=F32)
    kr = kr_ref[0].astype(F32) * cos + krs_ref[0].astype(F32) * sin
    ones_blk = jnp.where(lax.broadcasted_iota(jnp.int32, kr.shape, 1) == 0, 1.0, 0.0).astype(F32)
    for h in range(hd):
        k_out[0, h] = jnp.concatenate([kv[:, h * nd:(h + 1) * nd], kr], axis=-1).astype(k_out.dtype)
        v_h = kv[:, hd * nd + h * nd: hd * nd + (h + 1) * nd]
        v_out[0, h] = jnp.concatenate([v_h, ones_blk], axis=-1).astype(v_out.dtype)


def _mla_prep(proj3, q_norm_g, kv_norm_g, wqa, wqb, wkv, cos128, sin128):
    b, s, _ = proj3.shape
    tm = MLA_TM
    hd, nd = C_HEADS, C_NOPE_DIM
    blk = lambda width, off: pl.BlockSpec((1, tm, width), lambda bi, i: (bi, i, off // width))
    full = lambda a: pl.BlockSpec(a.shape, lambda bi, i: (0,) * a.ndim)
    tab = pl.BlockSpec((tm, LANES), lambda bi, i: (i, 0))
    qg = q_norm_g.reshape(1, -1)
    kvg = kv_norm_g.reshape(1, -1)
    head_out = lambda width: pl.BlockSpec((1, hd, tm, width), lambda bi, i: (bi, 0, i, 0))
    return pl.pallas_call(
        _mla_prep_kernel,
        grid=(b, s // tm),
        in_specs=[blk(C_Q_LORA, P_CQ), blk(C_KV_LORA, P_CKV), blk(LANES, P_CKR), blk(LANES, P_CKRS),
                  full(qg), full(kvg), full(wqa), full(wqb), full(wkv), tab, tab],
        out_specs=[head_out(2 * nd), head_out(2 * nd), head_out(C_V_DIM + LANES)],
        out_shape=[jax.ShapeDtypeStruct((b, hd, s, 2 * nd), BF16),
                   jax.ShapeDtypeStruct((b, hd, s, 2 * nd), BF16),
                   jax.ShapeDtypeStruct((b, hd, s, C_V_DIM + LANES), BF16)],
        compiler_params=_cparams(("parallel", "arbitrary")),
        name="mla_prep",
    )(proj3, proj3, proj3, proj3, qg, kvg, wqa, wqb, wkv, cos128, sin128)


def _flash_kernel(q_ref, k_ref, v_ref, o_ref):
    qi = pl.program_id(2)
    bq = FLASH_BQ
    nh = q_ref.shape[1]
    dv = C_V_DIM

    def attend(q, h, key0, nkeys, m, acc, causal):
        ks = k_ref[0, h, pl.ds(key0, nkeys), :]
        vs = v_ref[0, h, pl.ds(key0, nkeys), :]
        s = lax.dot_general(q, ks, (((1,), (1,)), ((), ())), preferred_element_type=F32)
        if causal:
            row = lax.broadcasted_iota(jnp.int32, s.shape, 0)
            col = lax.broadcasted_iota(jnp.int32, s.shape, 1)
            s = jnp.where(row >= col, s, NEG_INF)
        m_new = jnp.maximum(m, jnp.max(s, axis=-1, keepdims=True))
        p = jnp.exp2(s - m_new)
        acc = jnp.exp2(m - m_new) * acc + jnp.dot(p.astype(BF16), vs, preferred_element_type=F32)
        return m_new, acc

    def full_step(j, carries):
        start = pl.multiple_of(j * bq, bq)
        return tuple(attend(q_ref[0, h], h, start, bq, *carries[h], False) for h in range(nh))

    init = tuple((jnp.full((bq, 1), NEG_INF, F32), jnp.zeros((bq, v_ref.shape[3]), F32)) for _ in range(nh))
    carries = lax.fori_loop(0, qi, full_step, init)

    half = bq // 2
    diag0 = pl.multiple_of(qi * bq, bq)
    for h in range(nh):
        m, acc = attend(q_ref[0, h], h, diag0, half, *carries[h], True)
        m_lo, acc_lo = attend(q_ref[0, h, half:, :], h, diag0 + half, half, m[half:], acc[half:], True)
        cols = slice(h * dv, (h + 1) * dv)
        o_ref[0, :half, cols] = (acc[:half, :dv] / acc[:half, dv:dv + 1]).astype(o_ref.dtype)
        o_ref[0, half:, cols] = (acc_lo[:, :dv] / acc_lo[:, dv:dv + 1]).astype(o_ref.dtype)


def _flash(qc, kc, vc):
    b, hd, s, dk = qc.shape
    dv = C_V_DIM
    bq = FLASH_BQ
    nh = FLASH_HEADS
    return pl.pallas_call(
        _flash_kernel,
        grid=(b, hd // nh, s // bq),
        in_specs=[pl.BlockSpec((1, nh, bq, dk), lambda bi, h, i: (bi, h, i, 0)),
                  pl.BlockSpec((1, nh, s, dk), lambda bi, h, i: (bi, h, 0, 0), pipeline_mode=pl.Buffered(1)),
                  pl.BlockSpec((1, nh, s, vc.shape[3]), lambda bi, h, i: (bi, h, 0, 0),
                               pipeline_mode=pl.Buffered(1))],
        out_specs=pl.BlockSpec((1, bq, nh * dv), lambda bi, h, i: (bi, i, h)),
        out_shape=jax.ShapeDtypeStruct((b, s, hd * dv), BF16),
        compiler_params=_cparams(("parallel", "parallel", "arbitrary")),
        name="flash",
    )(qc, kc, vc)


def _layer_norm(z, g, b):
    mu = jnp.mean(z, axis=-1, keepdims=True)
    cen = z - mu
    var = jnp.mean(cen * cen, axis=-1, keepdims=True)
    return cen * lax.rsqrt(var + LN_EPS) * g + b


def _outproj_kernel(ya_ref, yb_ref, yc_ref, x_ref, w_ref, g_ref, b_ref, wr_ref, br_ref,
                    x1_ref, x1rc_ref, e_ref, gate_ref, rank_ref, cnt_ref, cnt_sc):
    i = pl.program_id(0)
    tm = x_ref.shape[0]
    wa, wb = ya_ref.shape[1], yb_ref.shape[1]

    @pl.when(i == 0)
    def _init():
        cnt_sc[...] = jnp.zeros(cnt_sc.shape, F32)

    mix = jnp.dot(ya_ref[...], w_ref[0:wa, :], preferred_element_type=F32)
    mix = mix + jnp.dot(yb_ref[...], w_ref[wa:wa + wb, :], preferred_element_type=F32)
    mix = mix + jnp.dot(yc_ref[...], w_ref[wa + wb:, :], preferred_element_type=F32)
    x1 = _layer_norm(DN_ALPHA * x_ref[...] + mix, g_ref[...], b_ref[...])
    x1_ref[...] = x1
    _store_rc(x1rc_ref, x1)

    logits = jnp.dot(x1.astype(BF16), wr_ref[...], preferred_element_type=F32) + br_ref[...]
    lane = lax.broadcasted_iota(jnp.int32, logits.shape, 1)
    logits = jnp.where(lane < N_EXPERTS, logits, NEG_INF)
    e_out = jnp.zeros(logits.shape, jnp.int32)
    p_out = jnp.zeros(logits.shape, F32)
    top = None
    den = None
    onehots = []
    for r in range(TOP_K):
        mx = jnp.max(logits, axis=-1, keepdims=True)
        idx = jnp.min(jnp.where(logits == mx, lane, LANES), axis=-1, keepdims=True)
        if r == 0:
            top = mx
        p = jnp.exp(mx - top)
        den = p if r == 0 else den + p
        sel = lane == idx
        onehots.append(jnp.where(sel, 1.0, 0.0).astype(F32))
        e_out = jnp.where(lane == r, idx, e_out)
        p_out = jnp.where(lane == r, p, p_out)
        logits = jnp.where(sel, NEG_INF, logits)
    e_ref[...] = e_out
    gate_ref[...] = p_out / den

    oh_sum = onehots[0] + onehots[1] + onehots[2] + onehots[3]
    row = lax.broadcasted_iota(jnp.int32, (tm, tm), 0)
    col = lax.broadcasted_iota(jnp.int32, (tm, tm), 1)
    before = jnp.where(row > col, 1.0, 0.0).astype(BF16)
    base = jnp.dot(before, oh_sum.astype(BF16), preferred_element_type=F32) + cnt_sc[0:1, :]
    rank_out = jnp.zeros(logits.shape, jnp.int32)
    for r in range(TOP_K):
        rk = jnp.sum(onehots[r] * base, axis=-1, keepdims=True)
        rank_out = jnp.where(lane == r, rk.astype(jnp.int32), rank_out)
    rank_ref[...] = rank_out
    total = cnt_sc[0:1, :] + jnp.sum(oh_sum, axis=0, keepdims=True)
    cnt_sc[0:1, :] = total
    cnt_ref[...] = jnp.broadcast_to(total, cnt_ref.shape).astype(jnp.int32)


def _outproj(ya, yb, yc, x2d, w_bf, g, b, wr128, br128):
    t, d = x2d.shape
    tm = OUT_TM
    rows = lambda a: pl.BlockSpec((tm, a.shape[1]), lambda i: (i, 0))
    full = lambda a: pl.BlockSpec(a.shape, lambda i: (0,) * a.ndim, pipeline_mode=pl.Buffered(1))
    g2, b2 = g.reshape(1, d), b.reshape(1, d)
    lane_blk = pl.BlockSpec((tm, LANES), lambda i: (i, 0))
    return pl.pallas_call(
        _outproj_kernel,
        grid=(t // tm,),
        in_specs=[rows(ya), rows(yb), rows(yc), rows(x2d), full(w_bf), full(g2), full(b2), full(wr128), full(br128)],
        out_specs=[pl.BlockSpec((tm, d), lambda i: (i, 0)),
                   pl.BlockSpec((tm * ROW_TILES, LANES), lambda i: (i, 0)),
                   lane_blk, lane_blk, lane_blk,
                   pl.BlockSpec((SUBLANES, LANES), lambda i: (0, 0))],
        out_shape=[jax.ShapeDtypeStruct((t, d), F32),
                   jax.ShapeDtypeStruct((t * ROW_TILES, LANES), U32),
                   jax.ShapeDtypeStruct((t, LANES), jnp.int32),
                   jax.ShapeDtypeStruct((t, LANES), F32),
                   jax.ShapeDtypeStruct((t, LANES), jnp.int32),
                   jax.ShapeDtypeStruct((SUBLANES, LANES), jnp.int32)],
        scratch_shapes=[pltpu.VMEM((SUBLANES, LANES), F32)],
        compiler_params=_cparams(("arbitrary",)),
        name="outproj",
    )(ya, yb, yc, x2d, w_bf, g2, b2, wr128, br128)


def _rc_rows(row, n=1):
    return pl.ds(pl.multiple_of(row * ROW_TILES, ROW_TILES), n * ROW_TILES)


def _dispatch_kernel(pos_ref, cnt_ref, pst_ref, pad_ref, nblk_ref, x_hbm, xs_hbm, buf, zbuf, sem_in, sem_out, sem_z):
    i = pl.program_id(0)
    n = pl.num_programs(0)
    tm = DSP_TM
    nslot = buf.shape[0]

    def in_copy(blk, slot):
        return pltpu.make_async_copy(x_hbm.at[_rc_rows(blk * tm, tm)], buf.at[slot], sem_in.at[slot])

    def row_out(slot, t, p):
        return pltpu.make_async_copy(buf.at[slot, _rc_rows(t)], xs_hbm.at[_rc_rows(p)], sem_out.at[slot])

    def wait_outs(slot):
        for _ in range(TOP_K):
            pltpu.make_async_copy(buf.at[slot], xs_hbm.at[_rc_rows(0, tm)], sem_out.at[slot]).wait()

    slot = lax.rem(i, nslot)

    @pl.when(i == 0)
    def _first_in():
        in_copy(0, 0).start()

    @pl.when(i >= 2)
    def _free_slot():
        wait_outs(lax.rem(i + 1, nslot))

    @pl.when(i + 1 < n)
    def _next_in():
        in_copy(i + 1, lax.rem(i + 1, nslot)).start()

    in_copy(i, slot).wait()
    base = i * (tm * TOP_K)

    def issue(t, _):
        for k in range(TOP_K):
            row_out(slot, t, pos_ref[base + t * TOP_K + k]).start(priority=k % 2)
        return 0

    lax.fori_loop(0, tm, issue, 0, unroll=2)

    @pl.when(i == n - 1)
    def _finish():
        wait_outs(lax.rem(i + 2, nslot))
        wait_outs(slot)
        zbuf[...] = jnp.zeros(zbuf.shape, zbuf.dtype)

        def pad_rows(e, _):
            cnt = cnt_ref[e]
            first = pst_ref[e] + cnt
            npad = pad_ref[e] - cnt

            def zero_row(q):
                return pltpu.make_async_copy(zbuf.at[_rc_rows(0)], xs_hbm.at[_rc_rows(first + q)], sem_z)

            def start(q, c):
                zero_row(q).start()
                return c

            def wait(q, c):
                zero_row(q).wait()
                return c

            lax.fori_loop(0, npad, start, 0)
            lax.fori_loop(0, npad, wait, 0)
            return 0

        lax.fori_loop(0, N_EXPERTS, pad_rows, 0)

        def tail_block(bk, _):
            cp = pltpu.make_async_copy(zbuf, xs_hbm.at[_rc_rows(bk * MOE_PAD, MOE_PAD)], sem_z)
            cp.start()
            cp.wait()
            return 0

        lax.fori_loop(nblk_ref[0], xs_hbm.shape[0] // (MOE_PAD * ROW_TILES), tail_block, 0)


def _dispatch(pos, counts, pad_start, padded, nblk, x1rc, rows):
    t = x1rc.shape[0] // ROW_TILES
    tm = DSP_TM
    return pl.pallas_call(
        _dispatch_kernel,
        grid_spec=pltpu.PrefetchScalarGridSpec(
            num_scalar_prefetch=5,
            grid=(t // tm,),
            in_specs=[pl.BlockSpec(memory_space=pl.ANY)],
            out_specs=pl.BlockSpec(memory_space=pl.ANY),
            scratch_shapes=[pltpu.VMEM((3, tm * ROW_TILES, LANES), U32),
                            pltpu.VMEM((MOE_PAD * ROW_TILES, LANES), U32),
                            pltpu.SemaphoreType.DMA((3,)),
                            pltpu.SemaphoreType.DMA((3,)),
                            pltpu.SemaphoreType.DMA(())]),
        out_shape=jax.ShapeDtypeStruct((rows * ROW_TILES, LANES), U32),
        compiler_params=_cparams(("arbitrary",)),
        name="dispatch",
    )(pos, counts, pad_start, padded, nblk, x1rc)


def _moe_kernel(layer, sbe_ref, sbr_ref, sbn_ref, nblk_ref, xs_hbm, wgu_hbm, wdn_hbm, bgu_ref, bd_ref, y_hbm,
                xbuf, acc, stage_in, stage_out, wg_f, wu_f, wd_f, wg_bf, wu_bf, wd_bf, sem_in, sem_out, sem_w):
    sb = pl.program_id(0)
    nsb = pl.num_programs(0)
    tf = wg_f.shape[2]
    dff = wdn_hbm.shape[2]
    nj = dff // tf
    n = sbn_ref[sb]
    row0 = sbr_ref[sb]
    nchunk = n // MOE_PAD
    cur = lax.rem(sb, 2)
    big, huge = 2 * MOE_PAD, 4 * MOE_PAD

    def weight_copies(sb_idx, j, slot):
        e = sbe_ref[sb_idx]
        cols = pl.ds(pl.multiple_of(j * tf, tf), tf)
        up_cols = pl.ds(pl.multiple_of(dff + j * tf, tf), tf)
        copies = [pltpu.make_async_copy(wdn_hbm.at[layer, e, cols, :], wd_f.at[slot], sem_w.at[slot])]
        part = wg_f.shape[1] // MOE_W_PARTS
        for p in range(MOE_W_PARTS):
            band = pl.ds(p * part, part)
            copies.append(pltpu.make_async_copy(wgu_hbm.at[layer, e, band, cols], wg_f.at[slot, band], sem_w.at[slot]))
            copies.append(pltpu.make_async_copy(wgu_hbm.at[layer, e, band, up_cols], wu_f.at[slot, band],
                                                sem_w.at[slot]))
        return copies

    def chunk_rows(c):
        return pl.ds(pl.multiple_of(c * MOE_PAD, MOE_PAD), MOE_PAD)

    def chunk_in(first_row, c):
        return pltpu.make_async_copy(xs_hbm.at[_rc_rows(first_row + c * MOE_PAD, MOE_PAD)], stage_in, sem_in)

    def chunk_out(c, slot):
        return pltpu.make_async_copy(stage_out.at[slot], y_hbm.at[_rc_rows(row0 + c * MOE_PAD, MOE_PAD)],
                                     sem_out.at[slot])

    def wait_outs(count):
        @pl.when(count >= 2)
        def _older():
            chunk_out(0, lax.rem(count, 2)).wait()

        @pl.when(count >= 1)
        def _newest():
            chunk_out(0, lax.rem(count + 1, 2)).wait()

    def convert_in(slot, c):
        lo, hi = _load_rc(stage_in, MOE_PAD)
        xbuf[slot, chunk_rows(c), :] = jnp.concatenate([lo.astype(BF16), hi.astype(BF16)], axis=-1)

    @pl.when(sb == 0)
    def _load_first():
        for cp in weight_copies(0, 0, 0):
            cp.start()
        acc[...] = jnp.zeros(acc.shape, acc.dtype)

        def load(c, _):
            chunk_in(row0, c).start()
            chunk_in(row0, c).wait()
            convert_in(0, c)
            return 0

        lax.fori_loop(0, nchunk, load, 0)

    n_next = sbn_ref[sb + 1]

    def ffn(j, start, size, wg, wu, wd):
        rs = pl.ds(start, size)
        xr = xbuf[cur, rs, :]
        cols = pl.ds(pl.multiple_of(j * tf, tf), tf)
        up_cols = pl.ds(pl.multiple_of(dff + j * tf, tf), tf)
        gt = jnp.dot(xr, wg, preferred_element_type=F32) + bgu_ref[:, cols]
        up = jnp.dot(xr, wu, preferred_element_type=F32) + bgu_ref[:, up_cols]
        gt = jnp.minimum(gt, SWIGLU_LIMIT)
        up = jnp.clip(up, -SWIGLU_LIMIT, SWIGLU_LIMIT)
        act = ((up + 1.0) * gt * jax.nn.sigmoid(SWIGLU_ALPHA * gt)).astype(BF16)
        half = acc.shape[1] // 2
        for cs in (slice(0, half), slice(half, 2 * half)):
            prev = jnp.where(j == 0, bd_ref[:, cs], acc[rs, cs])
            acc[rs, cs] = prev + jnp.dot(act, wd[:, cs], preferred_element_type=F32)

    def first_chunk(j, wslot, size):
        wg = wg_f[wslot].astype(BF16)
        wu = wu_f[wslot].astype(BF16)
        wd = wd_f[wslot].astype(BF16)
        wg_bf[...] = wg
        wu_bf[...] = wu
        wd_bf[...] = wd
        ffn(j, 0, size, wg, wu, wd)

    def hidden_tile(j, _):
        wslot = lax.rem(j, 2)
        for cp in weight_copies(sb, j, wslot):
            cp.wait()

        @pl.when(j + 1 < nj)
        def _next_tile():
            for cp in weight_copies(sb, j + 1, 1 - wslot):
                cp.start()

        @pl.when((j + 1 == nj) & (n_next > 0))
        def _next_super_block():
            for cp in weight_copies(sb + 1, 0, 0):
                cp.start()

        fetch_next = j * MOE_PAD < n_next

        @pl.when(fetch_next)
        def _start_next():
            chunk_in(sbr_ref[sb + 1], j).start()

        @pl.when((j == 1) & (sb > 0))
        def _drain_prev_outs():
            wait_outs(sbn_ref[jnp.maximum(sb - 1, 0)] // MOE_PAD)

        def rest(start, size):
            ffn(j, start, size, wg_bf[...], wu_bf[...], wd_bf[...])

        @pl.when(n >= huge)
        def _rows_huge():
            first_chunk(j, wslot, huge)

            @pl.when(n - huge >= big)
            def _then_big():
                rest(huge, big)

            @pl.when(lax.rem(n, big) != 0)
            def _then_pad():
                rest(pl.multiple_of(n - MOE_PAD, MOE_PAD), MOE_PAD)

        @pl.when((n >= big) & (n < huge))
        def _rows_big():
            first_chunk(j, wslot, big)

            @pl.when(n > big)
            def _then_pad():
                rest(big, MOE_PAD)

        @pl.when(n < big)
        def _rows_small():
            first_chunk(j, wslot, MOE_PAD)

        @pl.when(j == nj - 1)
        def _store_rows():
            def store(c, _):
                slot = lax.rem(c, 2)

                @pl.when(c >= 2)
                def _slot_free():
                    chunk_out(c - 2, slot).wait()

                _store_rc(stage_out.at[slot], acc[chunk_rows(c), :])
                chunk_out(c, slot).start()
                return 0

            lax.fori_loop(0, nchunk, store, 0)

            @pl.when((sb == nsb - 1) | (n_next == 0))
            def _no_later_tile():
                wait_outs(nchunk)

        @pl.when(fetch_next)
        def _finish_next():
            chunk_in(sbr_ref[sb + 1], j).wait()
            convert_in(1 - cur, j)

        return 0

    @pl.when(n > 0)
    def _super_block():
        lax.fori_loop(0, nj, hidden_tile, 0)

    @pl.when(sb == nsb - 1)
    def _zero_tail():
        stage_out[0] = jnp.zeros(stage_out.shape[1:], stage_out.dtype)

        def tail_block(bk, _):
            cp = pltpu.make_async_copy(stage_out.at[0], y_hbm.at[_rc_rows(bk * MOE_PAD, MOE_PAD)], sem_out.at[0])
            cp.start()
            cp.wait()
            return 0

        lax.fori_loop(nblk_ref[0], y_hbm.shape[0] // (MOE_PAD * ROW_TILES), tail_block, 0)


def _moe(layer, sb_e, sb_row0, sb_n, nblk, xs, w_gate_up, b_gate_up, w_down, b_down):
    rows = xs.shape[0] // ROW_TILES
    d = D_MODEL
    dff = w_down.shape[2]
    tf = MOE_TF
    nj = dff // tf
    nsb = sb_e.shape[0] - 1
    assert nj % 2 == 0 and MOE_TMAX // MOE_PAD <= nj and MOE_TMAX == 6 * MOE_PAD
    bgu = b_gate_up.reshape(DEPTH, N_EXPERTS, 1, 2 * dff)
    bdn = b_down.reshape(DEPTH, N_EXPERTS, 1, d)
    expert_row = lambda width: pl.BlockSpec((None, None, 1, width), lambda sb, e, r, n, nb: (layer, e[sb], 0, 0))
    return pl.pallas_call(
        functools.partial(_moe_kernel, layer),
        grid_spec=pltpu.PrefetchScalarGridSpec(
            num_scalar_prefetch=4,
            grid=(nsb,),
            in_specs=[pl.BlockSpec(memory_space=pl.ANY),
                      pl.BlockSpec(memory_space=pl.ANY),
                      pl.BlockSpec(memory_space=pl.ANY),
                      expert_row(2 * dff),
                      expert_row(d)],
            out_specs=pl.BlockSpec(memory_space=pl.ANY),
            scratch_shapes=[pltpu.VMEM((2, MOE_TMAX, d), BF16),
                            pltpu.VMEM((MOE_TMAX, d), F32),
                            pltpu.VMEM((MOE_PAD * ROW_TILES, LANES), U32),
                            pltpu.VMEM((2, MOE_PAD * ROW_TILES, LANES), U32),
                            pltpu.VMEM((2, d, tf), F32),
                            pltpu.VMEM((2, d, tf), F32),
                            pltpu.VMEM((2, tf, d), F32),
                            pltpu.VMEM((d, tf), BF16),
                            pltpu.VMEM((d, tf), BF16),
                            pltpu.VMEM((tf, d), BF16),
                            pltpu.SemaphoreType.DMA(()),
                            pltpu.SemaphoreType.DMA((2,)),
                            pltpu.SemaphoreType.DMA((2,))]),
        out_shape=jax.ShapeDtypeStruct((rows * ROW_TILES, LANES), U32),
        compiler_params=_cparams(("arbitrary",)),
        name="moe",
    )(sb_e, sb_row0, sb_n, nblk, xs, w_gate_up, w_down, bgu, bdn)


def _combine_kernel(pos_ref, y_hbm, gate_ref, x1_ref, g_ref, b_ref, o_ref, buf, sem):
    i = pl.program_id(0)
    n = pl.num_programs(0)
    tm = CMB_TM
    slot = lax.rem(i, 2)

    def issue(blk, dst_slot):
        base = blk * (tm * TOP_K)

        def body(t, _):
            for k in range(TOP_K):
                p = pos_ref[base + t * TOP_K + k]
                pltpu.make_async_copy(y_hbm.at[_rc_rows(p)], buf.at[dst_slot, k, _rc_rows(t)],
                                      sem.at[dst_slot]).start(priority=k % 2)
            return 0

        lax.fori_loop(0, tm, body, 0, unroll=2)

    @pl.when(i == 0)
    def _first():
        issue(0, 0)

    @pl.when(i + 1 < n)
    def _next():
        issue(i + 1, 1 - slot)

    for k in range(TOP_K):
        pltpu.make_async_copy(y_hbm.at[_rc_rows(0, tm)], buf.at[slot, k], sem.at[slot]).wait()
    gate = gate_ref[...]
    ffn_lo, ffn_hi = None, None
    for k in range(TOP_K):
        lo, hi = _load_rc(buf.at[slot, k], tm)
        gk = gate[:, k:k + 1]
        ffn_lo = gk * lo if k == 0 else ffn_lo + gk * lo
        ffn_hi = gk * hi if k == 0 else ffn_hi + gk * hi
    ffn = jnp.concatenate([ffn_lo, ffn_hi], axis=-1)
    o_ref[...] = _layer_norm(DN_ALPHA * x1_ref[...] + ffn, g_ref[...], b_ref[...])


def _combine(pos_flat, y, gate, x1, g, b):
    t, d = x1.shape
    tm = CMB_TM
    g2, b2 = g.reshape(1, d), b.reshape(1, d)
    return pl.pallas_call(
        _combine_kernel,
        grid_spec=pltpu.PrefetchScalarGridSpec(
            num_scalar_prefetch=1,
            grid=(t // tm,),
            in_specs=[pl.BlockSpec(memory_space=pl.ANY),
                      pl.BlockSpec((tm, LANES), lambda i, p: (i, 0)),
                      pl.BlockSpec((tm, d), lambda i, p: (i, 0)),
                      pl.BlockSpec((1, d), lambda i, p: (0, 0)),
                      pl.BlockSpec((1, d), lambda i, p: (0, 0))],
            out_specs=pl.BlockSpec((tm, d), lambda i, p: (i, 0)),
            scratch_shapes=[pltpu.VMEM((2, TOP_K, tm * ROW_TILES, LANES), U32), pltpu.SemaphoreType.DMA((2,))]),
        out_shape=jax.ShapeDtypeStruct((t, d), F32),
        compiler_params=_cparams(("arbitrary",)),
        name="combine",
    )(pos_flat, y, gate, x1, g2, b2)


def _swap_halves(w):
    half = w.shape[-1] // 2
    return jnp.concatenate([w[..., half:], w[..., :half]], axis=-1)


def _w_in_columns(w):
    d = w.shape[0]
    widths = (512, 128, 128, 256, 256, 512, 512, 8, 512, 256, 64)
    offs = [0]
    for wd in widths:
        offs.append(offs[-1] + wd)
    a_q, a_k, a_v, m_q, m_k, m_v, m_o, m_if, c_q, c_kv, c_kr = [w[:, offs[i]:offs[i + 1]] for i in range(len(widths))]
    z = lambda n: jnp.zeros((d, n), w.dtype)
    return jnp.concatenate([a_q, m_v, m_o, c_q, m_q, m_k, c_kv, a_k, a_v,
                            c_kr, z(LANES - 64), m_if, z(LANES - 8), _swap_halves(c_kr), z(LANES - 64)], axis=1)


def _w_in_layout_kernel(w_ref, o_ref):
    o_ref[...] = _w_in_columns(w_ref[...]).astype(o_ref.dtype)


def _layout_w_in(w_in, layer):
    _, d, n = w_in.shape
    tk = 256
    return pl.pallas_call(
        _w_in_layout_kernel,
        grid=(d // tk,),
        in_specs=[pl.BlockSpec((None, tk, n), lambda i: (layer, i, 0))],
        out_specs=pl.BlockSpec((tk, N_PROJ), lambda i: (i, 0)),
        out_shape=jax.ShapeDtypeStruct((d, N_PROJ), BF16),
        compiler_params=_cparams(("parallel",)),
        name="w_in_layout",
    )(w_in)


def _layout_w_uq(w):
    r = w.shape[0]
    w3 = w.astype(BF16).reshape(r, C_HEADS, C_NOPE_DIM + C_ROPE_DIM)
    nope, rope = w3[..., :C_NOPE_DIM], w3[..., C_NOPE_DIM:]
    z = jnp.zeros((r, C_HEADS, LANES - C_ROPE_DIM), BF16)
    wa = jnp.concatenate([nope, rope, z], axis=-1).reshape(r, -1)
    wb = jnp.concatenate([_swap_halves(rope), z], axis=-1).reshape(r, -1)
    return wa, wb


def _layout_w_ukv(w):
    r = w.shape[0]
    w3 = w.astype(BF16).reshape(r, C_HEADS, C_NOPE_DIM + C_V_DIM)
    return jnp.concatenate([w3[..., :C_NOPE_DIM].reshape(r, -1), w3[..., C_NOPE_DIM:].reshape(r, -1)], axis=-1)


def _rope_tables(seq):
    dim = C_ROPE_DIM
    inv = 1.0 / (ROPE_THETA ** (jnp.arange(0, dim, 2, dtype=F32) / dim))
    ang = jnp.arange(seq, dtype=F32)[:, None] * inv[None, :]
    cos, sin = jnp.cos(ang), jnp.sin(ang)
    z = jnp.zeros((seq, LANES - dim), F32)
    return jnp.concatenate([cos, cos, z], axis=-1), jnp.concatenate([-sin, sin, z], axis=-1)


def _route(top_e, rank, counts, nsb):
    padded = (counts + MOE_PAD - 1) // MOE_PAD * MOE_PAD
    pad_end = jnp.cumsum(padded)
    pad_start = pad_end - padded
    is_e = top_e[:, :, None] == jnp.arange(N_EXPERTS, dtype=jnp.int32)[None, None, :]
    pos = (jnp.sum(jnp.where(is_e, pad_start[None, None, :], 0), axis=-1) + rank).reshape(-1).astype(jnp.int32)
    nblk = (pad_end[-1:] // MOE_PAD).astype(jnp.int32)
    nsb_e = (padded + MOE_TMAX - 1) // MOE_TMAX
    sb_end = jnp.cumsum(nsb_e)
    sb_start = sb_end - nsb_e
    sb_idx = jnp.arange(nsb + 1, dtype=jnp.int32)
    n_valid = sb_end[-1]
    sb_eff = jnp.minimum(sb_idx, n_valid - 1)
    sb_e = jnp.searchsorted(sb_end, sb_eff, side="right").astype(jnp.int32)
    part = sb_eff - sb_start[sb_e]
    sb_row0 = pad_start[sb_e] + part * MOE_TMAX
    sb_n = jnp.where(sb_idx < n_valid, jnp.minimum(padded[sb_e] - part * MOE_TMAX, MOE_TMAX), 0)
    tables = (counts, pad_start.astype(jnp.int32), padded.astype(jnp.int32), nblk)
    return pos, tables, sb_e, sb_row0.astype(jnp.int32), sb_n.astype(jnp.int32)


def kernel(x, w_in, conv_w, conv_b, m_gate_b, m_norm_g, sinks, q_norm_g, w_uq, kv_norm_g, w_ukv, w_out,
           ln1_g, ln1_b, w_router, b_router, w_gate_up, b_gate_up, w_down, b_down, ln2_g, ln2_b):
    b, s, d = x.shape
    t = b * s
    n_assign = t * TOP_K
    rows = (n_assign // MOE_PAD + N_EXPERTS) * MOE_PAD
    nsb = N_EXPERTS + rows // MOE_TMAX
    cos128, sin128 = _rope_tables(s)
    xt = x.reshape(t, d)
    for l in range(DEPTH):
        w_in_bf = _layout_w_in(w_in, l)
        wqa, wqb = _layout_w_uq(w_uq[l])
        wkv = _layout_w_ukv(w_ukv[l])
        w_out_bf = w_out[l].astype(BF16)
        wr128 = jnp.zeros((d, LANES), BF16).at[:, :N_EXPERTS].set(w_router[l].astype(BF16))
        br128 = jnp.zeros((1, LANES), F32).at[0, :N_EXPERTS].set(b_router[l])

        proj, gates = _proj(xt, w_in_bf)
        proj3 = proj.reshape(b, s, N_PROJ)
        y_a = _swa(proj3, sinks[l])
        y_b = _mlstm(proj3, gates.reshape(b, s, LANES), conv_w[l], conv_b[l], m_gate_b[l], m_norm_g[l])
        qc, kc, vc = _mla_prep(proj3, q_norm_g[l], kv_norm_g[l], wqa, wqb, wkv, cos128, sin128)
        y_c = _flash(qc, kc, vc)
        x1, x1rc, top_e, gate, rank, cnt = _outproj(y_a.reshape(t, -1), y_b.reshape(t, -1), y_c.reshape(t, -1), xt,
                                                    w_out_bf, ln1_g[l], ln1_b[l], wr128, br128)
        pos, tables, sb_e, sb_row0, sb_n = _route(top_e[:, :TOP_K], rank[:, :TOP_K], cnt[0, :N_EXPERTS], nsb)
        xs = _dispatch(pos, *tables, x1rc, rows)
        y = _moe(l, sb_e, sb_row0, sb_n, tables[3], xs, w_gate_up, b_gate_up, w_down, b_down)
        xt = _combine(pos, y, gate, x1, ln2_g[l], ln2_b[l])
    return xt.reshape(b, s, d)
```

```python
import functools

import jax
import jax.numpy as jnp
from jax import lax
from jax.experimental import pallas as pl
from jax.experimental.pallas import tpu as pltpu

F32 = jnp.float32
BF16 = jnp.bfloat16
NEG_INF = float("-inf")
LOG2_E = 1.4426950408889634

D_MODEL = 2048
DEPTH = 2
SWA_HEADS, SWA_KV_HEADS, SWA_HEAD_DIM, SWA_WINDOW = 8, 2, 64, 128
M_HEADS, M_QK_DIM, M_V_DIM, M_CONV = 4, 64, 128, 4
C_HEADS, C_NOPE_DIM, C_ROPE_DIM, C_V_DIM = 8, 128, 64, 128
C_Q_LORA, C_KV_LORA = 512, 256
ROPE_THETA = 10000.0
N_EXPERTS, TOP_K = 32, 4
SWIGLU_LIMIT, SWIGLU_ALPHA = 7.0, 1.702
DN_ALPHA = (2 * DEPTH) ** 0.25
LN_EPS, RMS_EPS = 1e-5, 1e-6

LANES = 128
SUBLANES = 8
VMEM_LIMIT = 56 * 1024 * 1024

P_AQ, P_MV, P_MO, P_CQ, P_MQK, P_CKV, P_AK, P_AV, P_CKR, P_MIF, P_CKRS = (
    0, 512, 1024, 1536, 2048, 2560, 2816, 2944, 3072, 3200, 3328)
N_PROJ = 3456

PROJ_TM, PROJ_TN = 1024, 1152
M_CHUNK = 256
MLA_TM = 512
FLASH_BQ = 1024
FLASH_HEADS = 4
OUT_TM = 512
MOE_PAD = 256
MOE_TMAX = 1536
MOE_TF = 256
MOE_W_PARTS = 4
CMB_TM = 256
DSP_TM = 512

ROW_TILES = D_MODEL // 2 // LANES
U32 = jnp.uint32
HI_MASK = 0xFFFF0000


def _cparams(sem, vmem=VMEM_LIMIT):
    return pltpu.CompilerParams(dimension_semantics=sem, vmem_limit_bytes=vmem)


def _bf16_bits(x):
    return lax.bitcast_convert_type(x.astype(BF16).astype(F32), U32)


def _store_rc(ref, val):
    n, d = val.shape
    words = (_bf16_bits(val[:, :d // 2]) >> 16) | (_bf16_bits(val[:, d // 2:]) & U32(HI_MASK))
    x = jnp.stack([words[:, s * LANES:(s + 1) * LANES] for s in range(ROW_TILES)], axis=0)
    ref[...] = pltpu.einshape("stl->tsl", x).reshape(n * ROW_TILES, LANES)


def _load_rc(ref, n):
    x = pltpu.einshape("tsl->stl", ref[...].reshape(n, ROW_TILES, LANES))
    words = jnp.concatenate([x[s] for s in range(ROW_TILES)], axis=-1)
    lo = lax.bitcast_convert_type(words << 16, F32)
    hi = lax.bitcast_convert_type(words & U32(HI_MASK), F32)
    return lo, hi


def _proj_kernel(x_ref, w_ref, o_ref, g_ref):
    j = pl.program_id(1)
    acc = jnp.dot(x_ref[...].astype(BF16), w_ref[...], preferred_element_type=F32)
    o_ref[...] = acc.astype(o_ref.dtype)

    @pl.when(j == P_MIF // PROJ_TN)
    def _gates():
        off = P_MIF % PROJ_TN
        g_ref[...] = acc[:, off:off + LANES]


def _proj(x2d, w_bf):
    t, d = x2d.shape
    n = w_bf.shape[1]
    return pl.pallas_call(
        _proj_kernel,
        grid=(t // PROJ_TM, n // PROJ_TN),
        in_specs=[pl.BlockSpec((PROJ_TM, d), lambda i, j: (i, 0)),
                  pl.BlockSpec((d, PROJ_TN), lambda i, j: (0, j))],
        out_specs=[pl.BlockSpec((PROJ_TM, PROJ_TN), lambda i, j: (i, j)),
                   pl.BlockSpec((PROJ_TM, LANES), lambda i, j: (i, 0))],
        out_shape=[jax.ShapeDtypeStruct((t, n), BF16), jax.ShapeDtypeStruct((t, LANES), F32)],
        compiler_params=_cparams(("parallel", "arbitrary")),
        name="proj",
    )(x2d, w_bf)


def _swa_kernel(sinks_ref, q_ref, kc_ref, kp_ref, vc_ref, vp_ref, o_ref):
    i = pl.program_id(0)
    w = SWA_WINDOW
    dh = SWA_HEAD_DIM
    grp = SWA_HEADS // SWA_KV_HEADS
    qi = lax.broadcasted_iota(jnp.int32, (w, 2 * w), 0)
    kj = lax.broadcasted_iota(jnp.int32, (w, 2 * w), 1)
    rel = qi + w - kj
    has_prev = jnp.where(i > 0, 0, w)
    mask = (rel >= 0) & (rel < w) & (kj >= has_prev)
    for b in range(q_ref.shape[0]):
        q = q_ref[b]
        outs = []
        for kvh in range(SWA_KV_HEADS):
            sl = slice(kvh * dh, (kvh + 1) * dh)
            k_cat = jnp.concatenate([kp_ref[b][:, sl], kc_ref[b][:, sl]], axis=0).astype(BF16)
            v_cat = jnp.concatenate([vp_ref[b][:, sl], vc_ref[b][:, sl]], axis=0).astype(BF16)
            for g in range(grp):
                h = kvh * grp + g
                qh = (q[:, h * dh:(h + 1) * dh] * (dh ** -0.5)).astype(BF16)
                s = lax.dot_general(qh, k_cat, (((1,), (1,)), ((), ())), preferred_element_type=F32)
                s = jnp.where(mask, s, NEG_INF)
                sink = sinks_ref[h]
                m = jnp.maximum(jnp.max(s, axis=-1, keepdims=True), sink)
                p = jnp.exp(s - m)
                den = jnp.sum(p, axis=-1, keepdims=True) + jnp.exp(sink - m)
                o = jnp.dot(p.astype(BF16), v_cat, preferred_element_type=F32)
                outs.append(o / den)
        o_ref[b] = jnp.concatenate(outs, axis=-1).astype(o_ref.dtype)


def _swa(proj3, sinks):
    b, s, _ = proj3.shape
    w = SWA_WINDOW
    nb = s // w
    kw = SWA_KV_HEADS * SWA_HEAD_DIM
    qw = SWA_HEADS * SWA_HEAD_DIM
    cur = lambda col: (lambda i: (0, i, col))
    prev = lambda col: (lambda i: (0, jnp.maximum(i - 1, 0), col))
    return pl.pallas_call(
        _swa_kernel,
        grid=(nb,),
        in_specs=[pl.BlockSpec(memory_space=pltpu.SMEM),
                  pl.BlockSpec((b, w, qw), cur(P_AQ // qw)),
                  pl.BlockSpec((b, w, kw), cur(P_AK // kw)),
                  pl.BlockSpec((b, w, kw), prev(P_AK // kw)),
                  pl.BlockSpec((b, w, kw), cur(P_AV // kw)),
                  pl.BlockSpec((b, w, kw), prev(P_AV // kw))],
        out_specs=pl.BlockSpec((b, w, qw), lambda i: (0, i, 0)),
        out_shape=jax.ShapeDtypeStruct((b, s, qw), BF16),
        compiler_params=_cparams(("arbitrary",)),
        name="swa",
    )(sinks, proj3, proj3, proj3, proj3, proj3)


def _mlstm_kernel(qk_ref, v_ref, og_ref, g_ref, cw_ref, cb_ref, gb_ref, ng_ref, y_ref, xbuf, ct_ref, m_ref):
    c = pl.program_id(0)
    nb = qk_ref.shape[0]
    ln = M_CHUNK
    dqk, dv, nh = M_QK_DIM, M_V_DIM, M_HEADS
    halo = SUBLANES

    @pl.when(c == 0)
    def _init():
        xbuf[:, 0:halo, :] = jnp.zeros((nb, halo, xbuf.shape[2]), F32)
        ct_ref[...] = jnp.zeros(ct_ref.shape, F32)
        m_ref[...] = jnp.zeros(m_ref.shape, F32)

    row = lax.broadcasted_iota(jnp.int32, (ln, ln), 0)
    col = lax.broadcasted_iota(jnp.int32, (ln, ln), 1)
    tril = row >= col
    trilf = jnp.where(tril, 1.0, 0.0).astype(F32)
    ones_blk = jnp.where(lax.broadcasted_iota(jnp.int32, (ln, dv), 1) == 0, 1.0, 0.0).astype(F32)

    for b in range(nb):
        xbuf[b, halo:halo + ln, :] = qk_ref[b].astype(F32)
        conv = cb_ref[...]
        for j in range(M_CONV):
            off = halo - (M_CONV - 1) + j
            conv = conv + cw_ref[j:j + 1, :] * xbuf[b, off:off + ln, :]
        qk = conv * jax.nn.sigmoid(conv)
        xbuf[b, 0:halo, :] = xbuf[b, ln:ln + halo, :]

        g = g_ref[b] + gb_ref[...]
        lf = jnp.minimum(g, 0.0) - jnp.log1p(jnp.exp(-jnp.abs(g)))
        cum = jnp.dot(trilf, lf, preferred_element_type=F32, precision=lax.Precision.HIGHEST)
        cum_t = cum.T
        g_t = g.T
        vv = v_ref[b].astype(F32)
        hs = []
        for h in range(nh):
            chain = b * nh + h
            q = (qk[:, h * dqk:(h + 1) * dqk] * (dqk ** -0.5)).astype(BF16)
            k = qk[:, nh * dqk + h * dqk: nh * dqk + (h + 1) * dqk].astype(BF16)
            v_ext = jnp.concatenate([vv[:, h * dv:(h + 1) * dv], ones_blk], axis=-1)
            bc_col = cum[:, nh + h:nh + h + 1]
            bc_row = cum_t[nh + h:nh + h + 1, :]
            i_col = g[:, h:h + 1]
            i_row = g_t[h:h + 1, :]
            m_prev = m_ref[chain:chain + 1, 0:1]
            dmat = jnp.where(tril, bc_col - bc_row + i_row, NEG_INF)
            m_inter = bc_col + m_prev
            m_j = jnp.maximum(m_inter, jnp.max(dmat, axis=-1, keepdims=True))
            w_intra = jnp.exp(dmat - m_j)
            w_inter = jnp.exp(m_inter - m_j)
            s = lax.dot_general(q, k, (((1,), (1,)), ((), ())), preferred_element_type=F32) * w_intra
            ct = ct_ref[chain]
            num_ext = (jnp.dot(s.astype(BF16), v_ext.astype(BF16), preferred_element_type=F32)
                       + w_inter * jnp.dot(q, ct.astype(BF16), preferred_element_type=F32))
            num = num_ext[:, :dv]
            nq = num_ext[:, dv:dv + 1]
            den = jnp.maximum(jnp.abs(nq), jnp.exp(-m_j))
            hs.append(num / den)
            m_new = m_j[ln - 1:ln, :]
            bc_last = bc_col[ln - 1:ln, :]
            w_s = jnp.exp(bc_last - bc_col + i_col - m_new)
            w_c = jnp.exp(bc_last + m_prev - m_new)
            upd = lax.dot_general(k, (w_s * v_ext).astype(BF16), (((0,), (0,)), ((), ())),
                                  preferred_element_type=F32)
            ct_ref[chain] = w_c * ct + upd
            m_ref[chain:chain + 1, :] = jnp.broadcast_to(m_new, (1, m_ref.shape[1]))
        og = og_ref[b].astype(F32)
        outs = []
        for h in range(nh):
            seg = jax.nn.sigmoid(og[:, h * dv:(h + 1) * dv]) * hs[h]
            mu = jnp.mean(seg, axis=-1, keepdims=True)
            cen = seg - mu
            var = jnp.mean(cen * cen, axis=-1, keepdims=True)
            outs.append(cen * lax.rsqrt(var + LN_EPS) * ng_ref[:, h * dv:(h + 1) * dv])
        y_ref[b] = jnp.concatenate(outs, axis=-1).astype(y_ref.dtype)


def _mlstm(proj3, gates3, conv_w, conv_b, gate_b, norm_g):
    b, s, _ = proj3.shape
    ln = M_CHUNK
    wq = 2 * M_HEADS * M_QK_DIM
    wv = M_HEADS * M_V_DIM
    blk = lambda width, off: pl.BlockSpec((b, ln, width), lambda c: (0, c, off // width))
    full = lambda a: pl.BlockSpec(a.shape, lambda c: (0,) * a.ndim)
    gate_b128 = jnp.zeros((1, LANES), F32).at[0, :2 * M_HEADS].set(gate_b)
    conv_b2 = conv_b.reshape(1, wq)
    norm_g2 = norm_g.reshape(1, wv)
    return pl.pallas_call(
        _mlstm_kernel,
        grid=(s // ln,),
        in_specs=[blk(wq, P_MQK), blk(wv, P_MV), blk(wv, P_MO), blk(LANES, 0),
                  full(conv_w), full(conv_b2), full(gate_b128), full(norm_g2)],
        out_specs=pl.BlockSpec((b, ln, wv), lambda c: (0, c, 0)),
        out_shape=jax.ShapeDtypeStruct((b, s, wv), BF16),
        scratch_shapes=[pltpu.VMEM((b, SUBLANES + ln, wq), F32),
                        pltpu.VMEM((b * M_HEADS, M_QK_DIM, 2 * M_V_DIM), F32),
                        pltpu.VMEM((b * M_HEADS, LANES), F32)],
        compiler_params=_cparams(("arbitrary",)),
        name="mlstm",
    )(proj3, proj3, proj3, gates3, conv_w, conv_b2, gate_b128, norm_g2)


def _mla_prep_kernel(cq_ref, ckv_ref, kr_ref, krs_ref, qg_ref, kvg_ref, wqa_ref, wqb_ref, wkv_ref,
                     cos_ref, sin_ref, q_out, k_out, v_out):
    nd, hd = C_NOPE_DIM, C_HEADS
    scale = (C_NOPE_DIM + C_ROPE_DIM) ** -0.5 * LOG2_E
    cos = cos_ref[...]
    sin = sin_ref[...]

    cq = cq_ref[0].astype(F32)
    qn = (cq * lax.rsqrt(jnp.mean(cq * cq, axis=-1, keepdims=True) + RMS_EPS) * qg_ref[...]).astype(BF16)
    qa = jnp.dot(qn, wqa_ref[...], preferred_element_type=F32)
    qb = jnp.dot(qn, wqb_ref[...], preferred_element_type=F32)
    for h in range(hd):
        nope = qa[:, h * 2 * nd: h * 2 * nd + nd]
        rope = qa[:, h * 2 * nd + nd:(h + 1) * 2 * nd] * cos + qb[:, h * nd:(h + 1) * nd] * sin
        q_out[0, h] = (jnp.concatenate([nope, rope], axis=-1) * scale).astype(q_out.dtype)

    ckv = ckv_ref[0].astype(F32)
    kvn = (ckv * lax.rsqrt(jnp.mean(ckv * ckv, axis=-1, keepdims=True) + RMS_EPS) * kvg_ref[...]).astype(BF16)
    kv = jnp.dot(kvn, wkv_ref[...], preferred_element_type=F32)
    kr = kr_ref[0].astype(F32) * cos + krs_ref[0].astype(F32) * sin
    ones_blk = jnp.where(lax.broadcasted_iota(jnp.int32, kr.shape, 1) == 0, 1.0, 0.0).astype(F32)
    for h in range(hd):
        k_out[0, h] = jnp.concatenate([kv[:, h * nd:(h + 1) * nd], kr], axis=-1).astype(k_out.dtype)
        v_h = kv[:, hd * nd + h * nd: hd * nd + (h + 1) * nd]
        v_out[0, h] = jnp.concatenate([v_h, ones_blk], axis=-1).astype(v_out.dtype)


def _mla_prep(proj3, q_norm_g, kv_norm_g, wqa, wqb, wkv, cos128, sin128):
    b, s, _ = proj3.shape
    tm = MLA_TM
    hd, nd = C_HEADS, C_NOPE_DIM
    blk = lambda width, off: pl.BlockSpec((1, tm, width), lambda bi, i: (bi, i, off // width))
    full = lambda a: pl.BlockSpec(a.shape, lambda bi, i: (0,) * a.ndim)
    tab = pl.BlockSpec((tm, LANES), lambda bi, i: (i, 0))
    qg = q_norm_g.reshape(1, -1)
    kvg = kv_norm_g.reshape(1, -1)
    head_out = lambda width: pl.BlockSpec((1, hd, tm, width), lambda bi, i: (bi, 0, i, 0))
    return pl.pallas_call(
        _mla_prep_kernel,
        grid=(b, s // tm),
        in_specs=[blk(C_Q_LORA, P_CQ), blk(C_KV_LORA, P_CKV), blk(LANES, P_CKR), blk(LANES, P_CKRS),
                  full(qg), full(kvg), full(wqa), full(wqb), full(wkv), tab, tab],
        out_specs=[head_out(2 * nd), head_out(2 * nd), head_out(C_V_DIM + LANES)],
        out_shape=[jax.ShapeDtypeStruct((b, hd, s, 2 * nd), BF16),
                   jax.ShapeDtypeStruct((b, hd, s, 2 * nd), BF16),
                   jax.ShapeDtypeStruct((b, hd, s, C_V_DIM + LANES), BF16)],
        compiler_params=_cparams(("parallel", "arbitrary")),
        name="mla_prep",
    )(proj3, proj3, proj3, proj3, qg, kvg, wqa, wqb, wkv, cos128, sin128)


def _flash_kernel(q_ref, k_ref, v_ref, o_ref):
    qi = pl.program_id(2)
    bq = FLASH_BQ
    nh = q_ref.shape[1]
    dv = C_V_DIM

    def attend(q, h, key0, nkeys, m, acc, causal):
        ks = k_ref[0, h, pl.ds(key0, nkeys), :]
        vs = v_ref[0, h, pl.ds(key0, nkeys), :]
        s = lax.dot_general(q, ks, (((1,), (1,)), ((), ())), preferred_element_type=F32)
        if causal:
            row = lax.broadcasted_iota(jnp.int32, s.shape, 0)
            col = lax.broadcasted_iota(jnp.int32, s.shape, 1)
            s = jnp.where(row >= col, s, NEG_INF)
        m_new = jnp.maximum(m, jnp.max(s, axis=-1, keepdims=True))
        p = jnp.exp2(s - m_new)
        acc = jnp.exp2(m - m_new) * acc + jnp.dot(p.astype(BF16), vs, preferred_element_type=F32)
        return m_new, acc

    def full_step(j, carries):
        start = pl.multiple_of(j * bq, bq)
        return tuple(attend(q_ref[0, h], h, start, bq, *carries[h], False) for h in range(nh))

    init = tuple((jnp.full((bq, 1), NEG_INF, F32), jnp.zeros((bq, v_ref.shape[3]), F32)) for _ in range(nh))
    carries = lax.fori_loop(0, qi, full_step, init)

    half = bq // 2
    diag0 = pl.multiple_of(qi * bq, bq)
    for h in range(nh):
        m, acc = attend(q_ref[0, h], h, diag0, half, *carries[h], True)
        m_lo, acc_lo = attend(q_ref[0, h, half:, :], h, diag0 + half, half, m[half:], acc[half:], True)
        cols = slice(h * dv, (h + 1) * dv)
        o_ref[0, :half, cols] = (acc[:half, :dv] / acc[:half, dv:dv + 1]).astype(o_ref.dtype)
        o_ref[0, half:, cols] = (acc_lo[:, :dv] / acc_lo[:, dv:dv + 1]).astype(o_ref.dtype)


def _flash(qc, kc, vc):
    b, hd, s, dk = qc.shape
    dv = C_V_DIM
    bq = FLASH_BQ
    nh = FLASH_HEADS
    return pl.pallas_call(
        _flash_kernel,
        grid=(b, hd // nh, s // bq),
        in_specs=[pl.BlockSpec((1, nh, bq, dk), lambda bi, h, i: (bi, h, i, 0)),
                  pl.BlockSpec((1, nh, s, dk), lambda bi, h, i: (bi, h, 0, 0), pipeline_mode=pl.Buffered(1)),
                  pl.BlockSpec((1, nh, s, vc.shape[3]), lambda bi, h, i: (bi, h, 0, 0),
                               pipeline_mode=pl.Buffered(1))],
        out_specs=pl.BlockSpec((1, bq, nh * dv), lambda bi, h, i: (bi, i, h)),
        out_shape=jax.ShapeDtypeStruct((b, s, hd * dv), BF16),
        compiler_params=_cparams(("parallel", "parallel", "arbitrary")),
        name="flash",
    )(qc, kc, vc)


def _layer_norm(z, g, b):
    mu = jnp.mean(z, axis=-1, keepdims=True)
    cen = z - mu
    var = jnp.mean(cen * cen, axis=-1, keepdims=True)
    return cen * lax.rsqrt(var + LN_EPS) * g + b


def _outproj_kernel(ya_ref, yb_ref, yc_ref, x_ref, w_ref, g_ref, b_ref, wr_ref, br_ref,
                    x1_ref, x1rc_ref, e_ref, gate_ref, rank_ref, cnt_ref, cnt_sc):
    i = pl.program_id(0)
    tm = x_ref.shape[0]
    wa, wb = ya_ref.shape[1], yb_ref.shape[1]

    @pl.when(i == 0)
    def _init():
        cnt_sc[...] = jnp.zeros(cnt_sc.shape, F32)

    mix = jnp.dot(ya_ref[...], w_ref[0:wa, :], preferred_element_type=F32)
    mix = mix + jnp.dot(yb_ref[...], w_ref[wa:wa + wb, :], preferred_element_type=F32)
    mix = mix + jnp.dot(yc_ref[...], w_ref[wa + wb:, :], preferred_element_type=F32)
    x1 = _layer_norm(DN_ALPHA * x_ref[...] + mix, g_ref[...], b_ref[...])
    x1_ref[...] = x1
    _store_rc(x1rc_ref, x1)

    logits = jnp.dot(x1.astype(BF16), wr_ref[...], preferred_element_type=F32) + br_ref[...]
    lane = lax.broadcasted_iota(jnp.int32, logits.shape, 1)
    logits = jnp.where(lane < N_EXPERTS, logits, NEG_INF)
    e_out = jnp.zeros(logits.shape, jnp.int32)
    p_out = jnp.zeros(logits.shape, F32)
    top = None
    den = None
    onehots = []
    for r in range(TOP_K):
        mx = jnp.max(logits, axis=-1, keepdims=True)
        idx = jnp.min(jnp.where(logits == mx, lane, LANES), axis=-1, keepdims=True)
        if r == 0:
            top = mx
        p = jnp.exp(mx - top)
        den = p if r == 0 else den + p
        sel = lane == idx
        onehots.append(jnp.where(sel, 1.0, 0.0).astype(F32))
        e_out = jnp.where(lane == r, idx, e_out)
        p_out = jnp.where(lane == r, p, p_out)
        logits = jnp.where(sel, NEG_INF, logits)
    e_ref[...] = e_out
    gate_ref[...] = p_out / den

    oh_sum = onehots[0] + onehots[1] + onehots[2] + onehots[3]
    row = lax.broadcasted_iota(jnp.int32, (tm, tm), 0)
    col = lax.broadcasted_iota(jnp.int32, (tm, tm), 1)
    before = jnp.where(row > col, 1.0, 0.0).astype(BF16)
    base = jnp.dot(before, oh_sum.astype(BF16), preferred_element_type=F32) + cnt_sc[0:1, :]
    rank_out = jnp.zeros(logits.shape, jnp.int32)
    for r in range(TOP_K):
        rk = jnp.sum(onehots[r] * base, axis=-1, keepdims=True)
        rank_out = jnp.where(lane == r, rk.astype(jnp.int32), rank_out)
    rank_ref[...] = rank_out
    total = cnt_sc[0:1, :] + jnp.sum(oh_sum, axis=0, keepdims=True)
    cnt_sc[0:1, :] = total
    cnt_ref[...] = jnp.broadcast_to(total, cnt_ref.shape).astype(jnp.int32)


def _outproj(ya, yb, yc, x2d, w_bf, g, b, wr128, br128):
    t, d = x2d.shape
    tm = OUT_TM
    rows = lambda a: pl.BlockSpec((tm, a.shape[1]), lambda i: (i, 0))
    full = lambda a: pl.BlockSpec(a.shape, lambda i: (0,) * a.ndim, pipeline_mode=pl.Buffered(1))
    g2, b2 = g.reshape(1, d), b.reshape(1, d)
    lane_blk = pl.BlockSpec((tm, LANES), lambda i: (i, 0))
    return pl.pallas_call(
        _outproj_kernel,
        grid=(t // tm,),
        in_specs=[rows(ya), rows(yb), rows(yc), rows(x2d), full(w_bf), full(g2), full(b2), full(wr128), full(br128)],
        out_specs=[pl.BlockSpec((tm, d), lambda i: (i, 0)),
                   pl.BlockSpec((tm * ROW_TILES, LANES), lambda i: (i, 0)),
                   lane_blk, lane_blk, lane_blk,
                   pl.BlockSpec((SUBLANES, LANES), lambda i: (0, 0))],
        out_shape=[jax.ShapeDtypeStruct((t, d), F32),
                   jax.ShapeDtypeStruct((t * ROW_TILES, LANES), U32),
                   jax.ShapeDtypeStruct((t, LANES), jnp.int32),
                   jax.ShapeDtypeStruct((t, LANES), F32),
                   jax.ShapeDtypeStruct((t, LANES), jnp.int32),
                   jax.ShapeDtypeStruct((SUBLANES, LANES), jnp.int32)],
        scratch_shapes=[pltpu.VMEM((SUBLANES, LANES), F32)],
        compiler_params=_cparams(("arbitrary",)),
        name="outproj",
    )(ya, yb, yc, x2d, w_bf, g2, b2, wr128, br128)


def _rc_rows(row, n=1):
    return pl.ds(pl.multiple_of(row * ROW_TILES, ROW_TILES), n * ROW_TILES)


def _dispatch_kernel(pos_ref, cnt_ref, pst_ref, pad_ref, nblk_ref, x_hbm, xs_hbm, buf, zbuf, sem_in, sem_out, sem_z):
    i = pl.program_id(0)
    n = pl.num_programs(0)
    tm = DSP_TM
    nslot = buf.shape[0]

    def in_copy(blk, slot):
        return pltpu.make_async_copy(x_hbm.at[_rc_rows(blk * tm, tm)], buf.at[slot], sem_in.at[slot])

    def row_out(slot, t, p):
        return pltpu.make_async_copy(buf.at[slot, _rc_rows(t)], xs_hbm.at[_rc_rows(p)], sem_out.at[slot])

    def wait_outs(slot):
        for _ in range(TOP_K):
            pltpu.make_async_copy(buf.at[slot], xs_hbm.at[_rc_rows(0, tm)], sem_out.at[slot]).wait()

    slot = lax.rem(i, nslot)

    @pl.when(i == 0)
    def _first_in():
        in_copy(0, 0).start()

    @pl.when(i >= 2)
    def _free_slot():
        wait_outs(lax.rem(i + 1, nslot))

    @pl.when(i + 1 < n)
    def _next_in():
        in_copy(i + 1, lax.rem(i + 1, nslot)).start()

    in_copy(i, slot).wait()
    base = i * (tm * TOP_K)

    def issue(t, _):
        for k in range(TOP_K):
            row_out(slot, t, pos_ref[base + t * TOP_K + k]).start(priority=k % 2)
        return 0

    lax.fori_loop(0, tm, issue, 0, unroll=2)

    @pl.when(i == n - 1)
    def _finish():
        wait_outs(lax.rem(i + 2, nslot))
        wait_outs(slot)
        zbuf[...] = jnp.zeros(zbuf.shape, zbuf.dtype)

        def pad_rows(e, _):
            cnt = cnt_ref[e]
            first = pst_ref[e] + cnt
            npad = pad_ref[e] - cnt

            def zero_row(q):
                return pltpu.make_async_copy(zbuf.at[_rc_rows(0)], xs_hbm.at[_rc_rows(first + q)], sem_z)

            def start(q, c):
                zero_row(q).start()
                return c

            def wait(q, c):
                zero_row(q).wait()
                return c

            lax.fori_loop(0, npad, start, 0)
            lax.fori_loop(0, npad, wait, 0)
            return 0

        lax.fori_loop(0, N_EXPERTS, pad_rows, 0)

        def tail_block(bk, _):
            cp = pltpu.make_async_copy(zbuf, xs_hbm.at[_rc_rows(bk * MOE_PAD, MOE_PAD)], sem_z)
            cp.start()
            cp.wait()
            return 0

        lax.fori_loop(nblk_ref[0], xs_hbm.shape[0] // (MOE_PAD * ROW_TILES), tail_block, 0)


def _dispatch(pos, counts, pad_start, padded, nblk, x1rc, rows):
    t = x1rc.shape[0] // ROW_TILES
    tm = DSP_TM
    return pl.pallas_call(
        _dispatch_kernel,
        grid_spec=pltpu.PrefetchScalarGridSpec(
            num_scalar_prefetch=5,
            grid=(t // tm,),
            in_specs=[pl.BlockSpec(memory_space=pl.ANY)],
            out_specs=pl.BlockSpec(memory_space=pl.ANY),
            scratch_shapes=[pltpu.VMEM((3, tm * ROW_TILES, LANES), U32),
                            pltpu.VMEM((MOE_PAD * ROW_TILES, LANES), U32),
                            pltpu.SemaphoreType.DMA((3,)),
                            pltpu.SemaphoreType.DMA((3,)),
                            pltpu.SemaphoreType.DMA(())]),
        out_shape=jax.ShapeDtypeStruct((rows * ROW_TILES, LANES), U32),
        compiler_params=_cparams(("arbitrary",)),
        name="dispatch",
    )(pos, counts, pad_start, padded, nblk, x1rc)


def _moe_kernel(layer, sbe_ref, sbr_ref, sbn_ref, nblk_ref, xs_hbm, wgu_hbm, wdn_hbm, bgu_ref, bd_ref, y_hbm,
                xbuf, acc, stage_in, stage_out, wg_f, wu_f, wd_f, wg_bf, wu_bf, wd_bf, sem_in, sem_out, sem_w):
    sb = pl.program_id(0)
    nsb = pl.num_programs(0)
    tf = wg_f.shape[2]
    dff = wdn_hbm.shape[2]
    nj = dff // tf
    n = sbn_ref[sb]
    row0 = sbr_ref[sb]
    nchunk = n // MOE_PAD
    cur = lax.rem(sb, 2)
    big, huge = 2 * MOE_PAD, 4 * MOE_PAD

    def weight_copies(sb_idx, j, slot):
        e = sbe_ref[sb_idx]
        cols = pl.ds(pl.multiple_of(j * tf, tf), tf)
        up_cols = pl.ds(pl.multiple_of(dff + j * tf, tf), tf)
        copies = [pltpu.make_async_copy(wdn_hbm.at[layer, e, cols, :], wd_f.at[slot], sem_w.at[slot])]
        part = wg_f.shape[1] // MOE_W_PARTS
        for p in range(MOE_W_PARTS):
            band = pl.ds(p * part, part)
            copies.append(pltpu.make_async_copy(wgu_hbm.at[layer, e, band, cols], wg_f.at[slot, band], sem_w.at[slot]))
            copies.append(pltpu.make_async_copy(wgu_hbm.at[layer, e, band, up_cols], wu_f.at[slot, band],
                                                sem_w.at[slot]))
        return copies

    def chunk_rows(c):
        return pl.ds(pl.multiple_of(c * MOE_PAD, MOE_PAD), MOE_PAD)

    def chunk_in(first_row, c):
        return pltpu.make_async_copy(xs_hbm.at[_rc_rows(first_row + c * MOE_PAD, MOE_PAD)], stage_in, sem_in)

    def chunk_out(c, slot):
        return pltpu.make_async_copy(stage_out.at[slot], y_hbm.at[_rc_rows(row0 + c * MOE_PAD, MOE_PAD)],
                                     sem_out.at[slot])

    def wait_outs(count):
        @pl.when(count >= 2)
        def _older():
            chunk_out(0, lax.rem(count, 2)).wait()

        @pl.when(count >= 1)
        def _newest():
            chunk_out(0, lax.rem(count + 1, 2)).wait()

    def convert_in(slot, c):
        lo, hi = _load_rc(stage_in, MOE_PAD)
        xbuf[slot, chunk_rows(c), :] = jnp.concatenate([lo.astype(BF16), hi.astype(BF16)], axis=-1)

    @pl.when(sb == 0)
    def _load_first():
        for cp in weight_copies(0, 0, 0):
            cp.start()
        acc[...] = jnp.zeros(acc.shape, acc.dtype)

        def load(c, _):
            chunk_in(row0, c).start()
            chunk_in(row0, c).wait()
            convert_in(0, c)
            return 0

        lax.fori_loop(0, nchunk, load, 0)

    n_next = sbn_ref[sb + 1]

    def ffn(j, start, size, wg, wu, wd):
        rs = pl.ds(start, size)
        xr = xbuf[cur, rs, :]
        cols = pl.ds(pl.multiple_of(j * tf, tf), tf)
        up_cols = pl.ds(pl.multiple_of(dff + j * tf, tf), tf)
        gt = jnp.dot(xr, wg, preferred_element_type=F32) + bgu_ref[:, cols]
        up = jnp.dot(xr, wu, preferred_element_type=F32) + bgu_ref[:, up_cols]
        gt = jnp.minimum(gt, SWIGLU_LIMIT)
        up = jnp.clip(up, -SWIGLU_LIMIT, SWIGLU_LIMIT)
        act = ((up + 1.0) * gt * jax.nn.sigmoid(SWIGLU_ALPHA * gt)).astype(BF16)
        half = acc.shape[1] // 2
        for cs in (slice(0, half), slice(half, 2 * half)):
            prev = jnp.where(j == 0, bd_ref[:, cs], acc[rs, cs])
            acc[rs, cs] = prev + jnp.dot(act, wd[:, cs], preferred_element_type=F32)

    def first_chunk(j, wslot, size):
        wg = wg_f[wslot].astype(BF16)
        wu = wu_f[wslot].astype(BF16)
        wd = wd_f[wslot].astype(BF16)
        wg_bf[...] = wg
        wu_bf[...] = wu
        wd_bf[...] = wd
        ffn(j, 0, size, wg, wu, wd)

    def hidden_tile(j, _):
        wslot = lax.rem(j, 2)
        for cp in weight_copies(sb, j, wslot):
            cp.wait()

        @pl.when(j + 1 < nj)
        def _next_tile():
            for cp in weight_copies(sb, j + 1, 1 - wslot):
                cp.start(priority=1)

        @pl.when((j + 1 == nj) & (n_next > 0))
        def _next_super_block():
            for cp in weight_copies(sb + 1, 0, 0):
                cp.start(priority=1)

        fetch_next = j * MOE_PAD < n_next

        @pl.when(fetch_next)
        def _start_next():
            chunk_in(sbr_ref[sb + 1], j).start()

        @pl.when((j == 1) & (sb > 0))
        def _drain_prev_outs():
            wait_outs(sbn_ref[jnp.maximum(sb - 1, 0)] // MOE_PAD)

        def rest(start, size):
            ffn(j, start, size, wg_bf[...], wu_bf[...], wd_bf[...])

        @pl.when(n >= huge)
        def _rows_huge():
            first_chunk(j, wslot, huge)

            @pl.when(n - huge >= big)
            def _then_big():
                rest(huge, big)

            @pl.when(lax.rem(n, big) != 0)
            def _then_pad():
                rest(pl.multiple_of(n - MOE_PAD, MOE_PAD), MOE_PAD)

        @pl.when((n >= big) & (n < huge))
        def _rows_big():
            first_chunk(j, wslot, big)

            @pl.when(n > big)
            def _then_pad():
                rest(big, MOE_PAD)

        @pl.when(n < big)
        def _rows_small():
            first_chunk(j, wslot, MOE_PAD)

        @pl.when(j == nj - 1)
        def _store_rows():
            def store(c, _):
                slot = lax.rem(c, 2)

                @pl.when(c >= 2)
                def _slot_free():
                    chunk_out(c - 2, slot).wait()

                _store_rc(stage_out.at[slot], acc[chunk_rows(c), :])
                chunk_out(c, slot).start()
                return 0

            lax.fori_loop(0, nchunk, store, 0)

            @pl.when((sb == nsb - 1) | (n_next == 0))
            def _no_later_tile():
                wait_outs(nchunk)

        @pl.when(fetch_next)
        def _finish_next():
            chunk_in(sbr_ref[sb + 1], j).wait()
            convert_in(1 - cur, j)

        return 0

    @pl.when(n > 0)
    def _super_block():
        lax.fori_loop(0, nj, hidden_tile, 0)

    @pl.when(sb == nsb - 1)
    def _zero_tail():
        stage_out[0] = jnp.zeros(stage_out.shape[1:], stage_out.dtype)

        def tail_block(bk, _):
            cp = pltpu.make_async_copy(stage_out.at[0], y_hbm.at[_rc_rows(bk * MOE_PAD, MOE_PAD)], sem_out.at[0])
            cp.start()
            cp.wait()
            return 0

        lax.fori_loop(nblk_ref[0], y_hbm.shape[0] // (MOE_PAD * ROW_TILES), tail_block, 0)


def _moe(layer, sb_e, sb_row0, sb_n, nblk, xs, w_gate_up, b_gate_up, w_down, b_down):
    rows = xs.shape[0] // ROW_TILES
    d = D_MODEL
    dff = w_down.shape[2]
    tf = MOE_TF
    nj = dff // tf
    nsb = sb_e.shape[0] - 1
    assert nj % 2 == 0 and MOE_TMAX // MOE_PAD <= nj and MOE_TMAX == 6 * MOE_PAD
    bgu = b_gate_up.reshape(DEPTH, N_EXPERTS, 1, 2 * dff)
    bdn = b_down.reshape(DEPTH, N_EXPERTS, 1, d)
    expert_row = lambda width: pl.BlockSpec((None, None, 1, width), lambda sb, e, r, n, nb: (layer, e[sb], 0, 0))
    return pl.pallas_call(
        functools.partial(_moe_kernel, layer),
        grid_spec=pltpu.PrefetchScalarGridSpec(
            num_scalar_prefetch=4,
            grid=(nsb,),
            in_specs=[pl.BlockSpec(memory_space=pl.ANY),
                      pl.BlockSpec(memory_space=pl.ANY),
                      pl.BlockSpec(memory_space=pl.ANY),
                      expert_row(2 * dff),
                      expert_row(d)],
            out_specs=pl.BlockSpec(memory_space=pl.ANY),
            scratch_shapes=[pltpu.VMEM((2, MOE_TMAX, d), BF16),
                            pltpu.VMEM((MOE_TMAX, d), F32),
                            pltpu.VMEM((MOE_PAD * ROW_TILES, LANES), U32),
                            pltpu.VMEM((2, MOE_PAD * ROW_TILES, LANES), U32),
                            pltpu.VMEM((2, d, tf), F32),
                            pltpu.VMEM((2, d, tf), F32),
                            pltpu.VMEM((2, tf, d), F32),
                            pltpu.VMEM((d, tf), BF16),
                            pltpu.VMEM((d, tf), BF16),
                            pltpu.VMEM((tf, d), BF16),
                            pltpu.SemaphoreType.DMA(()),
                            pltpu.SemaphoreType.DMA((2,)),
                            pltpu.SemaphoreType.DMA((2,))]),
        out_shape=jax.ShapeDtypeStruct((rows * ROW_TILES, LANES), U32),
        compiler_params=_cparams(("arbitrary",)),
        name="moe",
    )(sb_e, sb_row0, sb_n, nblk, xs, w_gate_up, w_down, bgu, bdn)


def _combine_kernel(pos_ref, y_hbm, gate_ref, x1_ref, g_ref, b_ref, o_ref, buf, sem):
    i = pl.program_id(0)
    n = pl.num_programs(0)
    tm = CMB_TM
    slot = lax.rem(i, 2)

    def issue(blk, dst_slot):
        base = blk * (tm * TOP_K)

        def body(t, _):
            for k in range(TOP_K):
                p = pos_ref[base + t * TOP_K + k]
                pltpu.make_async_copy(y_hbm.at[_rc_rows(p)], buf.at[dst_slot, k, _rc_rows(t)],
                                      sem.at[dst_slot]).start(priority=k % 2)
            return 0

        lax.fori_loop(0, tm, body, 0, unroll=2)

    @pl.when(i == 0)
    def _first():
        issue(0, 0)

    @pl.when(i + 1 < n)
    def _next():
        issue(i + 1, 1 - slot)

    for k in range(TOP_K):
        pltpu.make_async_copy(y_hbm.at[_rc_rows(0, tm)], buf.at[slot, k], sem.at[slot]).wait()
    gate = gate_ref[...]
    ffn_lo, ffn_hi = None, None
    for k in range(TOP_K):
        lo, hi = _load_rc(buf.at[slot, k], tm)
        gk = gate[:, k:k + 1]
        ffn_lo = gk * lo if k == 0 else ffn_lo + gk * lo
        ffn_hi = gk * hi if k == 0 else ffn_hi + gk * hi
    ffn = jnp.concatenate([ffn_lo, ffn_hi], axis=-1)
    o_ref[...] = _layer_norm(DN_ALPHA * x1_ref[...] + ffn, g_ref[...], b_ref[...])


def _combine(pos_flat, y, gate, x1, g, b):
    t, d = x1.shape
    tm = CMB_TM
    g2, b2 = g.reshape(1, d), b.reshape(1, d)
    return pl.pallas_call(
        _combine_kernel,
        grid_spec=pltpu.PrefetchScalarGridSpec(
            num_scalar_prefetch=1,
            grid=(t // tm,),
            in_specs=[pl.BlockSpec(memory_space=pl.ANY),
                      pl.BlockSpec((tm, LANES), lambda i, p: (i, 0)),
                      pl.BlockSpec((tm, d), lambda i, p: (i, 0)),
                      pl.BlockSpec((1, d), lambda i, p: (0, 0)),
                      pl.BlockSpec((1, d), lambda i, p: (0, 0))],
            out_specs=pl.BlockSpec((tm, d), lambda i, p: (i, 0)),
            scratch_shapes=[pltpu.VMEM((2, TOP_K, tm * ROW_TILES, LANES), U32), pltpu.SemaphoreType.DMA((2,))]),
        out_shape=jax.ShapeDtypeStruct((t, d), F32),
        compiler_params=_cparams(("arbitrary",)),
        name="combine",
    )(pos_flat, y, gate, x1, g2, b2)


def _swap_halves(w):
    half = w.shape[-1] // 2
    return jnp.concatenate([w[..., half:], w[..., :half]], axis=-1)


def _w_in_columns(w):
    d = w.shape[0]
    widths = (512, 128, 128, 256, 256, 512, 512, 8, 512, 256, 64)
    offs = [0]
    for wd in widths:
        offs.append(offs[-1] + wd)
    a_q, a_k, a_v, m_q, m_k, m_v, m_o, m_if, c_q, c_kv, c_kr = [w[:, offs[i]:offs[i + 1]] for i in range(len(widths))]
    z = lambda n: jnp.zeros((d, n), w.dtype)
    return jnp.concatenate([a_q, m_v, m_o, c_q, m_q, m_k, c_kv, a_k, a_v,
                            c_kr, z(LANES - 64), m_if, z(LANES - 8), _swap_halves(c_kr), z(LANES - 64)], axis=1)


def _w_in_layout_kernel(w_ref, o_ref):
    o_ref[...] = _w_in_columns(w_ref[...]).astype(o_ref.dtype)


def _layout_w_in(w_in, layer):
    _, d, n = w_in.shape
    tk = 256
    return pl.pallas_call(
        _w_in_layout_kernel,
        grid=(d // tk,),
        in_specs=[pl.BlockSpec((None, tk, n), lambda i: (layer, i, 0))],
        out_specs=pl.BlockSpec((tk, N_PROJ), lambda i: (i, 0)),
        out_shape=jax.ShapeDtypeStruct((d, N_PROJ), BF16),
        compiler_params=_cparams(("parallel",)),
        name="w_in_layout",
    )(w_in)


def _layout_w_uq(w):
    r = w.shape[0]
    w3 = w.astype(BF16).reshape(r, C_HEADS, C_NOPE_DIM + C_ROPE_DIM)
    nope, rope = w3[..., :C_NOPE_DIM], w3[..., C_NOPE_DIM:]
    z = jnp.zeros((r, C_HEADS, LANES - C_ROPE_DIM), BF16)
    wa = jnp.concatenate([nope, rope, z], axis=-1).reshape(r, -1)
    wb = jnp.concatenate([_swap_halves(rope), z], axis=-1).reshape(r, -1)
    return wa, wb


def _layout_w_ukv(w):
    r = w.shape[0]
    w3 = w.astype(BF16).reshape(r, C_HEADS, C_NOPE_DIM + C_V_DIM)
    return jnp.concatenate([w3[..., :C_NOPE_DIM].reshape(r, -1), w3[..., C_NOPE_DIM:].reshape(r, -1)], axis=-1)


def _rope_tables(seq):
    dim = C_ROPE_DIM
    inv = 1.0 / (ROPE_THETA ** (jnp.arange(0, dim, 2, dtype=F32) / dim))
    ang = jnp.arange(seq, dtype=F32)[:, None] * inv[None, :]
    cos, sin = jnp.cos(ang), jnp.sin(ang)
    z = jnp.zeros((seq, LANES - dim), F32)
    return jnp.concatenate([cos, cos, z], axis=-1), jnp.concatenate([-sin, sin, z], axis=-1)


def _route(top_e, rank, counts, nsb):
    padded = (counts + MOE_PAD - 1) // MOE_PAD * MOE_PAD
    pad_end = jnp.cumsum(padded)
    pad_start = pad_end - padded
    is_e = top_e[:, :, None] == jnp.arange(N_EXPERTS, dtype=jnp.int32)[None, None, :]
    pos = (jnp.sum(jnp.where(is_e, pad_start[None, None, :], 0), axis=-1) + rank).reshape(-1).astype(jnp.int32)
    nblk = (pad_end[-1:] // MOE_PAD).astype(jnp.int32)
    nsb_e = (padded + MOE_TMAX - 1) // MOE_TMAX
    sb_end = jnp.cumsum(nsb_e)
    sb_start = sb_end - nsb_e
    sb_idx = jnp.arange(nsb + 1, dtype=jnp.int32)
    n_valid = sb_end[-1]
    sb_eff = jnp.minimum(sb_idx, n_valid - 1)
    sb_e = jnp.searchsorted(sb_end, sb_eff, side="right").astype(jnp.int32)
    part = sb_eff - sb_start[sb_e]
    sb_row0 = pad_start[sb_e] + part * MOE_TMAX
    sb_n = jnp.where(sb_idx < n_valid, jnp.minimum(padded[sb_e] - part * MOE_TMAX, MOE_TMAX), 0)
    tables = (counts, pad_start.astype(jnp.int32), padded.astype(jnp.int32), nblk)
    return pos, tables, sb_e, sb_row0.astype(jnp.int32), sb_n.astype(jnp.int32)


def kernel(x, w_in, conv_w, conv_b, m_gate_b, m_norm_g, sinks, q_norm_g, w_uq, kv_norm_g, w_ukv, w_out,
           ln1_g, ln1_b, w_router, b_router, w_gate_up, b_gate_up, w_down, b_down, ln2_g, ln2_b):
    b, s, d = x.shape
    t = b * s
    n_assign = t * TOP_K
    rows = (n_assign // MOE_PAD + N_EXPERTS) * MOE_PAD
    nsb = N_EXPERTS + rows // MOE_TMAX
    cos128, sin128 = _rope_tables(s)
    xt = x.reshape(t, d)
    for l in range(DEPTH):
        w_in_bf = _layout_w_in(w_in, l)
        wqa, wqb = _layout_w_uq(w_uq[l])
        wkv = _layout_w_ukv(w_ukv[l])
        w_out_bf = w_out[l].astype(BF16)
        wr128 = jnp.zeros((d, LANES), BF16).at[:, :N_EXPERTS].set(w_router[l].astype(BF16))
        br128 = jnp.zeros((1, LANES), F32).at[0, :N_EXPERTS].set(b_router[l])

        proj, gates = _proj(xt, w_in_bf)
        proj3 = proj.reshape(b, s, N_PROJ)
        y_a = _swa(proj3, sinks[l])
        y_b = _mlstm(proj3, gates.reshape(b, s, LANES), conv_w[l], conv_b[l], m_gate_b[l], m_norm_g[l])
        qc, kc, vc = _mla_prep(proj3, q_norm_g[l], kv_norm_g[l], wqa, wqb, wkv, cos128, sin128)
        y_c = _flash(qc, kc, vc)
        x1, x1rc, top_e, gate, rank, cnt = _outproj(y_a.reshape(t, -1), y_b.reshape(t, -1), y_c.reshape(t, -1), xt,
                                                    w_out_bf, ln1_g[l], ln1_b[l], wr128, br128)
        pos, tables, sb_e, sb_row0, sb_n = _route(top_e[:, :TOP_K], rank[:, :TOP_K], cnt[0, :N_EXPERTS], nsb)
        xs = _dispatch(pos, *tables, x1rc, rows)
        y = _moe(l, sb_e, sb_row0, sb_n, tables[3], xs, w_gate_up, b_gate_up, w_down, b_down)
        xt = _combine(pos, y, gate, x1, ln2_g[l], ln2_b[l])
    return xt.reshape(b, s, d)
```
